```python
import math
import jax
import jax.numpy as jnp
from jax import lax
import numpy as np

D_MODEL = 4096
BATCH = 4
SEQ = 2048
DEPTH = 2
DEC_BATCH = 8
DEC_SEQ = 4
PAST_LEN = 16384
PAGE_SIZE = 128

BRANCH = D_MODEL // 4
HG_HEADS = 8
HG_K = BRANCH // HG_HEADS
HG_V = BRANCH // HG_HEADS
SSM_HEAD_DIM = 64
SSM_HEADS = BRANCH // SSM_HEAD_DIM
SSM_GROUPS = 4
SSM_STATE = 128
SSM_CONV = 4
SSM_CONV_DIM = BRANCH + 2 * SSM_GROUPS * SSM_STATE
S5_CH = 16
S5_GROUPS = BRANCH // S5_CH
S5_STATE = 64
ATT_HEAD_DIM = 128
ATT_HEADS = BRANCH // ATT_HEAD_DIM
KV_HEADS = 2
IDX_HEADS = 16
IDX_DIM = 64
TOPK_MAX = 256
Q_BLOCK = 128
SCAN_CHUNK = 64
REL_BUCKETS = 32
REL_MAX_DIST = 1024
EPS = 1e-6
MASK_VALUE = -1e30
F_FLOOR = 1e-30

PROJ_SIZES = (BRANCH, BRANCH, BRANCH, BRANCH,
              BRANCH, SSM_CONV_DIM, SSM_HEADS,
              BRANCH, BRANCH,
              ATT_HEADS * ATT_HEAD_DIM, KV_HEADS * ATT_HEAD_DIM,
              KV_HEADS * ATT_HEAD_DIM, BRANCH,
              IDX_HEADS * IDX_DIM, IDX_DIM, IDX_HEADS)
D_IN = sum(PROJ_SIZES)

kernel_name = 'hybrid_hgrn2_ssd_s5_dsa_step'


def rms_norm(x, w):
    xf = x.astype(jnp.float32)
    xf = xf * lax.rsqrt(jnp.mean(xf * xf, axis=-1, keepdims=True) + EPS)
    return (xf * w.astype(jnp.float32)).astype(x.dtype)


def split_proj(z):
    offs = np.cumsum(PROJ_SIZES)[:-1]
    return jnp.split(z, [int(o) for o in offs], axis=-1)


def to_chunks(a, c):
    b, l = a.shape[0], a.shape[1]
    return jnp.moveaxis(a.reshape((b, l // c, c) + a.shape[2:]), 1, 0)


def from_chunks(a):
    a = jnp.moveaxis(a, 0, 1)
    return a.reshape((a.shape[0], a.shape[1] * a.shape[2]) + a.shape[3:])


def hgrn2_recurrence(q, k, v, log_f, s0, chunk):
    mask = jnp.tril(jnp.ones((chunk, chunk), dtype=bool))

    def step(s, inp):
        qc, kc, vc, gc = inp
        g = jnp.cumsum(gc, axis=1)
        diff = g[:, :, None] - g[:, None, :]
        decay = jnp.exp(jnp.where(mask[None, :, :, None, None], diff, MASK_VALUE))
        att = jnp.einsum('bthk,bshk,btshk->bhts', qc, kc, decay)
        o = jnp.einsum('bhts,bshv->bthv', att, vc) + jnp.einsum('bthk,bhkv->bthv', qc * jnp.exp(g), s)
        g_last = g[:, -1]
        s = jnp.exp(g_last)[..., None] * s + jnp.einsum('bshk,bshv->bhkv', kc * jnp.exp(g_last[:, None] - g), vc)
        return s, o

    s, o = lax.scan(step, s0, (to_chunks(q, chunk), to_chunks(k, chunk), to_chunks(v, chunk), to_chunks(log_f, chunk)))
    return from_chunks(o), s


def ssd_recurrence(x, dt, a, bm, cm, s0, chunk):
    b, L, H, P = x.shape
    G, N = bm.shape[2], bm.shape[3]
    hpg = H // G
    xg = (x * dt[..., None]).reshape(b, L, G, hpg, P)
    la = (dt * a).reshape(b, L, G, hpg)
    mask = jnp.tril(jnp.ones((chunk, chunk), dtype=bool))

    def step(s, inp):
        xc, lac, bc, cc = inp
        cum = jnp.cumsum(lac, axis=1)
        diff = cum[:, :, None] - cum[:, None, :]
        decay = jnp.exp(jnp.where(mask[None, :, :, None, None], diff, MASK_VALUE))
        cb = jnp.einsum('btgn,bsgn->btsg', cc, bc)
        y = jnp.einsum('btsg,btsgh,bsghp->btghp', cb, decay, xc)
        y = y + jnp.einsum('btgn,bghpn->btghp', cc, s) * jnp.exp(cum)[..., None]
        last = cum[:, -1]
        s = jnp.exp(last)[..., None, None] * s + jnp.einsum(
            'bsghp,bsgn->bghpn', xc * jnp.exp(last[:, None] - cum)[..., None], bc)
        return s, y

    s, y = lax.scan(step, s0.reshape(b, G, hpg, P, N),
                    (to_chunks(xg, chunk), to_chunks(la, chunk), to_chunks(bm, chunk), to_chunks(cm, chunk)))
    return from_chunks(y).reshape(b, L, H, P), s.reshape(b, H, P, N)


def causal_dwconv(x, buf, w, bias):
    xp = jnp.concatenate([buf, x], axis=1)
    L = x.shape[1]
    y = bias
    for j in range(SSM_CONV):
        y = y + xp[:, j:j + L] * w[j]
    return y, xp[:, xp.shape[1] - (SSM_CONV - 1):]


def s5_recurrence(u, a_re, a_im, log_dt, b_re, b_im, c_re, c_im, d, x0_re, x0_im):
    f32 = jnp.float32
    u = u.astype(f32)
    a_re, a_im = a_re.astype(f32), a_im.astype(f32)
    dt = jnp.exp(log_dt.astype(f32))[:, None]
    mag = jnp.exp(a_re * dt)
    ab_re, ab_im = mag * jnp.cos(a_im * dt), mag * jnp.sin(a_im * dt)
    den = a_re * a_re + a_im * a_im
    nr = ab_re - 1.0
    coef_re = (nr * a_re + ab_im * a_im) / den
    coef_im = (ab_im * a_re - nr * a_im) / den
    bu_re = jnp.einsum('blgc,gpc->blgp', u, b_re.astype(f32))
    bu_im = jnp.einsum('blgc,gpc->blgp', u, b_im.astype(f32))
    bb_re = coef_re * bu_re - coef_im * bu_im
    bb_im = coef_re * bu_im + coef_im * bu_re
    x0_re, x0_im = x0_re.astype(f32), x0_im.astype(f32)
    bb_re = bb_re.at[:, 0].add(ab_re * x0_re - ab_im * x0_im)
    bb_im = bb_im.at[:, 0].add(ab_re * x0_im + ab_im * x0_re)
    el_re = jnp.broadcast_to(ab_re, bb_re.shape)
    el_im = jnp.broadcast_to(ab_im, bb_im.shape)

    def combine(e1, e2):
        a1r, a1i, b1r, b1i = e1
        a2r, a2i, b2r, b2i = e2
        return (a1r * a2r - a1i * a2i, a1r * a2i + a1i * a2r,
                a2r * b1r - a2i * b1i + b2r, a2r * b1i + a2i * b1r + b2i)

    _, _, xr, xi = lax.associative_scan(combine, (el_re, el_im, bb_re, bb_im), axis=1)
    y = (jnp.einsum('blgp,gcp->blgc', xr, c_re.astype(f32))
         - jnp.einsum('blgp,gcp->blgc', xi, c_im.astype(f32)) + d.astype(f32) * u)
    return y, xr[:, -1], xi[:, -1]


def t5_bucket(dist):
    exact = REL_BUCKETS // 2
    d = dist.astype(jnp.float32)
    large = exact + jnp.log(jnp.maximum(d, 1.0) / exact) / math.log(REL_MAX_DIST / exact) * (REL_BUCKETS - exact)
    large = jnp.minimum(jnp.maximum(large, 0.0).astype(jnp.int32), REL_BUCKETS - 1)
    return jnp.where(dist < exact, dist, large)


def dsa_attend(q, qi, wi, k, v, ki, q_pos, k_pos, rel_bias, topk):
    f32 = jnp.float32
    b, t, h, dh = q.shape
    n_kv = k.shape[2]
    dots = jnp.einsum('bthd,bsd->bths', qi.astype(f32), ki.astype(f32)) * (IDX_DIM ** -0.5)
    score = jnp.einsum('bths,bth->bts', jax.nn.relu(dots), wi.astype(f32)) * (IDX_HEADS ** -0.5)
    admissible = k_pos[None, :] <= q_pos[:, None]
    score = jnp.where(admissible[None], score, MASK_VALUE)
    _, idx = lax.top_k(score, topk)
    sel_pos = k_pos[idx]
    dist = q_pos[None, :, None] - sel_pos
    valid = dist >= 0
    take = jax.vmap(lambda arr, ii: arr[ii])
    k_sel = take(k, idx).astype(f32)
    v_sel = take(v, idx).astype(f32)
    qg = q.reshape(b, t, n_kv, h // n_kv, dh).astype(f32)
    logits = jnp.einsum('btkgd,btnkd->btkgn', qg, k_sel) * (dh ** -0.5)
    bias = rel_bias.astype(f32)[t5_bucket(jnp.maximum(dist, 0))]
    bias = jnp.moveaxis(bias, 2, -1).reshape(b, t, n_kv, h // n_kv, topk)
    logits = jnp.where(valid[:, :, None, None, :], logits + bias, MASK_VALUE)
    p = jax.nn.softmax(logits, axis=-1)
    out = jnp.einsum('btkgn,btnkd->btkgd', p, v_sel)
    return out.reshape(b, t, h * dh).astype(q.dtype)


def prompt_attention(q, k, v, qi, ki, wi, rel_bias):
    L = q.shape[1]
    pos = jnp.arange(L, dtype=jnp.int32)
    topk = min(TOPK_MAX, L // 4)
    blk = min(Q_BLOCK, L)

    def one(args):
        qb, qib, wib, pb = args
        return dsa_attend(qb, qib, wib, k, v, ki, pb, pos, rel_bias, topk)

    out = lax.map(one, (to_chunks(q, blk), to_chunks(qi, blk), to_chunks(wi, blk), pos.reshape(L // blk, blk)))
    return from_chunks(out)


def sample_attention(q, k, v, qi, ki, wi, rel_bias, ck, cv, cik, page_table):
    nb, t = q.shape[0], q.shape[1]
    past = page_table.shape[1] * PAGE_SIZE

    def gather(c, new):
        rows = c[page_table].reshape((nb, past) + c.shape[2:])
        return jnp.concatenate([rows, new.astype(c.dtype)], axis=1)

    k_all, v_all, ki_all = gather(ck, k), gather(cv, v), gather(cik, ki)
    k_pos = jnp.arange(past + t, dtype=jnp.int32)
    q_pos = past + jnp.arange(t, dtype=jnp.int32)
    return dsa_attend(q, qi, wi, k_all, v_all, ki_all, q_pos, k_pos, rel_bias, min(TOPK_MAX, (past + t) // 4))


def mixer_layer(x, lp, lb, hg_s0, ssm_s0, conv_s0, s5_re0, s5_im0, attn_fn):
    f32 = jnp.float32
    b, L, _ = x.shape
    chunk = math.gcd(L, SCAN_CHUNK)
    hn = rms_norm(x, lp['norm_w'])
    (hg_q, hg_f, hg_i, hg_g, sm_z, sm_xbc, sm_dt, s5_u, s5_g,
     at_q, at_k, at_v, at_g, ix_q, ix_k, ix_w) = split_proj(hn @ lp['w_in'])
    lbh = lb.reshape(HG_HEADS, HG_K)
    f_pre = hg_f.reshape(b, L, HG_HEADS, HG_K).astype(f32)
    f_gate = lbh + (1.0 - lbh) * jax.nn.sigmoid(f_pre)
    log_f = jnp.log(jnp.maximum(f_gate, F_FLOOR))
    k_hg = (1.0 - lbh) * jax.nn.sigmoid(-f_pre)
    q_hg = jax.nn.silu(hg_q.reshape(b, L, HG_HEADS, HG_K).astype(f32))
    v_hg = hg_i.reshape(b, L, HG_HEADS, HG_V).astype(f32)
    o_hg, hg_s = hgrn2_recurrence(q_hg, k_hg, v_hg, log_f, hg_s0.astype(f32), chunk)
    o_hg = rms_norm(o_hg, lp['hg_norm_w']).reshape(b, L, BRANCH) * jax.nn.silu(hg_g.astype(f32))
    xbc, conv_s = causal_dwconv(sm_xbc, conv_s0.astype(sm_xbc.dtype), lp['conv_w'], lp['conv_b'])
    xbc = jax.nn.silu(xbc.astype(f32))
    xs, bm, cm = jnp.split(xbc, [BRANCH, BRANCH + SSM_GROUPS * SSM_STATE], axis=-1)
    xs = xs.reshape(b, L, SSM_HEADS, SSM_HEAD_DIM)
    dt = jax.nn.softplus(sm_dt.astype(f32) + lp['dt_bias'].astype(f32))
    a = -jnp.exp(lp['a_log'].astype(f32))
    y_ssm, ssm_s = ssd_recurrence(xs, dt, a, bm.reshape(b, L, SSM_GROUPS, SSM_STATE),
                                  cm.reshape(b, L, SSM_GROUPS, SSM_STATE), ssm_s0.astype(f32), chunk)
    y_ssm = (y_ssm + lp['ssm_d'].astype(f32)[:, None] * xs).reshape(b, L, BRANCH) * jax.nn.silu(sm_z.astype(f32))
    gw = BRANCH // SSM_GROUPS
    y_ssm = rms_norm(y_ssm.reshape(b, L, SSM_GROUPS, gw), lp['ssm_norm_w'].reshape(SSM_GROUPS, gw)).reshape(b, L, BRANCH)
    y5, s5_re, s5_im = s5_recurrence(s5_u.reshape(b, L, S5_GROUPS, S5_CH), lp['a_re'], lp['a_im'], lp['log_dt'],
                                     lp['b_re'], lp['b_im'], lp['c_re'], lp['c_im'], lp['s5_d'], s5_re0, s5_im0)
    h5 = jax.nn.gelu(y5.reshape(b, L, BRANCH))
    o5 = h5 * jax.nn.sigmoid(h5 @ lp['glu_w'].astype(f32) + lp['glu_b'].astype(f32)) * jax.nn.silu(s5_g.astype(f32))
    q = rms_norm(at_q.reshape(b, L, ATT_HEADS, ATT_HEAD_DIM), lp['q_norm'])
    k = rms_norm(at_k.reshape(b, L, KV_HEADS, ATT_HEAD_DIM), lp['k_norm'])
    v = at_v.reshape(b, L, KV_HEADS, ATT_HEAD_DIM)
    qi = ix_q.reshape(b, L, IDX_HEADS, IDX_DIM)
    o_att = attn_fn(q, k, v, qi, ix_k, ix_w).astype(f32) * jax.nn.silu(at_g.astype(f32))
    mixed = jnp.concatenate([o_hg, y_ssm, o5, o_att], axis=-1).astype(x.dtype)
    y = x + mixed @ lp['w_out']
    return y, (k, v, ix_k, hg_s, ssm_s, conv_s, s5_re, s5_im)


def setup_inputs(seed: int = 0) -> dict:
    key = jax.random.key(seed)
    kit = iter(jax.random.split(key, 64))
    f32 = jnp.float32

    def nrm(shape, scale):
        return scale * jax.random.normal(next(kit), shape, f32)

    def unif(shape, lo, hi):
        return jax.random.uniform(next(kit), shape, f32, lo, hi)

    n_pages = PAST_LEN // PAGE_SIZE
    n_pool = (DEC_BATCH * n_pages * 5) // 4
    perm = jax.random.permutation(next(kit), n_pool)
    page_table = perm[:DEC_BATCH * n_pages].reshape(DEC_BATCH, n_pages).astype(jnp.int32)
    dt0 = jnp.exp(unif((DEPTH, SSM_HEADS), math.log(1e-3), math.log(1e-1)))
    return {
        'x_prompt': nrm((BATCH, SEQ, D_MODEL), 1.0),
        'x_sample': nrm((DEC_BATCH, DEC_SEQ, D_MODEL), 1.0),
        'cache_k': nrm((DEPTH, n_pool, PAGE_SIZE, KV_HEADS, ATT_HEAD_DIM), 1.0),
        'cache_v': nrm((DEPTH, n_pool, PAGE_SIZE, KV_HEADS, ATT_HEAD_DIM), 1.0),
        'cache_idx_k': nrm((DEPTH, n_pool, PAGE_SIZE, IDX_DIM), 1.0),
        'state_hgrn': nrm((DEPTH, DEC_BATCH, HG_HEADS, HG_K, HG_V), 0.3),
        'state_ssm': nrm((DEPTH, DEC_BATCH, SSM_HEADS, SSM_HEAD_DIM, SSM_STATE), 0.3),
        'state_conv': nrm((DEPTH, DEC_BATCH, SSM_CONV - 1, SSM_CONV_DIM), 1.0),
        'state_s5_re': nrm((DEPTH, DEC_BATCH, S5_GROUPS, S5_STATE), 0.3),
        'state_s5_im': nrm((DEPTH, DEC_BATCH, S5_GROUPS, S5_STATE), 0.3),
        'page_table': page_table,
        'norm_w': 1.0 + nrm((DEPTH, D_MODEL), 0.02),
        'w_in': nrm((DEPTH, D_MODEL, D_IN), D_MODEL ** -0.5),
        'w_out': nrm((DEPTH, D_MODEL, D_MODEL), D_MODEL ** -0.5),
        'hg_lb_logits': nrm((DEPTH, BRANCH), 1.0),
        'hg_norm_w': 1.0 + nrm((DEPTH, HG_V), 0.02),
        'ssm_conv_w': nrm((DEPTH, SSM_CONV, SSM_CONV_DIM), 0.5),
        'ssm_conv_b': nrm((DEPTH, SSM_CONV_DIM), 0.02),
        'ssm_dt_bias': dt0 + jnp.log(-jnp.expm1(-dt0)),
        'ssm_a_log': jnp.log(unif((DEPTH, SSM_HEADS), 1.0, 16.0)),
        'ssm_d': 1.0 + nrm((DEPTH, SSM_HEADS), 0.1),
        'ssm_norm_w': 1.0 + nrm((DEPTH, BRANCH), 0.02),
        's5_a_re': -0.5 + nrm((DEPTH, S5_GROUPS, S5_STATE), 0.01),
        's5_a_im': math.pi * jnp.arange(S5_STATE, dtype=f32) + nrm((DEPTH, S5_GROUPS, S5_STATE), 0.01),
        's5_log_dt': unif((DEPTH, S5_GROUPS), math.log(1e-3), math.log(1e-1)),
        's5_b_re': nrm((DEPTH, S5_GROUPS, S5_STATE, S5_CH), (2 * S5_CH) ** -0.5),
        's5_b_im': nrm((DEPTH, S5_GROUPS, S5_STATE, S5_CH), (2 * S5_CH) ** -0.5),
        's5_c_re': nrm((DEPTH, S5_GROUPS, S5_CH, S5_STATE), S5_STATE ** -0.5),
        's5_c_im': nrm((DEPTH, S5_GROUPS, S5_CH, S5_STATE), S5_STATE ** -0.5),
        's5_d': nrm((DEPTH, S5_GROUPS, S5_CH), 0.5),
        's5_glu_w': nrm((DEPTH, BRANCH, BRANCH), BRANCH ** -0.5),
        's5_glu_b': nrm((DEPTH, BRANCH), 0.02),
        'att_q_norm': 1.0 + nrm((DEPTH, ATT_HEAD_DIM), 0.02),
        'att_k_norm': 1.0 + nrm((DEPTH, ATT_HEAD_DIM), 0.02),
        'rel_bias': nrm((REL_BUCKETS, ATT_HEADS), 0.5),
    }


def reference(x_prompt, x_sample, cache_k, cache_v, cache_idx_k, state_hgrn, state_ssm, state_conv,
              state_s5_re, state_s5_im, page_table, norm_w, w_in, w_out, hg_lb_logits, hg_norm_w,
              ssm_conv_w, ssm_conv_b, ssm_dt_bias, ssm_a_log, ssm_d, ssm_norm_w, s5_a_re, s5_a_im,
              s5_log_dt, s5_b_re, s5_b_im, s5_c_re, s5_c_im, s5_d, s5_glu_w, s5_glu_b,
              att_q_norm, att_k_norm, rel_bias):
    f32 = jnp.float32
    sm = jax.nn.softmax(hg_lb_logits.astype(f32), axis=0)
    lower = jnp.cumsum(sm, axis=0) - sm[0]
    bp = x_prompt.shape[0]
    yp, ys = x_prompt, x_sample
    new_p = [[] for _ in range(8)]
    new_s = [[] for _ in range(8)]
    for l in range(DEPTH):
        lp = {'norm_w': norm_w[l], 'w_in': w_in[l], 'w_out': w_out[l], 'hg_norm_w': hg_norm_w[l],
              'conv_w': ssm_conv_w[l], 'conv_b': ssm_conv_b[l], 'dt_bias': ssm_dt_bias[l],
              'a_log': ssm_a_log[l], 'ssm_d': ssm_d[l], 'ssm_norm_w': ssm_norm_w[l],
              'a_re': s5_a_re[l], 'a_im': s5_a_im[l], 'log_dt': s5_log_dt[l],
              'b_re': s5_b_re[l], 'b_im': s5_b_im[l], 'c_re': s5_c_re[l], 'c_im': s5_c_im[l],
              's5_d': s5_d[l], 'glu_w': s5_glu_w[l], 'glu_b': s5_glu_b[l],
              'q_norm': att_q_norm[l], 'k_norm': att_k_norm[l]}
        yp, st_p = mixer_layer(
            yp, lp, lower[l],
            jnp.zeros((bp, HG_HEADS, HG_K, HG_V), f32),
            jnp.zeros((bp, SSM_HEADS, SSM_HEAD_DIM, SSM_STATE), f32),
            jnp.zeros((bp, SSM_CONV - 1, SSM_CONV_DIM), x_prompt.dtype),
            jnp.zeros((bp, S5_GROUPS, S5_STATE), f32),
            jnp.zeros((bp, S5_GROUPS, S5_STATE), f32),
            lambda q, k, v, qi, ki, wi: prompt_attention(q, k, v, qi, ki, wi, rel_bias))
        ys, st_s = mixer_layer(
            ys, lp, lower[l], state_hgrn[l], state_ssm[l], state_conv[l], state_s5_re[l], state_s5_im[l],
            lambda q, k, v, qi, ki, wi: sample_attention(q, k, v, qi, ki, wi, rel_bias, cache_k[l], cache_v[l],
                                                         cache_idx_k[l], page_table))
        for i in range(8):
            new_p[i].append(st_p[i])
            new_s[i].append(st_s[i])
    pk, pv, pik, phg, pssm, pconv, ps5r, ps5i = [jnp.stack(a) for a in new_p]
    sk, sv, sik, shg, sssm, sconv, ss5r, ss5i = [jnp.stack(a) for a in new_s]
    return (yp, ys, pk, pv, pik, phg, pssm, pconv, ps5r, ps5i, sk, sv, sik, shg, sssm, sconv, ss5r, ss5i)
```

```python
import functools
import math

import jax
import jax.numpy as jnp
import numpy as np
from jax import lax
from jax.experimental import pallas as pl
from jax.experimental.pallas import tpu as pltpu

D_MODEL = 4096
DEPTH = 2
PAGE_SIZE = 128
BRANCH = D_MODEL // 4
HG_HEADS = 8
HG_K = BRANCH // HG_HEADS
HG_V = BRANCH // HG_HEADS
SSM_HEAD_DIM = 64
SSM_HEADS = BRANCH // SSM_HEAD_DIM
SSM_GROUPS = 4
SSM_STATE = 128
SSM_CONV = 4
SSM_CONV_DIM = BRANCH + 2 * SSM_GROUPS * SSM_STATE
S5_CH = 16
S5_GROUPS = BRANCH // S5_CH
S5_STATE = 64
ATT_HEAD_DIM = 128
ATT_HEADS = BRANCH // ATT_HEAD_DIM
KV_HEADS = 2
IDX_HEADS = 16
IDX_DIM = 64
TOPK_MAX = 256
Q_BLOCK = 128
SCAN_CHUNK = 64
REL_BUCKETS = 32
REL_MAX_DIST = 1024
EPS = 1e-6
MASK_VALUE = -1e30
F_FLOOR = 1e-30

_SRC_SMALL = 7168
_SRC_AFTER_DT = 7184
_SRC_IXK = 12816
D_IN_SRC = 12896

OFF_HG_Q, OFF_HG_F, OFF_HG_I, OFF_HG_G = 0, 1024, 2048, 3072
OFF_SM_Z, OFF_SM_XBC = 4096, 5120
OFF_S5_U, OFF_S5_G = 7168, 8192
OFF_AT_Q, OFF_AT_K, OFF_AT_V, OFF_AT_G = 9216, 10240, 10496, 10752
OFF_IX_Q = 11776
OFF_SMALL = 12800
SMALL_IXK, SMALL_IXW, SMALL_DT = 0, 64, 80
D_IN_PAD = 13312

V7X_VMEM_LIMIT = 56 * 1024 * 1024


def _pack_w_in(w):
    zeros = jnp.zeros((w.shape[0], D_IN_PAD - (OFF_SMALL + 96)), w.dtype)
    packed = jnp.concatenate(
        [w[:, :_SRC_SMALL], w[:, _SRC_AFTER_DT:_SRC_IXK], w[:, _SRC_IXK:], w[:, _SRC_SMALL:_SRC_AFTER_DT], zeros],
        axis=1)
    return packed.astype(jnp.bfloat16)


def _inproj_kernel(x_ref, nw_ref, w_ref, z_ref, hn_ref):
    @pl.when(pl.program_id(1) == 0)
    def _():
        xf = x_ref[...]
        ms = jnp.mean(xf * xf, axis=-1, keepdims=True)
        hn_ref[...] = (xf * lax.rsqrt(ms + EPS) * nw_ref[...]).astype(jnp.bfloat16)

    z_ref[...] = jnp.dot(hn_ref[...], w_ref[...], preferred_element_type=jnp.float32)


def _inproj(x2d, norm_w, w_packed, tm, tn=1024):
    m = x2d.shape[0]
    return pl.pallas_call(
        _inproj_kernel,
        grid=(m // tm, D_IN_PAD // tn),
        in_specs=[pl.BlockSpec((tm, D_MODEL), lambda i, j: (i, 0)),
                  pl.BlockSpec((1, D_MODEL), lambda i, j: (0, 0)),
                  pl.BlockSpec((D_MODEL, tn), lambda i, j: (0, j))],
        out_specs=pl.BlockSpec((tm, tn), lambda i, j: (i, j)),
        out_shape=jax.ShapeDtypeStruct((m, D_IN_PAD), jnp.float32),
        scratch_shapes=[pltpu.VMEM((tm, D_MODEL), jnp.bfloat16)],
        compiler_params=pltpu.CompilerParams(dimension_semantics=("arbitrary", "arbitrary"),
                                             vmem_limit_bytes=V7X_VMEM_LIMIT),
        name="inproj",
    )(x2d, norm_w.reshape(1, D_MODEL), w_packed)


def _outproj_kernel(m_ref, w_ref, x_ref, y_ref):
    y_ref[...] = x_ref[...] + jnp.dot(m_ref[...], w_ref[...], preferred_element_type=jnp.float32)


def _outproj(mixed, w_out_bf16, x2d, tm, tn=1024):
    m = x2d.shape[0]
    return pl.pallas_call(
        _outproj_kernel,
        grid=(m // tm, D_MODEL // tn),
        in_specs=[pl.BlockSpec((tm, D_MODEL), lambda i, j: (i, 0)),
                  pl.BlockSpec((D_MODEL, tn), lambda i, j: (0, j)),
                  pl.BlockSpec((tm, tn), lambda i, j: (i, j))],
        out_specs=pl.BlockSpec((tm, tn), lambda i, j: (i, j)),
        out_shape=jax.ShapeDtypeStruct((m, D_MODEL), jnp.float32),
        compiler_params=pltpu.CompilerParams(dimension_semantics=("arbitrary", "arbitrary"),
                                             vmem_limit_bytes=V7X_VMEM_LIMIT),
        name="outproj",
    )(mixed, w_out_bf16, x2d)


def _rms_norm(x, w):
    xf = x.astype(jnp.float32)
    xf = xf * lax.rsqrt(jnp.mean(xf * xf, axis=-1, keepdims=True) + EPS)
    return (xf * w.astype(jnp.float32)).astype(x.dtype)


def _to_chunks(a, c):
    b, l = a.shape[0], a.shape[1]
    return jnp.moveaxis(a.reshape((b, l // c, c) + a.shape[2:]), 1, 0)


def _from_chunks(a):
    a = jnp.moveaxis(a, 0, 1)
    return a.reshape((a.shape[0], a.shape[1] * a.shape[2]) + a.shape[3:])


def _hgrn2_recurrence(q, k, v, log_f, s0, chunk):
    mask = jnp.tril(jnp.ones((chunk, chunk), dtype=bool))

    def step(s, inp):
        qc, kc, vc, gc = inp
        g = jnp.cumsum(gc, axis=1)
        diff = g[:, :, None] - g[:, None, :]
        decay = jnp.exp(jnp.where(mask[None, :, :, None, None], diff, MASK_VALUE))
        att = jnp.einsum('bthk,bshk,btshk->bhts', qc, kc, decay)
        o = jnp.einsum('bhts,bshv->bthv', att, vc) + jnp.einsum('bthk,bhkv->bthv', qc * jnp.exp(g), s)
        g_last = g[:, -1]
        s = jnp.exp(g_last)[..., None] * s + jnp.einsum('bshk,bshv->bhkv', kc * jnp.exp(g_last[:, None] - g), vc)
        return s, o

    s, o = lax.scan(step, s0, (_to_chunks(q, chunk), _to_chunks(k, chunk), _to_chunks(v, chunk),
                               _to_chunks(log_f, chunk)))
    return _from_chunks(o), s


def _ssd_recurrence(x, dt, a, bm, cm, s0, chunk):
    b, L, H, P = x.shape
    G, N = bm.shape[2], bm.shape[3]
    hpg = H // G
    xg = (x * dt[..., None]).reshape(b, L, G, hpg, P)
    la = (dt * a).reshape(b, L, G, hpg)
    mask = jnp.tril(jnp.ones((chunk, chunk), dtype=bool))

    def step(s, inp):
        xc, lac, bc, cc = inp
        cum = jnp.cumsum(lac, axis=1)
        diff = cum[:, :, None] - cum[:, None, :]
        decay = jnp.exp(jnp.where(mask[None, :, :, None, None], diff, MASK_VALUE))
        cb = jnp.einsum('btgn,bsgn->btsg', cc, bc)
        y = jnp.einsum('btsg,btsgh,bsghp->btghp', cb, decay, xc)
        y = y + jnp.einsum('btgn,bghpn->btghp', cc, s) * jnp.exp(cum)[..., None]
        last = cum[:, -1]
        s = jnp.exp(last)[..., None, None] * s + jnp.einsum(
            'bsghp,bsgn->bghpn', xc * jnp.exp(last[:, None] - cum)[..., None], bc)
        return s, y

    s, y = lax.scan(step, s0.reshape(b, G, hpg, P, N),
                    (_to_chunks(xg, chunk), _to_chunks(la, chunk), _to_chunks(bm, chunk), _to_chunks(cm, chunk)))
    return _from_chunks(y).reshape(b, L, H, P), s.reshape(b, H, P, N)


def _causal_dwconv(x, buf, w, bias):
    xp = jnp.concatenate([buf, x], axis=1)
    L = x.shape[1]
    y = bias
    for j in range(SSM_CONV):
        y = y + xp[:, j:j + L] * w[j]
    return y, xp[:, xp.shape[1] - (SSM_CONV - 1):]


def _s5_recurrence(u, a_re, a_im, log_dt, b_re, b_im, c_re, c_im, d, x0_re, x0_im):
    f32 = jnp.float32
    dt = jnp.exp(log_dt.astype(f32))[:, None]
    mag = jnp.exp(a_re * dt)
    ab_re, ab_im = mag * jnp.cos(a_im * dt), mag * jnp.sin(a_im * dt)
    den = a_re * a_re + a_im * a_im
    nr = ab_re - 1.0
    coef_re = (nr * a_re + ab_im * a_im) / den
    coef_im = (ab_im * a_re - nr * a_im) / den
    bu_re = jnp.einsum('blgc,gpc->blgp', u, b_re)
    bu_im = jnp.einsum('blgc,gpc->blgp', u, b_im)
    bb_re = coef_re * bu_re - coef_im * bu_im
    bb_im = coef_re * bu_im + coef_im * bu_re
    bb_re = bb_re.at[:, 0].add(ab_re * x0_re - ab_im * x0_im)
    bb_im = bb_im.at[:, 0].add(ab_re * x0_im + ab_im * x0_re)
    el_re = jnp.broadcast_to(ab_re, bb_re.shape)
    el_im = jnp.broadcast_to(ab_im, bb_im.shape)

    def combine(e1, e2):
        a1r, a1i, b1r, b1i = e1
        a2r, a2i, b2r, b2i = e2
        return (a1r * a2r - a1i * a2i, a1r * a2i + a1i * a2r,
                a2r * b1r - a2i * b1i + b2r, a2r * b1i + a2i * b1r + b2i)

    _, _, xr, xi = lax.associative_scan(combine, (el_re, el_im, bb_re, bb_im), axis=1)
    y = (jnp.einsum('blgp,gcp->blgc', xr, c_re) - jnp.einsum('blgp,gcp->blgc', xi, c_im) + d * u)
    return y, xr[:, -1], xi[:, -1]


def _t5_bucket(dist):
    exact = REL_BUCKETS // 2
    d = dist.astype(jnp.float32)
    large = exact + jnp.log(jnp.maximum(d, 1.0) / exact) / math.log(REL_MAX_DIST / exact) * (REL_BUCKETS - exact)
    large = jnp.minimum(jnp.maximum(large, 0.0).astype(jnp.int32), REL_BUCKETS - 1)
    return jnp.where(dist < exact, dist, large)


def _dsa_attend(q, qi, wi, k, v, ki, q_pos, k_pos, rel_bias, topk):
    f32 = jnp.float32
    b, t, h, dh = q.shape
    n_kv = k.shape[2]
    dots = jnp.einsum('bthd,bsd->bths', qi, ki) * (IDX_DIM ** -0.5)
    score = jnp.einsum('bths,bth->bts', jax.nn.relu(dots), wi) * (IDX_HEADS ** -0.5)
    admissible = k_pos[None, :] <= q_pos[:, None]
    score = jnp.where(admissible[None], score, MASK_VALUE)
    _, idx = lax.top_k(score, topk)
    sel_pos = k_pos[idx]
    dist = q_pos[None, :, None] - sel_pos
    valid = dist >= 0
    take = jax.vmap(lambda arr, ii: arr[ii])
    k_sel = take(k, idx)
    v_sel = take(v, idx)
    qg = q.reshape(b, t, n_kv, h // n_kv, dh)
    logits = jnp.einsum('btkgd,btnkd->btkgn', qg, k_sel) * (dh ** -0.5)
    bias = rel_bias[_t5_bucket(jnp.maximum(dist, 0))]
    bias = jnp.moveaxis(bias, 2, -1).reshape(b, t, n_kv, h // n_kv, topk)
    logits = jnp.where(valid[:, :, None, None, :], logits + bias, MASK_VALUE)
    p = jax.nn.softmax(logits, axis=-1)
    out = jnp.einsum('btkgn,btnkd->btkgd', p, v_sel)
    return out.reshape(b, t, h * dh)


def _prompt_attention(q, k, v, qi, ki, wi, rel_bias):
    L = q.shape[1]
    pos = jnp.arange(L, dtype=jnp.int32)
    topk = min(TOPK_MAX, L // 4)
    blk = min(Q_BLOCK, L)

    def one(args):
        qb, qib, wib, pb = args
        return _dsa_attend(qb, qib, wib, k, v, ki, pb, pos, rel_bias, topk)

    out = lax.map(one, (_to_chunks(q, blk), _to_chunks(qi, blk), _to_chunks(wi, blk), pos.reshape(L // blk, blk)))
    return _from_chunks(out)


def _sample_attention(q, k, v, qi, ki, wi, rel_bias, ck, cv, cik, page_table):
    nb, t = q.shape[0], q.shape[1]
    past = page_table.shape[1] * PAGE_SIZE

    def gather(c, new):
        rows = c[page_table].reshape((nb, past) + c.shape[2:])
        return jnp.concatenate([rows, new], axis=1)

    k_all, v_all, ki_all = gather(ck, k), gather(cv, v), gather(cik, ki)
    k_pos = jnp.arange(past + t, dtype=jnp.int32)
    q_pos = past + jnp.arange(t, dtype=jnp.int32)
    return _dsa_attend(q, qi, wi, k_all, v_all, ki_all, q_pos, k_pos, rel_bias, min(TOPK_MAX, (past + t) // 4))


def _mixers_jnp(z, lp, lb, hg_s0, ssm_s0, conv_s0, s5_re0, s5_im0, attn_fn):
    f32 = jnp.float32
    b, L, _ = z.shape
    chunk = math.gcd(L, SCAN_CHUNK)

    def seg(off, width):
        return z[:, :, off:off + width]

    hg_q, hg_f, hg_i, hg_g = seg(OFF_HG_Q, BRANCH), seg(OFF_HG_F, BRANCH), seg(OFF_HG_I, BRANCH), seg(OFF_HG_G, BRANCH)
    sm_z, sm_xbc = seg(OFF_SM_Z, BRANCH), seg(OFF_SM_XBC, SSM_CONV_DIM)
    s5_u, s5_g = seg(OFF_S5_U, BRANCH), seg(OFF_S5_G, BRANCH)
    at_q, at_k, at_v, at_g = (seg(OFF_AT_Q, BRANCH), seg(OFF_AT_K, KV_HEADS * ATT_HEAD_DIM),
                              seg(OFF_AT_V, KV_HEADS * ATT_HEAD_DIM), seg(OFF_AT_G, BRANCH))
    ix_q = seg(OFF_IX_Q, IDX_HEADS * IDX_DIM)
    ix_k = seg(OFF_SMALL + SMALL_IXK, IDX_DIM)
    ix_w = seg(OFF_SMALL + SMALL_IXW, IDX_HEADS)
    sm_dt = seg(OFF_SMALL + SMALL_DT, SSM_HEADS)

    lbh = lb.reshape(HG_HEADS, HG_K)
    f_pre = hg_f.reshape(b, L, HG_HEADS, HG_K)
    f_gate = lbh + (1.0 - lbh) * jax.nn.sigmoid(f_pre)
    log_f = jnp.log(jnp.maximum(f_gate, F_FLOOR))
    k_hg = (1.0 - lbh) * jax.nn.sigmoid(-f_pre)
    q_hg = jax.nn.silu(hg_q.reshape(b, L, HG_HEADS, HG_K))
    v_hg = hg_i.reshape(b, L, HG_HEADS, HG_V)
    o_hg, hg_s = _hgrn2_recurrence(q_hg, k_hg, v_hg, log_f, hg_s0, chunk)
    o_hg = _rms_norm(o_hg, lp['hg_norm_w']).reshape(b, L, BRANCH) * jax.nn.silu(hg_g)

    xbc, conv_s = _causal_dwconv(sm_xbc, conv_s0, lp['conv_w'], lp['conv_b'])
    xbc = jax.nn.silu(xbc)
    xs, bm, cm = jnp.split(xbc, [BRANCH, BRANCH + SSM_GROUPS * SSM_STATE], axis=-1)
    xs = xs.reshape(b, L, SSM_HEADS, SSM_HEAD_DIM)
    dt = jax.nn.softplus(sm_dt + lp['dt_bias'])
    a = -jnp.exp(lp['a_log'])
    y_ssm, ssm_s = _ssd_recurrence(xs, dt, a, bm.reshape(b, L, SSM_GROUPS, SSM_STATE),
                                   cm.reshape(b, L, SSM_GROUPS, SSM_STATE), ssm_s0, chunk)
    y_ssm = (y_ssm + lp['ssm_d'][:, None] * xs).reshape(b, L, BRANCH) * jax.nn.silu(sm_z)
    gw = BRANCH // SSM_GROUPS
    y_ssm = _rms_norm(y_ssm.reshape(b, L, SSM_GROUPS, gw), lp['ssm_norm_w'].reshape(SSM_GROUPS, gw)).reshape(b, L, BRANCH)

    y5, s5_re, s5_im = _s5_recurrence(s5_u.reshape(b, L, S5_GROUPS, S5_CH), lp['a_re'], lp['a_im'], lp['log_dt'],
                                      lp['b_re'], lp['b_im'], lp['c_re'], lp['c_im'], lp['s5_d'], s5_re0, s5_im0)
    h5 = jax.nn.gelu(y5.reshape(b, L, BRANCH))
    o5 = h5 * jax.nn.sigmoid(h5 @ lp['glu_w'] + lp['glu_b']) * jax.nn.silu(s5_g)

    q = _rms_norm(at_q.reshape(b, L, ATT_HEADS, ATT_HEAD_DIM), lp['q_norm'])
    k = _rms_norm(at_k.reshape(b, L, KV_HEADS, ATT_HEAD_DIM), lp['k_norm'])
    v = at_v.reshape(b, L, KV_HEADS, ATT_HEAD_DIM)
    qi = ix_q.reshape(b, L, IDX_HEADS, IDX_DIM)
    o_att = attn_fn(q, k, v, qi, ix_k, ix_w) * jax.nn.silu(at_g)
    mixed = jnp.concatenate([o_hg, y_ssm, o5, o_att], axis=-1)
    return mixed, (k, v, ix_k, hg_s, ssm_s, conv_s, s5_re, s5_im)


def _layer(x, lp, lb, w_in_packed, w_out_bf16, states, attn_fn, tm):
    b, L, _ = x.shape
    x2d = x.reshape(b * L, D_MODEL)
    z = _inproj(x2d, lp['norm_w'], w_in_packed, tm).reshape(b, L, D_IN_PAD)
    mixed, st = _mixers_jnp(z, lp, lb, *states, attn_fn)
    y = _outproj(mixed.reshape(b * L, D_MODEL).astype(jnp.bfloat16), w_out_bf16, x2d, tm)
    return y.reshape(b, L, D_MODEL), st


def kernel(x_prompt, x_sample, cache_k, cache_v, cache_idx_k, state_hgrn, state_ssm, state_conv, state_s5_re, state_s5_im, page_table, norm_w, w_in, w_out, hg_lb_logits, hg_norm_w, ssm_conv_w, ssm_conv_b, ssm_dt_bias, ssm_a_log, ssm_d, ssm_norm_w, s5_a_re, s5_a_im, s5_log_dt, s5_b_re, s5_b_im, s5_c_re, s5_c_im, s5_d, s5_glu_w, s5_glu_b, att_q_norm, att_k_norm, rel_bias):
    f32 = jnp.float32
    sm = jax.nn.softmax(hg_lb_logits.astype(f32), axis=0)
    lower = jnp.cumsum(sm, axis=0) - sm[0]
    bp = x_prompt.shape[0]
    yp, ys = x_prompt, x_sample
    new_p = [[] for _ in range(8)]
    new_s = [[] for _ in range(8)]
    for l in range(DEPTH):
        lp = {'norm_w': norm_w[l], 'hg_norm_w': hg_norm_w[l],
              'conv_w': ssm_conv_w[l], 'conv_b': ssm_conv_b[l], 'dt_bias': ssm_dt_bias[l],
              'a_log': ssm_a_log[l], 'ssm_d': ssm_d[l], 'ssm_norm_w': ssm_norm_w[l],
              'a_re': s5_a_re[l], 'a_im': s5_a_im[l], 'log_dt': s5_log_dt[l],
              'b_re': s5_b_re[l], 'b_im': s5_b_im[l], 'c_re': s5_c_re[l], 'c_im': s5_c_im[l],
              's5_d': s5_d[l], 'glu_w': s5_glu_w[l], 'glu_b': s5_glu_b[l],
              'q_norm': att_q_norm[l], 'k_norm': att_k_norm[l]}
        w_in_packed = _pack_w_in(w_in[l])
        w_out_bf16 = w_out[l].astype(jnp.bfloat16)
        zero_states = (jnp.zeros((bp, HG_HEADS, HG_K, HG_V), f32),
                       jnp.zeros((bp, SSM_HEADS, SSM_HEAD_DIM, SSM_STATE), f32),
                       jnp.zeros((bp, SSM_CONV - 1, SSM_CONV_DIM), f32),
                       jnp.zeros((bp, S5_GROUPS, S5_STATE), f32),
                       jnp.zeros((bp, S5_GROUPS, S5_STATE), f32))
        yp, st_p = _layer(yp, lp, lower[l], w_in_packed, w_out_bf16, zero_states,
                          lambda q, k, v, qi, ki, wi: _prompt_attention(q, k, v, qi, ki, wi, rel_bias), 512)
        samp_states = (state_hgrn[l], state_ssm[l], state_conv[l], state_s5_re[l], state_s5_im[l])
        ys, st_s = _layer(ys, lp, lower[l], w_in_packed, w_out_bf16, samp_states,
                          lambda q, k, v, qi, ki, wi: _sample_attention(q, k, v, qi, ki, wi, rel_bias, cache_k[l],
                                                                        cache_v[l], cache_idx_k[l], page_table), 32)
        for i in range(8):
            new_p[i].append(st_p[i])
            new_s[i].append(st_s[i])
    pk, pv, pik, phg, pssm, pconv, ps5r, ps5i = [jnp.stack(a) for a in new_p]
    sk, sv, sik, shg, sssm, sconv, ss5r, ss5i = [jnp.stack(a) for a in new_s]
    return (yp, ys, pk, pv, pik, phg, pssm, pconv, ps5r, ps5i, sk, sv, sik, shg, sssm, sconv, ss5r, ss5i)
```

```python
import functools
import math

import jax
import jax.numpy as jnp
from jax import lax
from jax.experimental import pallas as pl
from jax.experimental.pallas import tpu as pltpu

D_MODEL = 4096
DEPTH = 2
BRANCH = D_MODEL // 4
HG_HEADS = 8
SSM_HEADS = 16
SSM_GROUPS = 4
SSM_STATE = 128
SSM_CONV = 4
SSM_CONV_DIM = BRANCH + 2 * SSM_GROUPS * SSM_STATE
SSM_HPG = SSM_HEADS // SSM_GROUPS
SSM_P = 64
S5_GROUPS = 64
S5_STATE = 64
S5_BLK_CH = 128
S5_BLK_ST = 512
ATT_D = 128
ATT_HEADS = 8
KV_HEADS = 2
IDX_HEADS = 16
IDX_D = 64
TOPK_MAX = 256
Q_BLK = 128
PAGE = 128
REL_BUCKETS = 32
REL_MAX_DIST = 1024
EPS = 1e-6
MASK_VALUE = -1e30
F_FLOOR = 1e-30
INT_MIN = -2 ** 31

LANES = 128
SUBLANES = 8
V7X_VMEM_LIMIT = 56 * 1024 * 1024

_SRC_DT, _SRC_S5U, _SRC_ATK, _SRC_ATG, _SRC_IXK, D_IN_SRC = 7168, 7184, 10256, 10768, 12816, 12896

OFF_HG_Q, OFF_HG_F, OFF_HG_I, OFF_HG_G = 0, 1024, 2048, 3072
OFF_SM_Z, OFF_SM_XBC = 4096, 5120
OFF_S5_U, OFF_S5_G = 7168, 8192
OFF_AT_Q, OFF_AT_G, OFF_IX_Q = 9216, 10240, 11264
OFF_AT_KV = 12288
OFF_SMALL = 12800
SMALL_IXW, SMALL_DT = 64, 80
D_IN_PAD = 13312

N_BIAS_TILES = 9
NPG = 8
T_PAD = 8
CHUNK_KEYS = NPG * PAGE
SAMPLE_ROWS = 16

_TN = (((0,), (0,)), ((), ()))
_NT = (((1,), (1,)), ((), ()))


def _params(n_axes):
    return pltpu.CompilerParams(dimension_semantics=("arbitrary",) * n_axes, vmem_limit_bytes=V7X_VMEM_LIMIT)


def _bf(x):
    return x.astype(jnp.bfloat16)


def _dot(a, b, dims=None):
    if dims is None:
        return jnp.dot(_bf(a), _bf(b), preferred_element_type=jnp.float32)
    return lax.dot_general(_bf(a), _bf(b), dims, preferred_element_type=jnp.float32)


def _sigmoid(x):
    return 1.0 / (1.0 + jnp.exp(-x))


def _silu(x):
    return x * _sigmoid(x)


def _softplus(x):
    return jnp.maximum(x, 0.0) + jnp.log(1.0 + jnp.exp(-jnp.abs(x)))


def _gelu_tanh(x):
    return 0.5 * x * (1.0 + jnp.tanh(math.sqrt(2.0 / math.pi) * (x + 0.044715 * (x * x * x))))


def _cumsum_rows(x):
    n = x.shape[0]
    row = lax.broadcasted_iota(jnp.int32, x.shape, 0)
    sh = 1
    while sh < n:
        x = x + jnp.where(row >= sh, pltpu.roll(x, sh, axis=0), 0.0)
        sh *= 2
    return x


def _split3(x):
    hi = x.astype(jnp.bfloat16)
    r = x - hi.astype(jnp.float32)
    mid = r.astype(jnp.bfloat16)
    lo = (r - mid.astype(jnp.float32)).astype(jnp.bfloat16)
    return hi, mid, lo


def _select_lanes_as_rows(sel, x):
    out = None
    for part in _split3(x):
        t = lax.dot_general(sel, part, _NT, preferred_element_type=jnp.float32)
        out = t if out is None else out + t
    return out


def _cmul(ar, ai, br, bi):
    return ar * br - ai * bi, ar * bi + ai * br


def _pack_w_in(w):
    zeros = jnp.zeros((w.shape[0], D_IN_PAD - (OFF_SMALL + 96)), w.dtype)
    packed = jnp.concatenate(
        [w[:, :_SRC_DT],
         w[:, _SRC_S5U:_SRC_ATK],
         w[:, _SRC_ATG:_SRC_IXK],
         w[:, _SRC_ATK:_SRC_ATG],
         w[:, _SRC_IXK:],
         w[:, _SRC_DT:_SRC_S5U],
         zeros], axis=1)
    return packed.astype(jnp.bfloat16)


def _inproj_kernel(x_ref, nw_ref, w_ref, z_ref, hn_ref):
    @pl.when(pl.program_id(1) == 0)
    def _():
        xf = x_ref[...]
        ms = jnp.mean(xf * xf, axis=-1, keepdims=True)
        hn_ref[...] = (xf * lax.rsqrt(ms + EPS) * nw_ref[...]).astype(jnp.bfloat16)

    z_ref[...] = jnp.dot(hn_ref[...], w_ref[...], preferred_element_type=jnp.float32)


def _inproj(x2d, norm_w, w_packed, tm, tn=1024):
    m = x2d.shape[0]
    return pl.pallas_call(
        _inproj_kernel,
        grid=(m // tm, D_IN_PAD // tn),
        in_specs=[pl.BlockSpec((tm, D_MODEL), lambda i, j: (i, 0)),
                  pl.BlockSpec((1, D_MODEL), lambda i, j: (0, 0)),
                  pl.BlockSpec((D_MODEL, tn), lambda i, j: (0, j))],
        out_specs=pl.BlockSpec((tm, tn), lambda i, j: (i, j)),
        out_shape=jax.ShapeDtypeStruct((m, D_IN_PAD), jnp.float32),
        scratch_shapes=[pltpu.VMEM((tm, D_MODEL), jnp.bfloat16)],
        compiler_params=_params(2),
        name="inproj",
    )(x2d, norm_w.reshape(1, D_MODEL), w_packed)


def _outproj_kernel(m0_ref, m1_ref, m2_ref, m3_ref, w_ref, x_ref, y_ref):
    acc = x_ref[...]
    for i, m_ref in enumerate((m0_ref, m1_ref, m2_ref, m3_ref)):
        acc = acc + jnp.dot(m_ref[...], w_ref[pl.ds(BRANCH * i, BRANCH), :], preferred_element_type=jnp.float32)
    y_ref[...] = acc


def _outproj(mixed4, w_out_bf16, x2d, tm, tn=1024):
    m = x2d.shape[0]
    mspec = pl.BlockSpec((tm, BRANCH), lambda i, j: (i, 0))
    return pl.pallas_call(
        _outproj_kernel,
        grid=(m // tm, D_MODEL // tn),
        in_specs=[mspec, mspec, mspec, mspec,
                  pl.BlockSpec((D_MODEL, tn), lambda i, j: (0, j)),
                  pl.BlockSpec((tm, tn), lambda i, j: (i, j))],
        out_specs=pl.BlockSpec((tm, tn), lambda i, j: (i, j)),
        out_shape=jax.ShapeDtypeStruct((m, D_MODEL), jnp.float32),
        compiler_params=_params(2),
        name="outproj",
    )(*mixed4, w_out_bf16, x2d)


def _hgrn_kernel(q_ref, f_ref, i_ref, g_ref, lb_ref, nw_ref, s0_ref, o_ref, s_out_ref, st_ref, *, chunk, l_valid):
    L = q_ref.shape[1]
    C = chunk
    nblk = C // SUBLANES
    st_ref[...] = s0_ref[0, 0].T
    lb = lb_ref[0]
    nw = nw_ref[...]

    def body(c, carry):
        t0 = pl.multiple_of(c * C, C)
        fp = f_ref[0, pl.ds(t0, C), :]
        qp = q_ref[0, pl.ds(t0, C), :]
        v = i_ref[0, pl.ds(t0, C), :]
        gp = g_ref[0, pl.ds(t0, C), :]
        fg = lb + (1.0 - lb) * _sigmoid(fp)
        logf = jnp.log(jnp.maximum(fg, F_FLOOR))
        kk = (1.0 - lb) * _sigmoid(-fp)
        if l_valid < L:
            rows = lax.broadcasted_iota(jnp.int32, (C, LANES), 0) + t0
            logf = jnp.where(rows < l_valid, logf, 0.0)
            kk = jnp.where(rows < l_valid, kk, 0.0)
        qq = _silu(qp)
        G = _cumsum_rows(logf)
        st = st_ref[...]
        o_inter = _dot(qq * jnp.exp(G), st, _NT)
        acc = [None] * nblk
        qb = [qq[SUBLANES * tb:SUBLANES * (tb + 1)] for tb in range(nblk)]
        Gb = [G[SUBLANES * tb:SUBLANES * (tb + 1)] for tb in range(nblk)]
        rowi = lax.broadcasted_iota(jnp.int32, (SUBLANES, LANES), 0)
        for s in range(C):
            sb = s // SUBLANES
            gs = G[s:s + 1]
            ks = kk[s:s + 1]
            vs = v[s:s + 1]
            for tb in range(sb, nblk):
                d = Gb[tb] - gs
                if tb == sb:
                    d = jnp.where(rowi >= s - SUBLANES * sb, d, MASK_VALUE)
                w = jnp.sum(qb[tb] * ks * jnp.exp(d), axis=-1, keepdims=True)
                contrib = w * vs
                acc[tb] = contrib if acc[tb] is None else acc[tb] + contrib
        o = o_inter + jnp.concatenate(acc, axis=0)
        g_last = G[C - 1:C]
        kd = kk * jnp.exp(g_last - G)
        st_ref[...] = jnp.exp(g_last) * st + _dot(v, kd, _TN)
        ms = jnp.mean(o * o, axis=-1, keepdims=True)
        on = o * lax.rsqrt(ms + EPS) * nw
        o_ref[0, pl.ds(t0, C), :] = (on * _silu(gp)).astype(o_ref.dtype)
        return carry

    lax.fori_loop(0, L // C, body, 0)
    s_out_ref[0, 0] = st_ref[...].T


def _hgrn_call(z, lb, hg_norm_w, s0, *, chunk, l_valid):
    B, L, _ = z.shape
    zspec = lambda off: pl.BlockSpec((1, L, LANES), lambda b, h: (b, 0, off // LANES + h))
    return pl.pallas_call(
        functools.partial(_hgrn_kernel, chunk=chunk, l_valid=l_valid),
        grid=(B, HG_HEADS),
        in_specs=[zspec(OFF_HG_Q), zspec(OFF_HG_F), zspec(OFF_HG_I), zspec(OFF_HG_G),
                  pl.BlockSpec((1, 1, LANES), lambda b, h: (h, 0, 0)),
                  pl.BlockSpec((1, LANES), lambda b, h: (0, 0)),
                  pl.BlockSpec((1, 1, LANES, LANES), lambda b, h: (b, h, 0, 0))],
        out_specs=[pl.BlockSpec((1, L, LANES), lambda b, h: (b, 0, h)),
                   pl.BlockSpec((1, 1, LANES, LANES), lambda b, h: (b, h, 0, 0))],
        out_shape=[jax.ShapeDtypeStruct((B, L, BRANCH), jnp.bfloat16),
                   jax.ShapeDtypeStruct((B, HG_HEADS, LANES, LANES), jnp.float32)],
        scratch_shapes=[pltpu.VMEM((LANES, LANES), jnp.float32)],
        compiler_params=_params(2),
        name="hgrn",
    )(z, z, z, z, lb.reshape(HG_HEADS, 1, LANES), hg_norm_w.reshape(1, LANES), s0)


def _ssd_kernel(xs_ref, b_ref, c_ref, zg_ref, sm_ref, wx_ref, wb_ref, wc_ref, bx_ref, bb_ref, bc_ref,
                cx_ref, cb_ref, cc_ref, dtb_ref, alog_ref, dskip_ref, nw_ref, s0_ref,
                y_ref, s_out_ref, ox_ref, ob_ref, oc_ref,
                ax_ref, ab_ref, ac_ref, hx_ref, hb_ref, hc_ref, sp_ref, *, chunk, l_valid, conv_rows):
    L = xs_ref.shape[1]
    C = chunk
    R = conv_rows
    g = pl.program_id(1)

    def conv(src_ref, head_ref, cst_ref, w_ref, bias_ref, act_ref, out_state_ref):
        head_ref[pl.ds(0, SUBLANES), :] = cst_ref[0]
        head_ref[pl.ds(SUBLANES, R), :] = src_ref[0, pl.ds(0, R), :]
        w = w_ref[...]
        for r0 in range(0, L, R):
            acc = bias_ref[...]
            for j in range(SSM_CONV):
                if r0 == 0:
                    xj = head_ref[pl.ds(SUBLANES - j, R), :]
                else:
                    xj = src_ref[0, pl.ds(r0 - j, R), :]
                acc = acc + xj * w[SSM_CONV - 1 - j:SSM_CONV - j]
            act_ref[pl.ds(r0, R), :] = _silu(acc)
        if l_valid >= SUBLANES:
            out_state_ref[0] = src_ref[0, pl.ds(l_valid - SUBLANES, SUBLANES), :]
        else:
            out_state_ref[0] = head_ref[pl.ds(l_valid, SUBLANES), :]

    conv(xs_ref, hx_ref, cx_ref, wx_ref, bx_ref, ax_ref, ox_ref)
    conv(b_ref, hb_ref, cb_ref, wb_ref, bb_ref, ab_ref, ob_ref)
    conv(c_ref, hc_ref, cc_ref, wc_ref, bc_ref, ac_ref, oc_ref)

    for p in range(SSM_HPG // 2):
        sp_ref[p] = s0_ref[0, 2 * p:2 * p + 2].reshape(2 * SSM_P, LANES)

    lane1 = lax.broadcasted_iota(jnp.int32, (1, LANES), 1)
    lane8 = lax.broadcasted_iota(jnp.int32, (SUBLANES, LANES), 1)
    row8 = lax.broadcasted_iota(jnp.int32, (SUBLANES, LANES), 0)
    lane0 = SMALL_DT + SSM_HPG * g
    sel = jnp.where((lane8 == lane0 + row8) & (row8 < SSM_HPG), 1.0, 0.0).astype(jnp.bfloat16)
    a_all = -jnp.exp(alog_ref[...])
    dskip = [jnp.sum(jnp.where(lane1 == lane0 + j, dskip_ref[...], 0.0), axis=-1, keepdims=True)
             for j in range(SSM_HPG)]
    lane_c = lax.broadcasted_iota(jnp.int32, (C, LANES), 1)
    first_half = lane_c < SSM_P
    row_p = lax.broadcasted_iota(jnp.int32, (2 * SSM_P, LANES), 0) < SSM_P
    tril = lax.broadcasted_iota(jnp.int32, (C, C), 0) >= lax.broadcasted_iota(jnp.int32, (C, C), 1)
    nw = nw_ref[...]

    def body(c, carry):
        t0 = pl.multiple_of(c * C, C)
        sm = sm_ref[0, pl.ds(t0, C), :]
        dt_all = _softplus(sm + dtb_ref[...])
        if l_valid < L:
            rows = lax.broadcasted_iota(jnp.int32, (C, LANES), 0) + t0
            dt_all = jnp.where(rows < l_valid, dt_all, 0.0)
        cum = _cumsum_rows(dt_all * a_all)
        cum_rows = _select_lanes_as_rows(sel, cum)
        col = [jnp.sum(jnp.where(lane_c == lane0 + j, cum, 0.0), axis=-1, keepdims=True) for j in range(SSM_HPG)]
        dtc = [jnp.sum(jnp.where(lane_c == lane0 + j, dt_all, 0.0), axis=-1, keepdims=True) for j in range(SSM_HPG)]
        bact = ab_ref[pl.ds(t0, C), :]
        cact = ac_ref[pl.ds(t0, C), :]
        cb = _dot(cact, bact, _NT)
        ys = []
        for p in range(SSM_HPG // 2):
            ja, jb = 2 * p, 2 * p + 1
            xs = ax_ref[pl.ds(t0, C), pl.ds(2 * SSM_P * p, 2 * SSM_P)]
            xdt = xs * jnp.where(first_half, dtc[ja], dtc[jb])
            dec_a = jnp.exp(jnp.where(tril, col[ja] - cum_rows[ja:ja + 1], MASK_VALUE))
            dec_b = jnp.exp(jnp.where(tril, col[jb] - cum_rows[jb:jb + 1], MASK_VALUE))
            y = jnp.where(first_half, _dot(cb * dec_a, xdt), _dot(cb * dec_b, xdt))
            sp = sp_ref[p]
            y = y + _dot(cact, sp, _NT) * jnp.where(first_half, jnp.exp(col[ja]), jnp.exp(col[jb]))
            y = y + jnp.where(first_half, dskip[ja], dskip[jb]) * xs
            last_a, last_b = col[ja][C - 1:C], col[jb][C - 1:C]
            xw = xdt * jnp.where(first_half, jnp.exp(last_a - col[ja]), jnp.exp(last_b - col[jb]))
            sp_ref[p] = jnp.where(row_p, jnp.exp(last_a), jnp.exp(last_b)) * sp + _dot(xw, bact, _TN)
            zg = zg_ref[0, pl.ds(t0, C), pl.ds(2 * SSM_P * p, 2 * SSM_P)]
            ys.append(y * _silu(zg))
        ms = sum(jnp.sum(y * y, axis=-1, keepdims=True) for y in ys) * (1.0 / (SSM_HPG * SSM_P))
        inv = lax.rsqrt(ms + EPS)
        for p in range(SSM_HPG // 2):
            y_ref[0, pl.ds(t0, C), pl.ds(2 * SSM_P * p, 2 * SSM_P)] = (
                ys[p] * inv * nw[:, 2 * SSM_P * p:2 * SSM_P * (p + 1)]).astype(y_ref.dtype)
        return carry

    lax.fori_loop(0, L // C, body, 0)
    for p in range(SSM_HPG // 2):
        s_out_ref[0, 2 * p:2 * p + 2] = sp_ref[p].reshape(2, SSM_P, LANES)


def _ssd_call(z, conv_w, conv_b, conv_state8, dtb_pad, alog_pad, dskip_pad, norm_w, s0, *, chunk, l_valid):
    B, L, _ = z.shape
    assert l_valid >= SSM_CONV - 1
    xw = SSM_HPG * SSM_P
    conv_rows = min(L, 256)
    ox, ob = OFF_SM_XBC // xw, (OFF_SM_XBC + BRANCH) // LANES
    oc = (OFF_SM_XBC + BRANCH + SSM_GROUPS * SSM_STATE) // LANES
    wb0, wc0 = BRANCH // LANES, (BRANCH + SSM_GROUPS * SSM_STATE) // LANES
    in_specs = [
        pl.BlockSpec((1, L, xw), lambda b, g: (b, 0, ox + g)),
        pl.BlockSpec((1, L, LANES), lambda b, g: (b, 0, ob + g)),
        pl.BlockSpec((1, L, LANES), lambda b, g: (b, 0, oc + g)),
        pl.BlockSpec((1, L, xw), lambda b, g: (b, 0, OFF_SM_Z // xw + g)),
        pl.BlockSpec((1, L, LANES), lambda b, g: (b, 0, OFF_SMALL // LANES)),
        pl.BlockSpec((SSM_CONV, xw), lambda b, g: (0, g)),
        pl.BlockSpec((SSM_CONV, LANES), lambda b, g: (0, wb0 + g)),
        pl.BlockSpec((SSM_CONV, LANES), lambda b, g: (0, wc0 + g)),
        pl.BlockSpec((1, xw), lambda b, g: (0, g)),
        pl.BlockSpec((1, LANES), lambda b, g: (0, wb0 + g)),
        pl.BlockSpec((1, LANES), lambda b, g: (0, wc0 + g)),
        pl.BlockSpec((1, SUBLANES, xw), lambda b, g: (b, 0, g)),
        pl.BlockSpec((1, SUBLANES, LANES), lambda b, g: (b, 0, wb0 + g)),
        pl.BlockSpec((1, SUBLANES, LANES), lambda b, g: (b, 0, wc0 + g)),
        pl.BlockSpec((1, LANES), lambda b, g: (0, 0)),
        pl.BlockSpec((1, LANES), lambda b, g: (0, 0)),
        pl.BlockSpec((1, LANES), lambda b, g: (0, 0)),
        pl.BlockSpec((1, xw), lambda b, g: (0, g)),
        pl.BlockSpec((1, SSM_HPG, SSM_P, LANES), lambda b, g: (b, g, 0, 0)),
    ]
    out_specs = [
        pl.BlockSpec((1, L, xw), lambda b, g: (b, 0, g)),
        pl.BlockSpec((1, SSM_HPG, SSM_P, LANES), lambda b, g: (b, g, 0, 0)),
        pl.BlockSpec((1, SUBLANES, xw), lambda b, g: (b, 0, g)),
        pl.BlockSpec((1, SUBLANES, LANES), lambda b, g: (b, 0, g)),
        pl.BlockSpec((1, SUBLANES, LANES), lambda b, g: (b, 0, g)),
    ]
    f32 = jnp.float32
    out_shape = [
        jax.ShapeDtypeStruct((B, L, BRANCH), jnp.bfloat16),
        jax.ShapeDtypeStruct((B, SSM_HEADS, SSM_P, LANES), f32),
        jax.ShapeDtypeStruct((B, SUBLANES, BRANCH), f32),
        jax.ShapeDtypeStruct((B, SUBLANES, SSM_GROUPS * SSM_STATE), f32),
        jax.ShapeDtypeStruct((B, SUBLANES, SSM_GROUPS * SSM_STATE), f32),
    ]
    scratch = [pltpu.VMEM((L, xw), f32), pltpu.VMEM((L, LANES), f32), pltpu.VMEM((L, LANES), f32),
               pltpu.VMEM((SUBLANES + conv_rows, xw), f32), pltpu.VMEM((SUBLANES + conv_rows, LANES), f32),
               pltpu.VMEM((SUBLANES + conv_rows, LANES), f32),
               pltpu.VMEM((SSM_HPG // 2, 2 * SSM_P, LANES), f32)]
    cbias = conv_b.reshape(1, -1)
    return pl.pallas_call(
        functools.partial(_ssd_kernel, chunk=chunk, l_valid=l_valid, conv_rows=conv_rows),
        grid=(B, SSM_GROUPS), in_specs=in_specs, out_specs=out_specs, out_shape=out_shape, scratch_shapes=scratch,
        compiler_params=_params(2),
        name="ssd",
    )(z, z, z, z, z, conv_w, conv_w, conv_w, cbias, cbias, cbias, conv_state8, conv_state8, conv_state8,
      dtb_pad, alog_pad, dskip_pad, norm_w.reshape(1, -1), s0)


def _s5_kernel(u_ref, bre_ref, bim_ref, cre_ref, cim_ref, are_ref, aim_ref, ldt_ref, d_ref, x0r_ref, x0i_ref,
               h_ref, xr_out_ref, xi_out_ref, xr_ref, xi_ref, *, nseg, seg_len, row_blk):
    L = u_ref.shape[1]
    a_re, a_im = are_ref[...], aim_ref[...]
    dt = jnp.exp(ldt_ref[...])
    mag = jnp.exp(a_re * dt)
    ab_re, ab_im = mag * jnp.cos(a_im * dt), mag * jnp.sin(a_im * dt)
    den = a_re * a_re + a_im * a_im
    nr = ab_re - 1.0
    coef_re = (nr * a_re + ab_im * a_im) / den
    coef_im = (ab_im * a_re - nr * a_im) / den

    NQ = S5_BLK_ST // LANES
    lq = lambda v, q: v[:, LANES * q:LANES * (q + 1)]
    abr = [lq(ab_re, q) for q in range(NQ)]
    abi = [lq(ab_im, q) for q in range(NQ)]

    for r0 in range(0, L, row_blk):
        u = u_ref[0, pl.ds(r0, row_blk), :]
        bu_re, bu_im = _dot(u, bre_ref[0]), _dot(u, bim_ref[0])
        bb_re = coef_re * bu_re - coef_im * bu_im
        bb_im = coef_re * bu_im + coef_im * bu_re
        for q in range(NQ):
            xr_ref[q, pl.ds(r0, row_blk), :] = lq(bb_re, q)
            xi_ref[q, pl.ds(r0, row_blk), :] = lq(bb_im, q)

    def rows(i):
        return pl.ds(i, nseg, stride=seg_len) if nseg > 1 else pl.ds(i, 1)

    def scan(i, carry):
        out = []
        for q in range(NQ):
            xr, xi = carry[q]
            pr, pi = _cmul(abr[q], abi[q], xr, xi)
            xr, xi = pr + xr_ref[q, rows(i), :], pi + xi_ref[q, rows(i), :]
            xr_ref[q, rows(i), :] = xr
            xi_ref[q, rows(i), :] = xi
            out.append((xr, xi))
        return tuple(out)

    zero = jnp.zeros((nseg, LANES), jnp.float32)
    ends = lax.fori_loop(0, seg_len, scan, tuple((zero, zero) for _ in range(NQ)))

    pw_r, pw_i = ab_re, ab_im
    acc_r, acc_i = None, None
    n = seg_len
    while n:
        if n & 1:
            acc_r, acc_i = (pw_r, pw_i) if acc_r is None else _cmul(acc_r, acc_i, pw_r, pw_i)
        n >>= 1
        if n:
            pw_r, pw_i = _cmul(pw_r, pw_i, pw_r, pw_i)
    x0r, x0i = x0r_ref[0], x0i_ref[0]
    cs = []
    for q in range(NQ):
        cr, ci = [lq(x0r, q)], [lq(x0i, q)]
        for k in range(nseg - 1):
            pr, pi = _cmul(lq(acc_r, q), lq(acc_i, q), cr[k], ci[k])
            cr.append(pr + ends[q][0][k:k + 1])
            ci.append(pi + ends[q][1][k:k + 1])
        cs.append((jnp.concatenate(cr, axis=0), jnp.concatenate(ci, axis=0)) if nseg > 1 else (cr[0], ci[0]))

    def fix(i, carry):
        out = []
        for q in range(NQ):
            pr, pi = carry[q]
            dr, di = _cmul(pr, pi, cs[q][0], cs[q][1])
            xr_ref[q, rows(i), :] = xr_ref[q, rows(i), :] + dr
            xi_ref[q, rows(i), :] = xi_ref[q, rows(i), :] + di
            out.append(_cmul(pr, pi, abr[q], abi[q]))
        return tuple(out)

    lax.fori_loop(0, seg_len, fix, tuple((abr[q], abi[q]) for q in range(NQ)))

    last = nseg * seg_len - 1
    for q in range(NQ):
        xr_out_ref[0, :, pl.ds(LANES * q, LANES)] = xr_ref[q, pl.ds(last, 1), :]
        xi_out_ref[0, :, pl.ds(LANES * q, LANES)] = xi_ref[q, pl.ds(last, 1), :]
    dskip = d_ref[...]
    for r0 in range(0, L, row_blk):
        u = u_ref[0, pl.ds(r0, row_blk), :]
        xr = jnp.concatenate([xr_ref[q, pl.ds(r0, row_blk), :] for q in range(NQ)], axis=-1)
        xi = jnp.concatenate([xi_ref[q, pl.ds(r0, row_blk), :] for q in range(NQ)], axis=-1)
        y = _dot(xr, cre_ref[0]) - _dot(xi, cim_ref[0]) + dskip * u
        h_ref[0, pl.ds(r0, row_blk), :] = _gelu_tanh(y)


def _s5_call(z, bblk_re, bblk_im, cblk_re, cblk_im, a_re, a_im, log_dt_exp, d_flat, x0_re, x0_im, *, l_valid):
    B, L, _ = z.shape
    nb = bblk_re.shape[0]
    nseg = SUBLANES if l_valid % (SUBLANES * SUBLANES) == 0 else 1
    seg_len = l_valid // nseg
    row_blk = min(L, 256)
    vec = lambda w: pl.BlockSpec((1, w), lambda b, j: (0, j))
    st = pl.BlockSpec((1, 1, S5_BLK_ST), lambda b, j: (b, 0, j))
    return pl.pallas_call(
        functools.partial(_s5_kernel, nseg=nseg, seg_len=seg_len, row_blk=row_blk),
        grid=(B, nb),
        in_specs=[pl.BlockSpec((1, L, S5_BLK_CH), lambda b, j: (b, 0, OFF_S5_U // S5_BLK_CH + j)),
                  pl.BlockSpec((1, S5_BLK_CH, S5_BLK_ST), lambda b, j: (j, 0, 0)),
                  pl.BlockSpec((1, S5_BLK_CH, S5_BLK_ST), lambda b, j: (j, 0, 0)),
                  pl.BlockSpec((1, S5_BLK_ST, S5_BLK_CH), lambda b, j: (j, 0, 0)),
                  pl.BlockSpec((1, S5_BLK_ST, S5_BLK_CH), lambda b, j: (j, 0, 0)),
                  vec(S5_BLK_ST), vec(S5_BLK_ST), vec(S5_BLK_ST), vec(S5_BLK_CH), st, st],
        out_specs=[pl.BlockSpec((1, L, S5_BLK_CH), lambda b, j: (b, 0, j)), st, st],
        out_shape=[jax.ShapeDtypeStruct((B, L, nb * S5_BLK_CH), jnp.float32),
                   jax.ShapeDtypeStruct((B, 1, nb * S5_BLK_ST), jnp.float32),
                   jax.ShapeDtypeStruct((B, 1, nb * S5_BLK_ST), jnp.float32)],
        scratch_shapes=[pltpu.VMEM((S5_BLK_ST // LANES, L, LANES), jnp.float32),
                        pltpu.VMEM((S5_BLK_ST // LANES, L, LANES), jnp.float32)],
        compiler_params=_params(2),
        name="s5",
    )(z, bblk_re, bblk_im, cblk_re, cblk_im, a_re, a_im, log_dt_exp, d_flat, x0_re, x0_im)


def _s5_block_diag(b_re, b_im, c_re, c_im):
    G, P, Cc = b_re.shape
    nb = G // 8
    same = jnp.eye(8, dtype=bool)

    def bblk(b):
        t = jnp.transpose(b.reshape(nb, 8, P, Cc), (0, 1, 3, 2))[:, :, :, None, :]
        t = jnp.where(same[None, :, None, :, None], t, 0.0)
        return t.reshape(nb, 8 * Cc, 8 * P).astype(jnp.bfloat16)

    def cblk(c):
        t = jnp.transpose(c.reshape(nb, 8, Cc, P), (0, 1, 3, 2))[:, :, :, None, :]
        t = jnp.where(same[None, :, None, :, None], t, 0.0)
        return t.reshape(nb, 8 * P, 8 * Cc).astype(jnp.bfloat16)

    return bblk(b_re), bblk(b_im), cblk(c_re), cblk(c_im)


def _glu_kernel(h_ref, w_ref, b_ref, g_ref, o_ref):
    h = h_ref[...]
    t = _dot(h, w_ref[...]) + b_ref[...]
    o_ref[...] = (h * _sigmoid(t) * _silu(g_ref[...])).astype(o_ref.dtype)


def _glu_call(h2d, glu_w_bf16, glu_b, z2d, *, tm):
    M, W = h2d.shape
    return pl.pallas_call(
        _glu_kernel,
        grid=(M // tm,),
        in_specs=[pl.BlockSpec((tm, W), lambda i: (i, 0)),
                  pl.BlockSpec((W, W), lambda i: (0, 0)),
                  pl.BlockSpec((1, W), lambda i: (0, 0)),
                  pl.BlockSpec((tm, W), lambda i: (i, OFF_S5_G // W))],
        out_specs=pl.BlockSpec((tm, W), lambda i: (i, 0)),
        out_shape=jax.ShapeDtypeStruct((M, W), jnp.bfloat16),
        compiler_params=_params(1),
        name="glu",
    )(h2d, glu_w_bf16, glu_b.reshape(1, W), z2d)


def _kvprep_kernel(kv_ref, sm_ref, knw_ref, k_ref, v_ref, ik_ref, *rest, transposed):
    kvw = KV_HEADS * ATT_D
    kv = kv_ref[0]
    knw = knw_ref[...]
    ks = []
    for h in range(KV_HEADS):
        kh = kv[:, ATT_D * h:ATT_D * (h + 1)]
        ms = jnp.mean(kh * kh, axis=-1, keepdims=True)
        ks.append(kh * lax.rsqrt(ms + EPS) * knw)
    k = jnp.concatenate(ks, axis=-1)
    v = kv[:, kvw:]
    sm = sm_ref[0]
    k_ref[0] = k
    v_ref[0] = v
    ik_ref[0] = sm[:, :IDX_D]
    if transposed:
        kbf_ref, vt_ref, kia_ref, kib_ref = rest
        kbf_ref[0] = k.astype(jnp.bfloat16)
        vt_ref[0] = v.T.astype(jnp.bfloat16)
        lane = lax.broadcasted_iota(jnp.int32, sm.shape, 1)
        kia_ref[0] = jnp.where(lane < IDX_D, sm, 0.0).astype(jnp.bfloat16)
        kib_ref[0] = jnp.where(lane >= IDX_D, pltpu.roll(sm, IDX_D, axis=1), 0.0).astype(jnp.bfloat16)


def _kvprep_call(z, k_norm_w, *, tr, transposed):
    B, L, _ = z.shape
    kvw = KV_HEADS * ATT_D
    f32, bf = jnp.float32, jnp.bfloat16
    rowblk = lambda w: pl.BlockSpec((1, tr, w), lambda b, r: (b, r, 0))
    out_specs = [rowblk(kvw), rowblk(kvw), rowblk(IDX_D)]
    out_shape = [jax.ShapeDtypeStruct((B, L, kvw), f32), jax.ShapeDtypeStruct((B, L, kvw), f32),
                 jax.ShapeDtypeStruct((B, L, IDX_D), f32)]
    if transposed:
        out_specs += [rowblk(kvw), pl.BlockSpec((1, kvw, tr), lambda b, r: (b, 0, r)), rowblk(LANES), rowblk(LANES)]
        out_shape += [jax.ShapeDtypeStruct((B, L, kvw), bf), jax.ShapeDtypeStruct((B, kvw, L), bf),
                      jax.ShapeDtypeStruct((B, L, LANES), bf), jax.ShapeDtypeStruct((B, L, LANES), bf)]
    return pl.pallas_call(
        functools.partial(_kvprep_kernel, transposed=transposed),
        grid=(B, L // tr),
        in_specs=[pl.BlockSpec((1, tr, 2 * kvw), lambda b, r: (b, r, OFF_AT_KV // (2 * kvw))),
                  pl.BlockSpec((1, tr, LANES), lambda b, r: (b, r, OFF_SMALL // LANES)),
                  pl.BlockSpec((1, ATT_D), lambda b, r: (0, 0))],
        out_specs=out_specs, out_shape=out_shape,
        compiler_params=_params(2),
        name="kvprep",
    )(z, z, k_norm_w.reshape(1, ATT_D))


def _sortable_key(score):
    bits = lax.bitcast_convert_type(score, jnp.int32)
    return jnp.where(bits < 0, bits ^ 0x7FFFFFFF, bits)


def _kth_largest_key(key, k, axis):
    shape = list(key.shape)
    shape[axis] = 1

    def it(n, tu):
        cand_u = tu | jnp.left_shift(jnp.int32(1), 31 - n)
        cand_s = cand_u ^ jnp.int32(INT_MIN)
        cnt = jnp.sum(jnp.where(key >= cand_s, 1.0, 0.0), axis=axis, keepdims=True)
        return jnp.where(cnt >= k, cand_u, tu)

    tu = lax.fori_loop(0, 32, it, jnp.zeros(shape, jnp.int32))
    return tu ^ jnp.int32(INT_MIN)


def _t5_bucket(dist):
    exact = REL_BUCKETS // 2
    d = dist.astype(jnp.float32)
    large = exact + jnp.log(jnp.maximum(d, 1.0) / exact) / math.log(REL_MAX_DIST / exact) * (REL_BUCKETS - exact)
    large = jnp.minimum(jnp.maximum(large, 0.0).astype(jnp.int32), REL_BUCKETS - 1)
    return jnp.where(dist < exact, dist, large)


def _pattn_kernel(zq_ref, zg_ref, ziq_ref, zsm_ref, kbf_ref, vt_ref, kia_ref, kib_ref, qnw_ref, bias_ref, o_ref,
                  *, topk):
    i = pl.program_id(1)
    S = kbf_ref.shape[1]
    nkb = S // Q_BLK
    qnw = qnw_ref[...]
    wt = zsm_ref[0].T * (1.0 / 32.0)
    kia, kib = kia_ref[0], kib_ref[0]
    score = jnp.zeros((S, Q_BLK), jnp.float32)
    for p in range(IDX_HEADS // 2):
        qi = ziq_ref[0, :, pl.ds(LANES * p, LANES)].astype(jnp.bfloat16)
        da = lax.dot_general(kia, qi, _NT, preferred_element_type=jnp.float32)
        db = lax.dot_general(kib, qi, _NT, preferred_element_type=jnp.float32)
        r = SMALL_IXW + 2 * p
        score = score + jnp.maximum(da, 0.0) * wt[r:r + 1] + jnp.maximum(db, 0.0) * wt[r + 1:r + 2]
    s_pos = lax.broadcasted_iota(jnp.int32, (S, Q_BLK), 0)
    t_pos = lax.broadcasted_iota(jnp.int32, (S, Q_BLK), 1) + i * Q_BLK
    adm = s_pos <= t_pos
    key = jnp.where(adm, _sortable_key(score), jnp.int32(INT_MIN))
    thr = _kth_largest_key(key, topk, 0)
    sel = (key >= thr) & adm

    scale = ATT_D ** -0.5
    for h in range(ATT_HEADS):
        kvh = h // (ATT_HEADS // KV_HEADS)
        q = zq_ref[0, :, pl.ds(ATT_D * h, ATT_D)]
        ms = jnp.mean(q * q, axis=-1, keepdims=True)
        qn = (q * lax.rsqrt(ms + EPS) * qnw).astype(jnp.bfloat16)
        logit = lax.dot_general(kbf_ref[0, :, pl.ds(ATT_D * kvh, ATT_D)], qn, _NT,
                                preferred_element_type=jnp.float32) * scale
        bias = jnp.concatenate([bias_ref[jnp.clip(i - j, 0, N_BIAS_TILES - 1), h] for j in range(nkb)], axis=0)
        x = jnp.where(sel, logit + bias, MASK_VALUE)
        m = jnp.max(x, axis=0, keepdims=True)
        pexp = jnp.exp(x - m)
        l = jnp.sum(pexp, axis=0, keepdims=True)
        ot = jnp.dot(vt_ref[0, pl.ds(ATT_D * kvh, ATT_D), :], pexp.astype(jnp.bfloat16),
                     preferred_element_type=jnp.float32)
        out = (ot / l).T
        g = zg_ref[0, :, pl.ds(ATT_D * h, ATT_D)]
        o_ref[0, :, pl.ds(ATT_D * h, ATT_D)] = (out * _silu(g)).astype(o_ref.dtype)


def _pattn_call(z, kbf, vt, kia, kib, q_norm_w, bias_tiles, *, topk):
    B, L, _ = z.shape
    W = ATT_HEADS * ATT_D
    kvw = KV_HEADS * ATT_D
    zblk = lambda off, w: pl.BlockSpec((1, Q_BLK, w), lambda b, i: (b, i, off // w))
    full = lambda shape: pl.BlockSpec((1,) + shape, lambda b, i: (b, 0, 0))
    return pl.pallas_call(
        functools.partial(_pattn_kernel, topk=topk),
        grid=(B, L // Q_BLK),
        in_specs=[zblk(OFF_AT_Q, W), zblk(OFF_AT_G, W), zblk(OFF_IX_Q, W), zblk(OFF_SMALL, LANES),
                  full((L, kvw)), full((kvw, L)), full((L, LANES)), full((L, LANES)),
                  pl.BlockSpec((1, ATT_D), lambda b, i: (0, 0)),
                  pl.BlockSpec((N_BIAS_TILES, ATT_HEADS, Q_BLK, Q_BLK), lambda b, i: (0, 0, 0, 0))],
        out_specs=pl.BlockSpec((1, Q_BLK, W), lambda b, i: (b, i, 0)),
        out_shape=jax.ShapeDtypeStruct((B, L, W), jnp.bfloat16),
        compiler_params=_params(2),
        name="pattn",
    )(z, z, z, z, kbf, vt, kia, kib, q_norm_w.reshape(1, ATT_D), bias_tiles)


def _prompt_bias_tiles(rel_bias):
    o = jnp.arange(N_BIAS_TILES, dtype=jnp.int32)[:, None, None]
    s = jnp.arange(Q_BLK, dtype=jnp.int32)[None, :, None]
    t = jnp.arange(Q_BLK, dtype=jnp.int32)[None, None, :]
    tiles = rel_bias[_t5_bucket(jnp.maximum(Q_BLK * o + t - s, 0))]
    return jnp.moveaxis(tiles, -1, 1)


def _sattn_score_kernel(pt_ref, qi_ref, w_ref, iknew_ref, *rest):
    pages, s_ref = rest[:NPG], rest[NPG]
    pc = pl.program_id(1)
    last = pl.num_programs(1) - 1
    qi = qi_ref[0]
    w = w_ref[0]

    def scores(keys_bf16):
        d = lax.dot_general(qi, keys_bf16, _NT, preferred_element_type=jnp.float32)
        r = jnp.maximum(d, 0.0) * w
        return jnp.sum(r.reshape(T_PAD, IDX_HEADS, r.shape[-1]), axis=1)

    @pl.when(pc < last)
    def _():
        for i in range(NPG):
            s_ref[0, :, pl.ds(PAGE * i, PAGE)] = scores(pages[i][0, 0].astype(jnp.bfloat16))

    @pl.when(pc == last)
    def _():
        s_ref[0] = jnp.full((T_PAD, CHUNK_KEYS), MASK_VALUE, jnp.float32)
        s_ref[0, :, pl.ds(0, PAGE)] = scores(iknew_ref[0].astype(jnp.bfloat16))


def _sattn_score_call(page_table, qi_rows, w_col, ik_new, cache_ik, layer):
    B, n_pages = page_table.shape
    n_chunks = n_pages // NPG
    page_spec = lambda i: pl.BlockSpec(
        (1, 1, PAGE, IDX_D), lambda b, pc, pt: (layer, pt[b, jnp.minimum(pc * NPG + i, n_pages - 1)], 0, 0))
    per_b = lambda shape: pl.BlockSpec((1,) + shape, lambda b, pc, pt: (b, 0, 0))
    grid_spec = pltpu.PrefetchScalarGridSpec(
        num_scalar_prefetch=1, grid=(B, n_chunks + 1),
        in_specs=[per_b((T_PAD * IDX_HEADS, IDX_D)), per_b((T_PAD * IDX_HEADS, 1)), per_b((PAGE, IDX_D))]
        + [page_spec(i) for i in range(NPG)],
        out_specs=pl.BlockSpec((1, T_PAD, CHUNK_KEYS), lambda b, pc, pt: (b, 0, pc)))
    return pl.pallas_call(
        _sattn_score_kernel,
        grid_spec=grid_spec,
        out_shape=jax.ShapeDtypeStruct((B, T_PAD, (n_chunks + 1) * CHUNK_KEYS), jnp.float32),
        compiler_params=_params(2),
        name="sattn_score",
    )(page_table, qi_rows, w_col, ik_new, *([cache_ik] * NPG))


def _sattn_kernel(pt_ref, s_ref, q_ref, g_ref, qnw_ref, bias_ref, knew_ref, vnew_ref, *rest, topk, n_new, past):
    kp, vp = rest[:NPG], rest[NPG:2 * NPG]
    o_ref, thr_ref, qn_ref, m_ref, l_ref, acc_ref = rest[2 * NPG:]
    pc = pl.program_id(1)
    last = pl.num_programs(1) - 1
    G = ATT_HEADS // KV_HEADS

    def admissible(col0, width):
        pos = lax.broadcasted_iota(jnp.int32, (T_PAD, width), 1) + col0
        t = lax.broadcasted_iota(jnp.int32, (T_PAD, width), 0)
        return (pos < past) | ((pos - past <= t) & (pos - past < n_new))

    @pl.when(pc == 0)
    def _():
        sc = s_ref[0]
        key = jnp.where(admissible(0, sc.shape[-1]), _sortable_key(sc), jnp.int32(INT_MIN))
        thr_ref[...] = jnp.broadcast_to(_kth_largest_key(key, topk, 1), thr_ref.shape)
        q = q_ref[0]
        ms = jnp.mean(q * q, axis=-1, keepdims=True)
        qn_ref[...] = (q * lax.rsqrt(ms + EPS) * qnw_ref[...]).astype(jnp.bfloat16)
        m_ref[...] = jnp.full(m_ref.shape, MASK_VALUE, jnp.float32)
        l_ref[...] = jnp.zeros(l_ref.shape, jnp.float32)
        acc_ref[...] = jnp.zeros(acc_ref.shape, jnp.float32)

    def attend(kc, vc, width):
        col0 = pl.multiple_of(pc * CHUNK_KEYS, CHUNK_KEYS)
        sc = s_ref[0, :, pl.ds(col0, width)]
        adm = admissible(col0, width)
        key = jnp.where(adm, _sortable_key(sc), jnp.int32(INT_MIN))
        sel8 = (key >= thr_ref[:, :1]) & adm
        sel = jnp.concatenate([sel8] * G, axis=0)
        for kv in range(KV_HEADS):
            kk = kc[:, ATT_D * kv:ATT_D * (kv + 1)].astype(jnp.bfloat16)
            x = lax.dot_general(qn_ref[kv], kk, _NT, preferred_element_type=jnp.float32) * (ATT_D ** -0.5)
            x = jnp.where(sel, x + bias_ref[kv, :, pl.ds(0, width)], MASK_VALUE)
            m_old = m_ref[kv]
            m_new = jnp.maximum(m_old, jnp.max(x, axis=-1, keepdims=True))
            p = jnp.where(sel, jnp.exp(x - m_new), 0.0)
            alpha = jnp.exp(m_old - m_new)
            l_ref[kv] = alpha * l_ref[kv] + jnp.sum(p, axis=-1, keepdims=True)
            acc_ref[kv] = alpha * acc_ref[kv] + jnp.dot(
                p.astype(jnp.bfloat16), vc[:, ATT_D * kv:ATT_D * (kv + 1)].astype(jnp.bfloat16),
                preferred_element_type=jnp.float32)
            m_ref[kv] = m_new

    @pl.when(pc < last)
    def _():
        attend(jnp.concatenate([kp[i][0, 0] for i in range(NPG)], axis=0),
               jnp.concatenate([vp[i][0, 0] for i in range(NPG)], axis=0), CHUNK_KEYS)

    @pl.when(pc == last)
    def _():
        attend(knew_ref[0], vnew_ref[0], PAGE)
        o_ref[0] = acc_ref[...] / l_ref[...] * _silu(g_ref[0])


def _sattn_call(page_table, scores, q_rows, g_rows, q_norm_w, bias, k_new, v_new, cache_k, cache_v, layer,
                *, topk, n_new):
    B, n_pages = page_table.shape
    n_chunks = n_pages // NPG
    kvw = KV_HEADS * ATT_D
    rows = (ATT_HEADS // KV_HEADS) * T_PAD
    n_keys = (n_chunks + 1) * CHUNK_KEYS
    page_spec = lambda i: pl.BlockSpec(
        (1, 1, PAGE, kvw), lambda b, pc, pt: (layer, pt[b, jnp.minimum(pc * NPG + i, n_pages - 1)], 0, 0))
    per_b = lambda shape: pl.BlockSpec((1,) + shape, lambda b, pc, pt: (b,) + (0,) * len(shape))
    grid_spec = pltpu.PrefetchScalarGridSpec(
        num_scalar_prefetch=1, grid=(B, n_chunks + 1),
        in_specs=[per_b((T_PAD, n_keys)), per_b((KV_HEADS, rows, ATT_D)), per_b((KV_HEADS, rows, ATT_D)),
                  pl.BlockSpec((1, ATT_D), lambda b, pc, pt: (0, 0)),
                  pl.BlockSpec((KV_HEADS, rows, CHUNK_KEYS), lambda b, pc, pt: (0, 0, pc)),
                  per_b((PAGE, kvw)), per_b((PAGE, kvw))]
        + [page_spec(i) for i in range(NPG)] + [page_spec(i) for i in range(NPG)],
        out_specs=per_b((KV_HEADS, rows, ATT_D)),
        scratch_shapes=[pltpu.VMEM((T_PAD, LANES), jnp.int32), pltpu.VMEM((KV_HEADS, rows, ATT_D), jnp.bfloat16),
                        pltpu.VMEM((KV_HEADS, rows, 1), jnp.float32), pltpu.VMEM((KV_HEADS, rows, 1), jnp.float32),
                        pltpu.VMEM((KV_HEADS, rows, ATT_D), jnp.float32)])
    return pl.pallas_call(
        functools.partial(_sattn_kernel, topk=topk, n_new=n_new, past=n_pages * PAGE),
        grid_spec=grid_spec,
        out_shape=jax.ShapeDtypeStruct((B, KV_HEADS, rows, ATT_D), jnp.float32),
        compiler_params=_params(2),
        name="sattn",
    )(page_table, scores, q_rows, g_rows, q_norm_w.reshape(1, ATT_D), bias, k_new, v_new,
      *([cache_k] * NPG), *([cache_v] * NPG))


def _sample_bias(rel_bias, past, n_keys):
    G = ATT_HEADS // KV_HEADS
    t = jnp.arange(T_PAD, dtype=jnp.int32)[:, None]
    pos = jnp.arange(n_keys, dtype=jnp.int32)[None, :]
    tab = rel_bias[_t5_bucket(jnp.maximum(past + t - pos, 0))]
    return jnp.moveaxis(tab, -1, 0).reshape(KV_HEADS, G * T_PAD, n_keys)


def _to_head_rows(a):
    B = a.shape[0]
    G = ATT_HEADS // KV_HEADS
    a = a.reshape(B, T_PAD, KV_HEADS, G, ATT_D)
    return jnp.transpose(a, (0, 2, 3, 1, 4)).reshape(B, KV_HEADS, G * T_PAD, ATT_D)


def _from_head_rows(a):
    B = a.shape[0]
    G = ATT_HEADS // KV_HEADS
    a = a.reshape(B, KV_HEADS, G, T_PAD, ATT_D)
    return jnp.transpose(a, (0, 3, 1, 2, 4)).reshape(B, T_PAD, ATT_HEADS * ATT_D)


def _layer(x, lp, states, attend, *, l_valid, tm, chunk_hg, chunk_ssd, tr):
    B, L, _ = x.shape
    hg_s0, ssm_s0, conv_s0, s5_re0, s5_im0 = states
    x2d = x.reshape(B * L, D_MODEL)
    z2d = _inproj(x2d, lp['norm_w'], lp['w_in'], tm)
    z = z2d.reshape(B, L, D_IN_PAD)

    o_hg, hg_s = _hgrn_call(z, lp['lb'], lp['hg_norm_w'], hg_s0, chunk=chunk_hg, l_valid=l_valid)

    conv8 = jnp.pad(conv_s0, ((0, 0), (SUBLANES - (SSM_CONV - 1), 0), (0, 0)))
    y_ssm, ssm_s, cx, cb, cc = _ssd_call(z, lp['conv_w'], lp['conv_b'], conv8, lp['dtb_pad'], lp['alog_pad'],
                                         lp['dskip_pad'], lp['ssm_norm_w'], ssm_s0, chunk=chunk_ssd, l_valid=l_valid)
    tail = SUBLANES - (SSM_CONV - 1)
    conv_s = jnp.concatenate([cx[:, tail:], cb[:, tail:], cc[:, tail:]], axis=-1)

    h5, s5_re, s5_im = _s5_call(z, *lp['s5_blocks'], lp['a_re'], lp['a_im'], lp['log_dt'], lp['s5_d'],
                                s5_re0.reshape(B, 1, -1), s5_im0.reshape(B, 1, -1), l_valid=l_valid)
    o5 = _glu_call(h5.reshape(B * L, BRANCH), lp['glu_w'], lp['glu_b'], z2d, tm=tm)

    o_att, k, v, ik = attend(z)

    y = _outproj((o_hg.reshape(B * L, BRANCH), y_ssm.reshape(B * L, BRANCH), o5, o_att.reshape(B * L, BRANCH)),
                 lp['w_out'], x2d, tm)
    st = (k[:, :l_valid].reshape(B, l_valid, KV_HEADS, ATT_D), v[:, :l_valid].reshape(B, l_valid, KV_HEADS, ATT_D),
          ik[:, :l_valid], hg_s, ssm_s, conv_s,
          s5_re.reshape(B, S5_GROUPS, S5_STATE), s5_im.reshape(B, S5_GROUPS, S5_STATE))
    return y.reshape(B, L, D_MODEL), st


def kernel(x_prompt, x_sample, cache_k, cache_v, cache_idx_k, state_hgrn, state_ssm, state_conv, state_s5_re, state_s5_im, page_table, norm_w, w_in, w_out, hg_lb_logits, hg_norm_w, ssm_conv_w, ssm_conv_b, ssm_dt_bias, ssm_a_log, ssm_d, ssm_norm_w, s5_a_re, s5_a_im, s5_log_dt, s5_b_re, s5_b_im, s5_c_re, s5_c_im, s5_d, s5_glu_w, s5_glu_b, att_q_norm, att_k_norm, rel_bias):
    f32 = jnp.float32
    bp, lp_len, _ = x_prompt.shape
    bs, ls, _ = x_sample.shape
    n_pool = cache_k.shape[1]
    n_pages = page_table.shape[1]
    past = n_pages * PAGE
    n_keys = (n_pages // NPG + 1) * CHUNK_KEYS

    sm = jax.nn.softmax(hg_lb_logits.astype(f32), axis=0)
    lower = jnp.cumsum(sm, axis=0) - sm[0]

    bias_tiles = _prompt_bias_tiles(rel_bias)
    bias_sample = _sample_bias(rel_bias, past, n_keys)
    cache_k2 = cache_k.reshape(DEPTH, n_pool, PAGE, KV_HEADS * ATT_D)
    cache_v2 = cache_v.reshape(DEPTH, n_pool, PAGE, KV_HEADS * ATT_D)
    topk_p = min(TOPK_MAX, lp_len // 4)
    topk_s = min(TOPK_MAX, (past + ls) // 4)

    def head_lanes(v):
        return jnp.zeros((1, LANES), f32).at[0, SMALL_DT:SMALL_DT + SSM_HEADS].set(v)

    xs_pad = jnp.pad(x_sample, ((0, 0), (0, SAMPLE_ROWS - ls), (0, 0)))
    pad_page = lambda a: jnp.pad(a, ((0, 0), (0, PAGE - a.shape[1]), (0, 0)))

    yp, ys = x_prompt, xs_pad
    new_p = [[] for _ in range(8)]
    new_s = [[] for _ in range(8)]
    for l in range(DEPTH):
        lp = {'norm_w': norm_w[l], 'w_in': _pack_w_in(w_in[l]), 'w_out': w_out[l].astype(jnp.bfloat16),
              'lb': lower[l], 'hg_norm_w': hg_norm_w[l],
              'conv_w': ssm_conv_w[l], 'conv_b': ssm_conv_b[l], 'dtb_pad': head_lanes(ssm_dt_bias[l]),
              'alog_pad': head_lanes(ssm_a_log[l]), 'dskip_pad': head_lanes(ssm_d[l]), 'ssm_norm_w': ssm_norm_w[l],
              's5_blocks': _s5_block_diag(s5_b_re[l], s5_b_im[l], s5_c_re[l], s5_c_im[l]),
              'a_re': s5_a_re[l].reshape(1, -1), 'a_im': s5_a_im[l].reshape(1, -1),
              'log_dt': jnp.repeat(s5_log_dt[l], S5_STATE).reshape(1, -1), 's5_d': s5_d[l].reshape(1, -1),
              'glu_w': s5_glu_w[l].astype(jnp.bfloat16), 'glu_b': s5_glu_b[l]}

        def attend_prompt(z):
            k, v, ik, kbf, vt, kia, kib = _kvprep_call(z, att_k_norm[l], tr=256, transposed=True)
            o = _pattn_call(z, kbf, vt, kia, kib, att_q_norm[l], bias_tiles, topk=topk_p)
            return o, k, v, ik

        def attend_sample(z):
            k, v, ik = _kvprep_call(z, att_k_norm[l], tr=SAMPLE_ROWS, transposed=False)
            z8 = z[:, :T_PAD]
            qi_rows = z8[..., OFF_IX_Q:OFF_IX_Q + IDX_HEADS * IDX_D].reshape(bs, T_PAD * IDX_HEADS, IDX_D)
            w_col = z8[..., OFF_SMALL + SMALL_IXW:OFF_SMALL + SMALL_IXW + IDX_HEADS] * (1.0 / 32.0)
            scores = _sattn_score_call(page_table, qi_rows.astype(jnp.bfloat16),
                                       w_col.reshape(bs, T_PAD * IDX_HEADS, 1), pad_page(ik), cache_idx_k, l)
            o = _sattn_call(page_table, scores, _to_head_rows(z8[..., OFF_AT_Q:OFF_AT_Q + BRANCH]),
                            _to_head_rows(z8[..., OFF_AT_G:OFF_AT_G + BRANCH]), att_q_norm[l], bias_sample,
                            pad_page(k), pad_page(v), cache_k2, cache_v2, l, topk=topk_s, n_new=ls)
            o = jnp.pad(_from_head_rows(o), ((0, 0), (0, SAMPLE_ROWS - T_PAD), (0, 0)))
            return o.astype(jnp.bfloat16), k, v, ik

        zero_states = (jnp.zeros((bp, HG_HEADS, LANES, LANES), f32),
                       jnp.zeros((bp, SSM_HEADS, SSM_P, SSM_STATE), f32),
                       jnp.zeros((bp, SSM_CONV - 1, SSM_CONV_DIM), f32),
                       jnp.zeros((bp, S5_GROUPS, S5_STATE), f32),
                       jnp.zeros((bp, S5_GROUPS, S5_STATE), f32))
        yp, st_p = _layer(yp, lp, zero_states, attend_prompt, l_valid=lp_len, tm=512, chunk_hg=64, chunk_ssd=128,
                          tr=256)
        samp_states = (state_hgrn[l], state_ssm[l], state_conv[l], state_s5_re[l], state_s5_im[l])
        ys, st_s = _layer(ys, lp, samp_states, attend_sample, l_valid=ls, tm=bs * SAMPLE_ROWS,
                          chunk_hg=SAMPLE_ROWS, chunk_ssd=SAMPLE_ROWS, tr=SAMPLE_ROWS)
        for i in range(8):
            new_p[i].append(st_p[i])
            new_s[i].append(st_s[i])
    pk, pv, pik, phg, pssm, pconv, ps5r, ps5i = [jnp.stack(a) for a in new_p]
    sk, sv, sik, shg, sssm, sconv, ss5r, ss5i = [jnp.stack(a) for a in new_s]
    return (yp, ys[:, :ls], pk, pv, pik, phg, pssm, pconv, ps5r, ps5i, sk, sv, sik, shg, sssm, sconv, ss5r, ss5i)
```

```python
import functools
import math

import jax
import jax.numpy as jnp
from jax import lax
from jax.experimental import pallas as pl
from jax.experimental.pallas import tpu as pltpu

D_MODEL = 4096
DEPTH = 2
BRANCH = D_MODEL // 4
HG_HEADS = 8
SSM_HEADS = 16
SSM_GROUPS = 4
SSM_STATE = 128
SSM_CONV = 4
SSM_CONV_DIM = BRANCH + 2 * SSM_GROUPS * SSM_STATE
SSM_HPG = SSM_HEADS // SSM_GROUPS
SSM_P = 64
S5_GROUPS = 64
S5_STATE = 64
S5_BLK_CH = 128
S5_BLK_ST = 512
ATT_D = 128
ATT_HEADS = 8
KV_HEADS = 2
IDX_HEADS = 16
IDX_D = 64
TOPK_MAX = 256
Q_BLK = 128
PAGE = 128
REL_BUCKETS = 32
REL_MAX_DIST = 1024
EPS = 1e-6
MASK_VALUE = -1e30
F_FLOOR = 1e-30
INT_MIN = -2 ** 31

LANES = 128
SUBLANES = 8
V7X_VMEM_LIMIT = 56 * 1024 * 1024

_SRC_DT, _SRC_S5U, _SRC_ATK, _SRC_ATG, _SRC_IXK, D_IN_SRC = 7168, 7184, 10256, 10768, 12816, 12896

OFF_HG_Q, OFF_HG_F, OFF_HG_I, OFF_HG_G = 0, 1024, 2048, 3072
OFF_SM_Z, OFF_SM_XBC = 4096, 5120
OFF_S5_U, OFF_S5_G = 7168, 8192
OFF_AT_Q, OFF_AT_G, OFF_IX_Q = 9216, 10240, 11264
OFF_AT_KV = 12288
OFF_SMALL = 12800
SMALL_IXW, SMALL_DT = 64, 80
D_IN_PAD = 13312

N_BIAS_TILES = 9
HG_SUB = 16
N_CAUSAL_VARIANTS = 4
NPG = 8
T_PAD = 8
CHUNK_KEYS = NPG * PAGE
SAMPLE_ROWS = 16

_TN = (((0,), (0,)), ((), ()))
_NT = (((1,), (1,)), ((), ()))


def _params(n_axes):
    return pltpu.CompilerParams(dimension_semantics=("arbitrary",) * n_axes, vmem_limit_bytes=V7X_VMEM_LIMIT)


def _bf(x):
    return x.astype(jnp.bfloat16)


def _dot(a, b, dims=None):
    if dims is None:
        return jnp.dot(_bf(a), _bf(b), preferred_element_type=jnp.float32)
    return lax.dot_general(_bf(a), _bf(b), dims, preferred_element_type=jnp.float32)


def _sigmoid(x):
    return 1.0 / (1.0 + jnp.exp(-x))


def _silu(x):
    return x * _sigmoid(x)


def _softplus(x):
    return jnp.maximum(x, 0.0) + jnp.log(1.0 + jnp.exp(-jnp.abs(x)))


def _gelu_tanh(x):
    return 0.5 * x * (1.0 + jnp.tanh(math.sqrt(2.0 / math.pi) * (x + 0.044715 * (x * x * x))))


def _cumsum_rows(x):
    n = x.shape[0]
    row = lax.broadcasted_iota(jnp.int32, x.shape, 0)
    sh = 1
    while sh < n:
        x = x + jnp.where(row >= sh, pltpu.roll(x, sh, axis=0), 0.0)
        sh *= 2
    return x


def _split3(x):
    hi = x.astype(jnp.bfloat16)
    r = x - hi.astype(jnp.float32)
    mid = r.astype(jnp.bfloat16)
    lo = (r - mid.astype(jnp.float32)).astype(jnp.bfloat16)
    return hi, mid, lo


def _select_lanes_as_rows(sel, x):
    out = None
    for part in _split3(x):
        t = lax.dot_general(sel, part, _NT, preferred_element_type=jnp.float32)
        out = t if out is None else out + t
    return out


def _cmul(ar, ai, br, bi):
    return ar * br - ai * bi, ar * bi + ai * br


def _pack_w_in(w):
    zeros = jnp.zeros((w.shape[0], D_IN_PAD - (OFF_SMALL + 96)), w.dtype)
    packed = jnp.concatenate(
        [w[:, :_SRC_DT],
         w[:, _SRC_S5U:_SRC_ATK],
         w[:, _SRC_ATG:_SRC_IXK],
         w[:, _SRC_ATK:_SRC_ATG],
         w[:, _SRC_IXK:],
         w[:, _SRC_DT:_SRC_S5U],
         zeros], axis=1)
    return packed.astype(jnp.bfloat16)


def _inproj_kernel(x_ref, nw_ref, w_ref, z_ref, hn_ref):
    @pl.when(pl.program_id(1) == 0)
    def _():
        xf = x_ref[...]
        ms = jnp.mean(xf * xf, axis=-1, keepdims=True)
        hn_ref[...] = (xf * lax.rsqrt(ms + EPS) * nw_ref[...]).astype(jnp.bfloat16)

    z_ref[...] = jnp.dot(hn_ref[...], w_ref[...], preferred_element_type=jnp.float32)


def _inproj(x2d, norm_w, w_packed, tm, tn=1024):
    m = x2d.shape[0]
    return pl.pallas_call(
        _inproj_kernel,
        grid=(m // tm, D_IN_PAD // tn),
        in_specs=[pl.BlockSpec((tm, D_MODEL), lambda i, j: (i, 0)),
                  pl.BlockSpec((1, D_MODEL), lambda i, j: (0, 0)),
                  pl.BlockSpec((D_MODEL, tn), lambda i, j: (0, j))],
        out_specs=pl.BlockSpec((tm, tn), lambda i, j: (i, j)),
        out_shape=jax.ShapeDtypeStruct((m, D_IN_PAD), jnp.float32),
        scratch_shapes=[pltpu.VMEM((tm, D_MODEL), jnp.bfloat16)],
        compiler_params=_params(2),
        name="inproj",
    )(x2d, norm_w.reshape(1, D_MODEL), w_packed)


def _outproj_kernel(m0_ref, m1_ref, m2_ref, m3_ref, w_ref, x_ref, y_ref):
    acc = x_ref[...]
    for i, m_ref in enumerate((m0_ref, m1_ref, m2_ref, m3_ref)):
        acc = acc + jnp.dot(m_ref[...], w_ref[pl.ds(BRANCH * i, BRANCH), :], preferred_element_type=jnp.float32)
    y_ref[...] = acc


def _outproj(mixed4, w_out_bf16, x2d, tm, tn=1024):
    m = x2d.shape[0]
    mspec = pl.BlockSpec((tm, BRANCH), lambda i, j: (i, 0))
    return pl.pallas_call(
        _outproj_kernel,
        grid=(m // tm, D_MODEL // tn),
        in_specs=[mspec, mspec, mspec, mspec,
                  pl.BlockSpec((D_MODEL, tn), lambda i, j: (0, j)),
                  pl.BlockSpec((tm, tn), lambda i, j: (i, j))],
        out_specs=pl.BlockSpec((tm, tn), lambda i, j: (i, j)),
        out_shape=jax.ShapeDtypeStruct((m, D_MODEL), jnp.float32),
        compiler_params=_params(2),
        name="outproj",
    )(*mixed4, w_out_bf16, x2d)


def _hgrn_kernel(q_ref, f_ref, i_ref, g_ref, lb_ref, nw_ref, s0_ref, o_ref, s_out_ref, st_ref, *, chunk, l_valid):
    L = q_ref.shape[1]
    C = chunk
    nblk = C // SUBLANES
    st_ref[...] = s0_ref[0, 0].T
    lb = lb_ref[0]
    nw = nw_ref[...]

    def body(c, carry):
        t0 = pl.multiple_of(c * C, C)
        fp = f_ref[0, pl.ds(t0, C), :]
        qp = q_ref[0, pl.ds(t0, C), :]
        v = i_ref[0, pl.ds(t0, C), :]
        gp = g_ref[0, pl.ds(t0, C), :]
        fg = lb + (1.0 - lb) * _sigmoid(fp)
        logf = jnp.log(jnp.maximum(fg, F_FLOOR))
        kk = (1.0 - lb) * _sigmoid(-fp)
        if l_valid < L:
            rows = lax.broadcasted_iota(jnp.int32, (C, LANES), 0) + t0
            logf = jnp.where(rows < l_valid, logf, 0.0)
            kk = jnp.where(rows < l_valid, kk, 0.0)
        qq = _silu(qp)
        G = _cumsum_rows(logf)
        st = st_ref[...]
        o_inter = _dot(qq * jnp.exp(G), st, _NT)
        acc = [None] * nblk
        qb = [qq[SUBLANES * tb:SUBLANES * (tb + 1)] for tb in range(nblk)]
        Gb = [G[SUBLANES * tb:SUBLANES * (tb + 1)] for tb in range(nblk)]
        rowi = lax.broadcasted_iota(jnp.int32, (SUBLANES, LANES), 0)
        for s in range(C):
            sb = s // SUBLANES
            gs = G[s:s + 1]
            ks = kk[s:s + 1]
            vs = v[s:s + 1]
            for tb in range(sb, (s // HG_SUB + 1) * (HG_SUB // SUBLANES)):
                d = Gb[tb] - gs
                if tb == sb:
                    d = jnp.where(rowi >= s - SUBLANES * sb, d, MASK_VALUE)
                w = jnp.sum(qb[tb] * ks * jnp.exp(d), axis=-1, keepdims=True)
                contrib = w * vs
                acc[tb] = contrib if acc[tb] is None else acc[tb] + contrib
        for j in range(1, C // HG_SUB):
            r0 = HG_SUB * j
            gb = G[r0 - 1:r0]
            qj = qq[r0:r0 + HG_SUB] * jnp.exp(G[r0:r0 + HG_SUB] - gb)
            kj = kk[:r0] * jnp.exp(gb - G[:r0])
            oj = _dot(_dot(qj, kj, _NT), v[:r0])
            for half in range(HG_SUB // SUBLANES):
                tb = r0 // SUBLANES + half
                acc[tb] = acc[tb] + oj[SUBLANES * half:SUBLANES * (half + 1)]
        o = o_inter + jnp.concatenate(acc, axis=0)
        g_last = G[C - 1:C]
        kd = kk * jnp.exp(g_last - G)
        st_ref[...] = jnp.exp(g_last) * st + _dot(v, kd, _TN)
        ms = jnp.mean(o * o, axis=-1, keepdims=True)
        on = o * lax.rsqrt(ms + EPS) * nw
        o_ref[0, pl.ds(t0, C), :] = (on * _silu(gp)).astype(o_ref.dtype)
        return carry

    lax.fori_loop(0, L // C, body, 0)
    s_out_ref[0, 0] = st_ref[...].T


def _hgrn_call(z, lb, hg_norm_w, s0, *, chunk, l_valid):
    B, L, _ = z.shape
    zspec = lambda off: pl.BlockSpec((1, L, LANES), lambda b, h: (b, 0, off // LANES + h))
    return pl.pallas_call(
        functools.partial(_hgrn_kernel, chunk=chunk, l_valid=l_valid),
        grid=(B, HG_HEADS),
        in_specs=[zspec(OFF_HG_Q), zspec(OFF_HG_F), zspec(OFF_HG_I), zspec(OFF_HG_G),
                  pl.BlockSpec((1, 1, LANES), lambda b, h: (h, 0, 0)),
                  pl.BlockSpec((1, LANES), lambda b, h: (0, 0)),
                  pl.BlockSpec((1, 1, LANES, LANES), lambda b, h: (b, h, 0, 0))],
        out_specs=[pl.BlockSpec((1, L, LANES), lambda b, h: (b, 0, h)),
                   pl.BlockSpec((1, 1, LANES, LANES), lambda b, h: (b, h, 0, 0))],
        out_shape=[jax.ShapeDtypeStruct((B, L, BRANCH), jnp.bfloat16),
                   jax.ShapeDtypeStruct((B, HG_HEADS, LANES, LANES), jnp.float32)],
        scratch_shapes=[pltpu.VMEM((LANES, LANES), jnp.float32)],
        compiler_params=_params(2),
        name="hgrn",
    )(z, z, z, z, lb.reshape(HG_HEADS, 1, LANES), hg_norm_w.reshape(1, LANES), s0)


def _ssd_kernel(xs_ref, b_ref, c_ref, zg_ref, sm_ref, wx_ref, wb_ref, wc_ref, bx_ref, bb_ref, bc_ref,
                cx_ref, cb_ref, cc_ref, dtb_ref, alog_ref, dskip_ref, nw_ref, s0_ref,
                y_ref, s_out_ref, ox_ref, ob_ref, oc_ref,
                ax_ref, ab_ref, ac_ref, hx_ref, hb_ref, hc_ref, sp_ref, *, chunk, l_valid, conv_rows):
    L = xs_ref.shape[1]
    C = chunk
    R = conv_rows
    g = pl.program_id(1)

    def conv(src_ref, head_ref, cst_ref, w_ref, bias_ref, act_ref, out_state_ref):
        head_ref[pl.ds(0, SUBLANES), :] = cst_ref[0]
        head_ref[pl.ds(SUBLANES, R), :] = src_ref[0, pl.ds(0, R), :]
        w = w_ref[...]
        for r0 in range(0, L, R):
            acc = bias_ref[...]
            for j in range(SSM_CONV):
                if r0 == 0:
                    xj = head_ref[pl.ds(SUBLANES - j, R), :]
                else:
                    xj = src_ref[0, pl.ds(r0 - j, R), :]
                acc = acc + xj * w[SSM_CONV - 1 - j:SSM_CONV - j]
            act_ref[pl.ds(r0, R), :] = _silu(acc)
        if l_valid >= SUBLANES:
            out_state_ref[0] = src_ref[0, pl.ds(l_valid - SUBLANES, SUBLANES), :]
        else:
            out_state_ref[0] = head_ref[pl.ds(l_valid, SUBLANES), :]

    conv(xs_ref, hx_ref, cx_ref, wx_ref, bx_ref, ax_ref, ox_ref)
    conv(b_ref, hb_ref, cb_ref, wb_ref, bb_ref, ab_ref, ob_ref)
    conv(c_ref, hc_ref, cc_ref, wc_ref, bc_ref, ac_ref, oc_ref)

    for p in range(SSM_HPG // 2):
        sp_ref[p] = s0_ref[0, 2 * p:2 * p + 2].reshape(2 * SSM_P, LANES)

    lane1 = lax.broadcasted_iota(jnp.int32, (1, LANES), 1)
    lane8 = lax.broadcasted_iota(jnp.int32, (SUBLANES, LANES), 1)
    row8 = lax.broadcasted_iota(jnp.int32, (SUBLANES, LANES), 0)
    lane0 = SMALL_DT + SSM_HPG * g
    sel = jnp.where((lane8 == lane0 + row8) & (row8 < SSM_HPG), 1.0, 0.0).astype(jnp.bfloat16)
    a_all = -jnp.exp(alog_ref[...])
    dskip = [jnp.sum(jnp.where(lane1 == lane0 + j, dskip_ref[...], 0.0), axis=-1, keepdims=True)
             for j in range(SSM_HPG)]
    lane_c = lax.broadcasted_iota(jnp.int32, (C, LANES), 1)
    first_half = lane_c < SSM_P
    row_p = lax.broadcasted_iota(jnp.int32, (2 * SSM_P, LANES), 0) < SSM_P
    tril = lax.broadcasted_iota(jnp.int32, (C, C), 0) >= lax.broadcasted_iota(jnp.int32, (C, C), 1)
    nw = nw_ref[...]

    def body(c, carry):
        t0 = pl.multiple_of(c * C, C)
        sm = sm_ref[0, pl.ds(t0, C), :]
        dt_all = _softplus(sm + dtb_ref[...])
        if l_valid < L:
            rows = lax.broadcasted_iota(jnp.int32, (C, LANES), 0) + t0
            dt_all = jnp.where(rows < l_valid, dt_all, 0.0)
        cum = _cumsum_rows(dt_all * a_all)
        cum_rows = _select_lanes_as_rows(sel, cum)
        col = [jnp.sum(jnp.where(lane_c == lane0 + j, cum, 0.0), axis=-1, keepdims=True) for j in range(SSM_HPG)]
        dtc = [jnp.sum(jnp.where(lane_c == lane0 + j, dt_all, 0.0), axis=-1, keepdims=True) for j in range(SSM_HPG)]
        bact = ab_ref[pl.ds(t0, C), :]
        cact = ac_ref[pl.ds(t0, C), :]
        cb = _dot(cact, bact, _NT)
        ys = []
        for p in range(SSM_HPG // 2):
            ja, jb = 2 * p, 2 * p + 1
            xs = ax_ref[pl.ds(t0, C), pl.ds(2 * SSM_P * p, 2 * SSM_P)]
            xdt = xs * jnp.where(first_half, dtc[ja], dtc[jb])
            dec_a = jnp.exp(jnp.where(tril, col[ja] - cum_rows[ja:ja + 1], MASK_VALUE))
            dec_b = jnp.exp(jnp.where(tril, col[jb] - cum_rows[jb:jb + 1], MASK_VALUE))
            y = jnp.where(first_half, _dot(cb * dec_a, xdt), _dot(cb * dec_b, xdt))
            sp = sp_ref[p]
            y = y + _dot(cact, sp, _NT) * jnp.where(first_half, jnp.exp(col[ja]), jnp.exp(col[jb]))
            y = y + jnp.where(first_half, dskip[ja], dskip[jb]) * xs
            last_a, last_b = col[ja][C - 1:C], col[jb][C - 1:C]
            xw = xdt * jnp.where(first_half, jnp.exp(last_a - col[ja]), jnp.exp(last_b - col[jb]))
            sp_ref[p] = jnp.where(row_p, jnp.exp(last_a), jnp.exp(last_b)) * sp + _dot(xw, bact, _TN)
            zg = zg_ref[0, pl.ds(t0, C), pl.ds(2 * SSM_P * p, 2 * SSM_P)]
            ys.append(y * _silu(zg))
        ms = sum(jnp.sum(y * y, axis=-1, keepdims=True) for y in ys) * (1.0 / (SSM_HPG * SSM_P))
        inv = lax.rsqrt(ms + EPS)
        for p in range(SSM_HPG // 2):
            y_ref[0, pl.ds(t0, C), pl.ds(2 * SSM_P * p, 2 * SSM_P)] = (
                ys[p] * inv * nw[:, 2 * SSM_P * p:2 * SSM_P * (p + 1)]).astype(y_ref.dtype)
        return carry

    lax.fori_loop(0, L // C, body, 0)
    for p in range(SSM_HPG // 2):
        s_out_ref[0, 2 * p:2 * p + 2] = sp_ref[p].reshape(2, SSM_P, LANES)


def _ssd_call(z, conv_w, conv_b, conv_state8, dtb_pad, alog_pad, dskip_pad, norm_w, s0, *, chunk, l_valid):
    B, L, _ = z.shape
    assert l_valid >= SSM_CONV - 1
    xw = SSM_HPG * SSM_P
    conv_rows = min(L, 256)
    ox, ob = OFF_SM_XBC // xw, (OFF_SM_XBC + BRANCH) // LANES
    oc = (OFF_SM_XBC + BRANCH + SSM_GROUPS * SSM_STATE) // LANES
    wb0, wc0 = BRANCH // LANES, (BRANCH + SSM_GROUPS * SSM_STATE) // LANES
    in_specs = [
        pl.BlockSpec((1, L, xw), lambda b, g: (b, 0, ox + g)),
        pl.BlockSpec((1, L, LANES), lambda b, g: (b, 0, ob + g)),
        pl.BlockSpec((1, L, LANES), lambda b, g: (b, 0, oc + g)),
        pl.BlockSpec((1, L, xw), lambda b, g: (b, 0, OFF_SM_Z // xw + g)),
        pl.BlockSpec((1, L, LANES), lambda b, g: (b, 0, OFF_SMALL // LANES)),
        pl.BlockSpec((SSM_CONV, xw), lambda b, g: (0, g)),
        pl.BlockSpec((SSM_CONV, LANES), lambda b, g: (0, wb0 + g)),
        pl.BlockSpec((SSM_CONV, LANES), lambda b, g: (0, wc0 + g)),
        pl.BlockSpec((1, xw), lambda b, g: (0, g)),
        pl.BlockSpec((1, LANES), lambda b, g: (0, wb0 + g)),
        pl.BlockSpec((1, LANES), lambda b, g: (0, wc0 + g)),
        pl.BlockSpec((1, SUBLANES, xw), lambda b, g: (b, 0, g)),
        pl.BlockSpec((1, SUBLANES, LANES), lambda b, g: (b, 0, wb0 + g)),
        pl.BlockSpec((1, SUBLANES, LANES), lambda b, g: (b, 0, wc0 + g)),
        pl.BlockSpec((1, LANES), lambda b, g: (0, 0)),
        pl.BlockSpec((1, LANES), lambda b, g: (0, 0)),
        pl.BlockSpec((1, LANES), lambda b, g: (0, 0)),
        pl.BlockSpec((1, xw), lambda b, g: (0, g)),
        pl.BlockSpec((1, SSM_HPG, SSM_P, LANES), lambda b, g: (b, g, 0, 0)),
    ]
    out_specs = [
        pl.BlockSpec((1, L, xw), lambda b, g: (b, 0, g)),
        pl.BlockSpec((1, SSM_HPG, SSM_P, LANES), lambda b, g: (b, g, 0, 0)),
        pl.BlockSpec((1, SUBLANES, xw), lambda b, g: (b, 0, g)),
        pl.BlockSpec((1, SUBLANES, LANES), lambda b, g: (b, 0, g)),
        pl.BlockSpec((1, SUBLANES, LANES), lambda b, g: (b, 0, g)),
    ]
    f32 = jnp.float32
    out_shape = [
        jax.ShapeDtypeStruct((B, L, BRANCH), jnp.bfloat16),
        jax.ShapeDtypeStruct((B, SSM_HEADS, SSM_P, LANES), f32),
        jax.ShapeDtypeStruct((B, SUBLANES, BRANCH), f32),
        jax.ShapeDtypeStruct((B, SUBLANES, SSM_GROUPS * SSM_STATE), f32),
        jax.ShapeDtypeStruct((B, SUBLANES, SSM_GROUPS * SSM_STATE), f32),
    ]
    scratch = [pltpu.VMEM((L, xw), f32), pltpu.VMEM((L, LANES), f32), pltpu.VMEM((L, LANES), f32),
               pltpu.VMEM((SUBLANES + conv_rows, xw), f32), pltpu.VMEM((SUBLANES + conv_rows, LANES), f32),
               pltpu.VMEM((SUBLANES + conv_rows, LANES), f32),
               pltpu.VMEM((SSM_HPG // 2, 2 * SSM_P, LANES), f32)]
    cbias = conv_b.reshape(1, -1)
    return pl.pallas_call(
        functools.partial(_ssd_kernel, chunk=chunk, l_valid=l_valid, conv_rows=conv_rows),
        grid=(B, SSM_GROUPS), in_specs=in_specs, out_specs=out_specs, out_shape=out_shape, scratch_shapes=scratch,
        compiler_params=_params(2),
        name="ssd",
    )(z, z, z, z, z, conv_w, conv_w, conv_w, cbias, cbias, cbias, conv_state8, conv_state8, conv_state8,
      dtb_pad, alog_pad, dskip_pad, norm_w.reshape(1, -1), s0)


def _s5_kernel(u_ref, bre_ref, bim_ref, cre_ref, cim_ref, are_ref, aim_ref, ldt_ref, d_ref, x0r_ref, x0i_ref,
               h_ref, xr_out_ref, xi_out_ref, bbr_ref, bbi_ref, xr_ref, xi_ref, pwr_ref, pwi_ref,
               *, nseg, seg_len):
    L = u_ref.shape[1]
    n_scan = nseg * seg_len
    row_blk = min(L, 256)
    a_re, a_im = are_ref[...], aim_ref[...]
    dt = jnp.exp(ldt_ref[...])
    mag = jnp.exp(a_re * dt)
    ab_re, ab_im = mag * jnp.cos(a_im * dt), mag * jnp.sin(a_im * dt)
    den = a_re * a_re + a_im * a_im
    nr = ab_re - 1.0
    coef_re = (nr * a_re + ab_im * a_im) / den
    coef_im = (ab_im * a_re - nr * a_im) / den

    NQ = S5_BLK_ST // LANES
    lq = lambda v, q: v[:, LANES * q:LANES * (q + 1)]
    abr = [lq(ab_re, q) for q in range(NQ)]
    abi = [lq(ab_im, q) for q in range(NQ)]

    for r0 in range(0, L, row_blk):
        u = u_ref[0, pl.ds(r0, row_blk), :]
        bu_re, bu_im = _dot(u, bre_ref[0]), _dot(u, bim_ref[0])
        bb_re = coef_re * bu_re - coef_im * bu_im
        bb_im = coef_re * bu_im + coef_im * bu_re
        for q in range(NQ):
            bbr_ref[q, pl.ds(r0, row_blk), :] = lq(bb_re, q)
            bbi_ref[q, pl.ds(r0, row_blk), :] = lq(bb_im, q)

    if n_scan < L:
        for q in range(NQ):
            xr_ref[q, pl.ds(n_scan, L - n_scan), :] = jnp.zeros((L - n_scan, LANES), jnp.float32)
            xi_ref[q, pl.ds(n_scan, L - n_scan), :] = jnp.zeros((L - n_scan, LANES), jnp.float32)

    def rows(i):
        return pl.ds(i, nseg, stride=seg_len) if nseg > 1 else pl.ds(i, 1)

    def scan(i, carry):
        out = []
        for q in range(NQ):
            xr, xi = carry[q]
            pr, pi = _cmul(abr[q], abi[q], xr, xi)
            xr, xi = pr + bbr_ref[q, rows(i), :], pi + bbi_ref[q, rows(i), :]
            xr_ref[q, rows(i), :] = xr
            xi_ref[q, rows(i), :] = xi
            out.append((xr, xi))
        return tuple(out)

    zero = jnp.zeros((nseg, LANES), jnp.float32)
    ends = lax.fori_loop(0, seg_len, scan, tuple((zero, zero) for _ in range(NQ)), unroll=4)

    for q in range(NQ):
        pwr_ref[q, pl.ds(0, 1), :] = abr[q]
        pwi_ref[q, pl.ds(0, 1), :] = abi[q]
    an_r, an_i = ab_re, ab_im
    n = 1
    while n < seg_len:
        m = min(n, seg_len - n)
        for q in range(NQ):
            pr, pi = _cmul(pwr_ref[q, pl.ds(0, m), :], pwi_ref[q, pl.ds(0, m), :], lq(an_r, q), lq(an_i, q))
            pwr_ref[q, pl.ds(n, m), :] = pr
            pwi_ref[q, pl.ds(n, m), :] = pi
        an_r, an_i = _cmul(an_r, an_i, an_r, an_i)
        n *= 2
    seg_r = [pwr_ref[q, pl.ds(seg_len - 1, 1), :] for q in range(NQ)]
    seg_i = [pwi_ref[q, pl.ds(seg_len - 1, 1), :] for q in range(NQ)]

    x0r, x0i = x0r_ref[0], x0i_ref[0]
    dskip = d_ref[...]
    c_r = [lq(x0r, q) for q in range(NQ)]
    c_i = [lq(x0i, q) for q in range(NQ)]
    for k in range(nseg):
        r0 = k * seg_len
        xr_q, xi_q = [], []
        for q in range(NQ):
            dr, di = _cmul(pwr_ref[q], pwi_ref[q], c_r[q], c_i[q])
            xr_q.append(xr_ref[q, pl.ds(r0, seg_len), :] + dr)
            xi_q.append(xi_ref[q, pl.ds(r0, seg_len), :] + di)
            pr, pi = _cmul(seg_r[q], seg_i[q], c_r[q], c_i[q])
            c_r[q], c_i[q] = pr + ends[q][0][k:k + 1], pi + ends[q][1][k:k + 1]
        if k == nseg - 1:
            for q in range(NQ):
                xr_out_ref[0, :, pl.ds(LANES * q, LANES)] = c_r[q]
                xi_out_ref[0, :, pl.ds(LANES * q, LANES)] = c_i[q]
        if n_scan == L:
            u = u_ref[0, pl.ds(r0, seg_len), :]
            y = (_dot(jnp.concatenate(xr_q, axis=-1), cre_ref[0]) - _dot(jnp.concatenate(xi_q, axis=-1), cim_ref[0])
                 + dskip * u)
            h_ref[0, pl.ds(r0, seg_len), :] = _gelu_tanh(y)
        else:
            for q in range(NQ):
                xr_ref[q, pl.ds(r0, seg_len), :] = xr_q[q]
                xi_ref[q, pl.ds(r0, seg_len), :] = xi_q[q]
    if n_scan < L:
        xr = jnp.concatenate([xr_ref[q] for q in range(NQ)], axis=-1)
        xi = jnp.concatenate([xi_ref[q] for q in range(NQ)], axis=-1)
        y = _dot(xr, cre_ref[0]) - _dot(xi, cim_ref[0]) + dskip * u_ref[0]
        h_ref[0] = _gelu_tanh(y)


def _s5_call(z, bblk_re, bblk_im, cblk_re, cblk_im, a_re, a_im, log_dt_exp, d_flat, x0_re, x0_im, *, l_valid):
    B, L, _ = z.shape
    nb = bblk_re.shape[0]
    nseg = SUBLANES if l_valid % (SUBLANES * SUBLANES) == 0 else 1
    seg_len = l_valid // nseg
    vec = lambda w: pl.BlockSpec((1, w), lambda b, j: (0, j))
    st = pl.BlockSpec((1, 1, S5_BLK_ST), lambda b, j: (b, 0, j))
    return pl.pallas_call(
        functools.partial(_s5_kernel, nseg=nseg, seg_len=seg_len),
        grid=(B, nb),
        in_specs=[pl.BlockSpec((1, L, S5_BLK_CH), lambda b, j: (b, 0, OFF_S5_U // S5_BLK_CH + j)),
                  pl.BlockSpec((1, S5_BLK_CH, S5_BLK_ST), lambda b, j: (j, 0, 0)),
                  pl.BlockSpec((1, S5_BLK_CH, S5_BLK_ST), lambda b, j: (j, 0, 0)),
                  pl.BlockSpec((1, S5_BLK_ST, S5_BLK_CH), lambda b, j: (j, 0, 0)),
                  pl.BlockSpec((1, S5_BLK_ST, S5_BLK_CH), lambda b, j: (j, 0, 0)),
                  vec(S5_BLK_ST), vec(S5_BLK_ST), vec(S5_BLK_ST), vec(S5_BLK_CH), st, st],
        out_specs=[pl.BlockSpec((1, L, S5_BLK_CH), lambda b, j: (b, 0, j)), st, st],
        out_shape=[jax.ShapeDtypeStruct((B, L, nb * S5_BLK_CH), jnp.float32),
                   jax.ShapeDtypeStruct((B, 1, nb * S5_BLK_ST), jnp.float32),
                   jax.ShapeDtypeStruct((B, 1, nb * S5_BLK_ST), jnp.float32)],
        scratch_shapes=[pltpu.VMEM((S5_BLK_ST // LANES, L, LANES), jnp.float32)] * 4
        + [pltpu.VMEM((S5_BLK_ST // LANES, seg_len, LANES), jnp.float32)] * 2,
        compiler_params=_params(2),
        name="s5",
    )(z, bblk_re, bblk_im, cblk_re, cblk_im, a_re, a_im, log_dt_exp, d_flat, x0_re, x0_im)


def _s5_block_diag(b_re, b_im, c_re, c_im):
    G, P, Cc = b_re.shape
    nb = G // 8
    same = jnp.eye(8, dtype=bool)

    def bblk(b):
        t = jnp.transpose(b.reshape(nb, 8, P, Cc), (0, 1, 3, 2))[:, :, :, None, :]
        t = jnp.where(same[None, :, None, :, None], t, 0.0)
        return t.reshape(nb, 8 * Cc, 8 * P).astype(jnp.bfloat16)

    def cblk(c):
        t = jnp.transpose(c.reshape(nb, 8, Cc, P), (0, 1, 3, 2))[:, :, :, None, :]
        t = jnp.where(same[None, :, None, :, None], t, 0.0)
        return t.reshape(nb, 8 * P, 8 * Cc).astype(jnp.bfloat16)

    return bblk(b_re), bblk(b_im), cblk(c_re), cblk(c_im)


def _glu_kernel(h_ref, w_ref, b_ref, g_ref, o_ref):
    h = h_ref[...]
    t = _dot(h, w_ref[...]) + b_ref[...]
    o_ref[...] = (h * _sigmoid(t) * _silu(g_ref[...])).astype(o_ref.dtype)


def _glu_call(h2d, glu_w_bf16, glu_b, z2d, *, tm):
    M, W = h2d.shape
    return pl.pallas_call(
        _glu_kernel,
        grid=(M // tm,),
        in_specs=[pl.BlockSpec((tm, W), lambda i: (i, 0)),
                  pl.BlockSpec((W, W), lambda i: (0, 0)),
                  pl.BlockSpec((1, W), lambda i: (0, 0)),
                  pl.BlockSpec((tm, W), lambda i: (i, OFF_S5_G // W))],
        out_specs=pl.BlockSpec((tm, W), lambda i: (i, 0)),
        out_shape=jax.ShapeDtypeStruct((M, W), jnp.bfloat16),
        compiler_params=_params(1),
        name="glu",
    )(h2d, glu_w_bf16, glu_b.reshape(1, W), z2d)


def _kvprep_kernel(kv_ref, sm_ref, knw_ref, k_ref, v_ref, ik_ref, *rest, transposed):
    kvw = KV_HEADS * ATT_D
    kv = kv_ref[0]
    knw = knw_ref[...]
    ks = []
    for h in range(KV_HEADS):
        kh = kv[:, ATT_D * h:ATT_D * (h + 1)]
        ms = jnp.mean(kh * kh, axis=-1, keepdims=True)
        ks.append(kh * lax.rsqrt(ms + EPS) * knw)
    k = jnp.concatenate(ks, axis=-1)
    v = kv[:, kvw:]
    sm = sm_ref[0]
    k_ref[0] = k
    v_ref[0] = v
    ik_ref[0] = sm[:, :IDX_D]
    if transposed:
        kbf_ref, vt_ref, kia_ref, kib_ref = rest
        kbf_ref[0] = k.astype(jnp.bfloat16)
        vt_ref[0] = v.T.astype(jnp.bfloat16)
        lane = lax.broadcasted_iota(jnp.int32, sm.shape, 1)
        kia_ref[0] = jnp.where(lane < IDX_D, sm, 0.0).astype(jnp.bfloat16)
        kib_ref[0] = jnp.where(lane >= IDX_D, pltpu.roll(sm, IDX_D, axis=1), 0.0).astype(jnp.bfloat16)


def _kvprep_call(z, k_norm_w, *, tr, transposed):
    B, L, _ = z.shape
    kvw = KV_HEADS * ATT_D
    f32, bf = jnp.float32, jnp.bfloat16
    rowblk = lambda w: pl.BlockSpec((1, tr, w), lambda b, r: (b, r, 0))
    out_specs = [rowblk(kvw), rowblk(kvw), rowblk(IDX_D)]
    out_shape = [jax.ShapeDtypeStruct((B, L, kvw), f32), jax.ShapeDtypeStruct((B, L, kvw), f32),
                 jax.ShapeDtypeStruct((B, L, IDX_D), f32)]
    if transposed:
        out_specs += [rowblk(kvw), pl.BlockSpec((1, kvw, tr), lambda b, r: (b, 0, r)), rowblk(LANES), rowblk(LANES)]
        out_shape += [jax.ShapeDtypeStruct((B, L, kvw), bf), jax.ShapeDtypeStruct((B, kvw, L), bf),
                      jax.ShapeDtypeStruct((B, L, LANES), bf), jax.ShapeDtypeStruct((B, L, LANES), bf)]
    return pl.pallas_call(
        functools.partial(_kvprep_kernel, transposed=transposed),
        grid=(B, L // tr),
        in_specs=[pl.BlockSpec((1, tr, 2 * kvw), lambda b, r: (b, r, OFF_AT_KV // (2 * kvw))),
                  pl.BlockSpec((1, tr, LANES), lambda b, r: (b, r, OFF_SMALL // LANES)),
                  pl.BlockSpec((1, ATT_D), lambda b, r: (0, 0))],
        out_specs=out_specs, out_shape=out_shape,
        compiler_params=_params(2),
        name="kvprep",
    )(z, z, k_norm_w.reshape(1, ATT_D))


def _sortable_key(score):
    bits = lax.bitcast_convert_type(score, jnp.int32)
    return jnp.where(bits < 0, bits ^ 0x7FFFFFFF, bits)


def _rows_reduce(x, op):
    n = x.shape[0]
    slab = 8 * SUBLANES
    if n > slab and n % slab == 0:
        x = op(x.reshape(n // slab, slab, x.shape[1]), axis=0)
    return op(x, axis=0, keepdims=True)


def _kth_largest_key(key, k, axis):
    shape = list(key.shape)
    shape[axis] = 1

    def it(n, tu):
        cand_u = tu | jnp.left_shift(jnp.int32(1), 31 - n)
        cand_s = cand_u ^ jnp.int32(INT_MIN)
        ones = jnp.where(key >= cand_s, 1.0, 0.0)
        cnt = _rows_reduce(ones, jnp.sum) if axis == 0 else jnp.sum(ones, axis=axis, keepdims=True)
        return jnp.where(cnt >= k, cand_u, tu)

    tu = lax.fori_loop(0, 32, it, jnp.zeros(shape, jnp.int32))
    return tu ^ jnp.int32(INT_MIN)


def _t5_bucket(dist):
    exact = REL_BUCKETS // 2
    d = dist.astype(jnp.float32)
    large = exact + jnp.log(jnp.maximum(d, 1.0) / exact) / math.log(REL_MAX_DIST / exact) * (REL_BUCKETS - exact)
    large = jnp.minimum(jnp.maximum(large, 0.0).astype(jnp.int32), REL_BUCKETS - 1)
    return jnp.where(dist < exact, dist, large)


def _bias_lookup(rel_bias, bucket):
    onehot = (bucket[..., None] == jnp.arange(REL_BUCKETS, dtype=jnp.int32)).astype(jnp.float32)
    return jnp.einsum('...k,kh->...h', onehot, rel_bias.astype(jnp.float32), precision=lax.Precision.HIGHEST)


def _pattn_kernel(zq_ref, zg_ref, ziq_ref, zsm_ref, kbf_ref, vt_ref, kia_ref, kib_ref, qnw_ref, bias_ref, o_ref,
                  *, topk):
    i = pl.program_id(1)
    nqb = kbf_ref.shape[1] // Q_BLK
    qnw = qnw_ref[...]
    scale = ATT_D ** -0.5

    def body(S):
        nkb = S // Q_BLK
        wt = zsm_ref[0].T * (1.0 / 32.0)
        kia, kib = kia_ref[0, pl.ds(0, S), :], kib_ref[0, pl.ds(0, S), :]
        score = jnp.zeros((S, Q_BLK), jnp.float32)
        for p in range(IDX_HEADS // 2):
            qi = ziq_ref[0, :, pl.ds(LANES * p, LANES)].astype(jnp.bfloat16)
            da = lax.dot_general(kia, qi, _NT, preferred_element_type=jnp.float32)
            db = lax.dot_general(kib, qi, _NT, preferred_element_type=jnp.float32)
            r = SMALL_IXW + 2 * p
            score = score + jnp.maximum(da, 0.0) * wt[r:r + 1] + jnp.maximum(db, 0.0) * wt[r + 1:r + 2]
        s_pos = lax.broadcasted_iota(jnp.int32, (S, Q_BLK), 0)
        t_pos = lax.broadcasted_iota(jnp.int32, (S, Q_BLK), 1) + i * Q_BLK
        adm = s_pos <= t_pos
        key = jnp.where(adm, _sortable_key(score), jnp.int32(INT_MIN))
        thr = _kth_largest_key(key, topk, 0)
        sel = (key >= thr) & adm

        for h in range(ATT_HEADS):
            kvh = h // (ATT_HEADS // KV_HEADS)
            q = zq_ref[0, :, pl.ds(ATT_D * h, ATT_D)]
            ms = jnp.mean(q * q, axis=-1, keepdims=True)
            qn = (q * lax.rsqrt(ms + EPS) * qnw).astype(jnp.bfloat16)
            logit = lax.dot_general(kbf_ref[0, pl.ds(0, S), pl.ds(ATT_D * kvh, ATT_D)], qn, _NT,
                                    preferred_element_type=jnp.float32) * scale
            bias = jnp.concatenate([bias_ref[jnp.clip(i - j, 0, N_BIAS_TILES - 1), h] for j in range(nkb)], axis=0)
            x = jnp.where(sel, logit + bias, MASK_VALUE)
            m = _rows_reduce(x, jnp.max)
            pexp = jnp.exp(x - m)
            l = _rows_reduce(pexp, jnp.sum)
            ot = jnp.dot(vt_ref[0, pl.ds(ATT_D * kvh, ATT_D), pl.ds(0, S)], pexp.astype(jnp.bfloat16),
                         preferred_element_type=jnp.float32)
            out = (ot / l).T
            g = zg_ref[0, :, pl.ds(ATT_D * h, ATT_D)]
            o_ref[0, :, pl.ds(ATT_D * h, ATT_D)] = (out * _silu(g)).astype(o_ref.dtype)

    nv = min(N_CAUSAL_VARIANTS, nqb)
    per = nqb // nv
    for c in range(nv):
        pl.when(i // per == c)(functools.partial(body, (c + 1) * per * Q_BLK))


def _pattn_call(z, kbf, vt, kia, kib, q_norm_w, bias_tiles, *, topk):
    B, L, _ = z.shape
    W = ATT_HEADS * ATT_D
    kvw = KV_HEADS * ATT_D
    zblk = lambda off, w: pl.BlockSpec((1, Q_BLK, w), lambda b, i: (b, i, off // w))
    full = lambda shape: pl.BlockSpec((1,) + shape, lambda b, i: (b, 0, 0))
    return pl.pallas_call(
        functools.partial(_pattn_kernel, topk=topk),
        grid=(B, L // Q_BLK),
        in_specs=[zblk(OFF_AT_Q, W), zblk(OFF_AT_G, W), zblk(OFF_IX_Q, W), zblk(OFF_SMALL, LANES),
                  full((L, kvw)), full((kvw, L)), full((L, LANES)), full((L, LANES)),
                  pl.BlockSpec((1, ATT_D), lambda b, i: (0, 0)),
                  pl.BlockSpec((N_BIAS_TILES, ATT_HEADS, Q_BLK, Q_BLK), lambda b, i: (0, 0, 0, 0))],
        out_specs=pl.BlockSpec((1, Q_BLK, W), lambda b, i: (b, i, 0)),
        out_shape=jax.ShapeDtypeStruct((B, L, W), jnp.bfloat16),
        compiler_params=_params(2),
        name="pattn",
    )(z, z, z, z, kbf, vt, kia, kib, q_norm_w.reshape(1, ATT_D), bias_tiles)


def _prompt_bias_tiles(rel_bias):
    o = jnp.arange(N_BIAS_TILES, dtype=jnp.int32)[:, None, None]
    s = jnp.arange(Q_BLK, dtype=jnp.int32)[None, :, None]
    t = jnp.arange(Q_BLK, dtype=jnp.int32)[None, None, :]
    tiles = _bias_lookup(rel_bias, _t5_bucket(jnp.maximum(Q_BLK * o + t - s, 0)))
    return jnp.moveaxis(tiles, -1, 1)


def _sattn_score_kernel(pt_ref, qi_ref, w_ref, iknew_ref, *rest):
    pages, s_ref = rest[:NPG], rest[NPG]
    pc = pl.program_id(1)
    last = pl.num_programs(1) - 1
    qi = qi_ref[0]
    w = w_ref[0]

    def scores(keys_bf16):
        d = lax.dot_general(qi, keys_bf16, _NT, preferred_element_type=jnp.float32)
        r = jnp.maximum(d, 0.0) * w
        return jnp.sum(r.reshape(T_PAD, IDX_HEADS, r.shape[-1]), axis=1)

    @pl.when(pc < last)
    def _():
        for i in range(NPG):
            s_ref[0, :, pl.ds(PAGE * i, PAGE)] = scores(pages[i][0, 0].astype(jnp.bfloat16))

    @pl.when(pc == last)
    def _():
        s_ref[0] = jnp.full((T_PAD, CHUNK_KEYS), MASK_VALUE, jnp.float32)
        s_ref[0, :, pl.ds(0, PAGE)] = scores(iknew_ref[0].astype(jnp.bfloat16))


def _sattn_score_call(page_table, qi_rows, w_col, ik_new, cache_ik, layer):
    B, n_pages = page_table.shape
    n_chunks = n_pages // NPG
    page_spec = lambda i: pl.BlockSpec(
        (1, 1, PAGE, IDX_D), lambda b, pc, pt: (layer, pt[b, jnp.minimum(pc * NPG + i, n_pages - 1)], 0, 0))
    per_b = lambda shape: pl.BlockSpec((1,) + shape, lambda b, pc, pt: (b, 0, 0))
    grid_spec = pltpu.PrefetchScalarGridSpec(
        num_scalar_prefetch=1, grid=(B, n_chunks + 1),
        in_specs=[per_b((T_PAD * IDX_HEADS, IDX_D)), per_b((T_PAD * IDX_HEADS, 1)), per_b((PAGE, IDX_D))]
        + [page_spec(i) for i in range(NPG)],
        out_specs=pl.BlockSpec((1, T_PAD, CHUNK_KEYS), lambda b, pc, pt: (b, 0, pc)))
    return pl.pallas_call(
        _sattn_score_kernel,
        grid_spec=grid_spec,
        out_shape=jax.ShapeDtypeStruct((B, T_PAD, (n_chunks + 1) * CHUNK_KEYS), jnp.float32),
        compiler_params=_params(2),
        name="sattn_score",
    )(page_table, qi_rows, w_col, ik_new, *([cache_ik] * NPG))


def _sattn_kernel(pt_ref, s_ref, q_ref, g_ref, qnw_ref, bias_ref, knew_ref, vnew_ref, *rest, topk, n_new, past):
    kp, vp = rest[:NPG], rest[NPG:2 * NPG]
    o_ref, thr_ref, qn_ref, m_ref, l_ref, acc_ref = rest[2 * NPG:]
    pc = pl.program_id(1)
    last = pl.num_programs(1) - 1
    G = ATT_HEADS // KV_HEADS

    def admissible(col0, width):
        pos = lax.broadcasted_iota(jnp.int32, (T_PAD, width), 1) + col0
        t = lax.broadcasted_iota(jnp.int32, (T_PAD, width), 0)
        return (pos < past) | ((pos - past <= t) & (pos - past < n_new))

    @pl.when(pc == 0)
    def _():
        sc = s_ref[0]
        key = jnp.where(admissible(0, sc.shape[-1]), _sortable_key(sc), jnp.int32(INT_MIN))
        thr_ref[...] = jnp.broadcast_to(_kth_largest_key(key, topk, 1), thr_ref.shape)
        q = q_ref[0]
        ms = jnp.mean(q * q, axis=-1, keepdims=True)
        qn_ref[...] = (q * lax.rsqrt(ms + EPS) * qnw_ref[...]).astype(jnp.bfloat16)
        m_ref[...] = jnp.full(m_ref.shape, MASK_VALUE, jnp.float32)
        l_ref[...] = jnp.zeros(l_ref.shape, jnp.float32)
        acc_ref[...] = jnp.zeros(acc_ref.shape, jnp.float32)

    def attend(get_k, get_v, width):
        col0 = pl.multiple_of(pc * CHUNK_KEYS, CHUNK_KEYS)
        sc = s_ref[0, :, pl.ds(col0, width)]
        adm = admissible(col0, width)
        key = jnp.where(adm, _sortable_key(sc), jnp.int32(INT_MIN))
        sel8 = (key >= thr_ref[:, :1]) & adm
        sel = jnp.concatenate([sel8] * G, axis=0)
        for kv in range(KV_HEADS):
            kk = get_k(kv).astype(jnp.bfloat16)
            x = lax.dot_general(qn_ref[kv], kk, _NT, preferred_element_type=jnp.float32) * (ATT_D ** -0.5)
            x = jnp.where(sel, x + bias_ref[kv, :, pl.ds(0, width)], MASK_VALUE)
            m_old = m_ref[kv]
            m_new = jnp.maximum(m_old, jnp.max(x, axis=-1, keepdims=True))
            p = jnp.where(sel, jnp.exp(x - m_new), 0.0)
            alpha = jnp.exp(m_old - m_new)
            l_ref[kv] = alpha * l_ref[kv] + jnp.sum(p, axis=-1, keepdims=True)
            acc_ref[kv] = alpha * acc_ref[kv] + jnp.dot(
                p.astype(jnp.bfloat16), get_v(kv).astype(jnp.bfloat16),
                preferred_element_type=jnp.float32)
            m_ref[kv] = m_new

    @pl.when(pc < last)
    def _():
        attend(lambda kv: jnp.concatenate([kp[i][0, 0, :, kv, :] for i in range(NPG)], axis=0),
               lambda kv: jnp.concatenate([vp[i][0, 0, :, kv, :] for i in range(NPG)], axis=0), CHUNK_KEYS)

    @pl.when(pc == last)
    def _():
        attend(lambda kv: knew_ref[0, :, pl.ds(ATT_D * kv, ATT_D)],
               lambda kv: vnew_ref[0, :, pl.ds(ATT_D * kv, ATT_D)], PAGE)
        o_ref[0] = acc_ref[...] / l_ref[...] * _silu(g_ref[0])


def _sattn_call(page_table, scores, q_rows, g_rows, q_norm_w, bias, k_new, v_new, cache_k, cache_v, layer,
                *, topk, n_new):
    B, n_pages = page_table.shape
    n_chunks = n_pages // NPG
    kvw = KV_HEADS * ATT_D
    rows = (ATT_HEADS // KV_HEADS) * T_PAD
    n_keys = (n_chunks + 1) * CHUNK_KEYS
    page_spec = lambda i: pl.BlockSpec(
        (1, 1, PAGE, KV_HEADS, ATT_D),
        lambda b, pc, pt: (layer, pt[b, jnp.minimum(pc * NPG + i, n_pages - 1)], 0, 0, 0))
    per_b = lambda shape: pl.BlockSpec((1,) + shape, lambda b, pc, pt: (b,) + (0,) * len(shape))
    grid_spec = pltpu.PrefetchScalarGridSpec(
        num_scalar_prefetch=1, grid=(B, n_chunks + 1),
        in_specs=[per_b((T_PAD, n_keys)), per_b((KV_HEADS, rows, ATT_D)), per_b((KV_HEADS, rows, ATT_D)),
                  pl.BlockSpec((1, ATT_D), lambda b, pc, pt: (0, 0)),
                  pl.BlockSpec((KV_HEADS, rows, CHUNK_KEYS), lambda b, pc, pt: (0, 0, pc)),
                  per_b((PAGE, kvw)), per_b((PAGE, kvw))]
        + [page_spec(i) for i in range(NPG)] + [page_spec(i) for i in range(NPG)],
        out_specs=per_b((KV_HEADS, rows, ATT_D)),
        scratch_shapes=[pltpu.VMEM((T_PAD, LANES), jnp.int32), pltpu.VMEM((KV_HEADS, rows, ATT_D), jnp.bfloat16),
                        pltpu.VMEM((KV_HEADS, rows, 1), jnp.float32), pltpu.VMEM((KV_HEADS, rows, 1), jnp.float32),
                        pltpu.VMEM((KV_HEADS, rows, ATT_D), jnp.float32)])
    return pl.pallas_call(
        functools.partial(_sattn_kernel, topk=topk, n_new=n_new, past=n_pages * PAGE),
        grid_spec=grid_spec,
        out_shape=jax.ShapeDtypeStruct((B, KV_HEADS, rows, ATT_D), jnp.float32),
        compiler_params=_params(2),
        name="sattn",
    )(page_table, scores, q_rows, g_rows, q_norm_w.reshape(1, ATT_D), bias, k_new, v_new,
      *([cache_k] * NPG), *([cache_v] * NPG))


def _sample_bias(rel_bias, past, n_keys):
    G = ATT_HEADS // KV_HEADS
    t = jnp.arange(T_PAD, dtype=jnp.int32)[:, None]
    pos = jnp.arange(n_keys, dtype=jnp.int32)[None, :]
    tab = _bias_lookup(rel_bias, _t5_bucket(jnp.maximum(past + t - pos, 0)))
    return jnp.moveaxis(tab, -1, 0).reshape(KV_HEADS, G * T_PAD, n_keys)


def _to_head_rows(a):
    B = a.shape[0]
    G = ATT_HEADS // KV_HEADS
    a = a.reshape(B, T_PAD, KV_HEADS, G, ATT_D)
    return jnp.transpose(a, (0, 2, 3, 1, 4)).reshape(B, KV_HEADS, G * T_PAD, ATT_D)


def _from_head_rows(a):
    B = a.shape[0]
    G = ATT_HEADS // KV_HEADS
    a = a.reshape(B, KV_HEADS, G, T_PAD, ATT_D)
    return jnp.transpose(a, (0, 3, 1, 2, 4)).reshape(B, T_PAD, ATT_HEADS * ATT_D)


def _layer(x, lp, states, attend, *, l_valid, tm, chunk_hg, chunk_ssd):
    B, L, _ = x.shape
    hg_s0, ssm_s0, conv_s0, s5_re0, s5_im0 = states
    x2d = x.reshape(B * L, D_MODEL)
    z2d = _inproj(x2d, lp['norm_w'], lp['w_in'], tm)
    z = z2d.reshape(B, L, D_IN_PAD)

    o_hg, hg_s = _hgrn_call(z, lp['lb'], lp['hg_norm_w'], hg_s0, chunk=chunk_hg, l_valid=l_valid)

    conv8 = jnp.pad(conv_s0, ((0, 0), (SUBLANES - (SSM_CONV - 1), 0), (0, 0)))
    y_ssm, ssm_s, cx, cb, cc = _ssd_call(z, lp['conv_w'], lp['conv_b'], conv8, lp['dtb_pad'], lp['alog_pad'],
                                         lp['dskip_pad'], lp['ssm_norm_w'], ssm_s0, chunk=chunk_ssd, l_valid=l_valid)
    tail = SUBLANES - (SSM_CONV - 1)
    conv_s = jnp.concatenate([cx[:, tail:], cb[:, tail:], cc[:, tail:]], axis=-1)

    h5, s5_re, s5_im = _s5_call(z, *lp['s5_blocks'], lp['a_re'], lp['a_im'], lp['log_dt'], lp['s5_d'],
                                s5_re0.reshape(B, 1, -1), s5_im0.reshape(B, 1, -1), l_valid=l_valid)
    o5 = _glu_call(h5.reshape(B * L, BRANCH), lp['glu_w'], lp['glu_b'], z2d, tm=tm)

    o_att, k, v, ik = attend(z)

    y = _outproj((o_hg.reshape(B * L, BRANCH), y_ssm.reshape(B * L, BRANCH), o5, o_att.reshape(B * L, BRANCH)),
                 lp['w_out'], x2d, tm)
    st = (k[:, :l_valid].reshape(B, l_valid, KV_HEADS, ATT_D), v[:, :l_valid].reshape(B, l_valid, KV_HEADS, ATT_D),
          ik[:, :l_valid], hg_s, ssm_s, conv_s,
          s5_re.reshape(B, S5_GROUPS, S5_STATE), s5_im.reshape(B, S5_GROUPS, S5_STATE))
    return y.reshape(B, L, D_MODEL), st


def kernel(x_prompt, x_sample, cache_k, cache_v, cache_idx_k, state_hgrn, state_ssm, state_conv, state_s5_re, state_s5_im, page_table, norm_w, w_in, w_out, hg_lb_logits, hg_norm_w, ssm_conv_w, ssm_conv_b, ssm_dt_bias, ssm_a_log, ssm_d, ssm_norm_w, s5_a_re, s5_a_im, s5_log_dt, s5_b_re, s5_b_im, s5_c_re, s5_c_im, s5_d, s5_glu_w, s5_glu_b, att_q_norm, att_k_norm, rel_bias):
    f32 = jnp.float32
    bp, lp_len, _ = x_prompt.shape
    bs, ls, _ = x_sample.shape
    n_pool = cache_k.shape[1]
    n_pages = page_table.shape[1]
    past = n_pages * PAGE
    n_keys = (n_pages // NPG + 1) * CHUNK_KEYS

    sm = jax.nn.softmax(hg_lb_logits.astype(f32), axis=0)
    lower = jnp.cumsum(sm, axis=0) - sm[0]

    bias_tiles = _prompt_bias_tiles(rel_bias)
    bias_sample = _sample_bias(rel_bias, past, n_keys)
    topk_p = min(TOPK_MAX, lp_len // 4)
    topk_s = min(TOPK_MAX, (past + ls) // 4)

    def head_lanes(v):
        return jnp.zeros((1, LANES), f32).at[0, SMALL_DT:SMALL_DT + SSM_HEADS].set(v)

    xs_pad = jnp.pad(x_sample, ((0, 0), (0, SAMPLE_ROWS - ls), (0, 0)))
    pad_page = lambda a: jnp.pad(a, ((0, 0), (0, PAGE - a.shape[1]), (0, 0)))

    yp, ys = x_prompt, xs_pad
    new_p = [[] for _ in range(8)]
    new_s = [[] for _ in range(8)]
    for l in range(DEPTH):
        lp = {'norm_w': norm_w[l], 'w_in': _pack_w_in(w_in[l]), 'w_out': w_out[l].astype(jnp.bfloat16),
              'lb': lower[l], 'hg_norm_w': hg_norm_w[l],
              'conv_w': ssm_conv_w[l], 'conv_b': ssm_conv_b[l], 'dtb_pad': head_lanes(ssm_dt_bias[l]),
              'alog_pad': head_lanes(ssm_a_log[l]), 'dskip_pad': head_lanes(ssm_d[l]), 'ssm_norm_w': ssm_norm_w[l],
              's5_blocks': _s5_block_diag(s5_b_re[l], s5_b_im[l], s5_c_re[l], s5_c_im[l]),
              'a_re': s5_a_re[l].reshape(1, -1), 'a_im': s5_a_im[l].reshape(1, -1),
              'log_dt': jnp.repeat(s5_log_dt[l], S5_STATE).reshape(1, -1), 's5_d': s5_d[l].reshape(1, -1),
              'glu_w': s5_glu_w[l].astype(jnp.bfloat16), 'glu_b': s5_glu_b[l]}

        def attend_prompt(z):
            k, v, ik, kbf, vt, kia, kib = _kvprep_call(z, att_k_norm[l], tr=256, transposed=True)
            o = _pattn_call(z, kbf, vt, kia, kib, att_q_norm[l], bias_tiles, topk=topk_p)
            return o, k, v, ik

        def attend_sample(z):
            k, v, ik = _kvprep_call(z, att_k_norm[l], tr=SAMPLE_ROWS, transposed=False)
            z8 = z[:, :T_PAD]
            qi_rows = z8[..., OFF_IX_Q:OFF_IX_Q + IDX_HEADS * IDX_D].reshape(bs, T_PAD * IDX_HEADS, IDX_D)
            w_col = z8[..., OFF_SMALL + SMALL_IXW:OFF_SMALL + SMALL_IXW + IDX_HEADS] * (1.0 / 32.0)
            scores = _sattn_score_call(page_table, qi_rows.astype(jnp.bfloat16),
                                       w_col.reshape(bs, T_PAD * IDX_HEADS, 1), pad_page(ik), cache_idx_k, l)
            o = _sattn_call(page_table, scores, _to_head_rows(z8[..., OFF_AT_Q:OFF_AT_Q + BRANCH]),
                            _to_head_rows(z8[..., OFF_AT_G:OFF_AT_G + BRANCH]), att_q_norm[l], bias_sample,
                            pad_page(k), pad_page(v), cache_k, cache_v, l, topk=topk_s, n_new=ls)
            o = jnp.pad(_from_head_rows(o), ((0, 0), (0, SAMPLE_ROWS - T_PAD), (0, 0)))
            return o.astype(jnp.bfloat16), k, v, ik

        zero_states = (jnp.zeros((bp, HG_HEADS, LANES, LANES), f32),
                       jnp.zeros((bp, SSM_HEADS, SSM_P, SSM_STATE), f32),
                       jnp.zeros((bp, SSM_CONV - 1, SSM_CONV_DIM), f32),
                       jnp.zeros((bp, S5_GROUPS, S5_STATE), f32),
                       jnp.zeros((bp, S5_GROUPS, S5_STATE), f32))
        yp, st_p = _layer(yp, lp, zero_states, attend_prompt, l_valid=lp_len, tm=512, chunk_hg=64, chunk_ssd=128)
        samp_states = (state_hgrn[l], state_ssm[l], state_conv[l], state_s5_re[l], state_s5_im[l])
        ys, st_s = _layer(ys, lp, samp_states, attend_sample, l_valid=ls, tm=bs * SAMPLE_ROWS,
                          chunk_hg=SAMPLE_ROWS, chunk_ssd=SAMPLE_ROWS)
        for i in range(8):
            new_p[i].append(st_p[i])
            new_s[i].append(st_s[i])
    pk, pv, pik, phg, pssm, pconv, ps5r, ps5i = [jnp.stack(a) for a in new_p]
    sk, sv, sik, shg, sssm, sconv, ss5r, ss5i = [jnp.stack(a) for a in new_s]
    return (yp, ys[:, :ls], pk, pv, pik, phg, pssm, pconv, ps5r, ps5i, sk, sv, sik, shg, sssm, sconv, ss5r, ss5i)
```

```python
import functools
import math

import jax
import jax.numpy as jnp
from jax import lax
from jax.experimental import pallas as pl
from jax.experimental.pallas import tpu as pltpu

D_MODEL = 4096
DEPTH = 2
BRANCH = D_MODEL // 4
HG_HEADS = 8
SSM_HEADS = 16
SSM_GROUPS = 4
SSM_STATE = 128
SSM_CONV = 4
SSM_CONV_DIM = BRANCH + 2 * SSM_GROUPS * SSM_STATE
SSM_HPG = SSM_HEADS // SSM_GROUPS
SSM_P = 64
S5_GROUPS = 64
S5_STATE = 64
S5_BLK_CH = 128
S5_BLK_ST = 512
ATT_D = 128
ATT_HEADS = 8
KV_HEADS = 2
IDX_HEADS = 16
IDX_D = 64
TOPK_MAX = 256
Q_BLK = 128
PAGE = 128
REL_BUCKETS = 32
REL_MAX_DIST = 1024
EPS = 1e-6
MASK_VALUE = -1e30
F_FLOOR = 1e-30
INT_MIN = -2 ** 31

LANES = 128
SUBLANES = 8
V7X_VMEM_LIMIT = 56 * 1024 * 1024

_SRC_DT, _SRC_S5U, _SRC_ATK, _SRC_ATG, _SRC_IXK, D_IN_SRC = 7168, 7184, 10256, 10768, 12816, 12896

OFF_HG_Q, OFF_HG_F, OFF_HG_I, OFF_HG_G = 0, 1024, 2048, 3072
OFF_SM_Z, OFF_SM_XBC = 4096, 5120
OFF_S5_U, OFF_S5_G = 7168, 8192
OFF_AT_Q, OFF_AT_G, OFF_IX_Q = 9216, 10240, 11264
OFF_AT_KV = 12288
OFF_SMALL = 12800
SMALL_IXW, SMALL_DT = 64, 80
D_IN_PAD = 13312

PACK_RT = 512
PACK_TN = 1024
PACK_WIN = PACK_TN // LANES + 1
_PACK_SHIFT = _SRC_S5U - _SRC_DT
_PACK_ALIGNED_TILES = _SRC_DT // PACK_TN
_PACK_LAST = D_IN_PAD // PACK_TN - 1
S5_BULK_STEPS = 32

N_BIAS_TILES = 9
HG_SUB = 16
HG_HPB = 4
HG_ROW_BLOCK = 512
N_CAUSAL_VARIANTS = 4
NPG = 8
T_PAD = 8
CHUNK_KEYS = NPG * PAGE
SAMPLE_ROWS = 16

_TN = (((0,), (0,)), ((), ()))
_NT = (((1,), (1,)), ((), ()))


def _params(n_axes):
    return pltpu.CompilerParams(dimension_semantics=("arbitrary",) * n_axes, vmem_limit_bytes=V7X_VMEM_LIMIT)


def _bf(x):
    return x.astype(jnp.bfloat16)


def _dot(a, b, dims=None):
    if dims is None:
        return jnp.dot(_bf(a), _bf(b), preferred_element_type=jnp.float32)
    return lax.dot_general(_bf(a), _bf(b), dims, preferred_element_type=jnp.float32)


def _sigmoid(x):
    return 1.0 / (1.0 + jnp.exp(-x))


def _silu(x):
    return x * _sigmoid(x)


def _softplus(x):
    return jnp.maximum(x, 0.0) + jnp.log(1.0 + jnp.exp(-jnp.abs(x)))


def _gelu_tanh(x):
    return 0.5 * x * (1.0 + jnp.tanh(math.sqrt(2.0 / math.pi) * (x + 0.044715 * (x * x * x))))


def _cumsum_rows(x):
    n = x.shape[0]
    row = lax.broadcasted_iota(jnp.int32, x.shape, 0)
    sh = 1
    while sh < n:
        x = x + jnp.where(row >= sh, pltpu.roll(x, sh, axis=0), 0.0)
        sh *= 2
    return x


def _split3(x):
    hi = x.astype(jnp.bfloat16)
    r = x - hi.astype(jnp.float32)
    mid = r.astype(jnp.bfloat16)
    lo = (r - mid.astype(jnp.float32)).astype(jnp.bfloat16)
    return hi, mid, lo


def _select_lanes_as_rows(sel, x):
    out = None
    for part in _split3(x):
        t = lax.dot_general(sel, part, _NT, preferred_element_type=jnp.float32)
        out = t if out is None else out + t
    return out


def _cmul(ar, ai, br, bi):
    return ar * br - ai * bi, ar * bi + ai * br


def _pack_src_block(j):
    return jnp.where(j < 10, 8 * j, jnp.where(j < _PACK_LAST, 8 * j + 4, _SRC_ATK // LANES))


def _pack_kernel(*refs):
    win, small_ref, dt_ref, o_ref = refs[:PACK_WIN], refs[PACK_WIN], refs[PACK_WIN + 1], refs[PACK_WIN + 2]
    j = pl.program_id(1)
    lane = lax.broadcasted_iota(jnp.int32, (PACK_RT, LANES), 1)

    def shifted(c):
        a = pltpu.roll(win[c][0], LANES - _PACK_SHIFT, axis=1)
        b = pltpu.roll(win[c + 1][0], LANES - _PACK_SHIFT, axis=1)
        return jnp.where(lane < LANES - _PACK_SHIFT, a, b)

    @pl.when(j < _PACK_ALIGNED_TILES)
    def _():
        for c in range(PACK_TN // LANES):
            o_ref[:, pl.ds(LANES * c, LANES)] = win[c][0].astype(o_ref.dtype)

    @pl.when((j >= _PACK_ALIGNED_TILES) & (j < _PACK_LAST))
    def _():
        for c in range(PACK_TN // LANES):
            o_ref[:, pl.ds(LANES * c, LANES)] = shifted(c).astype(o_ref.dtype)

    @pl.when(j == _PACK_LAST)
    def _():
        n_kv = 2 * KV_HEADS * ATT_D // LANES
        for c in range(n_kv):
            o_ref[:, pl.ds(LANES * c, LANES)] = shifted(c).astype(o_ref.dtype)
        ixk = pltpu.roll(small_ref[0], LANES - _PACK_SHIFT, axis=1)
        dt = pltpu.roll(dt_ref[0], SMALL_DT, axis=1)
        small = jnp.where(lane < SMALL_DT, ixk, jnp.where(lane < SMALL_DT + SSM_HEADS, dt, 0.0))
        o_ref[:, pl.ds(LANES * n_kv, LANES)] = small.astype(o_ref.dtype)
        rest = PACK_TN - LANES * (n_kv + 1)
        o_ref[:, pl.ds(LANES * (n_kv + 1), rest)] = jnp.zeros((PACK_RT, rest), o_ref.dtype)


def _pack_w_in(w_in, layer):
    wspec = lambda m: pl.BlockSpec((1, PACK_RT, LANES), lambda r, j: (layer, r, _pack_src_block(j) + m))
    fixed = lambda col: pl.BlockSpec((1, PACK_RT, LANES), lambda r, j: (layer, r, col // LANES))
    return pl.pallas_call(
        _pack_kernel,
        grid=(D_MODEL // PACK_RT, D_IN_PAD // PACK_TN),
        in_specs=[wspec(m) for m in range(PACK_WIN)] + [fixed(_SRC_IXK), fixed(_SRC_DT)],
        out_specs=pl.BlockSpec((PACK_RT, PACK_TN), lambda r, j: (r, j)),
        out_shape=jax.ShapeDtypeStruct((D_MODEL, D_IN_PAD), jnp.bfloat16),
        compiler_params=_params(2),
        name="pack_w_in",
    )(*([w_in] * (PACK_WIN + 2)))


def _inproj_kernel(x_ref, nw_ref, w_ref, z_ref, hn_ref):
    @pl.when(pl.program_id(1) == 0)
    def _():
        xf = x_ref[...]
        ms = jnp.mean(xf * xf, axis=-1, keepdims=True)
        hn_ref[...] = (xf * lax.rsqrt(ms + EPS) * nw_ref[...]).astype(jnp.bfloat16)

    z_ref[...] = jnp.dot(hn_ref[...], w_ref[...], preferred_element_type=jnp.float32)


def _inproj(x2d, norm_w, w_packed, tm, tn=1024):
    m = x2d.shape[0]
    return pl.pallas_call(
        _inproj_kernel,
        grid=(m // tm, D_IN_PAD // tn),
        in_specs=[pl.BlockSpec((tm, D_MODEL), lambda i, j: (i, 0)),
                  pl.BlockSpec((1, D_MODEL), lambda i, j: (0, 0)),
                  pl.BlockSpec((D_MODEL, tn), lambda i, j: (0, j))],
        out_specs=pl.BlockSpec((tm, tn), lambda i, j: (i, j)),
        out_shape=jax.ShapeDtypeStruct((m, D_IN_PAD), jnp.float32),
        scratch_shapes=[pltpu.VMEM((tm, D_MODEL), jnp.bfloat16)],
        compiler_params=_params(2),
        name="inproj",
    )(x2d, norm_w.reshape(1, D_MODEL), w_packed)


def _outproj_kernel(m0_ref, m1_ref, m2_ref, m3_ref, w_ref, x_ref, y_ref):
    acc = x_ref[...]
    for i, m_ref in enumerate((m0_ref, m1_ref, m2_ref, m3_ref)):
        acc = acc + jnp.dot(m_ref[...], w_ref[pl.ds(BRANCH * i, BRANCH), :], preferred_element_type=jnp.float32)
    y_ref[...] = acc


def _outproj(mixed4, w_out_bf16, layer, x2d, tm, tn=1024):
    m = x2d.shape[0]
    mspec = pl.BlockSpec((tm, BRANCH), lambda i, j: (i, 0))
    return pl.pallas_call(
        _outproj_kernel,
        grid=(m // tm, D_MODEL // tn),
        in_specs=[mspec, mspec, mspec, mspec,
                  pl.BlockSpec((None, D_MODEL, tn), lambda i, j: (layer, 0, j)),
                  pl.BlockSpec((tm, tn), lambda i, j: (i, j))],
        out_specs=pl.BlockSpec((tm, tn), lambda i, j: (i, j)),
        out_shape=jax.ShapeDtypeStruct((m, D_MODEL), jnp.float32),
        compiler_params=_params(2),
        name="outproj",
    )(*mixed4, w_out_bf16, x2d)


def _hgrn_kernel(q_ref, f_ref, i_ref, g_ref, lb_ref, nw_ref, s0_ref, o_ref, s_out_ref, st_ref, *, chunk, l_valid):
    LB = q_ref.shape[1]
    C = chunk
    nblk = C // SUBLANES
    lb_i = pl.program_id(2)

    @pl.when(lb_i == 0)
    def _():
        for hh in range(HG_HPB):
            st_ref[hh] = s0_ref[0, hh].T

    nw = nw_ref[...]
    rowi = lax.broadcasted_iota(jnp.int32, (SUBLANES, LANES), 0)

    def head_chunk(hh, t0):
        lanes = pl.ds(LANES * hh, LANES)
        lb = lb_ref[0, :, lanes]
        fp = f_ref[0, pl.ds(t0, C), lanes]
        qp = q_ref[0, pl.ds(t0, C), lanes]
        v = i_ref[0, pl.ds(t0, C), lanes]
        gp = g_ref[0, pl.ds(t0, C), lanes]
        fg = lb + (1.0 - lb) * _sigmoid(fp)
        logf = jnp.log(jnp.maximum(fg, F_FLOOR))
        kk = (1.0 - lb) * _sigmoid(-fp)
        if l_valid < LB * pl.num_programs(2):
            rows = lax.broadcasted_iota(jnp.int32, (C, LANES), 0) + t0 + lb_i * LB
            logf = jnp.where(rows < l_valid, logf, 0.0)
            kk = jnp.where(rows < l_valid, kk, 0.0)
        qq = _silu(qp)
        G = _cumsum_rows(logf)
        st = st_ref[hh]
        o_inter = _dot(qq * jnp.exp(G), st, _NT)
        acc = [None] * nblk
        qb = [qq[SUBLANES * tb:SUBLANES * (tb + 1)] for tb in range(nblk)]
        Gb = [G[SUBLANES * tb:SUBLANES * (tb + 1)] for tb in range(nblk)]
        for s in range(C):
            sb = s // SUBLANES
            gs = G[s:s + 1]
            ks = kk[s:s + 1]
            vs = v[s:s + 1]
            for tb in range(sb, (s // HG_SUB + 1) * (HG_SUB // SUBLANES)):
                d = Gb[tb] - gs
                if tb == sb:
                    d = jnp.where(rowi >= s - SUBLANES * sb, d, MASK_VALUE)
                w = jnp.sum(qb[tb] * ks * jnp.exp(d), axis=-1, keepdims=True)
                contrib = w * vs
                acc[tb] = contrib if acc[tb] is None else acc[tb] + contrib
        for j in range(1, C // HG_SUB):
            r0 = HG_SUB * j
            gb = G[r0 - 1:r0]
            qj = qq[r0:r0 + HG_SUB] * jnp.exp(G[r0:r0 + HG_SUB] - gb)
            kj = kk[:r0] * jnp.exp(gb - G[:r0])
            oj = _dot(_dot(qj, kj, _NT), v[:r0])
            for half in range(HG_SUB // SUBLANES):
                tb = r0 // SUBLANES + half
                acc[tb] = acc[tb] + oj[SUBLANES * half:SUBLANES * (half + 1)]
        o = o_inter + jnp.concatenate(acc, axis=0)
        g_last = G[C - 1:C]
        kd = kk * jnp.exp(g_last - G)
        st_ref[hh] = jnp.exp(g_last) * st + _dot(v, kd, _TN)
        ms = jnp.mean(o * o, axis=-1, keepdims=True)
        on = o * lax.rsqrt(ms + EPS) * nw
        o_ref[0, pl.ds(t0, C), lanes] = (on * _silu(gp)).astype(o_ref.dtype)

    def body(c, carry):
        t0 = pl.multiple_of(c * C, C)
        for hh in range(HG_HPB):
            head_chunk(hh, t0)
        return carry

    lax.fori_loop(0, LB // C, body, 0)

    @pl.when(lb_i == pl.num_programs(2) - 1)
    def _():
        for hh in range(HG_HPB):
            s_out_ref[0, hh] = st_ref[hh].T


def _hgrn_call(z, lb, hg_norm_w, s0, *, chunk, l_valid):
    B, L, _ = z.shape
    W = HG_HPB * LANES
    LB = min(L, HG_ROW_BLOCK)
    zspec = lambda off: pl.BlockSpec((1, LB, W), lambda b, h, r: (b, r, off // W + h))
    return pl.pallas_call(
        functools.partial(_hgrn_kernel, chunk=chunk, l_valid=l_valid),
        grid=(B, HG_HEADS // HG_HPB, L // LB),
        in_specs=[zspec(OFF_HG_Q), zspec(OFF_HG_F), zspec(OFF_HG_I), zspec(OFF_HG_G),
                  pl.BlockSpec((1, 1, W), lambda b, h, r: (h, 0, 0)),
                  pl.BlockSpec((1, LANES), lambda b, h, r: (0, 0)),
                  pl.BlockSpec((1, HG_HPB, LANES, LANES), lambda b, h, r: (b, h, 0, 0))],
        out_specs=[pl.BlockSpec((1, LB, W), lambda b, h, r: (b, r, h)),
                   pl.BlockSpec((1, HG_HPB, LANES, LANES), lambda b, h, r: (b, h, 0, 0))],
        out_shape=[jax.ShapeDtypeStruct((B, L, BRANCH), jnp.bfloat16),
                   jax.ShapeDtypeStruct((B, HG_HEADS, LANES, LANES), jnp.float32)],
        scratch_shapes=[pltpu.VMEM((HG_HPB, LANES, LANES), jnp.float32)],
        compiler_params=_params(3),
        name="hgrn",
    )(z, z, z, z, lb.reshape(HG_HEADS // HG_HPB, 1, W), hg_norm_w.reshape(1, LANES), s0)


def _ssd_kernel(xs_ref, b_ref, c_ref, zg_ref, sm_ref, wx_ref, wb_ref, wc_ref, bx_ref, bb_ref, bc_ref,
                cx_ref, cb_ref, cc_ref, dtb_ref, alog_ref, dskip_ref, nw_ref, s0_ref,
                y_ref, s_out_ref, ox_ref, ob_ref, oc_ref,
                ax_ref, ab_ref, ac_ref, hx_ref, hb_ref, hc_ref, sp_ref, *, chunk, l_valid, conv_rows):
    L = xs_ref.shape[1]
    C = chunk
    R = conv_rows
    g = pl.program_id(1)

    def conv(src_ref, head_ref, cst_ref, w_ref, bias_ref, act_ref, out_state_ref):
        head_ref[pl.ds(0, SUBLANES), :] = cst_ref[0]
        head_ref[pl.ds(SUBLANES, R), :] = src_ref[0, pl.ds(0, R), :]
        w = w_ref[...]
        for r0 in range(0, L, R):
            acc = bias_ref[...]
            for j in range(SSM_CONV):
                if r0 == 0:
                    xj = head_ref[pl.ds(SUBLANES - j, R), :]
                else:
                    xj = src_ref[0, pl.ds(r0 - j, R), :]
                acc = acc + xj * w[SSM_CONV - 1 - j:SSM_CONV - j]
            act_ref[pl.ds(r0, R), :] = _silu(acc)
        if l_valid >= SUBLANES:
            out_state_ref[0] = src_ref[0, pl.ds(l_valid - SUBLANES, SUBLANES), :]
        else:
            out_state_ref[0] = head_ref[pl.ds(l_valid, SUBLANES), :]

    conv(xs_ref, hx_ref, cx_ref, wx_ref, bx_ref, ax_ref, ox_ref)
    conv(b_ref, hb_ref, cb_ref, wb_ref, bb_ref, ab_ref, ob_ref)
    conv(c_ref, hc_ref, cc_ref, wc_ref, bc_ref, ac_ref, oc_ref)

    for p in range(SSM_HPG // 2):
        sp_ref[p] = s0_ref[0, 2 * p:2 * p + 2].reshape(2 * SSM_P, LANES)

    lane1 = lax.broadcasted_iota(jnp.int32, (1, LANES), 1)
    lane8 = lax.broadcasted_iota(jnp.int32, (SUBLANES, LANES), 1)
    row8 = lax.broadcasted_iota(jnp.int32, (SUBLANES, LANES), 0)
    lane0 = SMALL_DT + SSM_HPG * g
    sel = jnp.where((lane8 == lane0 + row8) & (row8 < SSM_HPG), 1.0, 0.0).astype(jnp.bfloat16)
    a_all = -jnp.exp(alog_ref[...])
    dskip = [jnp.sum(jnp.where(lane1 == lane0 + j, dskip_ref[...], 0.0), axis=-1, keepdims=True)
             for j in range(SSM_HPG)]
    lane_c = lax.broadcasted_iota(jnp.int32, (C, LANES), 1)
    first_half = lane_c < SSM_P
    row_p = lax.broadcasted_iota(jnp.int32, (2 * SSM_P, LANES), 0) < SSM_P
    tril = lax.broadcasted_iota(jnp.int32, (C, C), 0) >= lax.broadcasted_iota(jnp.int32, (C, C), 1)
    nw = nw_ref[...]

    def body(c, carry):
        t0 = pl.multiple_of(c * C, C)
        sm = sm_ref[0, pl.ds(t0, C), :]
        dt_all = _softplus(sm + dtb_ref[...])
        if l_valid < L:
            rows = lax.broadcasted_iota(jnp.int32, (C, LANES), 0) + t0
            dt_all = jnp.where(rows < l_valid, dt_all, 0.0)
        cum = _cumsum_rows(dt_all * a_all)
        cum_rows = _select_lanes_as_rows(sel, cum)
        col = [jnp.sum(jnp.where(lane_c == lane0 + j, cum, 0.0), axis=-1, keepdims=True) for j in range(SSM_HPG)]
        dtc = [jnp.sum(jnp.where(lane_c == lane0 + j, dt_all, 0.0), axis=-1, keepdims=True) for j in range(SSM_HPG)]
        bact = ab_ref[pl.ds(t0, C), :]
        cact = ac_ref[pl.ds(t0, C), :]
        cb = _dot(cact, bact, _NT)
        ys = []
        for p in range(SSM_HPG // 2):
            ja, jb = 2 * p, 2 * p + 1
            xs = ax_ref[pl.ds(t0, C), pl.ds(2 * SSM_P * p, 2 * SSM_P)]
            xdt = xs * jnp.where(first_half, dtc[ja], dtc[jb])
            dec_a = jnp.exp(jnp.where(tril, col[ja] - cum_rows[ja:ja + 1], MASK_VALUE))
            dec_b = jnp.exp(jnp.where(tril, col[jb] - cum_rows[jb:jb + 1], MASK_VALUE))
            y = jnp.where(first_half, _dot(cb * dec_a, xdt), _dot(cb * dec_b, xdt))
            sp = sp_ref[p]
            y = y + _dot(cact, sp, _NT) * jnp.where(first_half, jnp.exp(col[ja]), jnp.exp(col[jb]))
            y = y + jnp.where(first_half, dskip[ja], dskip[jb]) * xs
            last_a, last_b = col[ja][C - 1:C], col[jb][C - 1:C]
            xw = xdt * jnp.where(first_half, jnp.exp(last_a - col[ja]), jnp.exp(last_b - col[jb]))
            sp_ref[p] = jnp.where(row_p, jnp.exp(last_a), jnp.exp(last_b)) * sp + _dot(xw, bact, _TN)
            zg = zg_ref[0, pl.ds(t0, C), pl.ds(2 * SSM_P * p, 2 * SSM_P)]
            ys.append(y * _silu(zg))
        ms = sum(jnp.sum(y * y, axis=-1, keepdims=True) for y in ys) * (1.0 / (SSM_HPG * SSM_P))
        inv = lax.rsqrt(ms + EPS)
        for p in range(SSM_HPG // 2):
            y_ref[0, pl.ds(t0, C), pl.ds(2 * SSM_P * p, 2 * SSM_P)] = (
                ys[p] * inv * nw[:, 2 * SSM_P * p:2 * SSM_P * (p + 1)]).astype(y_ref.dtype)
        return carry

    lax.fori_loop(0, L // C, body, 0)
    for p in range(SSM_HPG // 2):
        s_out_ref[0, 2 * p:2 * p + 2] = sp_ref[p].reshape(2, SSM_P, LANES)


def _ssd_call(z, conv_w, conv_b, conv_state8, dtb_pad, alog_pad, dskip_pad, norm_w, s0, *, chunk, l_valid):
    B, L, _ = z.shape
    assert l_valid >= SSM_CONV - 1
    xw = SSM_HPG * SSM_P
    conv_rows = min(L, 256)
    ox, ob = OFF_SM_XBC // xw, (OFF_SM_XBC + BRANCH) // LANES
    oc = (OFF_SM_XBC + BRANCH + SSM_GROUPS * SSM_STATE) // LANES
    wb0, wc0 = BRANCH // LANES, (BRANCH + SSM_GROUPS * SSM_STATE) // LANES
    in_specs = [
        pl.BlockSpec((1, L, xw), lambda b, g: (b, 0, ox + g)),
        pl.BlockSpec((1, L, LANES), lambda b, g: (b, 0, ob + g)),
        pl.BlockSpec((1, L, LANES), lambda b, g: (b, 0, oc + g)),
        pl.BlockSpec((1, L, xw), lambda b, g: (b, 0, OFF_SM_Z // xw + g)),
        pl.BlockSpec((1, L, LANES), lambda b, g: (b, 0, OFF_SMALL // LANES)),
        pl.BlockSpec((SSM_CONV, xw), lambda b, g: (0, g)),
        pl.BlockSpec((SSM_CONV, LANES), lambda b, g: (0, wb0 + g)),
        pl.BlockSpec((SSM_CONV, LANES), lambda b, g: (0, wc0 + g)),
        pl.BlockSpec((1, xw), lambda b, g: (0, g)),
        pl.BlockSpec((1, LANES), lambda b, g: (0, wb0 + g)),
        pl.BlockSpec((1, LANES), lambda b, g: (0, wc0 + g)),
        pl.BlockSpec((1, SUBLANES, xw), lambda b, g: (b, 0, g)),
        pl.BlockSpec((1, SUBLANES, LANES), lambda b, g: (b, 0, wb0 + g)),
        pl.BlockSpec((1, SUBLANES, LANES), lambda b, g: (b, 0, wc0 + g)),
        pl.BlockSpec((1, LANES), lambda b, g: (0, 0)),
        pl.BlockSpec((1, LANES), lambda b, g: (0, 0)),
        pl.BlockSpec((1, LANES), lambda b, g: (0, 0)),
        pl.BlockSpec((1, xw), lambda b, g: (0, g)),
        pl.BlockSpec((1, SSM_HPG, SSM_P, LANES), lambda b, g: (b, g, 0, 0)),
    ]
    out_specs = [
        pl.BlockSpec((1, L, xw), lambda b, g: (b, 0, g)),
        pl.BlockSpec((1, SSM_HPG, SSM_P, LANES), lambda b, g: (b, g, 0, 0)),
        pl.BlockSpec((1, SUBLANES, xw), lambda b, g: (b, 0, g)),
        pl.BlockSpec((1, SUBLANES, LANES), lambda b, g: (b, 0, g)),
        pl.BlockSpec((1, SUBLANES, LANES), lambda b, g: (b, 0, g)),
    ]
    f32 = jnp.float32
    out_shape = [
        jax.ShapeDtypeStruct((B, L, BRANCH), jnp.bfloat16),
        jax.ShapeDtypeStruct((B, SSM_HEADS, SSM_P, LANES), f32),
        jax.ShapeDtypeStruct((B, SUBLANES, BRANCH), f32),
        jax.ShapeDtypeStruct((B, SUBLANES, SSM_GROUPS * SSM_STATE), f32),
        jax.ShapeDtypeStruct((B, SUBLANES, SSM_GROUPS * SSM_STATE), f32),
    ]
    scratch = [pltpu.VMEM((L, xw), f32), pltpu.VMEM((L, LANES), f32), pltpu.VMEM((L, LANES), f32),
               pltpu.VMEM((SUBLANES + conv_rows, xw), f32), pltpu.VMEM((SUBLANES + conv_rows, LANES), f32),
               pltpu.VMEM((SUBLANES + conv_rows, LANES), f32),
               pltpu.VMEM((SSM_HPG // 2, 2 * SSM_P, LANES), f32)]
    cbias = conv_b.reshape(1, -1)
    return pl.pallas_call(
        functools.partial(_ssd_kernel, chunk=chunk, l_valid=l_valid, conv_rows=conv_rows),
        grid=(B, SSM_GROUPS), in_specs=in_specs, out_specs=out_specs, out_shape=out_shape, scratch_shapes=scratch,
        compiler_params=_params(2),
        name="ssd",
    )(z, z, z, z, z, conv_w, conv_w, conv_w, cbias, cbias, cbias, conv_state8, conv_state8, conv_state8,
      dtb_pad, alog_pad, dskip_pad, norm_w.reshape(1, -1), s0)


def _s5_kernel(u_ref, bre_ref, bim_ref, cre_ref, cim_ref, are_ref, aim_ref, ldt_ref, d_ref, x0r_ref, x0i_ref,
               h_ref, xr_out_ref, xi_out_ref, bbr_ref, bbi_ref, xr_ref, xi_ref, pwr_ref, pwi_ref,
               *, nseg, seg_len):
    L = u_ref.shape[1]
    n_scan = nseg * seg_len
    row_blk = min(L, 256)
    a_re, a_im = are_ref[...], aim_ref[...]
    dt = jnp.exp(ldt_ref[...])
    mag = jnp.exp(a_re * dt)
    ab_re, ab_im = mag * jnp.cos(a_im * dt), mag * jnp.sin(a_im * dt)
    den = a_re * a_re + a_im * a_im
    nr = ab_re - 1.0
    coef_re = (nr * a_re + ab_im * a_im) / den
    coef_im = (ab_im * a_re - nr * a_im) / den

    NQ = S5_BLK_ST // LANES
    lq = lambda v, q: v[:, LANES * q:LANES * (q + 1)]
    abr = [lq(ab_re, q) for q in range(NQ)]
    abi = [lq(ab_im, q) for q in range(NQ)]

    for r0 in range(0, L, row_blk):
        u = u_ref[0, pl.ds(r0, row_blk), :]
        bu_re, bu_im = _dot(u, bre_ref[0]), _dot(u, bim_ref[0])
        bb_re = coef_re * bu_re - coef_im * bu_im
        bb_im = coef_re * bu_im + coef_im * bu_re
        for q in range(NQ):
            bbr_ref[q, pl.ds(r0, row_blk), :] = lq(bb_re, q)
            bbi_ref[q, pl.ds(r0, row_blk), :] = lq(bb_im, q)

    if n_scan < L:
        for q in range(NQ):
            xr_ref[q, pl.ds(n_scan, L - n_scan), :] = jnp.zeros((L - n_scan, LANES), jnp.float32)
            xi_ref[q, pl.ds(n_scan, L - n_scan), :] = jnp.zeros((L - n_scan, LANES), jnp.float32)

    def rows(i):
        return pl.ds(i, nseg, stride=seg_len) if nseg > 1 else pl.ds(i, 1)

    def scan(i, carry):
        out = []
        for q in range(NQ):
            xr, xi = carry[q]
            pr, pi = _cmul(abr[q], abi[q], xr, xi)
            xr, xi = pr + bbr_ref[q, rows(i), :], pi + bbi_ref[q, rows(i), :]
            xr_ref[q, rows(i), :] = xr
            xi_ref[q, rows(i), :] = xi
            out.append((xr, xi))
        return tuple(out)

    zero = jnp.zeros((nseg, LANES), jnp.float32)
    ends = lax.fori_loop(0, seg_len, scan, tuple((zero, zero) for _ in range(NQ)), unroll=4)

    for q in range(NQ):
        pwr_ref[q, pl.ds(0, 1), :] = abr[q]
        pwi_ref[q, pl.ds(0, 1), :] = abi[q]
    an_r, an_i = ab_re, ab_im
    n = 1
    while n < seg_len:
        m = min(n, seg_len - n)
        for q in range(NQ):
            pr, pi = _cmul(pwr_ref[q, pl.ds(0, m), :], pwi_ref[q, pl.ds(0, m), :], lq(an_r, q), lq(an_i, q))
            pwr_ref[q, pl.ds(n, m), :] = pr
            pwi_ref[q, pl.ds(n, m), :] = pi
        an_r, an_i = _cmul(an_r, an_i, an_r, an_i)
        n *= 2
    seg_r = [pwr_ref[q, pl.ds(seg_len - 1, 1), :] for q in range(NQ)]
    seg_i = [pwi_ref[q, pl.ds(seg_len - 1, 1), :] for q in range(NQ)]

    x0r, x0i = x0r_ref[0], x0i_ref[0]
    dskip = d_ref[...]
    c_r = [lq(x0r, q) for q in range(NQ)]
    c_i = [lq(x0i, q) for q in range(NQ)]
    for k in range(nseg):
        r0 = k * seg_len
        xr_q, xi_q = [], []
        for q in range(NQ):
            dr, di = _cmul(pwr_ref[q], pwi_ref[q], c_r[q], c_i[q])
            xr_q.append(xr_ref[q, pl.ds(r0, seg_len), :] + dr)
            xi_q.append(xi_ref[q, pl.ds(r0, seg_len), :] + di)
            pr, pi = _cmul(seg_r[q], seg_i[q], c_r[q], c_i[q])
            c_r[q], c_i[q] = pr + ends[q][0][k:k + 1], pi + ends[q][1][k:k + 1]
        if k == nseg - 1:
            for q in range(NQ):
                xr_out_ref[0, :, pl.ds(LANES * q, LANES)] = c_r[q]
                xi_out_ref[0, :, pl.ds(LANES * q, LANES)] = c_i[q]
        if n_scan == L:
            u = u_ref[0, pl.ds(r0, seg_len), :]
            y = (_dot(jnp.concatenate(xr_q, axis=-1), cre_ref[0]) - _dot(jnp.concatenate(xi_q, axis=-1), cim_ref[0])
                 + dskip * u)
            h_ref[0, pl.ds(r0, seg_len), :] = _gelu_tanh(y)
        else:
            for q in range(NQ):
                xr_ref[q, pl.ds(r0, seg_len), :] = xr_q[q]
                xi_ref[q, pl.ds(r0, seg_len), :] = xi_q[q]
    if n_scan < L:
        xr = jnp.concatenate([xr_ref[q] for q in range(NQ)], axis=-1)
        xi = jnp.concatenate([xi_ref[q] for q in range(NQ)], axis=-1)
        y = _dot(xr, cre_ref[0]) - _dot(xi, cim_ref[0]) + dskip * u_ref[0]
        h_ref[0] = _gelu_tanh(y)


def _s5seg_kernel(u_ref, bre_ref, bim_ref, cre_ref, cim_ref, are_ref, aim_ref, ldt_ref, d_ref, x0r_ref, x0i_ref,
                  h_ref, xr_out_ref, xi_out_ref, up_ref, bbr_ref, bbi_ref, xr_ref, xi_ref, pwr_ref, pwi_ref, yp_ref,
                  *, seg_len):
    NS = SUBLANES
    a_re, a_im = are_ref[...], aim_ref[...]
    dt = jnp.exp(ldt_ref[...])
    mag = jnp.exp(a_re * dt)
    ab_re, ab_im = mag * jnp.cos(a_im * dt), mag * jnp.sin(a_im * dt)
    den = a_re * a_re + a_im * a_im
    nr = ab_re - 1.0
    coef_re = (nr * a_re + ab_im * a_im) / den
    coef_im = (ab_im * a_re - nr * a_im) / den

    NQ = S5_BLK_ST // LANES
    lq = lambda v, q: v[:, LANES * q:LANES * (q + 1)]
    rep = lambda v: jnp.broadcast_to(v, (NS, LANES))
    abr = [rep(lq(ab_re, q)) for q in range(NQ)]
    abi = [rep(lq(ab_im, q)) for q in range(NQ)]
    blk = S5_BULK_STEPS
    n_blk = seg_len // blk

    def gather(i, carry):
        up_ref[pl.ds(pl.multiple_of(i * NS, NS), NS), :] = u_ref[0, pl.ds(i, NS, stride=seg_len), :]
        return carry

    lax.fori_loop(0, seg_len, gather, 0, unroll=8)

    for rb in range(n_blk):
        u = up_ref[pl.ds(rb * blk * NS, blk * NS), :]
        bu_re, bu_im = _dot(u, bre_ref[0]), _dot(u, bim_ref[0])
        bb_re = coef_re * bu_re - coef_im * bu_im
        bb_im = coef_re * bu_im + coef_im * bu_re
        for q in range(NQ):
            bbr_ref[q, pl.ds(rb * blk, blk)] = lq(bb_re, q).reshape(blk, NS, LANES)
            bbi_ref[q, pl.ds(rb * blk, blk)] = lq(bb_im, q).reshape(blk, NS, LANES)

    def scan(i, carry):
        out = []
        for q in range(NQ):
            xr, xi = carry[q]
            pr, pi = _cmul(abr[q], abi[q], xr, xi)
            xr, xi = pr + bbr_ref[q, i], pi + bbi_ref[q, i]
            xr_ref[q, i] = xr
            xi_ref[q, i] = xi
            out.append((xr, xi))
        return tuple(out)

    zero = jnp.zeros((NS, LANES), jnp.float32)
    ends = lax.fori_loop(0, seg_len, scan, tuple((zero, zero) for _ in range(NQ)), unroll=8)

    for q in range(NQ):
        pwr_ref[q, 0] = abr[q]
        pwi_ref[q, 0] = abi[q]
    an = [(abr[q], abi[q]) for q in range(NQ)]
    n = 1
    while n < seg_len:
        m = min(n, seg_len - n)
        for q in range(NQ):
            pr, pi = _cmul(pwr_ref[q, pl.ds(0, m)], pwi_ref[q, pl.ds(0, m)], an[q][0], an[q][1])
            pwr_ref[q, pl.ds(n, m)] = pr
            pwi_ref[q, pl.ds(n, m)] = pi
            an[q] = _cmul(an[q][0], an[q][1], an[q][0], an[q][1])
        n *= 2

    x0r, x0i = x0r_ref[0], x0i_ref[0]
    cs = []
    for q in range(NQ):
        seg_r, seg_i = pwr_ref[q, seg_len - 1][:1], pwi_ref[q, seg_len - 1][:1]
        cr, ci = [lq(x0r, q)], [lq(x0i, q)]
        for k in range(NS):
            pr, pi = _cmul(seg_r, seg_i, cr[k], ci[k])
            cr.append(pr + ends[q][0][k:k + 1])
            ci.append(pi + ends[q][1][k:k + 1])
        xr_out_ref[0, :, pl.ds(LANES * q, LANES)] = cr[NS]
        xi_out_ref[0, :, pl.ds(LANES * q, LANES)] = ci[NS]
        cs.append((jnp.concatenate(cr[:NS], axis=0), jnp.concatenate(ci[:NS], axis=0)))

    dskip = d_ref[...]
    for rb in range(n_blk):
        xr_q, xi_q = [], []
        for q in range(NQ):
            dr, di = _cmul(pwr_ref[q, pl.ds(rb * blk, blk)], pwi_ref[q, pl.ds(rb * blk, blk)], cs[q][0], cs[q][1])
            xr_q.append((xr_ref[q, pl.ds(rb * blk, blk)] + dr).reshape(blk * NS, LANES))
            xi_q.append((xi_ref[q, pl.ds(rb * blk, blk)] + di).reshape(blk * NS, LANES))
        u = up_ref[pl.ds(rb * blk * NS, blk * NS), :]
        y = (_dot(jnp.concatenate(xr_q, axis=-1), cre_ref[0]) - _dot(jnp.concatenate(xi_q, axis=-1), cim_ref[0])
             + dskip * u)
        yp_ref[pl.ds(rb * blk * NS, blk * NS), :] = _gelu_tanh(y)

    for k in range(NS):
        h_ref[0, pl.ds(k * seg_len, seg_len), :] = yp_ref[pl.ds(k, seg_len, stride=NS), :]


def _s5_call(z, bblk_re, bblk_im, cblk_re, cblk_im, a_re, a_im, log_dt_exp, d_flat, x0_re, x0_im, *, l_valid):
    B, L, _ = z.shape
    nb = bblk_re.shape[0]
    NQ = S5_BLK_ST // LANES
    f32 = jnp.float32
    if l_valid == L and L % (SUBLANES * S5_BULK_STEPS) == 0:
        seg_len = L // SUBLANES
        kern = functools.partial(_s5seg_kernel, seg_len=seg_len)
        tile = pltpu.VMEM((NQ, seg_len, SUBLANES, LANES), f32)
        scratch = [pltpu.VMEM((L, LANES), f32)] + [tile] * 6 + [pltpu.VMEM((L, LANES), f32)]
    else:
        kern = functools.partial(_s5_kernel, nseg=1, seg_len=l_valid)
        scratch = [pltpu.VMEM((NQ, L, LANES), f32)] * 4 + [pltpu.VMEM((NQ, l_valid, LANES), f32)] * 2
    vec = lambda w: pl.BlockSpec((1, w), lambda b, j: (0, j))
    st = pl.BlockSpec((1, 1, S5_BLK_ST), lambda b, j: (b, 0, j))
    return pl.pallas_call(
        kern,
        grid=(B, nb),
        in_specs=[pl.BlockSpec((1, L, S5_BLK_CH), lambda b, j: (b, 0, OFF_S5_U // S5_BLK_CH + j)),
                  pl.BlockSpec((1, S5_BLK_CH, S5_BLK_ST), lambda b, j: (j, 0, 0)),
                  pl.BlockSpec((1, S5_BLK_CH, S5_BLK_ST), lambda b, j: (j, 0, 0)),
                  pl.BlockSpec((1, S5_BLK_ST, S5_BLK_CH), lambda b, j: (j, 0, 0)),
                  pl.BlockSpec((1, S5_BLK_ST, S5_BLK_CH), lambda b, j: (j, 0, 0)),
                  vec(S5_BLK_ST), vec(S5_BLK_ST), vec(S5_BLK_ST), vec(S5_BLK_CH), st, st],
        out_specs=[pl.BlockSpec((1, L, S5_BLK_CH), lambda b, j: (b, 0, j)), st, st],
        out_shape=[jax.ShapeDtypeStruct((B, L, nb * S5_BLK_CH), jnp.float32),
                   jax.ShapeDtypeStruct((B, 1, nb * S5_BLK_ST), jnp.float32),
                   jax.ShapeDtypeStruct((B, 1, nb * S5_BLK_ST), jnp.float32)],
        scratch_shapes=scratch,
        compiler_params=_params(2),
        name="s5",
    )(z, bblk_re, bblk_im, cblk_re, cblk_im, a_re, a_im, log_dt_exp, d_flat, x0_re, x0_im)


def _s5_block_diag(b_re, b_im, c_re, c_im):
    G, P, Cc = b_re.shape
    nb = G // 8
    same = jnp.eye(8, dtype=bool)

    def bblk(b):
        t = jnp.transpose(b.reshape(nb, 8, P, Cc), (0, 1, 3, 2))[:, :, :, None, :]
        t = jnp.where(same[None, :, None, :, None], t, 0.0)
        return t.reshape(nb, 8 * Cc, 8 * P).astype(jnp.bfloat16)

    def cblk(c):
        t = jnp.transpose(c.reshape(nb, 8, Cc, P), (0, 1, 3, 2))[:, :, :, None, :]
        t = jnp.where(same[None, :, None, :, None], t, 0.0)
        return t.reshape(nb, 8 * P, 8 * Cc).astype(jnp.bfloat16)

    return bblk(b_re), bblk(b_im), cblk(c_re), cblk(c_im)


def _glu_kernel(h_ref, w_ref, b_ref, g_ref, o_ref):
    h = h_ref[...]
    t = _dot(h, w_ref[...]) + b_ref[...]
    o_ref[...] = (h * _sigmoid(t) * _silu(g_ref[...])).astype(o_ref.dtype)


def _glu_call(h2d, glu_w_bf16, glu_b, z2d, *, tm):
    M, W = h2d.shape
    return pl.pallas_call(
        _glu_kernel,
        grid=(M // tm,),
        in_specs=[pl.BlockSpec((tm, W), lambda i: (i, 0)),
                  pl.BlockSpec((W, W), lambda i: (0, 0)),
                  pl.BlockSpec((1, W), lambda i: (0, 0)),
                  pl.BlockSpec((tm, W), lambda i: (i, OFF_S5_G // W))],
        out_specs=pl.BlockSpec((tm, W), lambda i: (i, 0)),
        out_shape=jax.ShapeDtypeStruct((M, W), jnp.bfloat16),
        compiler_params=_params(1),
        name="glu",
    )(h2d, glu_w_bf16, glu_b.reshape(1, W), z2d)


def _kvprep_kernel(kv_ref, sm_ref, knw_ref, k_ref, v_ref, ik_ref, *rest, transposed):
    kvw = KV_HEADS * ATT_D
    kv = kv_ref[0]
    knw = knw_ref[...]
    ks = []
    for h in range(KV_HEADS):
        kh = kv[:, ATT_D * h:ATT_D * (h + 1)]
        ms = jnp.mean(kh * kh, axis=-1, keepdims=True)
        ks.append(kh * lax.rsqrt(ms + EPS) * knw)
    k = jnp.concatenate(ks, axis=-1)
    v = kv[:, kvw:]
    sm = sm_ref[0]
    k_ref[0] = k
    v_ref[0] = v
    ik_ref[0] = sm[:, :IDX_D]
    if transposed:
        kbf_ref, vt_ref, kia_ref, kib_ref = rest
        kbf_ref[0] = k.astype(jnp.bfloat16)
        vt_ref[0] = v.T.astype(jnp.bfloat16)
        lane = lax.broadcasted_iota(jnp.int32, sm.shape, 1)
        kia_ref[0] = jnp.where(lane < IDX_D, sm, 0.0).astype(jnp.bfloat16)
        kib_ref[0] = jnp.where(lane >= IDX_D, pltpu.roll(sm, IDX_D, axis=1), 0.0).astype(jnp.bfloat16)


def _kvprep_call(z, k_norm_w, *, tr, transposed):
    B, L, _ = z.shape
    kvw = KV_HEADS * ATT_D
    f32, bf = jnp.float32, jnp.bfloat16
    rowblk = lambda w: pl.BlockSpec((1, tr, w), lambda b, r: (b, r, 0))
    out_specs = [rowblk(kvw), rowblk(kvw), rowblk(IDX_D)]
    out_shape = [jax.ShapeDtypeStruct((B, L, kvw), f32), jax.ShapeDtypeStruct((B, L, kvw), f32),
                 jax.ShapeDtypeStruct((B, L, IDX_D), f32)]
    if transposed:
        out_specs += [rowblk(kvw), pl.BlockSpec((1, kvw, tr), lambda b, r: (b, 0, r)), rowblk(LANES), rowblk(LANES)]
        out_shape += [jax.ShapeDtypeStruct((B, L, kvw), bf), jax.ShapeDtypeStruct((B, kvw, L), bf),
                      jax.ShapeDtypeStruct((B, L, LANES), bf), jax.ShapeDtypeStruct((B, L, LANES), bf)]
    return pl.pallas_call(
        functools.partial(_kvprep_kernel, transposed=transposed),
        grid=(B, L // tr),
        in_specs=[pl.BlockSpec((1, tr, 2 * kvw), lambda b, r: (b, r, OFF_AT_KV // (2 * kvw))),
                  pl.BlockSpec((1, tr, LANES), lambda b, r: (b, r, OFF_SMALL // LANES)),
                  pl.BlockSpec((1, ATT_D), lambda b, r: (0, 0))],
        out_specs=out_specs, out_shape=out_shape,
        compiler_params=_params(2),
        name="kvprep",
    )(z, z, k_norm_w.reshape(1, ATT_D))


def _sortable_key(score):
    bits = lax.bitcast_convert_type(score, jnp.int32)
    return jnp.where(bits < 0, bits ^ 0x7FFFFFFF, bits)


def _rows_reduce(x, op):
    n = x.shape[0]
    slab = 8 * SUBLANES
    if n > slab and n % slab == 0:
        x = op(x.reshape(n // slab, slab, x.shape[1]), axis=0)
    return op(x, axis=0, keepdims=True)


def _kth_largest_key(key, k, axis):
    shape = list(key.shape)
    shape[axis] = 1

    def it(n, tu):
        cand_u = tu | jnp.left_shift(jnp.int32(1), 31 - n)
        cand_s = cand_u ^ jnp.int32(INT_MIN)
        ones = jnp.where(key >= cand_s, 1.0, 0.0)
        cnt = _rows_reduce(ones, jnp.sum) if axis == 0 else jnp.sum(ones, axis=axis, keepdims=True)
        return jnp.where(cnt >= k, cand_u, tu)

    tu = lax.fori_loop(0, 32, it, jnp.zeros(shape, jnp.int32))
    return tu ^ jnp.int32(INT_MIN)


def _t5_bucket(dist):
    exact = REL_BUCKETS // 2
    d = dist.astype(jnp.float32)
    large = exact + jnp.log(jnp.maximum(d, 1.0) / exact) / math.log(REL_MAX_DIST / exact) * (REL_BUCKETS - exact)
    large = jnp.minimum(jnp.maximum(large, 0.0).astype(jnp.int32), REL_BUCKETS - 1)
    return jnp.where(dist < exact, dist, large)


def _bias_lookup(rel_bias, bucket):
    onehot = (bucket[..., None] == jnp.arange(REL_BUCKETS, dtype=jnp.int32)).astype(jnp.float32)
    return jnp.einsum('...k,kh->...h', onehot, rel_bias.astype(jnp.float32), precision=lax.Precision.HIGHEST)


def _pattn_kernel(zq_ref, zg_ref, ziq_ref, zsm_ref, kbf_ref, vt_ref, kia_ref, kib_ref, qnw_ref, bias_ref, o_ref,
                  *, topk):
    i = pl.program_id(1)
    nqb = kbf_ref.shape[1] // Q_BLK
    qnw = qnw_ref[...]
    scale = ATT_D ** -0.5

    def body(S):
        nkb = S // Q_BLK
        wt = zsm_ref[0].T * (1.0 / 32.0)
        kia, kib = kia_ref[0, pl.ds(0, S), :], kib_ref[0, pl.ds(0, S), :]
        score = jnp.zeros((S, Q_BLK), jnp.float32)
        for p in range(IDX_HEADS // 2):
            qi = ziq_ref[0, :, pl.ds(LANES * p, LANES)].astype(jnp.bfloat16)
            da = lax.dot_general(kia, qi, _NT, preferred_element_type=jnp.float32)
            db = lax.dot_general(kib, qi, _NT, preferred_element_type=jnp.float32)
            r = SMALL_IXW + 2 * p
            score = score + jnp.maximum(da, 0.0) * wt[r:r + 1] + jnp.maximum(db, 0.0) * wt[r + 1:r + 2]
        s_pos = lax.broadcasted_iota(jnp.int32, (S, Q_BLK), 0)
        t_pos = lax.broadcasted_iota(jnp.int32, (S, Q_BLK), 1) + i * Q_BLK
        adm = s_pos <= t_pos
        key = jnp.where(adm, _sortable_key(score), jnp.int32(INT_MIN))
        thr = _kth_largest_key(key, topk, 0)
        sel = (key >= thr) & adm

        for h in range(ATT_HEADS):
            kvh = h // (ATT_HEADS // KV_HEADS)
            q = zq_ref[0, :, pl.ds(ATT_D * h, ATT_D)]
            ms = jnp.mean(q * q, axis=-1, keepdims=True)
            qn = (q * lax.rsqrt(ms + EPS) * qnw).astype(jnp.bfloat16)
            logit = lax.dot_general(kbf_ref[0, pl.ds(0, S), pl.ds(ATT_D * kvh, ATT_D)], qn, _NT,
                                    preferred_element_type=jnp.float32) * scale
            bias = jnp.concatenate([bias_ref[jnp.clip(i - j, 0, N_BIAS_TILES - 1), h] for j in range(nkb)], axis=0)
            x = jnp.where(sel, logit + bias, MASK_VALUE)
            m = _rows_reduce(x, jnp.max)
            pexp = jnp.exp(x - m)
            l = _rows_reduce(pexp, jnp.sum)
            ot = jnp.dot(vt_ref[0, pl.ds(ATT_D * kvh, ATT_D), pl.ds(0, S)], pexp.astype(jnp.bfloat16),
                         preferred_element_type=jnp.float32)
            out = (ot / l).T
            g = zg_ref[0, :, pl.ds(ATT_D * h, ATT_D)]
            o_ref[0, :, pl.ds(ATT_D * h, ATT_D)] = (out * _silu(g)).astype(o_ref.dtype)

    nv = min(N_CAUSAL_VARIANTS, nqb)
    per = nqb // nv
    for c in range(nv):
        pl.when(i // per == c)(functools.partial(body, (c + 1) * per * Q_BLK))


def _pattn_call(z, kbf, vt, kia, kib, q_norm_w, bias_tiles, *, topk):
    B, L, _ = z.shape
    W = ATT_HEADS * ATT_D
    kvw = KV_HEADS * ATT_D
    zblk = lambda off, w: pl.BlockSpec((1, Q_BLK, w), lambda b, i: (b, i, off // w))
    full = lambda shape: pl.BlockSpec((1,) + shape, lambda b, i: (b, 0, 0))
    return pl.pallas_call(
        functools.partial(_pattn_kernel, topk=topk),
        grid=(B, L // Q_BLK),
        in_specs=[zblk(OFF_AT_Q, W), zblk(OFF_AT_G, W), zblk(OFF_IX_Q, W), zblk(OFF_SMALL, LANES),
                  full((L, kvw)), full((kvw, L)), full((L, LANES)), full((L, LANES)),
                  pl.BlockSpec((1, ATT_D), lambda b, i: (0, 0)),
                  pl.BlockSpec((N_BIAS_TILES, ATT_HEADS, Q_BLK, Q_BLK), lambda b, i: (0, 0, 0, 0))],
        out_specs=pl.BlockSpec((1, Q_BLK, W), lambda b, i: (b, i, 0)),
        out_shape=jax.ShapeDtypeStruct((B, L, W), jnp.bfloat16),
        compiler_params=_params(2),
        name="pattn",
    )(z, z, z, z, kbf, vt, kia, kib, q_norm_w.reshape(1, ATT_D), bias_tiles)


def _prompt_bias_tiles(rel_bias):
    o = jnp.arange(N_BIAS_TILES, dtype=jnp.int32)[:, None, None]
    s = jnp.arange(Q_BLK, dtype=jnp.int32)[None, :, None]
    t = jnp.arange(Q_BLK, dtype=jnp.int32)[None, None, :]
    tiles = _bias_lookup(rel_bias, _t5_bucket(jnp.maximum(Q_BLK * o + t - s, 0)))
    return jnp.moveaxis(tiles, -1, 1)


def _sattn_score_kernel(pt_ref, qi_ref, w_ref, iknew_ref, *rest):
    pages, s_ref = rest[:NPG], rest[NPG]
    pc = pl.program_id(1)
    last = pl.num_programs(1) - 1
    qi = qi_ref[0]
    w = w_ref[0]

    def head_sum(d):
        r = jnp.maximum(d, 0.0) * w
        return jnp.sum(r.reshape(T_PAD, IDX_HEADS, r.shape[-1]), axis=1)

    @pl.when(pc < last)
    def _():
        for i in range(NPG):
            d = jnp.dot(qi, pages[i][0, 0].astype(jnp.bfloat16), preferred_element_type=jnp.float32)
            s_ref[0, :, pl.ds(PAGE * i, PAGE)] = head_sum(d)

    @pl.when(pc == last)
    def _():
        s_ref[0] = jnp.full((T_PAD, CHUNK_KEYS), MASK_VALUE, jnp.float32)
        d = lax.dot_general(qi, iknew_ref[0].astype(jnp.bfloat16), _NT, preferred_element_type=jnp.float32)
        s_ref[0, :, pl.ds(0, PAGE)] = head_sum(d)


def _sattn_score_call(page_table, qi_rows, w_col, ik_new, cache_ik_t, layer):
    B, n_pages = page_table.shape
    n_chunks = n_pages // NPG
    page_spec = lambda i: pl.BlockSpec(
        (1, 1, IDX_D, PAGE), lambda b, pc, pt: (layer, pt[b, jnp.minimum(pc * NPG + i, n_pages - 1)], 0, 0))
    per_b = lambda shape: pl.BlockSpec((1,) + shape, lambda b, pc, pt: (b, 0, 0))
    grid_spec = pltpu.PrefetchScalarGridSpec(
        num_scalar_prefetch=1, grid=(B, n_chunks + 1),
        in_specs=[per_b((T_PAD * IDX_HEADS, IDX_D)), per_b((T_PAD * IDX_HEADS, 1)), per_b((PAGE, IDX_D))]
        + [page_spec(i) for i in range(NPG)],
        out_specs=pl.BlockSpec((1, T_PAD, CHUNK_KEYS), lambda b, pc, pt: (b, 0, pc)))
    return pl.pallas_call(
        _sattn_score_kernel,
        grid_spec=grid_spec,
        out_shape=jax.ShapeDtypeStruct((B, T_PAD, (n_chunks + 1) * CHUNK_KEYS), jnp.float32),
        compiler_params=_params(2),
        name="sattn_score",
    )(page_table, qi_rows, w_col, ik_new, *([cache_ik_t] * NPG))


def _sattn_kernel(pt_ref, s_ref, q_ref, g_ref, qnw_ref, bias_ref, knew_ref, vnew_ref, *rest, topk, n_new, past):
    kp, vp = rest[:NPG], rest[NPG:2 * NPG]
    o_ref, thr_ref, qn_ref, m_ref, l_ref, acc_ref = rest[2 * NPG:]
    pc = pl.program_id(1)
    last = pl.num_programs(1) - 1
    G = ATT_HEADS // KV_HEADS

    def admissible(col0, width):
        pos = lax.broadcasted_iota(jnp.int32, (T_PAD, width), 1) + col0
        t = lax.broadcasted_iota(jnp.int32, (T_PAD, width), 0)
        return (pos < past) | ((pos - past <= t) & (pos - past < n_new))

    @pl.when(pc == 0)
    def _():
        sc = s_ref[0]
        key = jnp.where(admissible(0, sc.shape[-1]), _sortable_key(sc), jnp.int32(INT_MIN))
        thr_ref[...] = jnp.broadcast_to(_kth_largest_key(key, topk, 1), thr_ref.shape)
        q = q_ref[0]
        ms = jnp.mean(q * q, axis=-1, keepdims=True)
        qn_ref[...] = (q * lax.rsqrt(ms + EPS) * qnw_ref[...]).astype(jnp.bfloat16)
        m_ref[...] = jnp.full(m_ref.shape, MASK_VALUE, jnp.float32)
        l_ref[...] = jnp.zeros(l_ref.shape, jnp.float32)
        acc_ref[...] = jnp.zeros(acc_ref.shape, jnp.float32)

    def attend(get_k, get_v, width):
        col0 = pl.multiple_of(pc * CHUNK_KEYS, CHUNK_KEYS)
        sc = s_ref[0, :, pl.ds(col0, width)]
        adm = admissible(col0, width)
        key = jnp.where(adm, _sortable_key(sc), jnp.int32(INT_MIN))
        sel8 = (key >= thr_ref[:, :1]) & adm
        sel = jnp.concatenate([sel8] * G, axis=0)
        for kv in range(KV_HEADS):
            kk = get_k(kv).astype(jnp.bfloat16)
            x = lax.dot_general(qn_ref[kv], kk, _NT, preferred_element_type=jnp.float32) * (ATT_D ** -0.5)
            x = jnp.where(sel, x + bias_ref[kv, :, pl.ds(0, width)], MASK_VALUE)
            m_old = m_ref[kv]
            m_new = jnp.maximum(m_old, jnp.max(x, axis=-1, keepdims=True))
            p = jnp.where(sel, jnp.exp(x - m_new), 0.0)
            alpha = jnp.exp(m_old - m_new)
            l_ref[kv] = alpha * l_ref[kv] + jnp.sum(p, axis=-1, keepdims=True)
            acc_ref[kv] = alpha * acc_ref[kv] + jnp.dot(
                p.astype(jnp.bfloat16), get_v(kv).astype(jnp.bfloat16),
                preferred_element_type=jnp.float32)
            m_ref[kv] = m_new

    @pl.when(pc < last)
    def _():
        attend(lambda kv: jnp.concatenate([kp[i][0, 0, :, kv, :] for i in range(NPG)], axis=0),
               lambda kv: jnp.concatenate([vp[i][0, 0, :, kv, :] for i in range(NPG)], axis=0), CHUNK_KEYS)

    @pl.when(pc == last)
    def _():
        attend(lambda kv: knew_ref[0, :, pl.ds(ATT_D * kv, ATT_D)],
               lambda kv: vnew_ref[0, :, pl.ds(ATT_D * kv, ATT_D)], PAGE)
        o_ref[0] = acc_ref[...] / l_ref[...] * _silu(g_ref[0])


def _sattn_call(page_table, scores, q_rows, g_rows, q_norm_w, bias, k_new, v_new, cache_k, cache_v, layer,
                *, topk, n_new):
    B, n_pages = page_table.shape
    n_chunks = n_pages // NPG
    kvw = KV_HEADS * ATT_D
    rows = (ATT_HEADS // KV_HEADS) * T_PAD
    n_keys = (n_chunks + 1) * CHUNK_KEYS
    page_spec = lambda i: pl.BlockSpec(
        (1, 1, PAGE, KV_HEADS, ATT_D),
        lambda b, pc, pt: (layer, pt[b, jnp.minimum(pc * NPG + i, n_pages - 1)], 0, 0, 0))
    per_b = lambda shape: pl.BlockSpec((1,) + shape, lambda b, pc, pt: (b,) + (0,) * len(shape))
    grid_spec = pltpu.PrefetchScalarGridSpec(
        num_scalar_prefetch=1, grid=(B, n_chunks + 1),
        in_specs=[per_b((T_PAD, n_keys)), per_b((KV_HEADS, rows, ATT_D)), per_b((KV_HEADS, rows, ATT_D)),
                  pl.BlockSpec((1, ATT_D), lambda b, pc, pt: (0, 0)),
                  pl.BlockSpec((KV_HEADS, rows, CHUNK_KEYS), lambda b, pc, pt: (0, 0, pc)),
                  per_b((PAGE, kvw)), per_b((PAGE, kvw))]
        + [page_spec(i) for i in range(NPG)] + [page_spec(i) for i in range(NPG)],
        out_specs=per_b((KV_HEADS, rows, ATT_D)),
        scratch_shapes=[pltpu.VMEM((T_PAD, LANES), jnp.int32), pltpu.VMEM((KV_HEADS, rows, ATT_D), jnp.bfloat16),
                        pltpu.VMEM((KV_HEADS, rows, 1), jnp.float32), pltpu.VMEM((KV_HEADS, rows, 1), jnp.float32),
                        pltpu.VMEM((KV_HEADS, rows, ATT_D), jnp.float32)])
    return pl.pallas_call(
        functools.partial(_sattn_kernel, topk=topk, n_new=n_new, past=n_pages * PAGE),
        grid_spec=grid_spec,
        out_shape=jax.ShapeDtypeStruct((B, KV_HEADS, rows, ATT_D), jnp.float32),
        compiler_params=_params(2),
        name="sattn",
    )(page_table, scores, q_rows, g_rows, q_norm_w.reshape(1, ATT_D), bias, k_new, v_new,
      *([cache_k] * NPG), *([cache_v] * NPG))


def _sample_bias(rel_bias, past, n_keys):
    G = ATT_HEADS // KV_HEADS
    t = jnp.arange(T_PAD, dtype=jnp.int32)[:, None]
    pos = jnp.arange(n_keys, dtype=jnp.int32)[None, :]
    tab = _bias_lookup(rel_bias, _t5_bucket(jnp.maximum(past + t - pos, 0)))
    return jnp.moveaxis(tab, -1, 0).reshape(KV_HEADS, G * T_PAD, n_keys)


def _to_head_rows(a):
    B = a.shape[0]
    G = ATT_HEADS // KV_HEADS
    a = a.reshape(B, T_PAD, KV_HEADS, G, ATT_D)
    return jnp.transpose(a, (0, 2, 3, 1, 4)).reshape(B, KV_HEADS, G * T_PAD, ATT_D)


def _from_head_rows(a):
    B = a.shape[0]
    G = ATT_HEADS // KV_HEADS
    a = a.reshape(B, KV_HEADS, G, T_PAD, ATT_D)
    return jnp.transpose(a, (0, 3, 1, 2, 4)).reshape(B, T_PAD, ATT_HEADS * ATT_D)


def _layer(x, lp, states, attend, *, l_valid, tm, chunk_hg, chunk_ssd):
    B, L, _ = x.shape
    hg_s0, ssm_s0, conv_s0, s5_re0, s5_im0 = states
    x2d = x.reshape(B * L, D_MODEL)
    z2d = _inproj(x2d, lp['norm_w'], lp['w_in'], tm)
    z = z2d.reshape(B, L, D_IN_PAD)

    o_hg, hg_s = _hgrn_call(z, lp['lb'], lp['hg_norm_w'], hg_s0, chunk=chunk_hg, l_valid=l_valid)

    conv8 = jnp.pad(conv_s0, ((0, 0), (SUBLANES - (SSM_CONV - 1), 0), (0, 0)))
    y_ssm, ssm_s, cx, cb, cc = _ssd_call(z, lp['conv_w'], lp['conv_b'], conv8, lp['dtb_pad'], lp['alog_pad'],
                                         lp['dskip_pad'], lp['ssm_norm_w'], ssm_s0, chunk=chunk_ssd, l_valid=l_valid)
    tail = SUBLANES - (SSM_CONV - 1)
    conv_s = jnp.concatenate([cx[:, tail:], cb[:, tail:], cc[:, tail:]], axis=-1)

    h5, s5_re, s5_im = _s5_call(z, *lp['s5_blocks'], lp['a_re'], lp['a_im'], lp['log_dt'], lp['s5_d'],
                                s5_re0.reshape(B, 1, -1), s5_im0.reshape(B, 1, -1), l_valid=l_valid)
    o5 = _glu_call(h5.reshape(B * L, BRANCH), lp['glu_w'], lp['glu_b'], z2d, tm=tm)

    o_att, k, v, ik = attend(z)

    y = _outproj((o_hg.reshape(B * L, BRANCH), y_ssm.reshape(B * L, BRANCH), o5, o_att.reshape(B * L, BRANCH)),
                 lp['w_out'], lp['layer'], x2d, tm)
    st = (k[:, :l_valid].reshape(B, l_valid, KV_HEADS, ATT_D), v[:, :l_valid].reshape(B, l_valid, KV_HEADS, ATT_D),
          ik[:, :l_valid], hg_s, ssm_s, conv_s,
          s5_re.reshape(B, S5_GROUPS, S5_STATE), s5_im.reshape(B, S5_GROUPS, S5_STATE))
    return y.reshape(B, L, D_MODEL), st


def kernel(x_prompt, x_sample, cache_k, cache_v, cache_idx_k, state_hgrn, state_ssm, state_conv, state_s5_re, state_s5_im, page_table, norm_w, w_in, w_out, hg_lb_logits, hg_norm_w, ssm_conv_w, ssm_conv_b, ssm_dt_bias, ssm_a_log, ssm_d, ssm_norm_w, s5_a_re, s5_a_im, s5_log_dt, s5_b_re, s5_b_im, s5_c_re, s5_c_im, s5_d, s5_glu_w, s5_glu_b, att_q_norm, att_k_norm, rel_bias):
    f32 = jnp.float32
    bp, lp_len, _ = x_prompt.shape
    bs, ls, _ = x_sample.shape
    n_pages = page_table.shape[1]
    past = n_pages * PAGE
    n_keys = (n_pages // NPG + 1) * CHUNK_KEYS

    sm = jax.nn.softmax(hg_lb_logits.astype(f32), axis=0)
    lower = jnp.cumsum(sm, axis=0) - sm[0]

    w_out_bf16 = w_out.astype(jnp.bfloat16)
    cache_ik_t = jnp.swapaxes(cache_idx_k, 2, 3)
    bias_tiles = _prompt_bias_tiles(rel_bias)
    bias_sample = _sample_bias(rel_bias, past, n_keys)
    topk_p = min(TOPK_MAX, lp_len // 4)
    topk_s = min(TOPK_MAX, (past + ls) // 4)

    def head_lanes(v):
        return jnp.zeros((1, LANES), f32).at[0, SMALL_DT:SMALL_DT + SSM_HEADS].set(v)

    xs_pad = jnp.pad(x_sample, ((0, 0), (0, SAMPLE_ROWS - ls), (0, 0)))
    pad_page = lambda a: jnp.pad(a, ((0, 0), (0, PAGE - a.shape[1]), (0, 0)))

    yp, ys = x_prompt, xs_pad
    new_p = [[] for _ in range(8)]
    new_s = [[] for _ in range(8)]
    for l in range(DEPTH):
        lp = {'norm_w': norm_w[l], 'w_in': _pack_w_in(w_in, l), 'w_out': w_out_bf16, 'layer': l,
              'lb': lower[l], 'hg_norm_w': hg_norm_w[l],
              'conv_w': ssm_conv_w[l], 'conv_b': ssm_conv_b[l], 'dtb_pad': head_lanes(ssm_dt_bias[l]),
              'alog_pad': head_lanes(ssm_a_log[l]), 'dskip_pad': head_lanes(ssm_d[l]), 'ssm_norm_w': ssm_norm_w[l],
              's5_blocks': _s5_block_diag(s5_b_re[l], s5_b_im[l], s5_c_re[l], s5_c_im[l]),
              'a_re': s5_a_re[l].reshape(1, -1), 'a_im': s5_a_im[l].reshape(1, -1),
              'log_dt': jnp.repeat(s5_log_dt[l], S5_STATE).reshape(1, -1), 's5_d': s5_d[l].reshape(1, -1),
              'glu_w': s5_glu_w[l].astype(jnp.bfloat16), 'glu_b': s5_glu_b[l]}

        def attend_prompt(z):
            k, v, ik, kbf, vt, kia, kib = _kvprep_call(z, att_k_norm[l], tr=256, transposed=True)
            o = _pattn_call(z, kbf, vt, kia, kib, att_q_norm[l], bias_tiles, topk=topk_p)
            return o, k, v, ik

        def attend_sample(z):
            k, v, ik = _kvprep_call(z, att_k_norm[l], tr=SAMPLE_ROWS, transposed=False)
            z8 = z[:, :T_PAD]
            qi_rows = z8[..., OFF_IX_Q:OFF_IX_Q + IDX_HEADS * IDX_D].reshape(bs, T_PAD * IDX_HEADS, IDX_D)
            w_col = z8[..., OFF_SMALL + SMALL_IXW:OFF_SMALL + SMALL_IXW + IDX_HEADS] * (1.0 / 32.0)
            scores = _sattn_score_call(page_table, qi_rows.astype(jnp.bfloat16),
                                       w_col.reshape(bs, T_PAD * IDX_HEADS, 1), pad_page(ik), cache_ik_t, l)
            o = _sattn_call(page_table, scores, _to_head_rows(z8[..., OFF_AT_Q:OFF_AT_Q + BRANCH]),
                            _to_head_rows(z8[..., OFF_AT_G:OFF_AT_G + BRANCH]), att_q_norm[l], bias_sample,
                            pad_page(k), pad_page(v), cache_k, cache_v, l, topk=topk_s, n_new=ls)
            o = jnp.pad(_from_head_rows(o), ((0, 0), (0, SAMPLE_ROWS - T_PAD), (0, 0)))
            return o.astype(jnp.bfloat16), k, v, ik

        zero_states = (jnp.zeros((bp, HG_HEADS, LANES, LANES), f32),
                       jnp.zeros((bp, SSM_HEADS, SSM_P, SSM_STATE), f32),
                       jnp.zeros((bp, SSM_CONV - 1, SSM_CONV_DIM), f32),
                       jnp.zeros((bp, S5_GROUPS, S5_STATE), f32),
                       jnp.zeros((bp, S5_GROUPS, S5_STATE), f32))
        yp, st_p = _layer(yp, lp, zero_states, attend_prompt, l_valid=lp_len, tm=512, chunk_hg=64, chunk_ssd=128)
        samp_states = (state_hgrn[l], state_ssm[l], state_conv[l], state_s5_re[l], state_s5_im[l])
        ys, st_s = _layer(ys, lp, samp_states, attend_sample, l_valid=ls, tm=bs * SAMPLE_ROWS,
                          chunk_hg=SAMPLE_ROWS, chunk_ssd=SAMPLE_ROWS)
        for i in range(8):
            new_p[i].append(st_p[i])
            new_s[i].append(st_s[i])
    pk, pv, pik, phg, pssm, pconv, ps5r, ps5i = [jnp.stack(a) for a in new_p]
    sk, sv, sik, shg, sssm, sconv, ss5r, ss5i = [jnp.stack(a) for a in new_s]
    return (yp, ys[:, :ls], pk, pv, pik, phg, pssm, pconv, ps5r, ps5i, sk, sv, sik, shg, sssm, sconv, ss5r, ss5i)
```

```python
import functools
import math

import jax
import jax.numpy as jnp
from jax import lax
from jax.experimental import pallas as pl
from jax.experimental.pallas import tpu as pltpu

D_MODEL = 4096
DEPTH = 2
BRANCH = D_MODEL // 4
HG_HEADS = 8
SSM_HEADS = 16
SSM_GROUPS = 4
SSM_STATE = 128
SSM_CONV = 4
SSM_CONV_DIM = BRANCH + 2 * SSM_GROUPS * SSM_STATE
SSM_HPG = SSM_HEADS // SSM_GROUPS
SSM_P = 64
S5_GROUPS = 64
S5_STATE = 64
S5_BLK_CH = 128
S5_BLK_ST = 512
ATT_D = 128
ATT_HEADS = 8
KV_HEADS = 2
IDX_HEADS = 16
IDX_D = 64
TOPK_MAX = 256
Q_BLK = 128
PAGE = 128
REL_BUCKETS = 32
REL_MAX_DIST = 1024
EPS = 1e-6
MASK_VALUE = -1e30
F_FLOOR = 1e-30
INT_MIN = -2 ** 31

LANES = 128
SUBLANES = 8
V7X_VMEM_LIMIT = 56 * 1024 * 1024

_SRC_DT, _SRC_S5U, _SRC_ATK, _SRC_ATG, _SRC_IXK, D_IN_SRC = 7168, 7184, 10256, 10768, 12816, 12896

OFF_HG_Q, OFF_HG_F, OFF_HG_I, OFF_HG_G = 0, 1024, 2048, 3072
OFF_SM_Z, OFF_SM_XBC = 4096, 5120
OFF_S5_U, OFF_S5_G = 7168, 8192
OFF_AT_Q, OFF_AT_G, OFF_IX_Q = 9216, 10240, 11264
OFF_AT_KV = 12288
OFF_SMALL = 12800
SMALL_IXW, SMALL_DT = 64, 80
D_IN_PAD = 13312

PACK_TN = 1024
PACK_TK = 1024
_PACK_LAST = D_IN_PAD // PACK_TN - 1
_PACK_ROW0 = (tuple(range(0, _SRC_DT, PACK_TN)) + tuple(range(_SRC_S5U, _SRC_ATK, PACK_TN))
              + tuple(range(_SRC_ATG, _SRC_IXK, PACK_TN)) + (_SRC_ATK,))
N_KV_ROWS = _SRC_ATG - _SRC_ATK
N_IX_ROWS = D_IN_SRC - _SRC_IXK
N_DT_ROWS = _SRC_S5U - _SRC_DT
NORM_ROWS = 128
S5_BULK_STEPS = 32

N_BIAS_TILES = 9
HG_SUB = 16
HG_HPB = 8
HG_ROW_BLOCK = 512
N_CAUSAL_VARIANTS = 4
NPG = 8
T_PAD = 8
CHUNK_KEYS = NPG * PAGE
SAMPLE_ROWS = 16

_TN = (((0,), (0,)), ((), ()))
_NT = (((1,), (1,)), ((), ()))


def _params(n_axes):
    return pltpu.CompilerParams(dimension_semantics=("arbitrary",) * n_axes, vmem_limit_bytes=V7X_VMEM_LIMIT)


def _bf(x):
    return x.astype(jnp.bfloat16)


def _dot(a, b, dims=None):
    if dims is None:
        return jnp.dot(_bf(a), _bf(b), preferred_element_type=jnp.float32)
    return lax.dot_general(_bf(a), _bf(b), dims, preferred_element_type=jnp.float32)


def _sigmoid(x):
    return 1.0 / (1.0 + jnp.exp(-x))


def _silu(x):
    return x * _sigmoid(x)


def _softplus(x):
    return jnp.maximum(x, 0.0) + jnp.log(1.0 + jnp.exp(-jnp.abs(x)))


def _gelu_tanh(x):
    return 0.5 * x * (1.0 + jnp.tanh(math.sqrt(2.0 / math.pi) * (x + 0.044715 * (x * x * x))))


def _cumsum_rows(x):
    n = x.shape[0]
    row = lax.broadcasted_iota(jnp.int32, x.shape, 0)
    sh = 1
    while sh < n:
        x = x + jnp.where(row >= sh, pltpu.roll(x, sh, axis=0), 0.0)
        sh *= 2
    return x


def _split3(x):
    hi = x.astype(jnp.bfloat16)
    r = x - hi.astype(jnp.float32)
    mid = r.astype(jnp.bfloat16)
    lo = (r - mid.astype(jnp.float32)).astype(jnp.bfloat16)
    return hi, mid, lo


def _select_lanes_as_rows(sel, x):
    out = None
    for part in _split3(x):
        t = lax.dot_general(sel, part, _NT, preferred_element_type=jnp.float32)
        out = t if out is None else out + t
    return out


def _cmul(ar, ai, br, bi):
    return ar * br - ai * bi, ar * bi + ai * br


def _pack_kernel(off_ref, w_ref, ix_ref, dt_ref, o_ref):
    j = pl.program_id(0)

    @pl.when(j < _PACK_LAST)
    def _():
        o_ref[...] = w_ref[0].astype(o_ref.dtype)

    @pl.when(j == _PACK_LAST)
    def _():
        o_ref[pl.ds(0, N_KV_ROWS), :] = w_ref[0, pl.ds(0, N_KV_ROWS), :].astype(o_ref.dtype)
        o_ref[pl.ds(N_KV_ROWS, N_IX_ROWS), :] = ix_ref[0].astype(o_ref.dtype)
        o_ref[pl.ds(N_KV_ROWS + N_IX_ROWS, N_DT_ROWS), :] = dt_ref[0].astype(o_ref.dtype)
        used = N_KV_ROWS + N_IX_ROWS + N_DT_ROWS
        o_ref[pl.ds(used, PACK_TN - used), :] = jnp.zeros((PACK_TN - used, PACK_TK), o_ref.dtype)


def _pack_w_in(w_in_t, layer):
    assert all(r % SUBLANES == 0 for r in _PACK_ROW0)
    row0 = jnp.asarray([r // SUBLANES for r in _PACK_ROW0], jnp.int32)
    E = pl.Element
    fixed = lambda rows, row: pl.BlockSpec(
        (E(1), E(rows), E(PACK_TK)), lambda j, kc, off: (layer, row, kc * PACK_TK))
    grid_spec = pltpu.PrefetchScalarGridSpec(
        num_scalar_prefetch=1, grid=(D_IN_PAD // PACK_TN, D_MODEL // PACK_TK),
        in_specs=[pl.BlockSpec((E(1), E(PACK_TN), E(PACK_TK)),
                               lambda j, kc, off: (layer, off[j] * SUBLANES, kc * PACK_TK)),
                  fixed(N_IX_ROWS, _SRC_IXK), fixed(N_DT_ROWS, _SRC_DT)],
        out_specs=pl.BlockSpec((PACK_TN, PACK_TK), lambda j, kc, off: (j, kc)))
    return pl.pallas_call(
        _pack_kernel, grid_spec=grid_spec,
        out_shape=jax.ShapeDtypeStruct((D_IN_PAD, D_MODEL), jnp.bfloat16),
        compiler_params=_params(2),
        name="pack_w_in",
    )(row0, w_in_t, w_in_t, w_in_t)


def _inproj_kernel(x_ref, nw_ref, w_ref, z_ref, hn_ref):
    @pl.when(pl.program_id(1) == 0)
    def _():
        nw = nw_ref[...]
        for r0 in range(0, x_ref.shape[0], NORM_ROWS):
            xf = x_ref[pl.ds(r0, NORM_ROWS), :]
            ms = jnp.mean(xf * xf, axis=-1, keepdims=True)
            hn_ref[pl.ds(r0, NORM_ROWS), :] = (xf * lax.rsqrt(ms + EPS) * nw).astype(jnp.bfloat16)

    z_ref[...] = lax.dot_general(hn_ref[...], w_ref[...], _NT, preferred_element_type=jnp.float32)


def _inproj(x2d, norm_w, w_packed, tm, tn=1024):
    m = x2d.shape[0]
    return pl.pallas_call(
        _inproj_kernel,
        grid=(m // tm, D_IN_PAD // tn),
        in_specs=[pl.BlockSpec((tm, D_MODEL), lambda i, j: (i, 0)),
                  pl.BlockSpec((1, D_MODEL), lambda i, j: (0, 0)),
                  pl.BlockSpec((tn, D_MODEL), lambda i, j: (j, 0))],
        out_specs=pl.BlockSpec((tm, tn), lambda i, j: (i, j)),
        out_shape=jax.ShapeDtypeStruct((m, D_IN_PAD), jnp.float32),
        scratch_shapes=[pltpu.VMEM((tm, D_MODEL), jnp.bfloat16)],
        compiler_params=_params(2),
        name="inproj",
    )(x2d, norm_w.reshape(1, D_MODEL), w_packed)


def _outproj_kernel(m0_ref, m1_ref, m2_ref, m3_ref, w_ref, x_ref, y_ref):
    acc = x_ref[...]
    for i, m_ref in enumerate((m0_ref, m1_ref, m2_ref, m3_ref)):
        acc = acc + jnp.dot(m_ref[...], w_ref[pl.ds(BRANCH * i, BRANCH), :], preferred_element_type=jnp.float32)
    y_ref[...] = acc


def _outproj(mixed4, w_out_bf16, layer, x2d, tm, tn=1024):
    m = x2d.shape[0]
    mspec = pl.BlockSpec((tm, BRANCH), lambda i, j: (i, 0))
    return pl.pallas_call(
        _outproj_kernel,
        grid=(m // tm, D_MODEL // tn),
        in_specs=[mspec, mspec, mspec, mspec,
                  pl.BlockSpec((None, D_MODEL, tn), lambda i, j: (layer, 0, j)),
                  pl.BlockSpec((tm, tn), lambda i, j: (i, j))],
        out_specs=pl.BlockSpec((tm, tn), lambda i, j: (i, j)),
        out_shape=jax.ShapeDtypeStruct((m, D_MODEL), jnp.float32),
        compiler_params=_params(2),
        name="outproj",
    )(*mixed4, w_out_bf16, x2d)


def _hgrn_kernel(q_ref, f_ref, i_ref, g_ref, lb_ref, nw_ref, s0_ref, o_ref, s_out_ref, st_ref, *, chunk, l_valid):
    LB = q_ref.shape[1]
    C = chunk
    nblk = C // SUBLANES
    lb_i = pl.program_id(2)

    @pl.when(lb_i == 0)
    def _():
        for hh in range(HG_HPB):
            st_ref[hh] = s0_ref[0, hh].T

    nw = nw_ref[...]
    rowi = lax.broadcasted_iota(jnp.int32, (SUBLANES, LANES), 0)

    def head_chunk(hh, t0):
        lanes = pl.ds(LANES * hh, LANES)
        lb = lb_ref[0, :, lanes]
        fp = f_ref[0, pl.ds(t0, C), lanes]
        qp = q_ref[0, pl.ds(t0, C), lanes]
        v = i_ref[0, pl.ds(t0, C), lanes]
        gp = g_ref[0, pl.ds(t0, C), lanes]
        fg = lb + (1.0 - lb) * _sigmoid(fp)
        logf = jnp.log(jnp.maximum(fg, F_FLOOR))
        kk = (1.0 - lb) * _sigmoid(-fp)
        if l_valid < LB * pl.num_programs(2):
            rows = lax.broadcasted_iota(jnp.int32, (C, LANES), 0) + t0 + lb_i * LB
            logf = jnp.where(rows < l_valid, logf, 0.0)
            kk = jnp.where(rows < l_valid, kk, 0.0)
        qq = _silu(qp)
        G = _cumsum_rows(logf)
        st = st_ref[hh]
        o_inter = _dot(qq * jnp.exp(G), st, _NT)
        acc = [None] * nblk
        qb = [qq[SUBLANES * tb:SUBLANES * (tb + 1)] for tb in range(nblk)]
        Gb = [G[SUBLANES * tb:SUBLANES * (tb + 1)] for tb in range(nblk)]
        for s in range(C):
            sb = s // SUBLANES
            gs = G[s:s + 1]
            ks = kk[s:s + 1]
            vs = v[s:s + 1]
            for tb in range(sb, (s // HG_SUB + 1) * (HG_SUB // SUBLANES)):
                d = Gb[tb] - gs
                if tb == sb:
                    d = jnp.where(rowi >= s - SUBLANES * sb, d, MASK_VALUE)
                w = jnp.sum(qb[tb] * ks * jnp.exp(d), axis=-1, keepdims=True)
                contrib = w * vs
                acc[tb] = contrib if acc[tb] is None else acc[tb] + contrib
        for j in range(1, C // HG_SUB):
            r0 = HG_SUB * j
            gb = G[r0 - 1:r0]
            qj = qq[r0:r0 + HG_SUB] * jnp.exp(G[r0:r0 + HG_SUB] - gb)
            kj = kk[:r0] * jnp.exp(gb - G[:r0])
            oj = _dot(_dot(qj, kj, _NT), v[:r0])
            for half in range(HG_SUB // SUBLANES):
                tb = r0 // SUBLANES + half
                acc[tb] = acc[tb] + oj[SUBLANES * half:SUBLANES * (half + 1)]
        o = o_inter + jnp.concatenate(acc, axis=0)
        g_last = G[C - 1:C]
        kd = kk * jnp.exp(g_last - G)
        st_ref[hh] = jnp.exp(g_last) * st + _dot(v, kd, _TN)
        ms = jnp.mean(o * o, axis=-1, keepdims=True)
        on = o * lax.rsqrt(ms + EPS) * nw
        o_ref[0, pl.ds(t0, C), lanes] = (on * _silu(gp)).astype(o_ref.dtype)

    def body(c, carry):
        t0 = pl.multiple_of(c * C, C)
        for hh in range(HG_HPB):
            head_chunk(hh, t0)
        return carry

    lax.fori_loop(0, LB // C, body, 0)

    @pl.when(lb_i == pl.num_programs(2) - 1)
    def _():
        for hh in range(HG_HPB):
            s_out_ref[0, hh] = st_ref[hh].T


def _hgrn_call(z, lb, hg_norm_w, s0, *, chunk, l_valid):
    B, L, _ = z.shape
    W = HG_HPB * LANES
    LB = min(L, HG_ROW_BLOCK)
    zspec = lambda off: pl.BlockSpec((1, LB, W), lambda b, h, r: (b, r, off // W + h))
    return pl.pallas_call(
        functools.partial(_hgrn_kernel, chunk=chunk, l_valid=l_valid),
        grid=(B, HG_HEADS // HG_HPB, L // LB),
        in_specs=[zspec(OFF_HG_Q), zspec(OFF_HG_F), zspec(OFF_HG_I), zspec(OFF_HG_G),
                  pl.BlockSpec((1, 1, W), lambda b, h, r: (h, 0, 0)),
                  pl.BlockSpec((1, LANES), lambda b, h, r: (0, 0)),
                  pl.BlockSpec((1, HG_HPB, LANES, LANES), lambda b, h, r: (b, h, 0, 0))],
        out_specs=[pl.BlockSpec((1, LB, W), lambda b, h, r: (b, r, h)),
                   pl.BlockSpec((1, HG_HPB, LANES, LANES), lambda b, h, r: (b, h, 0, 0))],
        out_shape=[jax.ShapeDtypeStruct((B, L, BRANCH), jnp.bfloat16),
                   jax.ShapeDtypeStruct((B, HG_HEADS, LANES, LANES), jnp.float32)],
        scratch_shapes=[pltpu.VMEM((HG_HPB, LANES, LANES), jnp.float32)],
        compiler_params=_params(3),
        name="hgrn",
    )(z, z, z, z, lb.reshape(HG_HEADS // HG_HPB, 1, W), hg_norm_w.reshape(1, LANES), s0)


def _ssd_kernel(xs_ref, b_ref, c_ref, zg_ref, sm_ref, wx_ref, wb_ref, wc_ref, bx_ref, bb_ref, bc_ref,
                cx_ref, cb_ref, cc_ref, dtb_ref, alog_ref, dskip_ref, nw_ref, s0_ref,
                y_ref, s_out_ref, ox_ref, ob_ref, oc_ref,
                ax_ref, ab_ref, ac_ref, hx_ref, hb_ref, hc_ref, sp_ref, *, chunk, l_valid, conv_rows):
    L = xs_ref.shape[1]
    C = chunk
    R = conv_rows
    g = pl.program_id(1)

    def conv(src_ref, head_ref, cst_ref, w_ref, bias_ref, act_ref, out_state_ref):
        head_ref[pl.ds(0, SUBLANES), :] = cst_ref[0]
        head_ref[pl.ds(SUBLANES, R), :] = src_ref[0, pl.ds(0, R), :]
        w = w_ref[...]
        for r0 in range(0, L, R):
            acc = bias_ref[...]
            for j in range(SSM_CONV):
                if r0 == 0:
                    xj = head_ref[pl.ds(SUBLANES - j, R), :]
                else:
                    xj = src_ref[0, pl.ds(r0 - j, R), :]
                acc = acc + xj * w[SSM_CONV - 1 - j:SSM_CONV - j]
            act_ref[pl.ds(r0, R), :] = _silu(acc)
        if l_valid >= SUBLANES:
            out_state_ref[0] = src_ref[0, pl.ds(l_valid - SUBLANES, SUBLANES), :]
        else:
            out_state_ref[0] = head_ref[pl.ds(l_valid, SUBLANES), :]

    conv(xs_ref, hx_ref, cx_ref, wx_ref, bx_ref, ax_ref, ox_ref)
    conv(b_ref, hb_ref, cb_ref, wb_ref, bb_ref, ab_ref, ob_ref)
    conv(c_ref, hc_ref, cc_ref, wc_ref, bc_ref, ac_ref, oc_ref)

    for p in range(SSM_HPG // 2):
        sp_ref[p] = s0_ref[0, 2 * p:2 * p + 2].reshape(2 * SSM_P, LANES)

    lane1 = lax.broadcasted_iota(jnp.int32, (1, LANES), 1)
    lane8 = lax.broadcasted_iota(jnp.int32, (SUBLANES, LANES), 1)
    row8 = lax.broadcasted_iota(jnp.int32, (SUBLANES, LANES), 0)
    lane0 = SMALL_DT + SSM_HPG * g
    sel = jnp.where((lane8 == lane0 + row8) & (row8 < SSM_HPG), 1.0, 0.0).astype(jnp.bfloat16)
    a_all = -jnp.exp(alog_ref[...])
    dskip = [jnp.sum(jnp.where(lane1 == lane0 + j, dskip_ref[...], 0.0), axis=-1, keepdims=True)
             for j in range(SSM_HPG)]
    lane_c = lax.broadcasted_iota(jnp.int32, (C, LANES), 1)
    first_half = lane_c < SSM_P
    row_p = lax.broadcasted_iota(jnp.int32, (2 * SSM_P, LANES), 0) < SSM_P
    tril = lax.broadcasted_iota(jnp.int32, (C, C), 0) >= lax.broadcasted_iota(jnp.int32, (C, C), 1)
    nw = nw_ref[...]

    def body(c, carry):
        t0 = pl.multiple_of(c * C, C)
        sm = sm_ref[0, pl.ds(t0, C), :]
        dt_all = _softplus(sm + dtb_ref[...])
        if l_valid < L:
            rows = lax.broadcasted_iota(jnp.int32, (C, LANES), 0) + t0
            dt_all = jnp.where(rows < l_valid, dt_all, 0.0)
        cum = _cumsum_rows(dt_all * a_all)
        cum_rows = _select_lanes_as_rows(sel, cum)
        col = [jnp.sum(jnp.where(lane_c == lane0 + j, cum, 0.0), axis=-1, keepdims=True) for j in range(SSM_HPG)]
        dtc = [jnp.sum(jnp.where(lane_c == lane0 + j, dt_all, 0.0), axis=-1, keepdims=True) for j in range(SSM_HPG)]
        bact = ab_ref[pl.ds(t0, C), :]
        cact = ac_ref[pl.ds(t0, C), :]
        cb = _dot(cact, bact, _NT)
        ys = []
        for p in range(SSM_HPG // 2):
            ja, jb = 2 * p, 2 * p + 1
            xs = ax_ref[pl.ds(t0, C), pl.ds(2 * SSM_P * p, 2 * SSM_P)]
            xdt = xs * jnp.where(first_half, dtc[ja], dtc[jb])
            dec_a = jnp.exp(jnp.where(tril, col[ja] - cum_rows[ja:ja + 1], MASK_VALUE))
            dec_b = jnp.exp(jnp.where(tril, col[jb] - cum_rows[jb:jb + 1], MASK_VALUE))
            y = jnp.where(first_half, _dot(cb * dec_a, xdt), _dot(cb * dec_b, xdt))
            sp = sp_ref[p]
            y = y + _dot(cact, sp, _NT) * jnp.where(first_half, jnp.exp(col[ja]), jnp.exp(col[jb]))
            y = y + jnp.where(first_half, dskip[ja], dskip[jb]) * xs
            last_a, last_b = col[ja][C - 1:C], col[jb][C - 1:C]
            xw = xdt * jnp.where(first_half, jnp.exp(last_a - col[ja]), jnp.exp(last_b - col[jb]))
            sp_ref[p] = jnp.where(row_p, jnp.exp(last_a), jnp.exp(last_b)) * sp + _dot(xw, bact, _TN)
            zg = zg_ref[0, pl.ds(t0, C), pl.ds(2 * SSM_P * p, 2 * SSM_P)]
            ys.append(y * _silu(zg))
        ms = sum(jnp.sum(y * y, axis=-1, keepdims=True) for y in ys) * (1.0 / (SSM_HPG * SSM_P))
        inv = lax.rsqrt(ms + EPS)
        for p in range(SSM_HPG // 2):
            y_ref[0, pl.ds(t0, C), pl.ds(2 * SSM_P * p, 2 * SSM_P)] = (
                ys[p] * inv * nw[:, 2 * SSM_P * p:2 * SSM_P * (p + 1)]).astype(y_ref.dtype)
        return carry

    lax.fori_loop(0, L // C, body, 0)
    for p in range(SSM_HPG // 2):
        s_out_ref[0, 2 * p:2 * p + 2] = sp_ref[p].reshape(2, SSM_P, LANES)


def _ssd_call(z, conv_w, conv_b, conv_state8, dtb_pad, alog_pad, dskip_pad, norm_w, s0, *, chunk, l_valid):
    B, L, _ = z.shape
    assert l_valid >= SSM_CONV - 1
    xw = SSM_HPG * SSM_P
    conv_rows = min(L, 256)
    ox, ob = OFF_SM_XBC // xw, (OFF_SM_XBC + BRANCH) // LANES
    oc = (OFF_SM_XBC + BRANCH + SSM_GROUPS * SSM_STATE) // LANES
    wb0, wc0 = BRANCH // LANES, (BRANCH + SSM_GROUPS * SSM_STATE) // LANES
    in_specs = [
        pl.BlockSpec((1, L, xw), lambda b, g: (b, 0, ox + g)),
        pl.BlockSpec((1, L, LANES), lambda b, g: (b, 0, ob + g)),
        pl.BlockSpec((1, L, LANES), lambda b, g: (b, 0, oc + g)),
        pl.BlockSpec((1, L, xw), lambda b, g: (b, 0, OFF_SM_Z // xw + g)),
        pl.BlockSpec((1, L, LANES), lambda b, g: (b, 0, OFF_SMALL // LANES)),
        pl.BlockSpec((SSM_CONV, xw), lambda b, g: (0, g)),
        pl.BlockSpec((SSM_CONV, LANES), lambda b, g: (0, wb0 + g)),
        pl.BlockSpec((SSM_CONV, LANES), lambda b, g: (0, wc0 + g)),
        pl.BlockSpec((1, xw), lambda b, g: (0, g)),
        pl.BlockSpec((1, LANES), lambda b, g: (0, wb0 + g)),
        pl.BlockSpec((1, LANES), lambda b, g: (0, wc0 + g)),
        pl.BlockSpec((1, SUBLANES, xw), lambda b, g: (b, 0, g)),
        pl.BlockSpec((1, SUBLANES, LANES), lambda b, g: (b, 0, wb0 + g)),
        pl.BlockSpec((1, SUBLANES, LANES), lambda b, g: (b, 0, wc0 + g)),
        pl.BlockSpec((1, LANES), lambda b, g: (0, 0)),
        pl.BlockSpec((1, LANES), lambda b, g: (0, 0)),
        pl.BlockSpec((1, LANES), lambda b, g: (0, 0)),
        pl.BlockSpec((1, xw), lambda b, g: (0, g)),
        pl.BlockSpec((1, SSM_HPG, SSM_P, LANES), lambda b, g: (b, g, 0, 0)),
    ]
    out_specs = [
        pl.BlockSpec((1, L, xw), lambda b, g: (b, 0, g)),
        pl.BlockSpec((1, SSM_HPG, SSM_P, LANES), lambda b, g: (b, g, 0, 0)),
        pl.BlockSpec((1, SUBLANES, xw), lambda b, g: (b, 0, g)),
        pl.BlockSpec((1, SUBLANES, LANES), lambda b, g: (b, 0, g)),
        pl.BlockSpec((1, SUBLANES, LANES), lambda b, g: (b, 0, g)),
    ]
    f32 = jnp.float32
    out_shape = [
        jax.ShapeDtypeStruct((B, L, BRANCH), jnp.bfloat16),
        jax.ShapeDtypeStruct((B, SSM_HEADS, SSM_P, LANES), f32),
        jax.ShapeDtypeStruct((B, SUBLANES, BRANCH), f32),
        jax.ShapeDtypeStruct((B, SUBLANES, SSM_GROUPS * SSM_STATE), f32),
        jax.ShapeDtypeStruct((B, SUBLANES, SSM_GROUPS * SSM_STATE), f32),
    ]
    scratch = [pltpu.VMEM((L, xw), f32), pltpu.VMEM((L, LANES), f32), pltpu.VMEM((L, LANES), f32),
               pltpu.VMEM((SUBLANES + conv_rows, xw), f32), pltpu.VMEM((SUBLANES + conv_rows, LANES), f32),
               pltpu.VMEM((SUBLANES + conv_rows, LANES), f32),
               pltpu.VMEM((SSM_HPG // 2, 2 * SSM_P, LANES), f32)]
    cbias = conv_b.reshape(1, -1)
    return pl.pallas_call(
        functools.partial(_ssd_kernel, chunk=chunk, l_valid=l_valid, conv_rows=conv_rows),
        grid=(B, SSM_GROUPS), in_specs=in_specs, out_specs=out_specs, out_shape=out_shape, scratch_shapes=scratch,
        compiler_params=_params(2),
        name="ssd",
    )(z, z, z, z, z, conv_w, conv_w, conv_w, cbias, cbias, cbias, conv_state8, conv_state8, conv_state8,
      dtb_pad, alog_pad, dskip_pad, norm_w.reshape(1, -1), s0)


def _s5_kernel(u_ref, bre_ref, bim_ref, cre_ref, cim_ref, are_ref, aim_ref, ldt_ref, d_ref, x0r_ref, x0i_ref,
               h_ref, xr_out_ref, xi_out_ref, bbr_ref, bbi_ref, xr_ref, xi_ref, pwr_ref, pwi_ref,
               *, nseg, seg_len):
    L = u_ref.shape[1]
    n_scan = nseg * seg_len
    row_blk = min(L, 256)
    a_re, a_im = are_ref[...], aim_ref[...]
    dt = jnp.exp(ldt_ref[...])
    mag = jnp.exp(a_re * dt)
    ab_re, ab_im = mag * jnp.cos(a_im * dt), mag * jnp.sin(a_im * dt)
    den = a_re * a_re + a_im * a_im
    nr = ab_re - 1.0
    coef_re = (nr * a_re + ab_im * a_im) / den
    coef_im = (ab_im * a_re - nr * a_im) / den

    NQ = S5_BLK_ST // LANES
    lq = lambda v, q: v[:, LANES * q:LANES * (q + 1)]
    abr = [lq(ab_re, q) for q in range(NQ)]
    abi = [lq(ab_im, q) for q in range(NQ)]

    for r0 in range(0, L, row_blk):
        u = u_ref[0, pl.ds(r0, row_blk), :]
        bu_re, bu_im = _dot(u, bre_ref[0]), _dot(u, bim_ref[0])
        bb_re = coef_re * bu_re - coef_im * bu_im
        bb_im = coef_re * bu_im + coef_im * bu_re
        for q in range(NQ):
            bbr_ref[q, pl.ds(r0, row_blk), :] = lq(bb_re, q)
            bbi_ref[q, pl.ds(r0, row_blk), :] = lq(bb_im, q)

    if n_scan < L:
        for q in range(NQ):
            xr_ref[q, pl.ds(n_scan, L - n_scan), :] = jnp.zeros((L - n_scan, LANES), jnp.float32)
            xi_ref[q, pl.ds(n_scan, L - n_scan), :] = jnp.zeros((L - n_scan, LANES), jnp.float32)

    def rows(i):
        return pl.ds(i, nseg, stride=seg_len) if nseg > 1 else pl.ds(i, 1)

    def scan(i, carry):
        out = []
        for q in range(NQ):
            xr, xi = carry[q]
            pr, pi = _cmul(abr[q], abi[q], xr, xi)
            xr, xi = pr + bbr_ref[q, rows(i), :], pi + bbi_ref[q, rows(i), :]
            xr_ref[q, rows(i), :] = xr
            xi_ref[q, rows(i), :] = xi
            out.append((xr, xi))
        return tuple(out)

    zero = jnp.zeros((nseg, LANES), jnp.float32)
    ends = lax.fori_loop(0, seg_len, scan, tuple((zero, zero) for _ in range(NQ)), unroll=4)

    for q in range(NQ):
        pwr_ref[q, pl.ds(0, 1), :] = abr[q]
        pwi_ref[q, pl.ds(0, 1), :] = abi[q]
    an_r, an_i = ab_re, ab_im
    n = 1
    while n < seg_len:
        m = min(n, seg_len - n)
        for q in range(NQ):
            pr, pi = _cmul(pwr_ref[q, pl.ds(0, m), :], pwi_ref[q, pl.ds(0, m), :], lq(an_r, q), lq(an_i, q))
            pwr_ref[q, pl.ds(n, m), :] = pr
            pwi_ref[q, pl.ds(n, m), :] = pi
        an_r, an_i = _cmul(an_r, an_i, an_r, an_i)
        n *= 2
    seg_r = [pwr_ref[q, pl.ds(seg_len - 1, 1), :] for q in range(NQ)]
    seg_i = [pwi_ref[q, pl.ds(seg_len - 1, 1), :] for q in range(NQ)]

    x0r, x0i = x0r_ref[0], x0i_ref[0]
    dskip = d_ref[...]
    c_r = [lq(x0r, q) for q in range(NQ)]
    c_i = [lq(x0i, q) for q in range(NQ)]
    for k in range(nseg):
        r0 = k * seg_len
        xr_q, xi_q = [], []
        for q in range(NQ):
            dr, di = _cmul(pwr_ref[q], pwi_ref[q], c_r[q], c_i[q])
            xr_q.append(xr_ref[q, pl.ds(r0, seg_len), :] + dr)
            xi_q.append(xi_ref[q, pl.ds(r0, seg_len), :] + di)
            pr, pi = _cmul(seg_r[q], seg_i[q], c_r[q], c_i[q])
            c_r[q], c_i[q] = pr + ends[q][0][k:k + 1], pi + ends[q][1][k:k + 1]
        if k == nseg - 1:
            for q in range(NQ):
                xr_out_ref[0, :, pl.ds(LANES * q, LANES)] = c_r[q]
                xi_out_ref[0, :, pl.ds(LANES * q, LANES)] = c_i[q]
        if n_scan == L:
            u = u_ref[0, pl.ds(r0, seg_len), :]
            y = (_dot(jnp.concatenate(xr_q, axis=-1), cre_ref[0]) - _dot(jnp.concatenate(xi_q, axis=-1), cim_ref[0])
                 + dskip * u)
            h_ref[0, pl.ds(r0, seg_len), :] = _gelu_tanh(y)
        else:
            for q in range(NQ):
                xr_ref[q, pl.ds(r0, seg_len), :] = xr_q[q]
                xi_ref[q, pl.ds(r0, seg_len), :] = xi_q[q]
    if n_scan < L:
        xr = jnp.concatenate([xr_ref[q] for q in range(NQ)], axis=-1)
        xi = jnp.concatenate([xi_ref[q] for q in range(NQ)], axis=-1)
        y = _dot(xr, cre_ref[0]) - _dot(xi, cim_ref[0]) + dskip * u_ref[0]
        h_ref[0] = _gelu_tanh(y)


def _s5seg_kernel(u_ref, bre_ref, bim_ref, cre_ref, cim_ref, are_ref, aim_ref, ldt_ref, d_ref, x0r_ref, x0i_ref,
                  h_ref, xr_out_ref, xi_out_ref, up_ref, bbr_ref, bbi_ref, xr_ref, xi_ref, pwr_ref, pwi_ref, yp_ref,
                  *, seg_len):
    NS = SUBLANES
    a_re, a_im = are_ref[...], aim_ref[...]
    dt = jnp.exp(ldt_ref[...])
    mag = jnp.exp(a_re * dt)
    ab_re, ab_im = mag * jnp.cos(a_im * dt), mag * jnp.sin(a_im * dt)
    den = a_re * a_re + a_im * a_im
    nr = ab_re - 1.0
    coef_re = (nr * a_re + ab_im * a_im) / den
    coef_im = (ab_im * a_re - nr * a_im) / den

    NQ = S5_BLK_ST // LANES
    lq = lambda v, q: v[:, LANES * q:LANES * (q + 1)]
    rep = lambda v: jnp.broadcast_to(v, (NS, LANES))
    abr = [rep(lq(ab_re, q)) for q in range(NQ)]
    abi = [rep(lq(ab_im, q)) for q in range(NQ)]
    blk = S5_BULK_STEPS
    n_blk = seg_len // blk

    def gather(i, carry):
        up_ref[pl.ds(pl.multiple_of(i * NS, NS), NS), :] = u_ref[0, pl.ds(i, NS, stride=seg_len), :]
        return carry

    lax.fori_loop(0, seg_len, gather, 0, unroll=8)

    for rb in range(n_blk):
        u = up_ref[pl.ds(rb * blk * NS, blk * NS), :]
        bu_re, bu_im = _dot(u, bre_ref[0]), _dot(u, bim_ref[0])
        bb_re = coef_re * bu_re - coef_im * bu_im
        bb_im = coef_re * bu_im + coef_im * bu_re
        for q in range(NQ):
            bbr_ref[q, pl.ds(rb * blk, blk)] = lq(bb_re, q).reshape(blk, NS, LANES)
            bbi_ref[q, pl.ds(rb * blk, blk)] = lq(bb_im, q).reshape(blk, NS, LANES)

    def scan(i, carry):
        out = []
        for q in range(NQ):
            xr, xi = carry[q]
            pr, pi = _cmul(abr[q], abi[q], xr, xi)
            xr, xi = pr + bbr_ref[q, i], pi + bbi_ref[q, i]
            xr_ref[q, i] = xr
            xi_ref[q, i] = xi
            out.append((xr, xi))
        return tuple(out)

    zero = jnp.zeros((NS, LANES), jnp.float32)
    ends = lax.fori_loop(0, seg_len, scan, tuple((zero, zero) for _ in range(NQ)), unroll=8)

    for q in range(NQ):
        pwr_ref[q, 0] = abr[q]
        pwi_ref[q, 0] = abi[q]
    an = [(abr[q], abi[q]) for q in range(NQ)]
    n = 1
    while n < seg_len:
        m = min(n, seg_len - n)
        for q in range(NQ):
            pr, pi = _cmul(pwr_ref[q, pl.ds(0, m)], pwi_ref[q, pl.ds(0, m)], an[q][0], an[q][1])
            pwr_ref[q, pl.ds(n, m)] = pr
            pwi_ref[q, pl.ds(n, m)] = pi
            an[q] = _cmul(an[q][0], an[q][1], an[q][0], an[q][1])
        n *= 2

    x0r, x0i = x0r_ref[0], x0i_ref[0]
    cs = []
    for q in range(NQ):
        seg_r, seg_i = pwr_ref[q, seg_len - 1][:1], pwi_ref[q, seg_len - 1][:1]
        cr, ci = [lq(x0r, q)], [lq(x0i, q)]
        for k in range(NS):
            pr, pi = _cmul(seg_r, seg_i, cr[k], ci[k])
            cr.append(pr + ends[q][0][k:k + 1])
            ci.append(pi + ends[q][1][k:k + 1])
        xr_out_ref[0, :, pl.ds(LANES * q, LANES)] = cr[NS]
        xi_out_ref[0, :, pl.ds(LANES * q, LANES)] = ci[NS]
        cs.append((jnp.concatenate(cr[:NS], axis=0), jnp.concatenate(ci[:NS], axis=0)))

    dskip = d_ref[...]
    for rb in range(n_blk):
        xr_q, xi_q = [], []
        for q in range(NQ):
            dr, di = _cmul(pwr_ref[q, pl.ds(rb * blk, blk)], pwi_ref[q, pl.ds(rb * blk, blk)], cs[q][0], cs[q][1])
            xr_q.append((xr_ref[q, pl.ds(rb * blk, blk)] + dr).reshape(blk * NS, LANES))
            xi_q.append((xi_ref[q, pl.ds(rb * blk, blk)] + di).reshape(blk * NS, LANES))
        u = up_ref[pl.ds(rb * blk * NS, blk * NS), :]
        y = (_dot(jnp.concatenate(xr_q, axis=-1), cre_ref[0]) - _dot(jnp.concatenate(xi_q, axis=-1), cim_ref[0])
             + dskip * u)
        yp_ref[pl.ds(rb * blk * NS, blk * NS), :] = _gelu_tanh(y)

    for k in range(NS):
        h_ref[0, pl.ds(k * seg_len, seg_len), :] = yp_ref[pl.ds(k, seg_len, stride=NS), :]


def _s5_call(z, bblk_re, bblk_im, cblk_re, cblk_im, a_re, a_im, log_dt_exp, d_flat, x0_re, x0_im, *, l_valid):
    B, L, _ = z.shape
    nb = bblk_re.shape[0]
    NQ = S5_BLK_ST // LANES
    f32 = jnp.float32
    if l_valid == L and L % (SUBLANES * S5_BULK_STEPS) == 0:
        seg_len = L // SUBLANES
        kern = functools.partial(_s5seg_kernel, seg_len=seg_len)
        tile = pltpu.VMEM((NQ, seg_len, SUBLANES, LANES), f32)
        scratch = [pltpu.VMEM((L, LANES), f32)] + [tile] * 6 + [pltpu.VMEM((L, LANES), f32)]
    else:
        kern = functools.partial(_s5_kernel, nseg=1, seg_len=l_valid)
        scratch = [pltpu.VMEM((NQ, L, LANES), f32)] * 4 + [pltpu.VMEM((NQ, l_valid, LANES), f32)] * 2
    vec = lambda w: pl.BlockSpec((1, w), lambda b, j: (0, j))
    st = pl.BlockSpec((1, 1, S5_BLK_ST), lambda b, j: (b, 0, j))
    return pl.pallas_call(
        kern,
        grid=(B, nb),
        in_specs=[pl.BlockSpec((1, L, S5_BLK_CH), lambda b, j: (b, 0, OFF_S5_U // S5_BLK_CH + j)),
                  pl.BlockSpec((1, S5_BLK_CH, S5_BLK_ST), lambda b, j: (j, 0, 0)),
                  pl.BlockSpec((1, S5_BLK_CH, S5_BLK_ST), lambda b, j: (j, 0, 0)),
                  pl.BlockSpec((1, S5_BLK_ST, S5_BLK_CH), lambda b, j: (j, 0, 0)),
                  pl.BlockSpec((1, S5_BLK_ST, S5_BLK_CH), lambda b, j: (j, 0, 0)),
                  vec(S5_BLK_ST), vec(S5_BLK_ST), vec(S5_BLK_ST), vec(S5_BLK_CH), st, st],
        out_specs=[pl.BlockSpec((1, L, S5_BLK_CH), lambda b, j: (b, 0, j)), st, st],
        out_shape=[jax.ShapeDtypeStruct((B, L, nb * S5_BLK_CH), jnp.float32),
                   jax.ShapeDtypeStruct((B, 1, nb * S5_BLK_ST), jnp.float32),
                   jax.ShapeDtypeStruct((B, 1, nb * S5_BLK_ST), jnp.float32)],
        scratch_shapes=scratch,
        compiler_params=_params(2),
        name="s5",
    )(z, bblk_re, bblk_im, cblk_re, cblk_im, a_re, a_im, log_dt_exp, d_flat, x0_re, x0_im)


def _s5_block_diag(b_re, b_im, c_re, c_im):
    G, P, Cc = b_re.shape
    nb = G // 8
    same = jnp.eye(8, dtype=bool)

    def bblk(b):
        t = jnp.transpose(b.reshape(nb, 8, P, Cc), (0, 1, 3, 2))[:, :, :, None, :]
        t = jnp.where(same[None, :, None, :, None], t, 0.0)
        return t.reshape(nb, 8 * Cc, 8 * P).astype(jnp.bfloat16)

    def cblk(c):
        t = jnp.transpose(c.reshape(nb, 8, Cc, P), (0, 1, 3, 2))[:, :, :, None, :]
        t = jnp.where(same[None, :, None, :, None], t, 0.0)
        return t.reshape(nb, 8 * P, 8 * Cc).astype(jnp.bfloat16)

    return bblk(b_re), bblk(b_im), cblk(c_re), cblk(c_im)


def _glu_kernel(h_ref, w_ref, b_ref, g_ref, o_ref):
    h = h_ref[...]
    t = _dot(h, w_ref[...]) + b_ref[...]
    o_ref[...] = (h * _sigmoid(t) * _silu(g_ref[...])).astype(o_ref.dtype)


def _glu_call(h2d, glu_w_bf16, glu_b, z2d, *, tm):
    M, W = h2d.shape
    return pl.pallas_call(
        _glu_kernel,
        grid=(M // tm,),
        in_specs=[pl.BlockSpec((tm, W), lambda i: (i, 0)),
                  pl.BlockSpec((W, W), lambda i: (0, 0)),
                  pl.BlockSpec((1, W), lambda i: (0, 0)),
                  pl.BlockSpec((tm, W), lambda i: (i, OFF_S5_G // W))],
        out_specs=pl.BlockSpec((tm, W), lambda i: (i, 0)),
        out_shape=jax.ShapeDtypeStruct((M, W), jnp.bfloat16),
        compiler_params=_params(1),
        name="glu",
    )(h2d, glu_w_bf16, glu_b.reshape(1, W), z2d)


def _kvprep_kernel(kv_ref, sm_ref, knw_ref, k_ref, v_ref, ik_ref, *rest, transposed):
    kvw = KV_HEADS * ATT_D
    kv = kv_ref[0]
    knw = knw_ref[...]
    ks = []
    for h in range(KV_HEADS):
        kh = kv[:, ATT_D * h:ATT_D * (h + 1)]
        ms = jnp.mean(kh * kh, axis=-1, keepdims=True)
        ks.append(kh * lax.rsqrt(ms + EPS) * knw)
    k = jnp.concatenate(ks, axis=-1)
    v = kv[:, kvw:]
    sm = sm_ref[0]
    k_ref[0] = k
    v_ref[0] = v
    ik_ref[0] = sm[:, :IDX_D]
    if transposed:
        kbf_ref, vt_ref, kia_ref, kib_ref = rest
        kbf_ref[0] = k.astype(jnp.bfloat16)
        vt_ref[0] = v.T.astype(jnp.bfloat16)
        lane = lax.broadcasted_iota(jnp.int32, sm.shape, 1)
        kia_ref[0] = jnp.where(lane < IDX_D, sm, 0.0).astype(jnp.bfloat16)
        kib_ref[0] = jnp.where(lane >= IDX_D, pltpu.roll(sm, IDX_D, axis=1), 0.0).astype(jnp.bfloat16)


def _kvprep_call(z, k_norm_w, *, tr, transposed):
    B, L, _ = z.shape
    kvw = KV_HEADS * ATT_D
    f32, bf = jnp.float32, jnp.bfloat16
    rowblk = lambda w: pl.BlockSpec((1, tr, w), lambda b, r: (b, r, 0))
    out_specs = [rowblk(kvw), rowblk(kvw), rowblk(IDX_D)]
    out_shape = [jax.ShapeDtypeStruct((B, L, kvw), f32), jax.ShapeDtypeStruct((B, L, kvw), f32),
                 jax.ShapeDtypeStruct((B, L, IDX_D), f32)]
    if transposed:
        out_specs += [rowblk(kvw), pl.BlockSpec((1, kvw, tr), lambda b, r: (b, 0, r)), rowblk(LANES), rowblk(LANES)]
        out_shape += [jax.ShapeDtypeStruct((B, L, kvw), bf), jax.ShapeDtypeStruct((B, kvw, L), bf),
                      jax.ShapeDtypeStruct((B, L, LANES), bf), jax.ShapeDtypeStruct((B, L, LANES), bf)]
    return pl.pallas_call(
        functools.partial(_kvprep_kernel, transposed=transposed),
        grid=(B, L // tr),
        in_specs=[pl.BlockSpec((1, tr, 2 * kvw), lambda b, r: (b, r, OFF_AT_KV // (2 * kvw))),
                  pl.BlockSpec((1, tr, LANES), lambda b, r: (b, r, OFF_SMALL // LANES)),
                  pl.BlockSpec((1, ATT_D), lambda b, r: (0, 0))],
        out_specs=out_specs, out_shape=out_shape,
        compiler_params=_params(2),
        name="kvprep",
    )(z, z, k_norm_w.reshape(1, ATT_D))


def _sortable_key(score):
    bits = lax.bitcast_convert_type(score, jnp.int32)
    return jnp.where(bits < 0, bits ^ 0x7FFFFFFF, bits)


def _rows_reduce(x, op):
    n = x.shape[0]
    slab = 8 * SUBLANES
    if n > slab and n % slab == 0:
        x = op(x.reshape(n // slab, slab, x.shape[1]), axis=0)
    return op(x, axis=0, keepdims=True)


def _kth_largest_key(key, k, axis):
    shape = list(key.shape)
    shape[axis] = 1

    def it(n, tu):
        cand_u = tu | jnp.left_shift(jnp.int32(1), 31 - n)
        cand_s = cand_u ^ jnp.int32(INT_MIN)
        ones = jnp.where(key >= cand_s, 1.0, 0.0)
        cnt = _rows_reduce(ones, jnp.sum) if axis == 0 else jnp.sum(ones, axis=axis, keepdims=True)
        return jnp.where(cnt >= k, cand_u, tu)

    tu = lax.fori_loop(0, 32, it, jnp.zeros(shape, jnp.int32))
    return tu ^ jnp.int32(INT_MIN)


def _t5_bucket(dist):
    exact = REL_BUCKETS // 2
    d = dist.astype(jnp.float32)
    large = exact + jnp.log(jnp.maximum(d, 1.0) / exact) / math.log(REL_MAX_DIST / exact) * (REL_BUCKETS - exact)
    large = jnp.minimum(jnp.maximum(large, 0.0).astype(jnp.int32), REL_BUCKETS - 1)
    return jnp.where(dist < exact, dist, large)


def _bias_lookup(rel_bias, bucket):
    onehot = (bucket[..., None] == jnp.arange(REL_BUCKETS, dtype=jnp.int32)).astype(jnp.float32)
    return jnp.einsum('...k,kh->...h', onehot, rel_bias.astype(jnp.float32), precision=lax.Precision.HIGHEST)


def _pattn_kernel(zq_ref, zg_ref, ziq_ref, zsm_ref, kbf_ref, vt_ref, kia_ref, kib_ref, qnw_ref, bias_ref, o_ref,
                  mask_ref, *, topk):
    i = pl.program_id(1)
    nqb = kbf_ref.shape[1] // Q_BLK
    qnw = qnw_ref[...]
    scale = ATT_D ** -0.5

    def body(S):
        nkb = S // Q_BLK
        wt = zsm_ref[0].T * (1.0 / 32.0)
        kia, kib = kia_ref[0, pl.ds(0, S), :], kib_ref[0, pl.ds(0, S), :]
        score = jnp.zeros((S, Q_BLK), jnp.float32)
        for p in range(IDX_HEADS // 2):
            qi = ziq_ref[0, :, pl.ds(LANES * p, LANES)].astype(jnp.bfloat16)
            da = lax.dot_general(kia, qi, _NT, preferred_element_type=jnp.float32)
            db = lax.dot_general(kib, qi, _NT, preferred_element_type=jnp.float32)
            r = SMALL_IXW + 2 * p
            score = score + jnp.maximum(da, 0.0) * wt[r:r + 1] + jnp.maximum(db, 0.0) * wt[r + 1:r + 2]
        s_pos = lax.broadcasted_iota(jnp.int32, (S, Q_BLK), 0)
        t_pos = lax.broadcasted_iota(jnp.int32, (S, Q_BLK), 1) + i * Q_BLK
        adm = s_pos <= t_pos
        key = jnp.where(adm, _sortable_key(score), jnp.int32(INT_MIN))
        thr = _kth_largest_key(key, topk, 0)
        ge = key >= thr
        cnt_ge = _rows_reduce(jnp.where(ge, 1.0, 0.0), jnp.sum)
        tied = (cnt_ge > topk) & (thr > jnp.int32(INT_MIN))
        any_tied = jnp.max(jnp.where(tied, 1.0, 0.0)) > 0.0

        @pl.when(jnp.logical_not(any_tied))
        def _():
            mask_ref[pl.ds(0, S), :] = jnp.where(ge & adm, 0.0, MASK_VALUE)

        @pl.when(any_tied)
        def _():
            gt = key > thr
            eq = key == thr
            need = topk - _rows_reduce(jnp.where(gt, 1.0, 0.0), jnp.sum)
            nbits = S.bit_length()

            def it(n, j_sel):
                cand = j_sel | jnp.left_shift(jnp.int32(1), nbits - 1 - n)
                c = _rows_reduce(jnp.where(eq & (s_pos < cand), 1.0, 0.0), jnp.sum)
                return jnp.where(c <= need, cand, j_sel)

            j_sel = lax.fori_loop(0, nbits, it, jnp.zeros((1, Q_BLK), jnp.int32))
            sel = (gt | (eq & (s_pos < j_sel))) & adm
            mask_ref[pl.ds(0, S), :] = jnp.where(sel, 0.0, MASK_VALUE)

        for h in range(ATT_HEADS):
            kvh = h // (ATT_HEADS // KV_HEADS)
            q = zq_ref[0, :, pl.ds(ATT_D * h, ATT_D)]
            ms = jnp.mean(q * q, axis=-1, keepdims=True)
            qn = (q * lax.rsqrt(ms + EPS) * qnw).astype(jnp.bfloat16)
            logit = lax.dot_general(kbf_ref[0, pl.ds(0, S), pl.ds(ATT_D * kvh, ATT_D)], qn, _NT,
                                    preferred_element_type=jnp.float32) * scale
            bias = jnp.concatenate([bias_ref[jnp.clip(i - j, 0, N_BIAS_TILES - 1), h] for j in range(nkb)], axis=0)
            x = logit + bias + mask_ref[pl.ds(0, S), :]
            m = _rows_reduce(x, jnp.max)
            pexp = jnp.exp(x - m)
            l = _rows_reduce(pexp, jnp.sum)
            ot = jnp.dot(vt_ref[0, pl.ds(ATT_D * kvh, ATT_D), pl.ds(0, S)], pexp.astype(jnp.bfloat16),
                         preferred_element_type=jnp.float32)
            out = (ot / l).T
            g = zg_ref[0, :, pl.ds(ATT_D * h, ATT_D)]
            o_ref[0, :, pl.ds(ATT_D * h, ATT_D)] = (out * _silu(g)).astype(o_ref.dtype)

    nv = min(N_CAUSAL_VARIANTS, nqb)
    per = nqb // nv
    for c in range(nv):
        pl.when(i // per == c)(functools.partial(body, (c + 1) * per * Q_BLK))


def _pattn_call(z, kbf, vt, kia, kib, q_norm_w, bias_tiles, *, topk):
    B, L, _ = z.shape
    W = ATT_HEADS * ATT_D
    kvw = KV_HEADS * ATT_D
    zblk = lambda off, w: pl.BlockSpec((1, Q_BLK, w), lambda b, i: (b, i, off // w))
    full = lambda shape: pl.BlockSpec((1,) + shape, lambda b, i: (b, 0, 0))
    return pl.pallas_call(
        functools.partial(_pattn_kernel, topk=topk),
        grid=(B, L // Q_BLK),
        in_specs=[zblk(OFF_AT_Q, W), zblk(OFF_AT_G, W), zblk(OFF_IX_Q, W), zblk(OFF_SMALL, LANES),
                  full((L, kvw)), full((kvw, L)), full((L, LANES)), full((L, LANES)),
                  pl.BlockSpec((1, ATT_D), lambda b, i: (0, 0)),
                  pl.BlockSpec((N_BIAS_TILES, ATT_HEADS, Q_BLK, Q_BLK), lambda b, i: (0, 0, 0, 0))],
        out_specs=pl.BlockSpec((1, Q_BLK, W), lambda b, i: (b, i, 0)),
        out_shape=jax.ShapeDtypeStruct((B, L, W), jnp.bfloat16),
        scratch_shapes=[pltpu.VMEM((L, Q_BLK), jnp.float32)],
        compiler_params=_params(2),
        name="pattn",
    )(z, z, z, z, kbf, vt, kia, kib, q_norm_w.reshape(1, ATT_D), bias_tiles)


def _prompt_bias_tiles(rel_bias):
    o = jnp.arange(N_BIAS_TILES, dtype=jnp.int32)[:, None, None]
    s = jnp.arange(Q_BLK, dtype=jnp.int32)[None, :, None]
    t = jnp.arange(Q_BLK, dtype=jnp.int32)[None, None, :]
    tiles = _bias_lookup(rel_bias, _t5_bucket(jnp.maximum(Q_BLK * o + t - s, 0)))
    return jnp.moveaxis(tiles, -1, 1)


def _sattn_score_kernel(pt_ref, qi_ref, w_ref, iknew_ref, *rest):
    pages, s_ref = rest[:NPG], rest[NPG]
    pc = pl.program_id(1)
    last = pl.num_programs(1) - 1
    qi = qi_ref[0]
    w = w_ref[0]

    def head_sum(d):
        r = jnp.maximum(d, 0.0) * w
        return jnp.sum(r.reshape(T_PAD, IDX_HEADS, r.shape[-1]), axis=1)

    @pl.when(pc < last)
    def _():
        for i in range(NPG):
            d = jnp.dot(qi, pages[i][0, 0].astype(jnp.bfloat16), preferred_element_type=jnp.float32)
            s_ref[0, :, pl.ds(PAGE * i, PAGE)] = head_sum(d)

    @pl.when(pc == last)
    def _():
        s_ref[0] = jnp.full((T_PAD, CHUNK_KEYS), MASK_VALUE, jnp.float32)
        d = lax.dot_general(qi, iknew_ref[0].astype(jnp.bfloat16), _NT, preferred_element_type=jnp.float32)
        s_ref[0, :, pl.ds(0, PAGE)] = head_sum(d)


def _sattn_score_call(page_table, qi_rows, w_col, ik_new, cache_ik_t, layer):
    B, n_pages = page_table.shape
    n_chunks = n_pages // NPG
    page_spec = lambda i: pl.BlockSpec(
        (1, 1, IDX_D, PAGE), lambda b, pc, pt: (layer, pt[b, jnp.minimum(pc * NPG + i, n_pages - 1)], 0, 0))
    per_b = lambda shape: pl.BlockSpec((1,) + shape, lambda b, pc, pt: (b, 0, 0))
    grid_spec = pltpu.PrefetchScalarGridSpec(
        num_scalar_prefetch=1, grid=(B, n_chunks + 1),
        in_specs=[per_b((T_PAD * IDX_HEADS, IDX_D)), per_b((T_PAD * IDX_HEADS, 1)), per_b((PAGE, IDX_D))]
        + [page_spec(i) for i in range(NPG)],
        out_specs=pl.BlockSpec((1, T_PAD, CHUNK_KEYS), lambda b, pc, pt: (b, 0, pc)))
    return pl.pallas_call(
        _sattn_score_kernel,
        grid_spec=grid_spec,
        out_shape=jax.ShapeDtypeStruct((B, T_PAD, (n_chunks + 1) * CHUNK_KEYS), jnp.float32),
        compiler_params=_params(2),
        name="sattn_score",
    )(page_table, qi_rows, w_col, ik_new, *([cache_ik_t] * NPG))


def _sattn_kernel(pt_ref, s_ref, q_ref, g_ref, qnw_ref, bias_ref, knew_ref, vnew_ref, *rest, topk, n_new, past):
    kp, vp = rest[:NPG], rest[NPG:2 * NPG]
    o_ref, thr_ref, tie_ref, qn_ref, m_ref, l_ref, acc_ref = rest[2 * NPG:]
    pc = pl.program_id(1)
    last = pl.num_programs(1) - 1
    G = ATT_HEADS // KV_HEADS

    def admissible(col0, width):
        pos = lax.broadcasted_iota(jnp.int32, (T_PAD, width), 1) + col0
        t = lax.broadcasted_iota(jnp.int32, (T_PAD, width), 0)
        return (pos < past) | ((pos - past <= t) & (pos - past < n_new))

    @pl.when(pc == 0)
    def _():
        sc = s_ref[0]
        n_all = sc.shape[-1]
        adm0 = admissible(0, n_all)
        key = jnp.where(adm0, _sortable_key(sc), jnp.int32(INT_MIN))
        thr = _kth_largest_key(key, topk, 1)
        thr_ref[...] = jnp.broadcast_to(thr, thr_ref.shape)
        tie_ref[...] = jnp.full(tie_ref.shape, n_all, jnp.int32)
        cnt_ge = jnp.sum(jnp.where(key >= thr, 1.0, 0.0), axis=1, keepdims=True)
        tied = (cnt_ge > topk) & (thr > jnp.int32(INT_MIN))

        @pl.when(jnp.max(jnp.where(tied, 1.0, 0.0)) > 0.0)
        def _():
            eq = key == thr
            pos = lax.broadcasted_iota(jnp.int32, key.shape, 1)
            need = topk - jnp.sum(jnp.where(key > thr, 1.0, 0.0), axis=1, keepdims=True)
            nbits = n_all.bit_length()

            def it(n, j_sel):
                cand = j_sel | jnp.left_shift(jnp.int32(1), nbits - 1 - n)
                c = jnp.sum(jnp.where(eq & (pos < cand), 1.0, 0.0), axis=1, keepdims=True)
                return jnp.where(c <= need, cand, j_sel)

            j_sel = lax.fori_loop(0, nbits, it, jnp.zeros((T_PAD, 1), jnp.int32))
            tie_ref[...] = jnp.broadcast_to(j_sel, tie_ref.shape)

        q = q_ref[0]
        ms = jnp.mean(q * q, axis=-1, keepdims=True)
        qn_ref[...] = (q * lax.rsqrt(ms + EPS) * qnw_ref[...]).astype(jnp.bfloat16)
        m_ref[...] = jnp.full(m_ref.shape, MASK_VALUE, jnp.float32)
        l_ref[...] = jnp.zeros(l_ref.shape, jnp.float32)
        acc_ref[...] = jnp.zeros(acc_ref.shape, jnp.float32)

    def attend(get_k, get_v, width):
        col0 = pl.multiple_of(pc * CHUNK_KEYS, CHUNK_KEYS)
        sc = s_ref[0, :, pl.ds(col0, width)]
        adm = admissible(col0, width)
        key = jnp.where(adm, _sortable_key(sc), jnp.int32(INT_MIN))
        pos = lax.broadcasted_iota(jnp.int32, (T_PAD, width), 1) + col0
        thr = thr_ref[:, :1]
        sel8 = ((key > thr) | ((key == thr) & (pos < tie_ref[:, :1]))) & adm
        sel = jnp.concatenate([sel8] * G, axis=0)
        for kv in range(KV_HEADS):
            kk = get_k(kv).astype(jnp.bfloat16)
            x = lax.dot_general(qn_ref[kv], kk, _NT, preferred_element_type=jnp.float32) * (ATT_D ** -0.5)
            x = jnp.where(sel, x + bias_ref[kv, :, pl.ds(0, width)], MASK_VALUE)
            m_old = m_ref[kv]
            m_new = jnp.maximum(m_old, jnp.max(x, axis=-1, keepdims=True))
            p = jnp.where(sel, jnp.exp(x - m_new), 0.0)
            alpha = jnp.exp(m_old - m_new)
            l_ref[kv] = alpha * l_ref[kv] + jnp.sum(p, axis=-1, keepdims=True)
            acc_ref[kv] = alpha * acc_ref[kv] + jnp.dot(
                p.astype(jnp.bfloat16), get_v(kv).astype(jnp.bfloat16),
                preferred_element_type=jnp.float32)
            m_ref[kv] = m_new

    @pl.when(pc < last)
    def _():
        attend(lambda kv: jnp.concatenate([kp[i][0, 0, :, kv, :] for i in range(NPG)], axis=0),
               lambda kv: jnp.concatenate([vp[i][0, 0, :, kv, :] for i in range(NPG)], axis=0), CHUNK_KEYS)

    @pl.when(pc == last)
    def _():
        attend(lambda kv: knew_ref[0, :, pl.ds(ATT_D * kv, ATT_D)],
               lambda kv: vnew_ref[0, :, pl.ds(ATT_D * kv, ATT_D)], PAGE)
        o_ref[0] = acc_ref[...] / l_ref[...] * _silu(g_ref[0])


def _sattn_call(page_table, scores, q_rows, g_rows, q_norm_w, bias, k_new, v_new, cache_k, cache_v, layer,
                *, topk, n_new):
    B, n_pages = page_table.shape
    n_chunks = n_pages // NPG
    kvw = KV_HEADS * ATT_D
    rows = (ATT_HEADS // KV_HEADS) * T_PAD
    n_keys = (n_chunks + 1) * CHUNK_KEYS
    page_spec = lambda i: pl.BlockSpec(
        (1, 1, PAGE, KV_HEADS, ATT_D),
        lambda b, pc, pt: (layer, pt[b, jnp.minimum(pc * NPG + i, n_pages - 1)], 0, 0, 0))
    per_b = lambda shape: pl.BlockSpec((1,) + shape, lambda b, pc, pt: (b,) + (0,) * len(shape))
    grid_spec = pltpu.PrefetchScalarGridSpec(
        num_scalar_prefetch=1, grid=(B, n_chunks + 1),
        in_specs=[per_b((T_PAD, n_keys)), per_b((KV_HEADS, rows, ATT_D)), per_b((KV_HEADS, rows, ATT_D)),
                  pl.BlockSpec((1, ATT_D), lambda b, pc, pt: (0, 0)),
                  pl.BlockSpec((KV_HEADS, rows, CHUNK_KEYS), lambda b, pc, pt: (0, 0, pc)),
                  per_b((PAGE, kvw)), per_b((PAGE, kvw))]
        + [page_spec(i) for i in range(NPG)] + [page_spec(i) for i in range(NPG)],
        out_specs=per_b((KV_HEADS, rows, ATT_D)),
        scratch_shapes=[pltpu.VMEM((T_PAD, LANES), jnp.int32), pltpu.VMEM((T_PAD, LANES), jnp.int32),
                        pltpu.VMEM((KV_HEADS, rows, ATT_D), jnp.bfloat16),
                        pltpu.VMEM((KV_HEADS, rows, 1), jnp.float32), pltpu.VMEM((KV_HEADS, rows, 1), jnp.float32),
                        pltpu.VMEM((KV_HEADS, rows, ATT_D), jnp.float32)])
    return pl.pallas_call(
        functools.partial(_sattn_kernel, topk=topk, n_new=n_new, past=n_pages * PAGE),
        grid_spec=grid_spec,
        out_shape=jax.ShapeDtypeStruct((B, KV_HEADS, rows, ATT_D), jnp.float32),
        compiler_params=_params(2),
        name="sattn",
    )(page_table, scores, q_rows, g_rows, q_norm_w.reshape(1, ATT_D), bias, k_new, v_new,
      *([cache_k] * NPG), *([cache_v] * NPG))


def _sample_bias(rel_bias, past, n_keys):
    G = ATT_HEADS // KV_HEADS
    t = jnp.arange(T_PAD, dtype=jnp.int32)[:, None]
    pos = jnp.arange(n_keys, dtype=jnp.int32)[None, :]
    tab = _bias_lookup(rel_bias, _t5_bucket(jnp.maximum(past + t - pos, 0)))
    return jnp.moveaxis(tab, -1, 0).reshape(KV_HEADS, G * T_PAD, n_keys)


def _to_head_rows(a):
    B = a.shape[0]
    G = ATT_HEADS // KV_HEADS
    a = a.reshape(B, T_PAD, KV_HEADS, G, ATT_D)
    return jnp.transpose(a, (0, 2, 3, 1, 4)).reshape(B, KV_HEADS, G * T_PAD, ATT_D)


def _from_head_rows(a):
    B = a.shape[0]
    G = ATT_HEADS // KV_HEADS
    a = a.reshape(B, KV_HEADS, G, T_PAD, ATT_D)
    return jnp.transpose(a, (0, 3, 1, 2, 4)).reshape(B, T_PAD, ATT_HEADS * ATT_D)


def _layer(x, lp, states, attend, *, l_valid, tm, chunk_hg, chunk_ssd):
    B, L, _ = x.shape
    hg_s0, ssm_s0, conv_s0, s5_re0, s5_im0 = states
    x2d = x.reshape(B * L, D_MODEL)
    z2d = _inproj(x2d, lp['norm_w'], lp['w_in'], tm)
    z = z2d.reshape(B, L, D_IN_PAD)

    o_hg, hg_s = _hgrn_call(z, lp['lb'], lp['hg_norm_w'], hg_s0, chunk=chunk_hg, l_valid=l_valid)

    conv8 = jnp.pad(conv_s0, ((0, 0), (SUBLANES - (SSM_CONV - 1), 0), (0, 0)))
    y_ssm, ssm_s, cx, cb, cc = _ssd_call(z, lp['conv_w'], lp['conv_b'], conv8, lp['dtb_pad'], lp['alog_pad'],
                                         lp['dskip_pad'], lp['ssm_norm_w'], ssm_s0, chunk=chunk_ssd, l_valid=l_valid)
    tail = SUBLANES - (SSM_CONV - 1)
    conv_s = jnp.concatenate([cx[:, tail:], cb[:, tail:], cc[:, tail:]], axis=-1)

    h5, s5_re, s5_im = _s5_call(z, *lp['s5_blocks'], lp['a_re'], lp['a_im'], lp['log_dt'], lp['s5_d'],
                                s5_re0.reshape(B, 1, -1), s5_im0.reshape(B, 1, -1), l_valid=l_valid)
    o5 = _glu_call(h5.reshape(B * L, BRANCH), lp['glu_w'], lp['glu_b'], z2d, tm=tm)

    o_att, k, v, ik = attend(z)

    y = _outproj((o_hg.reshape(B * L, BRANCH), y_ssm.reshape(B * L, BRANCH), o5, o_att.reshape(B * L, BRANCH)),
                 lp['w_out'], lp['layer'], x2d, tm)
    st = (k[:, :l_valid].reshape(B, l_valid, KV_HEADS, ATT_D), v[:, :l_valid].reshape(B, l_valid, KV_HEADS, ATT_D),
          ik[:, :l_valid], hg_s, ssm_s, conv_s,
          s5_re.reshape(B, S5_GROUPS, S5_STATE), s5_im.reshape(B, S5_GROUPS, S5_STATE))
    return y.reshape(B, L, D_MODEL), st


def kernel(x_prompt, x_sample, cache_k, cache_v, cache_idx_k, state_hgrn, state_ssm, state_conv, state_s5_re, state_s5_im, page_table, norm_w, w_in, w_out, hg_lb_logits, hg_norm_w, ssm_conv_w, ssm_conv_b, ssm_dt_bias, ssm_a_log, ssm_d, ssm_norm_w, s5_a_re, s5_a_im, s5_log_dt, s5_b_re, s5_b_im, s5_c_re, s5_c_im, s5_d, s5_glu_w, s5_glu_b, att_q_norm, att_k_norm, rel_bias):
    f32 = jnp.float32
    bp, lp_len, _ = x_prompt.shape
    bs, ls, _ = x_sample.shape
    n_pages = page_table.shape[1]
    past = n_pages * PAGE
    n_keys = (n_pages // NPG + 1) * CHUNK_KEYS

    sm = jax.nn.softmax(hg_lb_logits.astype(f32), axis=0)
    lower = jnp.cumsum(sm, axis=0) - sm[0]

    w_out_bf16 = w_out.astype(jnp.bfloat16)
    w_in_t = jnp.swapaxes(w_in, 1, 2)
    cache_ik_t = jnp.swapaxes(cache_idx_k, 2, 3)
    bias_tiles = _prompt_bias_tiles(rel_bias)
    bias_sample = _sample_bias(rel_bias, past, n_keys)
    topk_p = min(TOPK_MAX, lp_len // 4)
    topk_s = min(TOPK_MAX, (past + ls) // 4)

    def head_lanes(v):
        return jnp.zeros((1, LANES), f32).at[0, SMALL_DT:SMALL_DT + SSM_HEADS].set(v)

    xs_pad = jnp.pad(x_sample, ((0, 0), (0, SAMPLE_ROWS - ls), (0, 0)))
    pad_page = lambda a: jnp.pad(a, ((0, 0), (0, PAGE - a.shape[1]), (0, 0)))

    yp, ys = x_prompt, xs_pad
    new_p = [[] for _ in range(8)]
    new_s = [[] for _ in range(8)]
    for l in range(DEPTH):
        lp = {'norm_w': norm_w[l], 'w_in': _pack_w_in(w_in_t, l), 'w_out': w_out_bf16, 'layer': l,
              'lb': lower[l], 'hg_norm_w': hg_norm_w[l],
              'conv_w': ssm_conv_w[l], 'conv_b': ssm_conv_b[l], 'dtb_pad': head_lanes(ssm_dt_bias[l]),
              'alog_pad': head_lanes(ssm_a_log[l]), 'dskip_pad': head_lanes(ssm_d[l]), 'ssm_norm_w': ssm_norm_w[l],
              's5_blocks': _s5_block_diag(s5_b_re[l], s5_b_im[l], s5_c_re[l], s5_c_im[l]),
              'a_re': s5_a_re[l].reshape(1, -1), 'a_im': s5_a_im[l].reshape(1, -1),
              'log_dt': jnp.repeat(s5_log_dt[l], S5_STATE).reshape(1, -1), 's5_d': s5_d[l].reshape(1, -1),
              'glu_w': s5_glu_w[l].astype(jnp.bfloat16), 'glu_b': s5_glu_b[l]}

        def attend_prompt(z):
            k, v, ik, kbf, vt, kia, kib = _kvprep_call(z, att_k_norm[l], tr=256, transposed=True)
            o = _pattn_call(z, kbf, vt, kia, kib, att_q_norm[l], bias_tiles, topk=topk_p)
            return o, k, v, ik

        def attend_sample(z):
            k, v, ik = _kvprep_call(z, att_k_norm[l], tr=SAMPLE_ROWS, transposed=False)
            z8 = z[:, :T_PAD]
            qi_rows = z8[..., OFF_IX_Q:OFF_IX_Q + IDX_HEADS * IDX_D].reshape(bs, T_PAD * IDX_HEADS, IDX_D)
            w_col = z8[..., OFF_SMALL + SMALL_IXW:OFF_SMALL + SMALL_IXW + IDX_HEADS] * (1.0 / 32.0)
            scores = _sattn_score_call(page_table, qi_rows.astype(jnp.bfloat16),
                                       w_col.reshape(bs, T_PAD * IDX_HEADS, 1), pad_page(ik), cache_ik_t, l)
            o = _sattn_call(page_table, scores, _to_head_rows(z8[..., OFF_AT_Q:OFF_AT_Q + BRANCH]),
                            _to_head_rows(z8[..., OFF_AT_G:OFF_AT_G + BRANCH]), att_q_norm[l], bias_sample,
                            pad_page(k), pad_page(v), cache_k, cache_v, l, topk=topk_s, n_new=ls)
            o = jnp.pad(_from_head_rows(o), ((0, 0), (0, SAMPLE_ROWS - T_PAD), (0, 0)))
            return o.astype(jnp.bfloat16), k, v, ik

        zero_states = (jnp.zeros((bp, HG_HEADS, LANES, LANES), f32),
                       jnp.zeros((bp, SSM_HEADS, SSM_P, SSM_STATE), f32),
                       jnp.zeros((bp, SSM_CONV - 1, SSM_CONV_DIM), f32),
                       jnp.zeros((bp, S5_GROUPS, S5_STATE), f32),
                       jnp.zeros((bp, S5_GROUPS, S5_STATE), f32))
        yp, st_p = _layer(yp, lp, zero_states, attend_prompt, l_valid=lp_len, tm=512, chunk_hg=64, chunk_ssd=128)
        samp_states = (state_hgrn[l], state_ssm[l], state_conv[l], state_s5_re[l], state_s5_im[l])
        ys, st_s = _layer(ys, lp, samp_states, attend_sample, l_valid=ls, tm=bs * SAMPLE_ROWS,
                          chunk_hg=SAMPLE_ROWS, chunk_ssd=SAMPLE_ROWS)
        for i in range(8):
            new_p[i].append(st_p[i])
            new_s[i].append(st_s[i])
    pk, pv, pik, phg, pssm, pconv, ps5r, ps5i = [jnp.stack(a) for a in new_p]
    sk, sv, sik, shg, sssm, sconv, ss5r, ss5i = [jnp.stack(a) for a in new_s]
    return (yp, ys[:, :ls], pk, pv, pik, phg, pssm, pconv, ps5r, ps5i, sk, sv, sik, shg, sssm, sconv, ss5r, ss5i)
```

```python
import functools
import math

import jax
import jax.numpy as jnp
from jax import lax
from jax.experimental import pallas as pl
from jax.experimental.pallas import tpu as pltpu

D_MODEL = 4096
DEPTH = 2
BRANCH = D_MODEL // 4
HG_HEADS = 8
SSM_HEADS = 16
SSM_GROUPS = 4
SSM_STATE = 128
SSM_CONV = 4
SSM_CONV_DIM = BRANCH + 2 * SSM_GROUPS * SSM_STATE
SSM_HPG = SSM_HEADS // SSM_GROUPS
SSM_P = 64
S5_GROUPS = 64
S5_STATE = 64
S5_BLK_CH = 128
S5_BLK_ST = 512
ATT_D = 128
ATT_HEADS = 8
KV_HEADS = 2
IDX_HEADS = 16
IDX_D = 64
TOPK_MAX = 256
Q_BLK = 128
PAGE = 128
REL_BUCKETS = 32
REL_MAX_DIST = 1024
EPS = 1e-6
MASK_VALUE = -1e30
F_FLOOR = 1e-30
INT_MIN = -2 ** 31

LANES = 128
SUBLANES = 8
V7X_VMEM_LIMIT = 56 * 1024 * 1024

_SRC_DT, _SRC_S5U, _SRC_ATK, _SRC_ATG, _SRC_IXK, D_IN_SRC = 7168, 7184, 10256, 10768, 12816, 12896

OFF_HG_Q, OFF_HG_F, OFF_HG_I, OFF_HG_G = 0, 1024, 2048, 3072
OFF_SM_Z, OFF_SM_XBC = 4096, 5120
OFF_S5_U, OFF_S5_G = 7168, 8192
OFF_AT_Q, OFF_AT_G, OFF_IX_Q = 9216, 10240, 11264
OFF_AT_KV = 12288
OFF_SMALL = 12800
SMALL_IXW, SMALL_DT = 64, 80
D_IN_PAD = 13312

PACK_TN = 1024
PACK_TK = 1024
_PACK_LAST = D_IN_PAD // PACK_TN - 1
_PACK_ROW0 = (tuple(range(0, _SRC_DT, PACK_TN)) + tuple(range(_SRC_S5U, _SRC_ATK, PACK_TN))
              + tuple(range(_SRC_ATG, _SRC_IXK, PACK_TN)) + (_SRC_ATK,))
N_KV_ROWS = _SRC_ATG - _SRC_ATK
N_IX_ROWS = D_IN_SRC - _SRC_IXK
N_DT_ROWS = _SRC_S5U - _SRC_DT
NORM_ROWS = 128
S5_BULK_STEPS = 32

N_BIAS_TILES = 9
SSD_GPB = 2
HG_SUB = 16
HG_HPB = 8
HG_ROW_BLOCK = 512
N_CAUSAL_VARIANTS = 4
NPG = 8
T_PAD = 8
CHUNK_KEYS = NPG * PAGE
SAMPLE_ROWS = 16

_TN = (((0,), (0,)), ((), ()))
_NT = (((1,), (1,)), ((), ()))


def _params(n_axes):
    return pltpu.CompilerParams(dimension_semantics=("arbitrary",) * n_axes, vmem_limit_bytes=V7X_VMEM_LIMIT)


def _bf(x):
    return x.astype(jnp.bfloat16)


def _dot(a, b, dims=None):
    if dims is None:
        return jnp.dot(_bf(a), _bf(b), preferred_element_type=jnp.float32)
    return lax.dot_general(_bf(a), _bf(b), dims, preferred_element_type=jnp.float32)


def _sigmoid(x):
    return 1.0 / (1.0 + jnp.exp(-x))


def _silu(x):
    return x * _sigmoid(x)


def _softplus(x):
    return jnp.maximum(x, 0.0) + jnp.log(1.0 + jnp.exp(-jnp.abs(x)))


def _gelu_tanh(x):
    return 0.5 * x * (1.0 + jnp.tanh(math.sqrt(2.0 / math.pi) * (x + 0.044715 * (x * x * x))))


def _cumsum_rows(x):
    n = x.shape[0]
    row = lax.broadcasted_iota(jnp.int32, x.shape, 0)
    sh = 1
    while sh < n:
        x = x + jnp.where(row >= sh, pltpu.roll(x, sh, axis=0), 0.0)
        sh *= 2
    return x


def _split3(x):
    hi = x.astype(jnp.bfloat16)
    r = x - hi.astype(jnp.float32)
    mid = r.astype(jnp.bfloat16)
    lo = (r - mid.astype(jnp.float32)).astype(jnp.bfloat16)
    return hi, mid, lo


def _select_lanes_as_rows(sel, x):
    out = None
    for part in _split3(x):
        t = lax.dot_general(sel, part, _NT, preferred_element_type=jnp.float32)
        out = t if out is None else out + t
    return out


def _cmul(ar, ai, br, bi):
    return ar * br - ai * bi, ar * bi + ai * br


def _pack_kernel(off_ref, w_ref, ix_ref, dt_ref, o_ref):
    j = pl.program_id(0)

    @pl.when(j < _PACK_LAST)
    def _():
        o_ref[...] = w_ref[0].astype(o_ref.dtype)

    @pl.when(j == _PACK_LAST)
    def _():
        o_ref[pl.ds(0, N_KV_ROWS), :] = w_ref[0, pl.ds(0, N_KV_ROWS), :].astype(o_ref.dtype)
        o_ref[pl.ds(N_KV_ROWS, N_IX_ROWS), :] = ix_ref[0].astype(o_ref.dtype)
        o_ref[pl.ds(N_KV_ROWS + N_IX_ROWS, N_DT_ROWS), :] = dt_ref[0].astype(o_ref.dtype)
        used = N_KV_ROWS + N_IX_ROWS + N_DT_ROWS
        o_ref[pl.ds(used, PACK_TN - used), :] = jnp.zeros((PACK_TN - used, PACK_TK), o_ref.dtype)


def _pack_w_in(w_in_t, layer):
    assert all(r % SUBLANES == 0 for r in _PACK_ROW0)
    row0 = jnp.asarray([r // SUBLANES for r in _PACK_ROW0], jnp.int32)
    E = pl.Element
    fixed = lambda rows, row: pl.BlockSpec(
        (E(1), E(rows), E(PACK_TK)), lambda j, kc, off: (layer, row, kc * PACK_TK))
    grid_spec = pltpu.PrefetchScalarGridSpec(
        num_scalar_prefetch=1, grid=(D_IN_PAD // PACK_TN, D_MODEL // PACK_TK),
        in_specs=[pl.BlockSpec((E(1), E(PACK_TN), E(PACK_TK)),
                               lambda j, kc, off: (layer, off[j] * SUBLANES, kc * PACK_TK)),
                  fixed(N_IX_ROWS, _SRC_IXK), fixed(N_DT_ROWS, _SRC_DT)],
        out_specs=pl.BlockSpec((PACK_TN, PACK_TK), lambda j, kc, off: (j, kc)))
    return pl.pallas_call(
        _pack_kernel, grid_spec=grid_spec,
        out_shape=jax.ShapeDtypeStruct((D_IN_PAD, D_MODEL), jnp.bfloat16),
        compiler_params=_params(2),
        name="pack_w_in",
    )(row0, w_in_t, w_in_t, w_in_t)


def _inproj_kernel(x_ref, nw_ref, w_ref, z_ref, hn_ref):
    @pl.when(pl.program_id(1) == 0)
    def _():
        nw = nw_ref[...]
        for r0 in range(0, x_ref.shape[0], NORM_ROWS):
            xf = x_ref[pl.ds(r0, NORM_ROWS), :]
            ms = jnp.mean(xf * xf, axis=-1, keepdims=True)
            hn_ref[pl.ds(r0, NORM_ROWS), :] = (xf * lax.rsqrt(ms + EPS) * nw).astype(jnp.bfloat16)

    z_ref[...] = lax.dot_general(hn_ref[...], w_ref[...], _NT, preferred_element_type=jnp.float32)


def _inproj(x2d, norm_w, w_packed, tm, tn=1024):
    m = x2d.shape[0]
    return pl.pallas_call(
        _inproj_kernel,
        grid=(m // tm, D_IN_PAD // tn),
        in_specs=[pl.BlockSpec((tm, D_MODEL), lambda i, j: (i, 0)),
                  pl.BlockSpec((1, D_MODEL), lambda i, j: (0, 0)),
                  pl.BlockSpec((tn, D_MODEL), lambda i, j: (j, 0))],
        out_specs=pl.BlockSpec((tm, tn), lambda i, j: (i, j)),
        out_shape=jax.ShapeDtypeStruct((m, D_IN_PAD), jnp.float32),
        scratch_shapes=[pltpu.VMEM((tm, D_MODEL), jnp.bfloat16)],
        compiler_params=_params(2),
        name="inproj",
    )(x2d, norm_w.reshape(1, D_MODEL), w_packed)


def _outproj_kernel(m0_ref, m1_ref, m2_ref, m3_ref, w_ref, x_ref, y_ref):
    acc = x_ref[...]
    for i, m_ref in enumerate((m0_ref, m1_ref, m2_ref, m3_ref)):
        acc = acc + jnp.dot(m_ref[...], w_ref[pl.ds(BRANCH * i, BRANCH), :], preferred_element_type=jnp.float32)
    y_ref[...] = acc


def _outproj(mixed4, w_out_bf16, layer, x2d, tm, tn=1024):
    m = x2d.shape[0]
    mspec = pl.BlockSpec((tm, BRANCH), lambda i, j: (i, 0))
    return pl.pallas_call(
        _outproj_kernel,
        grid=(m // tm, D_MODEL // tn),
        in_specs=[mspec, mspec, mspec, mspec,
                  pl.BlockSpec((None, D_MODEL, tn), lambda i, j: (layer, 0, j)),
                  pl.BlockSpec((tm, tn), lambda i, j: (i, j))],
        out_specs=pl.BlockSpec((tm, tn), lambda i, j: (i, j)),
        out_shape=jax.ShapeDtypeStruct((m, D_MODEL), jnp.float32),
        compiler_params=_params(2),
        name="outproj",
    )(*mixed4, w_out_bf16, x2d)


def _hgrn_kernel(q_ref, f_ref, i_ref, g_ref, lb_ref, nw_ref, s0_ref, o_ref, s_out_ref, st_ref, *, chunk, l_valid):
    LB = q_ref.shape[1]
    C = chunk
    nblk = C // SUBLANES
    lb_i = pl.program_id(2)

    @pl.when(lb_i == 0)
    def _():
        for hh in range(HG_HPB):
            st_ref[hh] = s0_ref[0, hh].T

    nw = nw_ref[...]
    rowi = lax.broadcasted_iota(jnp.int32, (SUBLANES, LANES), 0)

    def head_chunk(hh, t0):
        lanes = pl.ds(LANES * hh, LANES)
        lb = lb_ref[0, :, lanes]
        fp = f_ref[0, pl.ds(t0, C), lanes]
        qp = q_ref[0, pl.ds(t0, C), lanes]
        v = i_ref[0, pl.ds(t0, C), lanes]
        gp = g_ref[0, pl.ds(t0, C), lanes]
        fg = lb + (1.0 - lb) * _sigmoid(fp)
        logf = jnp.log(jnp.maximum(fg, F_FLOOR))
        kk = (1.0 - lb) * _sigmoid(-fp)
        if l_valid < LB * pl.num_programs(2):
            rows = lax.broadcasted_iota(jnp.int32, (C, LANES), 0) + t0 + lb_i * LB
            logf = jnp.where(rows < l_valid, logf, 0.0)
            kk = jnp.where(rows < l_valid, kk, 0.0)
        qq = _silu(qp)
        G = _cumsum_rows(logf)
        st = st_ref[hh]
        o_inter = _dot(qq * jnp.exp(G), st, _NT)
        acc = [None] * nblk
        qb = [qq[SUBLANES * tb:SUBLANES * (tb + 1)] for tb in range(nblk)]
        Gb = [G[SUBLANES * tb:SUBLANES * (tb + 1)] for tb in range(nblk)]
        for s in range(C):
            sb = s // SUBLANES
            gs = G[s:s + 1]
            ks = kk[s:s + 1]
            vs = v[s:s + 1]
            for tb in range(sb, (s // HG_SUB + 1) * (HG_SUB // SUBLANES)):
                d = Gb[tb] - gs
                if tb == sb:
                    d = jnp.where(rowi >= s - SUBLANES * sb, d, MASK_VALUE)
                w = jnp.sum(qb[tb] * ks * jnp.exp(d), axis=-1, keepdims=True)
                contrib = w * vs
                acc[tb] = contrib if acc[tb] is None else acc[tb] + contrib
        for j in range(1, C // HG_SUB):
            r0 = HG_SUB * j
            gb = G[r0 - 1:r0]
            qj = qq[r0:r0 + HG_SUB] * jnp.exp(G[r0:r0 + HG_SUB] - gb)
            kj = kk[:r0] * jnp.exp(gb - G[:r0])
            oj = _dot(_dot(qj, kj, _NT), v[:r0])
            for half in range(HG_SUB // SUBLANES):
                tb = r0 // SUBLANES + half
                acc[tb] = acc[tb] + oj[SUBLANES * half:SUBLANES * (half + 1)]
        o = o_inter + jnp.concatenate(acc, axis=0)
        g_last = G[C - 1:C]
        kd = kk * jnp.exp(g_last - G)
        st_ref[hh] = jnp.exp(g_last) * st + _dot(v, kd, _TN)
        ms = jnp.mean(o * o, axis=-1, keepdims=True)
        on = o * lax.rsqrt(ms + EPS) * nw
        o_ref[0, pl.ds(t0, C), lanes] = (on * _silu(gp)).astype(o_ref.dtype)

    def body(c, carry):
        t0 = pl.multiple_of(c * C, C)
        for hh in range(HG_HPB):
            head_chunk(hh, t0)
        return carry

    lax.fori_loop(0, LB // C, body, 0)

    @pl.when(lb_i == pl.num_programs(2) - 1)
    def _():
        for hh in range(HG_HPB):
            s_out_ref[0, hh] = st_ref[hh].T


def _hgrn_call(z, lb, hg_norm_w, s0, *, chunk, l_valid):
    B, L, _ = z.shape
    W = HG_HPB * LANES
    LB = min(L, HG_ROW_BLOCK)
    zspec = lambda off: pl.BlockSpec((1, LB, W), lambda b, h, r: (b, r, off // W + h))
    return pl.pallas_call(
        functools.partial(_hgrn_kernel, chunk=chunk, l_valid=l_valid),
        grid=(B, HG_HEADS // HG_HPB, L // LB),
        in_specs=[zspec(OFF_HG_Q), zspec(OFF_HG_F), zspec(OFF_HG_I), zspec(OFF_HG_G),
                  pl.BlockSpec((1, 1, W), lambda b, h, r: (h, 0, 0)),
                  pl.BlockSpec((1, LANES), lambda b, h, r: (0, 0)),
                  pl.BlockSpec((1, HG_HPB, LANES, LANES), lambda b, h, r: (b, h, 0, 0))],
        out_specs=[pl.BlockSpec((1, LB, W), lambda b, h, r: (b, r, h)),
                   pl.BlockSpec((1, HG_HPB, LANES, LANES), lambda b, h, r: (b, h, 0, 0))],
        out_shape=[jax.ShapeDtypeStruct((B, L, BRANCH), jnp.bfloat16),
                   jax.ShapeDtypeStruct((B, HG_HEADS, LANES, LANES), jnp.float32)],
        scratch_shapes=[pltpu.VMEM((HG_HPB, LANES, LANES), jnp.float32)],
        compiler_params=_params(3),
        name="hgrn",
    )(z, z, z, z, lb.reshape(HG_HEADS // HG_HPB, 1, W), hg_norm_w.reshape(1, LANES), s0)


def _ssd_kernel(xs_ref, b_ref, c_ref, zg_ref, sm_ref, wx_ref, wb_ref, wc_ref, bx_ref, bb_ref, bc_ref,
                cx_ref, cb_ref, cc_ref, dtb_ref, alog_ref, dskip_ref, nw_ref, s0_ref,
                y_ref, s_out_ref, ox_ref, ob_ref, oc_ref,
                ax_ref, ab_ref, ac_ref, hx_ref, hb_ref, hc_ref, sp_ref, *, chunk, l_valid, conv_rows):
    L = xs_ref.shape[1]
    C = chunk
    R = conv_rows
    gb = pl.program_id(1)
    n_heads = SSD_GPB * SSM_HPG
    n_pairs = n_heads // 2

    def conv(src_ref, head_ref, cst_ref, w_ref, bias_ref, act_ref, out_state_ref):
        head_ref[pl.ds(0, SUBLANES), :] = cst_ref[0]
        head_ref[pl.ds(SUBLANES, R), :] = src_ref[0, pl.ds(0, R), :]
        w = w_ref[...]
        for r0 in range(0, L, R):
            acc = bias_ref[...]
            for j in range(SSM_CONV):
                if r0 == 0:
                    xj = head_ref[pl.ds(SUBLANES - j, R), :]
                else:
                    xj = src_ref[0, pl.ds(r0 - j, R), :]
                acc = acc + xj * w[SSM_CONV - 1 - j:SSM_CONV - j]
            act_ref[pl.ds(r0, R), :] = _silu(acc)
        if l_valid >= SUBLANES:
            out_state_ref[0] = src_ref[0, pl.ds(l_valid - SUBLANES, SUBLANES), :]
        else:
            out_state_ref[0] = head_ref[pl.ds(l_valid, SUBLANES), :]

    conv(xs_ref, hx_ref, cx_ref, wx_ref, bx_ref, ax_ref, ox_ref)
    conv(b_ref, hb_ref, cb_ref, wb_ref, bb_ref, ab_ref, ob_ref)
    conv(c_ref, hc_ref, cc_ref, wc_ref, bc_ref, ac_ref, oc_ref)

    for p in range(n_pairs):
        sp_ref[p] = s0_ref[0, 2 * p:2 * p + 2].reshape(2 * SSM_P, LANES)

    lane1 = lax.broadcasted_iota(jnp.int32, (1, LANES), 1)
    lane8 = lax.broadcasted_iota(jnp.int32, (SUBLANES, LANES), 1)
    row8 = lax.broadcasted_iota(jnp.int32, (SUBLANES, LANES), 0)
    lane0 = SMALL_DT + n_heads * gb
    sel = jnp.where((lane8 == lane0 + row8) & (row8 < n_heads), 1.0, 0.0).astype(jnp.bfloat16)
    a_all = -jnp.exp(alog_ref[...])
    dskip = [jnp.sum(jnp.where(lane1 == lane0 + j, dskip_ref[...], 0.0), axis=-1, keepdims=True)
             for j in range(n_heads)]
    pair_w = 2 * SSM_P
    lane_c = lax.broadcasted_iota(jnp.int32, (C, LANES), 1)
    first_half = lane_c < SSM_P
    row_p = lax.broadcasted_iota(jnp.int32, (2 * SSM_P, LANES), 0) < SSM_P
    tril = lax.broadcasted_iota(jnp.int32, (C, C), 0) >= lax.broadcasted_iota(jnp.int32, (C, C), 1)
    nw = nw_ref[...]

    def body(c, carry):
        t0 = pl.multiple_of(c * C, C)
        sm = sm_ref[0, pl.ds(t0, C), :]
        dt_all = _softplus(sm + dtb_ref[...])
        if l_valid < L:
            rows = lax.broadcasted_iota(jnp.int32, (C, LANES), 0) + t0
            dt_all = jnp.where(rows < l_valid, dt_all, 0.0)
        cum = _cumsum_rows(dt_all * a_all)
        cum_rows = _select_lanes_as_rows(sel, cum)
        col = [jnp.sum(jnp.where(lane_c == lane0 + j, cum, 0.0), axis=-1, keepdims=True) for j in range(n_heads)]
        dtc = [jnp.sum(jnp.where(lane_c == lane0 + j, dt_all, 0.0), axis=-1, keepdims=True) for j in range(n_heads)]
        for lg in range(SSD_GPB):
            bact = ab_ref[pl.ds(t0, C), pl.ds(LANES * lg, LANES)]
            cact = ac_ref[pl.ds(t0, C), pl.ds(LANES * lg, LANES)]
            cb = _dot(cact, bact, _NT)
            ys = []
            for p in range(SSM_HPG // 2):
                pp = lg * (SSM_HPG // 2) + p
                ja, jb = 2 * pp, 2 * pp + 1
                xs = ax_ref[pl.ds(t0, C), pl.ds(pair_w * pp, pair_w)]
                xdt = xs * jnp.where(first_half, dtc[ja], dtc[jb])
                dec_a = jnp.exp(jnp.where(tril, col[ja] - cum_rows[ja:ja + 1], MASK_VALUE))
                dec_b = jnp.exp(jnp.where(tril, col[jb] - cum_rows[jb:jb + 1], MASK_VALUE))
                y = jnp.where(first_half, _dot(cb * dec_a, xdt), _dot(cb * dec_b, xdt))
                sp = sp_ref[pp]
                y = y + _dot(cact, sp, _NT) * jnp.where(first_half, jnp.exp(col[ja]), jnp.exp(col[jb]))
                y = y + jnp.where(first_half, dskip[ja], dskip[jb]) * xs
                last_a, last_b = col[ja][C - 1:C], col[jb][C - 1:C]
                xw = xdt * jnp.where(first_half, jnp.exp(last_a - col[ja]), jnp.exp(last_b - col[jb]))
                sp_ref[pp] = jnp.where(row_p, jnp.exp(last_a), jnp.exp(last_b)) * sp + _dot(xw, bact, _TN)
                zg = zg_ref[0, pl.ds(t0, C), pl.ds(pair_w * pp, pair_w)]
                ys.append(y * _silu(zg))
            ms = sum(jnp.sum(y * y, axis=-1, keepdims=True) for y in ys) * (1.0 / (SSM_HPG * SSM_P))
            inv = lax.rsqrt(ms + EPS)
            for p in range(SSM_HPG // 2):
                pp = lg * (SSM_HPG // 2) + p
                y_ref[0, pl.ds(t0, C), pl.ds(pair_w * pp, pair_w)] = (
                    ys[p] * inv * nw[:, pair_w * pp:pair_w * (pp + 1)]).astype(y_ref.dtype)
        return carry

    lax.fori_loop(0, L // C, body, 0)
    for p in range(n_pairs):
        s_out_ref[0, 2 * p:2 * p + 2] = sp_ref[p].reshape(2, SSM_P, LANES)


def _ssd_call(z, conv_w, conv_b, conv_state8, dtb_pad, alog_pad, dskip_pad, norm_w, s0, *, chunk, l_valid):
    B, L, _ = z.shape
    assert l_valid >= SSM_CONV - 1
    xw = SSD_GPB * SSM_HPG * SSM_P
    bw = SSD_GPB * SSM_STATE
    nh = SSD_GPB * SSM_HPG
    conv_rows = min(L, 128)
    ox, ob = OFF_SM_XBC // xw, (OFF_SM_XBC + BRANCH) // bw
    oc = (OFF_SM_XBC + BRANCH + SSM_GROUPS * SSM_STATE) // bw
    wb0, wc0 = BRANCH // bw, (BRANCH + SSM_GROUPS * SSM_STATE) // bw
    in_specs = [
        pl.BlockSpec((1, L, xw), lambda b, g: (b, 0, ox + g)),
        pl.BlockSpec((1, L, bw), lambda b, g: (b, 0, ob + g)),
        pl.BlockSpec((1, L, bw), lambda b, g: (b, 0, oc + g)),
        pl.BlockSpec((1, L, xw), lambda b, g: (b, 0, OFF_SM_Z // xw + g)),
        pl.BlockSpec((1, L, LANES), lambda b, g: (b, 0, OFF_SMALL // LANES)),
        pl.BlockSpec((SSM_CONV, xw), lambda b, g: (0, g)),
        pl.BlockSpec((SSM_CONV, bw), lambda b, g: (0, wb0 + g)),
        pl.BlockSpec((SSM_CONV, bw), lambda b, g: (0, wc0 + g)),
        pl.BlockSpec((1, xw), lambda b, g: (0, g)),
        pl.BlockSpec((1, bw), lambda b, g: (0, wb0 + g)),
        pl.BlockSpec((1, bw), lambda b, g: (0, wc0 + g)),
        pl.BlockSpec((1, SUBLANES, xw), lambda b, g: (b, 0, g)),
        pl.BlockSpec((1, SUBLANES, bw), lambda b, g: (b, 0, wb0 + g)),
        pl.BlockSpec((1, SUBLANES, bw), lambda b, g: (b, 0, wc0 + g)),
        pl.BlockSpec((1, LANES), lambda b, g: (0, 0)),
        pl.BlockSpec((1, LANES), lambda b, g: (0, 0)),
        pl.BlockSpec((1, LANES), lambda b, g: (0, 0)),
        pl.BlockSpec((1, xw), lambda b, g: (0, g)),
        pl.BlockSpec((1, nh, SSM_P, LANES), lambda b, g: (b, g, 0, 0)),
    ]
    out_specs = [
        pl.BlockSpec((1, L, xw), lambda b, g: (b, 0, g)),
        pl.BlockSpec((1, nh, SSM_P, LANES), lambda b, g: (b, g, 0, 0)),
        pl.BlockSpec((1, SUBLANES, xw), lambda b, g: (b, 0, g)),
        pl.BlockSpec((1, SUBLANES, bw), lambda b, g: (b, 0, g)),
        pl.BlockSpec((1, SUBLANES, bw), lambda b, g: (b, 0, g)),
    ]
    f32 = jnp.float32
    out_shape = [
        jax.ShapeDtypeStruct((B, L, BRANCH), jnp.bfloat16),
        jax.ShapeDtypeStruct((B, SSM_HEADS, SSM_P, LANES), f32),
        jax.ShapeDtypeStruct((B, SUBLANES, BRANCH), f32),
        jax.ShapeDtypeStruct((B, SUBLANES, SSM_GROUPS * SSM_STATE), f32),
        jax.ShapeDtypeStruct((B, SUBLANES, SSM_GROUPS * SSM_STATE), f32),
    ]
    scratch = [pltpu.VMEM((L, xw), f32), pltpu.VMEM((L, bw), f32), pltpu.VMEM((L, bw), f32),
               pltpu.VMEM((SUBLANES + conv_rows, xw), f32), pltpu.VMEM((SUBLANES + conv_rows, bw), f32),
               pltpu.VMEM((SUBLANES + conv_rows, bw), f32),
               pltpu.VMEM((nh // 2, 2 * SSM_P, LANES), f32)]
    cbias = conv_b.reshape(1, -1)
    return pl.pallas_call(
        functools.partial(_ssd_kernel, chunk=chunk, l_valid=l_valid, conv_rows=conv_rows),
        grid=(B, SSM_GROUPS // SSD_GPB), in_specs=in_specs, out_specs=out_specs, out_shape=out_shape, scratch_shapes=scratch,
        compiler_params=_params(2),
        name="ssd",
    )(z, z, z, z, z, conv_w, conv_w, conv_w, cbias, cbias, cbias, conv_state8, conv_state8, conv_state8,
      dtb_pad, alog_pad, dskip_pad, norm_w.reshape(1, -1), s0)


def _s5_kernel(u_ref, bre_ref, bim_ref, cre_ref, cim_ref, are_ref, aim_ref, ldt_ref, d_ref, x0r_ref, x0i_ref,
               h_ref, xr_out_ref, xi_out_ref, bbr_ref, bbi_ref, xr_ref, xi_ref, pwr_ref, pwi_ref,
               *, nseg, seg_len):
    L = u_ref.shape[1]
    n_scan = nseg * seg_len
    row_blk = min(L, 256)
    a_re, a_im = are_ref[...], aim_ref[...]
    dt = jnp.exp(ldt_ref[...])
    mag = jnp.exp(a_re * dt)
    ab_re, ab_im = mag * jnp.cos(a_im * dt), mag * jnp.sin(a_im * dt)
    den = a_re * a_re + a_im * a_im
    nr = ab_re - 1.0
    coef_re = (nr * a_re + ab_im * a_im) / den
    coef_im = (ab_im * a_re - nr * a_im) / den

    NQ = S5_BLK_ST // LANES
    lq = lambda v, q: v[:, LANES * q:LANES * (q + 1)]
    abr = [lq(ab_re, q) for q in range(NQ)]
    abi = [lq(ab_im, q) for q in range(NQ)]

    for r0 in range(0, L, row_blk):
        u = u_ref[0, pl.ds(r0, row_blk), :]
        bu_re, bu_im = _dot(u, bre_ref[0]), _dot(u, bim_ref[0])
        bb_re = coef_re * bu_re - coef_im * bu_im
        bb_im = coef_re * bu_im + coef_im * bu_re
        for q in range(NQ):
            bbr_ref[q, pl.ds(r0, row_blk), :] = lq(bb_re, q)
            bbi_ref[q, pl.ds(r0, row_blk), :] = lq(bb_im, q)

    if n_scan < L:
        for q in range(NQ):
            xr_ref[q, pl.ds(n_scan, L - n_scan), :] = jnp.zeros((L - n_scan, LANES), jnp.float32)
            xi_ref[q, pl.ds(n_scan, L - n_scan), :] = jnp.zeros((L - n_scan, LANES), jnp.float32)

    def rows(i):
        return pl.ds(i, nseg, stride=seg_len) if nseg > 1 else pl.ds(i, 1)

    def scan(i, carry):
        out = []
        for q in range(NQ):
            xr, xi = carry[q]
            pr, pi = _cmul(abr[q], abi[q], xr, xi)
            xr, xi = pr + bbr_ref[q, rows(i), :], pi + bbi_ref[q, rows(i), :]
            xr_ref[q, rows(i), :] = xr
            xi_ref[q, rows(i), :] = xi
            out.append((xr, xi))
        return tuple(out)

    zero = jnp.zeros((nseg, LANES), jnp.float32)
    ends = lax.fori_loop(0, seg_len, scan, tuple((zero, zero) for _ in range(NQ)), unroll=4)

    for q in range(NQ):
        pwr_ref[q, pl.ds(0, 1), :] = abr[q]
        pwi_ref[q, pl.ds(0, 1), :] = abi[q]
    an_r, an_i = ab_re, ab_im
    n = 1
    while n < seg_len:
        m = min(n, seg_len - n)
        for q in range(NQ):
            pr, pi = _cmul(pwr_ref[q, pl.ds(0, m), :], pwi_ref[q, pl.ds(0, m), :], lq(an_r, q), lq(an_i, q))
            pwr_ref[q, pl.ds(n, m), :] = pr
            pwi_ref[q, pl.ds(n, m), :] = pi
        an_r, an_i = _cmul(an_r, an_i, an_r, an_i)
        n *= 2
    seg_r = [pwr_ref[q, pl.ds(seg_len - 1, 1), :] for q in range(NQ)]
    seg_i = [pwi_ref[q, pl.ds(seg_len - 1, 1), :] for q in range(NQ)]

    x0r, x0i = x0r_ref[0], x0i_ref[0]
    dskip = d_ref[...]
    c_r = [lq(x0r, q) for q in range(NQ)]
    c_i = [lq(x0i, q) for q in range(NQ)]
    for k in range(nseg):
        r0 = k * seg_len
        xr_q, xi_q = [], []
        for q in range(NQ):
            dr, di = _cmul(pwr_ref[q], pwi_ref[q], c_r[q], c_i[q])
            xr_q.append(xr_ref[q, pl.ds(r0, seg_len), :] + dr)
            xi_q.append(xi_ref[q, pl.ds(r0, seg_len), :] + di)
            pr, pi = _cmul(seg_r[q], seg_i[q], c_r[q], c_i[q])
            c_r[q], c_i[q] = pr + ends[q][0][k:k + 1], pi + ends[q][1][k:k + 1]
        if k == nseg - 1:
            for q in range(NQ):
                xr_out_ref[0, :, pl.ds(LANES * q, LANES)] = c_r[q]
                xi_out_ref[0, :, pl.ds(LANES * q, LANES)] = c_i[q]
        if n_scan == L:
            u = u_ref[0, pl.ds(r0, seg_len), :]
            y = (_dot(jnp.concatenate(xr_q, axis=-1), cre_ref[0]) - _dot(jnp.concatenate(xi_q, axis=-1), cim_ref[0])
                 + dskip * u)
            h_ref[0, pl.ds(r0, seg_len), :] = _gelu_tanh(y)
        else:
            for q in range(NQ):
                xr_ref[q, pl.ds(r0, seg_len), :] = xr_q[q]
                xi_ref[q, pl.ds(r0, seg_len), :] = xi_q[q]
    if n_scan < L:
        xr = jnp.concatenate([xr_ref[q] for q in range(NQ)], axis=-1)
        xi = jnp.concatenate([xi_ref[q] for q in range(NQ)], axis=-1)
        y = _dot(xr, cre_ref[0]) - _dot(xi, cim_ref[0]) + dskip * u_ref[0]
        h_ref[0] = _gelu_tanh(y)


def _s5seg_kernel(u_ref, bre_ref, bim_ref, cre_ref, cim_ref, are_ref, aim_ref, ldt_ref, d_ref, x0r_ref, x0i_ref,
                  h_ref, xr_out_ref, xi_out_ref, up_ref, bbr_ref, bbi_ref, xr_ref, xi_ref, pwr_ref, pwi_ref, yp_ref,
                  *, seg_len):
    NS = SUBLANES
    a_re, a_im = are_ref[...], aim_ref[...]
    dt = jnp.exp(ldt_ref[...])
    mag = jnp.exp(a_re * dt)
    ab_re, ab_im = mag * jnp.cos(a_im * dt), mag * jnp.sin(a_im * dt)
    den = a_re * a_re + a_im * a_im
    nr = ab_re - 1.0
    coef_re = (nr * a_re + ab_im * a_im) / den
    coef_im = (ab_im * a_re - nr * a_im) / den

    NQ = S5_BLK_ST // LANES
    lq = lambda v, q: v[:, LANES * q:LANES * (q + 1)]
    rep = lambda v: jnp.broadcast_to(v, (NS, LANES))
    abr = [rep(lq(ab_re, q)) for q in range(NQ)]
    abi = [rep(lq(ab_im, q)) for q in range(NQ)]
    blk = S5_BULK_STEPS
    n_blk = seg_len // blk

    def gather(i, carry):
        up_ref[pl.ds(pl.multiple_of(i * NS, NS), NS), :] = u_ref[0, pl.ds(i, NS, stride=seg_len), :]
        return carry

    lax.fori_loop(0, seg_len, gather, 0, unroll=8)

    for rb in range(n_blk):
        u = up_ref[pl.ds(rb * blk * NS, blk * NS), :]
        bu_re, bu_im = _dot(u, bre_ref[0]), _dot(u, bim_ref[0])
        bb_re = coef_re * bu_re - coef_im * bu_im
        bb_im = coef_re * bu_im + coef_im * bu_re
        for q in range(NQ):
            bbr_ref[q, pl.ds(rb * blk, blk)] = lq(bb_re, q).reshape(blk, NS, LANES)
            bbi_ref[q, pl.ds(rb * blk, blk)] = lq(bb_im, q).reshape(blk, NS, LANES)

    def scan(i, carry):
        out = []
        for q in range(NQ):
            xr, xi = carry[q]
            pr, pi = _cmul(abr[q], abi[q], xr, xi)
            xr, xi = pr + bbr_ref[q, i], pi + bbi_ref[q, i]
            xr_ref[q, i] = xr
            xi_ref[q, i] = xi
            out.append((xr, xi))
        return tuple(out)

    zero = jnp.zeros((NS, LANES), jnp.float32)
    ends = lax.fori_loop(0, seg_len, scan, tuple((zero, zero) for _ in range(NQ)), unroll=8)

    for q in range(NQ):
        pwr_ref[q, 0] = abr[q]
        pwi_ref[q, 0] = abi[q]
    an = [(abr[q], abi[q]) for q in range(NQ)]
    n = 1
    while n < seg_len:
        m = min(n, seg_len - n)
        for q in range(NQ):
            pr, pi = _cmul(pwr_ref[q, pl.ds(0, m)], pwi_ref[q, pl.ds(0, m)], an[q][0], an[q][1])
            pwr_ref[q, pl.ds(n, m)] = pr
            pwi_ref[q, pl.ds(n, m)] = pi
            an[q] = _cmul(an[q][0], an[q][1], an[q][0], an[q][1])
        n *= 2

    x0r, x0i = x0r_ref[0], x0i_ref[0]
    cs = []
    for q in range(NQ):
        seg_r, seg_i = pwr_ref[q, seg_len - 1][:1], pwi_ref[q, seg_len - 1][:1]
        cr, ci = [lq(x0r, q)], [lq(x0i, q)]
        for k in range(NS):
            pr, pi = _cmul(seg_r, seg_i, cr[k], ci[k])
            cr.append(pr + ends[q][0][k:k + 1])
            ci.append(pi + ends[q][1][k:k + 1])
        xr_out_ref[0, :, pl.ds(LANES * q, LANES)] = cr[NS]
        xi_out_ref[0, :, pl.ds(LANES * q, LANES)] = ci[NS]
        cs.append((jnp.concatenate(cr[:NS], axis=0), jnp.concatenate(ci[:NS], axis=0)))

    dskip = d_ref[...]
    for rb in range(n_blk):
        xr_q, xi_q = [], []
        for q in range(NQ):
            dr, di = _cmul(pwr_ref[q, pl.ds(rb * blk, blk)], pwi_ref[q, pl.ds(rb * blk, blk)], cs[q][0], cs[q][1])
            xr_q.append((xr_ref[q, pl.ds(rb * blk, blk)] + dr).reshape(blk * NS, LANES))
            xi_q.append((xi_ref[q, pl.ds(rb * blk, blk)] + di).reshape(blk * NS, LANES))
        u = up_ref[pl.ds(rb * blk * NS, blk * NS), :]
        y = (_dot(jnp.concatenate(xr_q, axis=-1), cre_ref[0]) - _dot(jnp.concatenate(xi_q, axis=-1), cim_ref[0])
             + dskip * u)
        yp_ref[pl.ds(rb * blk * NS, blk * NS), :] = _gelu_tanh(y)

    for k in range(NS):
        h_ref[0, pl.ds(k * seg_len, seg_len), :] = yp_ref[pl.ds(k, seg_len, stride=NS), :]


def _s5_call(z, bblk_re, bblk_im, cblk_re, cblk_im, a_re, a_im, log_dt_exp, d_flat, x0_re, x0_im, *, l_valid):
    B, L, _ = z.shape
    nb = bblk_re.shape[0]
    NQ = S5_BLK_ST // LANES
    f32 = jnp.float32
    if l_valid == L and L % (SUBLANES * S5_BULK_STEPS) == 0:
        seg_len = L // SUBLANES
        kern = functools.partial(_s5seg_kernel, seg_len=seg_len)
        tile = pltpu.VMEM((NQ, seg_len, SUBLANES, LANES), f32)
        scratch = [pltpu.VMEM((L, LANES), f32)] + [tile] * 6 + [pltpu.VMEM((L, LANES), f32)]
    else:
        kern = functools.partial(_s5_kernel, nseg=1, seg_len=l_valid)
        scratch = [pltpu.VMEM((NQ, L, LANES), f32)] * 4 + [pltpu.VMEM((NQ, l_valid, LANES), f32)] * 2
    vec = lambda w: pl.BlockSpec((1, w), lambda b, j: (0, j))
    st = pl.BlockSpec((1, 1, S5_BLK_ST), lambda b, j: (b, 0, j))
    return pl.pallas_call(
        kern,
        grid=(B, nb),
        in_specs=[pl.BlockSpec((1, L, S5_BLK_CH), lambda b, j: (b, 0, OFF_S5_U // S5_BLK_CH + j)),
                  pl.BlockSpec((1, S5_BLK_CH, S5_BLK_ST), lambda b, j: (j, 0, 0)),
                  pl.BlockSpec((1, S5_BLK_CH, S5_BLK_ST), lambda b, j: (j, 0, 0)),
                  pl.BlockSpec((1, S5_BLK_ST, S5_BLK_CH), lambda b, j: (j, 0, 0)),
                  pl.BlockSpec((1, S5_BLK_ST, S5_BLK_CH), lambda b, j: (j, 0, 0)),
                  vec(S5_BLK_ST), vec(S5_BLK_ST), vec(S5_BLK_ST), vec(S5_BLK_CH), st, st],
        out_specs=[pl.BlockSpec((1, L, S5_BLK_CH), lambda b, j: (b, 0, j)), st, st],
        out_shape=[jax.ShapeDtypeStruct((B, L, nb * S5_BLK_CH), jnp.float32),
                   jax.ShapeDtypeStruct((B, 1, nb * S5_BLK_ST), jnp.float32),
                   jax.ShapeDtypeStruct((B, 1, nb * S5_BLK_ST), jnp.float32)],
        scratch_shapes=scratch,
        compiler_params=_params(2),
        name="s5",
    )(z, bblk_re, bblk_im, cblk_re, cblk_im, a_re, a_im, log_dt_exp, d_flat, x0_re, x0_im)


def _s5_block_diag(b_re, b_im, c_re, c_im):
    G, P, Cc = b_re.shape
    nb = G // 8
    same = jnp.eye(8, dtype=bool)

    def bblk(b):
        t = jnp.transpose(b.reshape(nb, 8, P, Cc), (0, 1, 3, 2))[:, :, :, None, :]
        t = jnp.where(same[None, :, None, :, None], t, 0.0)
        return t.reshape(nb, 8 * Cc, 8 * P).astype(jnp.bfloat16)

    def cblk(c):
        t = jnp.transpose(c.reshape(nb, 8, Cc, P), (0, 1, 3, 2))[:, :, :, None, :]
        t = jnp.where(same[None, :, None, :, None], t, 0.0)
        return t.reshape(nb, 8 * P, 8 * Cc).astype(jnp.bfloat16)

    return bblk(b_re), bblk(b_im), cblk(c_re), cblk(c_im)


def _glu_kernel(h_ref, w_ref, b_ref, g_ref, o_ref):
    h = h_ref[...]
    t = _dot(h, w_ref[...]) + b_ref[...]
    o_ref[...] = (h * _sigmoid(t) * _silu(g_ref[...])).astype(o_ref.dtype)


def _glu_call(h2d, glu_w_bf16, glu_b, z2d, *, tm):
    M, W = h2d.shape
    return pl.pallas_call(
        _glu_kernel,
        grid=(M // tm,),
        in_specs=[pl.BlockSpec((tm, W), lambda i: (i, 0)),
                  pl.BlockSpec((W, W), lambda i: (0, 0)),
                  pl.BlockSpec((1, W), lambda i: (0, 0)),
                  pl.BlockSpec((tm, W), lambda i: (i, OFF_S5_G // W))],
        out_specs=pl.BlockSpec((tm, W), lambda i: (i, 0)),
        out_shape=jax.ShapeDtypeStruct((M, W), jnp.bfloat16),
        compiler_params=_params(1),
        name="glu",
    )(h2d, glu_w_bf16, glu_b.reshape(1, W), z2d)


def _kvprep_kernel(kv_ref, sm_ref, knw_ref, k_ref, v_ref, ik_ref, *rest, transposed):
    kvw = KV_HEADS * ATT_D
    kv = kv_ref[0]
    knw = knw_ref[...]
    ks = []
    for h in range(KV_HEADS):
        kh = kv[:, ATT_D * h:ATT_D * (h + 1)]
        ms = jnp.mean(kh * kh, axis=-1, keepdims=True)
        ks.append(kh * lax.rsqrt(ms + EPS) * knw)
    k = jnp.concatenate(ks, axis=-1)
    v = kv[:, kvw:]
    sm = sm_ref[0]
    k_ref[0] = k
    v_ref[0] = v
    ik_ref[0] = sm[:, :IDX_D]
    if transposed:
        kbf_ref, vt_ref, kia_ref, kib_ref = rest
        kbf_ref[0] = k.astype(jnp.bfloat16)
        vt_ref[0] = v.T.astype(jnp.bfloat16)
        lane = lax.broadcasted_iota(jnp.int32, sm.shape, 1)
        kia_ref[0] = jnp.where(lane < IDX_D, sm, 0.0).astype(jnp.bfloat16)
        kib_ref[0] = jnp.where(lane >= IDX_D, pltpu.roll(sm, IDX_D, axis=1), 0.0).astype(jnp.bfloat16)


def _kvprep_call(z, k_norm_w, *, tr, transposed):
    B, L, _ = z.shape
    kvw = KV_HEADS * ATT_D
    f32, bf = jnp.float32, jnp.bfloat16
    rowblk = lambda w: pl.BlockSpec((1, tr, w), lambda b, r: (b, r, 0))
    out_specs = [rowblk(kvw), rowblk(kvw), rowblk(IDX_D)]
    out_shape = [jax.ShapeDtypeStruct((B, L, kvw), f32), jax.ShapeDtypeStruct((B, L, kvw), f32),
                 jax.ShapeDtypeStruct((B, L, IDX_D), f32)]
    if transposed:
        out_specs += [rowblk(kvw), pl.BlockSpec((1, kvw, tr), lambda b, r: (b, 0, r)), rowblk(LANES), rowblk(LANES)]
        out_shape += [jax.ShapeDtypeStruct((B, L, kvw), bf), jax.ShapeDtypeStruct((B, kvw, L), bf),
                      jax.ShapeDtypeStruct((B, L, LANES), bf), jax.ShapeDtypeStruct((B, L, LANES), bf)]
    return pl.pallas_call(
        functools.partial(_kvprep_kernel, transposed=transposed),
        grid=(B, L // tr),
        in_specs=[pl.BlockSpec((1, tr, 2 * kvw), lambda b, r: (b, r, OFF_AT_KV // (2 * kvw))),
                  pl.BlockSpec((1, tr, LANES), lambda b, r: (b, r, OFF_SMALL // LANES)),
                  pl.BlockSpec((1, ATT_D), lambda b, r: (0, 0))],
        out_specs=out_specs, out_shape=out_shape,
        compiler_params=_params(2),
        name="kvprep",
    )(z, z, k_norm_w.reshape(1, ATT_D))


def _sortable_key(score):
    bits = lax.bitcast_convert_type(score, jnp.int32)
    return jnp.where(bits < 0, bits ^ 0x7FFFFFFF, bits)


def _rows_reduce(x, op):
    n = x.shape[0]
    slab = 8 * SUBLANES
    if n > slab and n % slab == 0:
        x = op(x.reshape(n // slab, slab, x.shape[1]), axis=0)
    return op(x, axis=0, keepdims=True)


def _lanes_sum(x):
    parts = [x[:, LANES * c:LANES * (c + 1)] for c in range(x.shape[1] // LANES)]
    while len(parts) > 1:
        parts = [a + b for a, b in zip(parts[::2], parts[1::2])] + ([parts[-1]] if len(parts) % 2 else [])
    return jnp.sum(parts[0], axis=1, keepdims=True)


def _kth_largest_key(key, k, axis):
    shape = list(key.shape)
    shape[axis] = 1

    def it(n, tu):
        cand_u = tu | jnp.left_shift(jnp.int32(1), 31 - n)
        cand_s = cand_u ^ jnp.int32(INT_MIN)
        ones = jnp.where(key >= cand_s, 1.0, 0.0)
        cnt = _rows_reduce(ones, jnp.sum) if axis == 0 else _lanes_sum(ones)
        return jnp.where(cnt >= k, cand_u, tu)

    tu = lax.fori_loop(0, 32, it, jnp.zeros(shape, jnp.int32))
    return tu ^ jnp.int32(INT_MIN)


def _t5_bucket(dist):
    exact = REL_BUCKETS // 2
    d = dist.astype(jnp.float32)
    large = exact + jnp.log(jnp.maximum(d, 1.0) / exact) / math.log(REL_MAX_DIST / exact) * (REL_BUCKETS - exact)
    large = jnp.minimum(jnp.maximum(large, 0.0).astype(jnp.int32), REL_BUCKETS - 1)
    return jnp.where(dist < exact, dist, large)


def _bias_lookup(rel_bias, bucket):
    onehot = (bucket[..., None] == jnp.arange(REL_BUCKETS, dtype=jnp.int32)).astype(jnp.float32)
    return jnp.einsum('...k,kh->...h', onehot, rel_bias.astype(jnp.float32), precision=lax.Precision.HIGHEST)


def _pattn_kernel(zq_ref, zg_ref, ziq_ref, zsm_ref, kbf_ref, vt_ref, kia_ref, kib_ref, qnw_ref, bias_ref, o_ref,
                  mask_ref, *, topk):
    i = pl.program_id(1)
    nqb = kbf_ref.shape[1] // Q_BLK
    qnw = qnw_ref[...]
    scale = ATT_D ** -0.5

    def body(S):
        nkb = S // Q_BLK
        wt = zsm_ref[0].T * (1.0 / 32.0)
        kia, kib = kia_ref[0, pl.ds(0, S), :], kib_ref[0, pl.ds(0, S), :]
        score = jnp.zeros((S, Q_BLK), jnp.float32)
        for p in range(IDX_HEADS // 2):
            qi = ziq_ref[0, :, pl.ds(LANES * p, LANES)].astype(jnp.bfloat16)
            da = lax.dot_general(kia, qi, _NT, preferred_element_type=jnp.float32)
            db = lax.dot_general(kib, qi, _NT, preferred_element_type=jnp.float32)
            r = SMALL_IXW + 2 * p
            score = score + jnp.maximum(da, 0.0) * wt[r:r + 1] + jnp.maximum(db, 0.0) * wt[r + 1:r + 2]
        s_pos = lax.broadcasted_iota(jnp.int32, (S, Q_BLK), 0)
        t_pos = lax.broadcasted_iota(jnp.int32, (S, Q_BLK), 1) + i * Q_BLK
        adm = s_pos <= t_pos
        key = jnp.where(adm, _sortable_key(score), jnp.int32(INT_MIN))
        thr = _kth_largest_key(key, topk, 0)
        ge = key >= thr
        cnt_ge = _rows_reduce(jnp.where(ge, 1.0, 0.0), jnp.sum)
        tied = (cnt_ge > topk) & (thr > jnp.int32(INT_MIN))
        any_tied = jnp.max(jnp.where(tied, 1.0, 0.0)) > 0.0

        @pl.when(jnp.logical_not(any_tied))
        def _():
            mask_ref[pl.ds(0, S), :] = jnp.where(ge & adm, 0.0, MASK_VALUE)

        @pl.when(any_tied)
        def _():
            gt = key > thr
            eq = key == thr
            need = topk - _rows_reduce(jnp.where(gt, 1.0, 0.0), jnp.sum)
            nbits = S.bit_length()

            def it(n, j_sel):
                cand = j_sel | jnp.left_shift(jnp.int32(1), nbits - 1 - n)
                c = _rows_reduce(jnp.where(eq & (s_pos < cand), 1.0, 0.0), jnp.sum)
                return jnp.where(c <= need, cand, j_sel)

            j_sel = lax.fori_loop(0, nbits, it, jnp.zeros((1, Q_BLK), jnp.int32))
            sel = (gt | (eq & (s_pos < j_sel))) & adm
            mask_ref[pl.ds(0, S), :] = jnp.where(sel, 0.0, MASK_VALUE)

        for h in range(ATT_HEADS):
            kvh = h // (ATT_HEADS // KV_HEADS)
            q = zq_ref[0, :, pl.ds(ATT_D * h, ATT_D)]
            ms = jnp.mean(q * q, axis=-1, keepdims=True)
            qn = (q * lax.rsqrt(ms + EPS) * qnw).astype(jnp.bfloat16)
            logit = lax.dot_general(kbf_ref[0, pl.ds(0, S), pl.ds(ATT_D * kvh, ATT_D)], qn, _NT,
                                    preferred_element_type=jnp.float32) * scale
            bias = jnp.concatenate([bias_ref[jnp.clip(i - j, 0, N_BIAS_TILES - 1), h] for j in range(nkb)], axis=0)
            x = logit + bias + mask_ref[pl.ds(0, S), :]
            m = _rows_reduce(x, jnp.max)
            pexp = jnp.exp(x - m)
            l = _rows_reduce(pexp, jnp.sum)
            ot = jnp.dot(vt_ref[0, pl.ds(ATT_D * kvh, ATT_D), pl.ds(0, S)], pexp.astype(jnp.bfloat16),
                         preferred_element_type=jnp.float32)
            out = (ot / l).T
            g = zg_ref[0, :, pl.ds(ATT_D * h, ATT_D)]
            o_ref[0, :, pl.ds(ATT_D * h, ATT_D)] = (out * _silu(g)).astype(o_ref.dtype)

    nv = min(N_CAUSAL_VARIANTS, nqb)
    per = nqb // nv
    for c in range(nv):
        pl.when(i // per == c)(functools.partial(body, (c + 1) * per * Q_BLK))


def _pattn_call(z, kbf, vt, kia, kib, q_norm_w, bias_tiles, *, topk):
    B, L, _ = z.shape
    W = ATT_HEADS * ATT_D
    kvw = KV_HEADS * ATT_D
    zblk = lambda off, w: pl.BlockSpec((1, Q_BLK, w), lambda b, i: (b, i, off // w))
    full = lambda shape: pl.BlockSpec((1,) + shape, lambda b, i: (b, 0, 0))
    return pl.pallas_call(
        functools.partial(_pattn_kernel, topk=topk),
        grid=(B, L // Q_BLK),
        in_specs=[zblk(OFF_AT_Q, W), zblk(OFF_AT_G, W), zblk(OFF_IX_Q, W), zblk(OFF_SMALL, LANES),
                  full((L, kvw)), full((kvw, L)), full((L, LANES)), full((L, LANES)),
                  pl.BlockSpec((1, ATT_D), lambda b, i: (0, 0)),
                  pl.BlockSpec((N_BIAS_TILES, ATT_HEADS, Q_BLK, Q_BLK), lambda b, i: (0, 0, 0, 0))],
        out_specs=pl.BlockSpec((1, Q_BLK, W), lambda b, i: (b, i, 0)),
        out_shape=jax.ShapeDtypeStruct((B, L, W), jnp.bfloat16),
        scratch_shapes=[pltpu.VMEM((L, Q_BLK), jnp.float32)],
        compiler_params=_params(2),
        name="pattn",
    )(z, z, z, z, kbf, vt, kia, kib, q_norm_w.reshape(1, ATT_D), bias_tiles)


def _prompt_bias_tiles(rel_bias):
    o = jnp.arange(N_BIAS_TILES, dtype=jnp.int32)[:, None, None]
    s = jnp.arange(Q_BLK, dtype=jnp.int32)[None, :, None]
    t = jnp.arange(Q_BLK, dtype=jnp.int32)[None, None, :]
    tiles = _bias_lookup(rel_bias, _t5_bucket(jnp.maximum(Q_BLK * o + t - s, 0)))
    return jnp.moveaxis(tiles, -1, 1)


def _sattn_score_kernel(pt_ref, qi_ref, w_ref, iknew_ref, *rest):
    pages, s_ref = rest[:NPG], rest[NPG]
    pc = pl.program_id(1)
    last = pl.num_programs(1) - 1
    qi = qi_ref[0]
    w = w_ref[0]

    def head_sum(d):
        r = jnp.maximum(d, 0.0) * w
        return jnp.sum(r.reshape(T_PAD, IDX_HEADS, r.shape[-1]), axis=1)

    @pl.when(pc < last)
    def _():
        for i in range(NPG):
            d = jnp.dot(qi, pages[i][0, 0].astype(jnp.bfloat16), preferred_element_type=jnp.float32)
            s_ref[0, :, pl.ds(PAGE * i, PAGE)] = head_sum(d)

    @pl.when(pc == last)
    def _():
        s_ref[0] = jnp.full((T_PAD, CHUNK_KEYS), MASK_VALUE, jnp.float32)
        d = lax.dot_general(qi, iknew_ref[0].astype(jnp.bfloat16), _NT, preferred_element_type=jnp.float32)
        s_ref[0, :, pl.ds(0, PAGE)] = head_sum(d)


def _sattn_score_call(page_table, qi_rows, w_col, ik_new, cache_ik_t, layer):
    B, n_pages = page_table.shape
    n_chunks = n_pages // NPG
    page_spec = lambda i: pl.BlockSpec(
        (1, 1, IDX_D, PAGE), lambda b, pc, pt: (layer, pt[b, jnp.minimum(pc * NPG + i, n_pages - 1)], 0, 0))
    per_b = lambda shape: pl.BlockSpec((1,) + shape, lambda b, pc, pt: (b, 0, 0))
    grid_spec = pltpu.PrefetchScalarGridSpec(
        num_scalar_prefetch=1, grid=(B, n_chunks + 1),
        in_specs=[per_b((T_PAD * IDX_HEADS, IDX_D)), per_b((T_PAD * IDX_HEADS, 1)), per_b((PAGE, IDX_D))]
        + [page_spec(i) for i in range(NPG)],
        out_specs=pl.BlockSpec((1, T_PAD, CHUNK_KEYS), lambda b, pc, pt: (b, 0, pc)))
    return pl.pallas_call(
        _sattn_score_kernel,
        grid_spec=grid_spec,
        out_shape=jax.ShapeDtypeStruct((B, T_PAD, (n_chunks + 1) * CHUNK_KEYS), jnp.float32),
        compiler_params=_params(2),
        name="sattn_score",
    )(page_table, qi_rows, w_col, ik_new, *([cache_ik_t] * NPG))


def _sattn_kernel(pt_ref, s_ref, q_ref, g_ref, qnw_ref, bias_ref, knew_ref, vnew_ref, *rest, topk, n_new, past):
    kp, vp = rest[:NPG], rest[NPG:2 * NPG]
    o_ref, thr_ref, tie_ref, qn_ref, m_ref, l_ref, acc_ref = rest[2 * NPG:]
    pc = pl.program_id(1)
    last = pl.num_programs(1) - 1
    G = ATT_HEADS // KV_HEADS

    def admissible(col0, width):
        pos = lax.broadcasted_iota(jnp.int32, (T_PAD, width), 1) + col0
        t = lax.broadcasted_iota(jnp.int32, (T_PAD, width), 0)
        return (pos < past) | ((pos - past <= t) & (pos - past < n_new))

    @pl.when(pc == 0)
    def _():
        sc = s_ref[0]
        n_all = sc.shape[-1]
        adm0 = admissible(0, n_all)
        key = jnp.where(adm0, _sortable_key(sc), jnp.int32(INT_MIN))
        thr = _kth_largest_key(key, topk, 1)
        thr_ref[...] = jnp.broadcast_to(thr, thr_ref.shape)
        tie_ref[...] = jnp.full(tie_ref.shape, n_all, jnp.int32)
        cnt_ge = _lanes_sum(jnp.where(key >= thr, 1.0, 0.0))
        tied = (cnt_ge > topk) & (thr > jnp.int32(INT_MIN))

        @pl.when(jnp.max(jnp.where(tied, 1.0, 0.0)) > 0.0)
        def _():
            eq = key == thr
            pos = lax.broadcasted_iota(jnp.int32, key.shape, 1)
            need = topk - _lanes_sum(jnp.where(key > thr, 1.0, 0.0))
            nbits = n_all.bit_length()

            def it(n, j_sel):
                cand = j_sel | jnp.left_shift(jnp.int32(1), nbits - 1 - n)
                c = _lanes_sum(jnp.where(eq & (pos < cand), 1.0, 0.0))
                return jnp.where(c <= need, cand, j_sel)

            j_sel = lax.fori_loop(0, nbits, it, jnp.zeros((T_PAD, 1), jnp.int32))
            tie_ref[...] = jnp.broadcast_to(j_sel, tie_ref.shape)

        q = q_ref[0]
        ms = jnp.mean(q * q, axis=-1, keepdims=True)
        qn_ref[...] = (q * lax.rsqrt(ms + EPS) * qnw_ref[...]).astype(jnp.bfloat16)
        m_ref[...] = jnp.full(m_ref.shape, MASK_VALUE, jnp.float32)
        l_ref[...] = jnp.zeros(l_ref.shape, jnp.float32)
        acc_ref[...] = jnp.zeros(acc_ref.shape, jnp.float32)

    def attend(get_k, get_v, width):
        col0 = pl.multiple_of(pc * CHUNK_KEYS, CHUNK_KEYS)
        sc = s_ref[0, :, pl.ds(col0, width)]
        adm = admissible(col0, width)
        key = jnp.where(adm, _sortable_key(sc), jnp.int32(INT_MIN))
        pos = lax.broadcasted_iota(jnp.int32, (T_PAD, width), 1) + col0
        thr = thr_ref[:, :1]
        sel8 = ((key > thr) | ((key == thr) & (pos < tie_ref[:, :1]))) & adm
        sel = jnp.concatenate([sel8] * G, axis=0)
        for kv in range(KV_HEADS):
            kk = get_k(kv).astype(jnp.bfloat16)
            x = lax.dot_general(qn_ref[kv], kk, _NT, preferred_element_type=jnp.float32) * (ATT_D ** -0.5)
            x = jnp.where(sel, x + bias_ref[kv, :, pl.ds(0, width)], MASK_VALUE)
            m_old = m_ref[kv]
            m_new = jnp.maximum(m_old, jnp.max(x, axis=-1, keepdims=True))
            p = jnp.where(sel, jnp.exp(x - m_new), 0.0)
            alpha = jnp.exp(m_old - m_new)
            l_ref[kv] = alpha * l_ref[kv] + jnp.sum(p, axis=-1, keepdims=True)
            acc_ref[kv] = alpha * acc_ref[kv] + jnp.dot(
                p.astype(jnp.bfloat16), get_v(kv).astype(jnp.bfloat16),
                preferred_element_type=jnp.float32)
            m_ref[kv] = m_new

    @pl.when(pc < last)
    def _():
        attend(lambda kv: jnp.concatenate([kp[i][0, 0, pl.ds(kv, PAGE, stride=KV_HEADS), :] for i in range(NPG)], axis=0),
               lambda kv: jnp.concatenate([vp[i][0, 0, pl.ds(kv, PAGE, stride=KV_HEADS), :] for i in range(NPG)], axis=0),
               CHUNK_KEYS)

    @pl.when(pc == last)
    def _():
        attend(lambda kv: knew_ref[0, :, pl.ds(ATT_D * kv, ATT_D)],
               lambda kv: vnew_ref[0, :, pl.ds(ATT_D * kv, ATT_D)], PAGE)
        o_ref[0] = acc_ref[...] / l_ref[...] * _silu(g_ref[0])


def _sattn_call(page_table, scores, q_rows, g_rows, q_norm_w, bias, k_new, v_new, cache_k, cache_v, layer,
                *, topk, n_new):
    B, n_pages = page_table.shape
    n_chunks = n_pages // NPG
    kvw = KV_HEADS * ATT_D
    rows = (ATT_HEADS // KV_HEADS) * T_PAD
    n_keys = (n_chunks + 1) * CHUNK_KEYS
    page_spec = lambda i: pl.BlockSpec(
        (1, 1, PAGE * KV_HEADS, ATT_D),
        lambda b, pc, pt: (layer, pt[b, jnp.minimum(pc * NPG + i, n_pages - 1)], 0, 0))
    per_b = lambda shape: pl.BlockSpec((1,) + shape, lambda b, pc, pt: (b,) + (0,) * len(shape))
    grid_spec = pltpu.PrefetchScalarGridSpec(
        num_scalar_prefetch=1, grid=(B, n_chunks + 1),
        in_specs=[per_b((T_PAD, n_keys)), per_b((KV_HEADS, rows, ATT_D)), per_b((KV_HEADS, rows, ATT_D)),
                  pl.BlockSpec((1, ATT_D), lambda b, pc, pt: (0, 0)),
                  pl.BlockSpec((KV_HEADS, rows, CHUNK_KEYS), lambda b, pc, pt: (0, 0, pc)),
                  per_b((PAGE, kvw)), per_b((PAGE, kvw))]
        + [page_spec(i) for i in range(NPG)] + [page_spec(i) for i in range(NPG)],
        out_specs=per_b((KV_HEADS, rows, ATT_D)),
        scratch_shapes=[pltpu.VMEM((T_PAD, LANES), jnp.int32), pltpu.VMEM((T_PAD, LANES), jnp.int32),
                        pltpu.VMEM((KV_HEADS, rows, ATT_D), jnp.bfloat16),
                        pltpu.VMEM((KV_HEADS, rows, 1), jnp.float32), pltpu.VMEM((KV_HEADS, rows, 1), jnp.float32),
                        pltpu.VMEM((KV_HEADS, rows, ATT_D), jnp.float32)])
    return pl.pallas_call(
        functools.partial(_sattn_kernel, topk=topk, n_new=n_new, past=n_pages * PAGE),
        grid_spec=grid_spec,
        out_shape=jax.ShapeDtypeStruct((B, KV_HEADS, rows, ATT_D), jnp.float32),
        compiler_params=_params(2),
        name="sattn",
    )(page_table, scores, q_rows, g_rows, q_norm_w.reshape(1, ATT_D), bias, k_new, v_new,
      *([cache_k] * NPG), *([cache_v] * NPG))


def _sample_bias(rel_bias, past, n_keys):
    G = ATT_HEADS // KV_HEADS
    t = jnp.arange(T_PAD, dtype=jnp.int32)[:, None]
    pos = jnp.arange(n_keys, dtype=jnp.int32)[None, :]
    tab = _bias_lookup(rel_bias, _t5_bucket(jnp.maximum(past + t - pos, 0)))
    return jnp.moveaxis(tab, -1, 0).reshape(KV_HEADS, G * T_PAD, n_keys)


def _to_head_rows(a):
    B = a.shape[0]
    G = ATT_HEADS // KV_HEADS
    a = a.reshape(B, T_PAD, KV_HEADS, G, ATT_D)
    return jnp.transpose(a, (0, 2, 3, 1, 4)).reshape(B, KV_HEADS, G * T_PAD, ATT_D)


def _from_head_rows(a):
    B = a.shape[0]
    G = ATT_HEADS // KV_HEADS
    a = a.reshape(B, KV_HEADS, G, T_PAD, ATT_D)
    return jnp.transpose(a, (0, 3, 1, 2, 4)).reshape(B, T_PAD, ATT_HEADS * ATT_D)


def _layer(x, lp, states, attend, *, l_valid, tm, chunk_hg, chunk_ssd):
    B, L, _ = x.shape
    hg_s0, ssm_s0, conv_s0, s5_re0, s5_im0 = states
    x2d = x.reshape(B * L, D_MODEL)
    z2d = _inproj(x2d, lp['norm_w'], lp['w_in'], tm)
    z = z2d.reshape(B, L, D_IN_PAD)

    o_hg, hg_s = _hgrn_call(z, lp['lb'], lp['hg_norm_w'], hg_s0, chunk=chunk_hg, l_valid=l_valid)

    conv8 = jnp.pad(conv_s0, ((0, 0), (SUBLANES - (SSM_CONV - 1), 0), (0, 0)))
    y_ssm, ssm_s, cx, cb, cc = _ssd_call(z, lp['conv_w'], lp['conv_b'], conv8, lp['dtb_pad'], lp['alog_pad'],
                                         lp['dskip_pad'], lp['ssm_norm_w'], ssm_s0, chunk=chunk_ssd, l_valid=l_valid)
    tail = SUBLANES - (SSM_CONV - 1)
    conv_s = jnp.concatenate([cx[:, tail:], cb[:, tail:], cc[:, tail:]], axis=-1)

    h5, s5_re, s5_im = _s5_call(z, *lp['s5_blocks'], lp['a_re'], lp['a_im'], lp['log_dt'], lp['s5_d'],
                                s5_re0.reshape(B, 1, -1), s5_im0.reshape(B, 1, -1), l_valid=l_valid)
    o5 = _glu_call(h5.reshape(B * L, BRANCH), lp['glu_w'], lp['glu_b'], z2d, tm=tm)

    o_att, k, v, ik = attend(z)

    y = _outproj((o_hg.reshape(B * L, BRANCH), y_ssm.reshape(B * L, BRANCH), o5, o_att.reshape(B * L, BRANCH)),
                 lp['w_out'], lp['layer'], x2d, tm)
    st = (k[:, :l_valid].reshape(B, l_valid, KV_HEADS, ATT_D), v[:, :l_valid].reshape(B, l_valid, KV_HEADS, ATT_D),
          ik[:, :l_valid], hg_s, ssm_s, conv_s,
          s5_re.reshape(B, S5_GROUPS, S5_STATE), s5_im.reshape(B, S5_GROUPS, S5_STATE))
    return y.reshape(B, L, D_MODEL), st


def kernel(x_prompt, x_sample, cache_k, cache_v, cache_idx_k, state_hgrn, state_ssm, state_conv, state_s5_re, state_s5_im, page_table, norm_w, w_in, w_out, hg_lb_logits, hg_norm_w, ssm_conv_w, ssm_conv_b, ssm_dt_bias, ssm_a_log, ssm_d, ssm_norm_w, s5_a_re, s5_a_im, s5_log_dt, s5_b_re, s5_b_im, s5_c_re, s5_c_im, s5_d, s5_glu_w, s5_glu_b, att_q_norm, att_k_norm, rel_bias):
    f32 = jnp.float32
    bp, lp_len, _ = x_prompt.shape
    bs, ls, _ = x_sample.shape
    n_pages = page_table.shape[1]
    past = n_pages * PAGE
    n_keys = (n_pages // NPG + 1) * CHUNK_KEYS

    sm = jax.nn.softmax(hg_lb_logits.astype(f32), axis=0)
    lower = jnp.cumsum(sm, axis=0) - sm[0]

    w_out_bf16 = w_out.astype(jnp.bfloat16)
    w_in_t = jnp.swapaxes(w_in, 1, 2)
    cache_ik_t = jnp.swapaxes(cache_idx_k, 2, 3)
    page_rows = lambda c: c.reshape(c.shape[0], c.shape[1], PAGE * KV_HEADS, ATT_D)
    cache_k2, cache_v2 = page_rows(cache_k), page_rows(cache_v)
    bias_tiles = _prompt_bias_tiles(rel_bias)
    bias_sample = _sample_bias(rel_bias, past, n_keys)
    topk_p = min(TOPK_MAX, lp_len // 4)
    topk_s = min(TOPK_MAX, (past + ls) // 4)

    def head_lanes(v):
        return jnp.zeros((1, LANES), f32).at[0, SMALL_DT:SMALL_DT + SSM_HEADS].set(v)

    xs_pad = jnp.pad(x_sample, ((0, 0), (0, SAMPLE_ROWS - ls), (0, 0)))
    pad_page = lambda a: jnp.pad(a, ((0, 0), (0, PAGE - a.shape[1]), (0, 0)))

    yp, ys = x_prompt, xs_pad
    new_p = [[] for _ in range(8)]
    new_s = [[] for _ in range(8)]
    for l in range(DEPTH):
        lp = {'norm_w': norm_w[l], 'w_in': _pack_w_in(w_in_t, l), 'w_out': w_out_bf16, 'layer': l,
              'lb': lower[l], 'hg_norm_w': hg_norm_w[l],
              'conv_w': ssm_conv_w[l], 'conv_b': ssm_conv_b[l], 'dtb_pad': head_lanes(ssm_dt_bias[l]),
              'alog_pad': head_lanes(ssm_a_log[l]), 'dskip_pad': head_lanes(ssm_d[l]), 'ssm_norm_w': ssm_norm_w[l],
              's5_blocks': _s5_block_diag(s5_b_re[l], s5_b_im[l], s5_c_re[l], s5_c_im[l]),
              'a_re': s5_a_re[l].reshape(1, -1), 'a_im': s5_a_im[l].reshape(1, -1),
              'log_dt': jnp.repeat(s5_log_dt[l], S5_STATE).reshape(1, -1), 's5_d': s5_d[l].reshape(1, -1),
              'glu_w': s5_glu_w[l].astype(jnp.bfloat16), 'glu_b': s5_glu_b[l]}

        def attend_prompt(z):
            k, v, ik, kbf, vt, kia, kib = _kvprep_call(z, att_k_norm[l], tr=256, transposed=True)
            o = _pattn_call(z, kbf, vt, kia, kib, att_q_norm[l], bias_tiles, topk=topk_p)
            return o, k, v, ik

        def attend_sample(z):
            k, v, ik = _kvprep_call(z, att_k_norm[l], tr=SAMPLE_ROWS, transposed=False)
            z8 = z[:, :T_PAD]
            qi_rows = z8[..., OFF_IX_Q:OFF_IX_Q + IDX_HEADS * IDX_D].reshape(bs, T_PAD * IDX_HEADS, IDX_D)
            w_col = z8[..., OFF_SMALL + SMALL_IXW:OFF_SMALL + SMALL_IXW + IDX_HEADS] * (1.0 / 32.0)
            scores = _sattn_score_call(page_table, qi_rows.astype(jnp.bfloat16),
                                       w_col.reshape(bs, T_PAD * IDX_HEADS, 1), pad_page(ik), cache_ik_t, l)
            o = _sattn_call(page_table, scores, _to_head_rows(z8[..., OFF_AT_Q:OFF_AT_Q + BRANCH]),
                            _to_head_rows(z8[..., OFF_AT_G:OFF_AT_G + BRANCH]), att_q_norm[l], bias_sample,
                            pad_page(k), pad_page(v), cache_k2, cache_v2, l, topk=topk_s, n_new=ls)
            o = jnp.pad(_from_head_rows(o), ((0, 0), (0, SAMPLE_ROWS - T_PAD), (0, 0)))
            return o.astype(jnp.bfloat16), k, v, ik

        zero_states = (jnp.zeros((bp, HG_HEADS, LANES, LANES), f32),
                       jnp.zeros((bp, SSM_HEADS, SSM_P, SSM_STATE), f32),
                       jnp.zeros((bp, SSM_CONV - 1, SSM_CONV_DIM), f32),
                       jnp.zeros((bp, S5_GROUPS, S5_STATE), f32),
                       jnp.zeros((bp, S5_GROUPS, S5_STATE), f32))
        yp, st_p = _layer(yp, lp, zero_states, attend_prompt, l_valid=lp_len, tm=512, chunk_hg=64, chunk_ssd=128)
        samp_states = (state_hgrn[l], state_ssm[l], state_conv[l], state_s5_re[l], state_s5_im[l])
        ys, st_s = _layer(ys, lp, samp_states, attend_sample, l_valid=ls, tm=bs * SAMPLE_ROWS,
                          chunk_hg=SAMPLE_ROWS, chunk_ssd=SAMPLE_ROWS)
        for i in range(8):
            new_p[i].append(st_p[i])
            new_s[i].append(st_s[i])
    pk, pv, pik, phg, pssm, pconv, ps5r, ps5i = [jnp.stack(a) for a in new_p]
    sk, sv, sik, shg, sssm, sconv, ss5r, ss5i = [jnp.stack(a) for a in new_s]
    return (yp, ys[:, :ls], pk, pv, pik, phg, pssm, pconv, ps5r, ps5i, sk, sv, sik, shg, sssm, sconv, ss5r, ss5i)
```

```python
import functools
import math

import jax
import jax.numpy as jnp
from jax import lax
from jax.experimental import pallas as pl
from jax.experimental.pallas import tpu as pltpu

D_MODEL = 4096
DEPTH = 2
BRANCH = D_MODEL // 4
HG_HEADS = 8
SSM_HEADS = 16
SSM_GROUPS = 4
SSM_STATE = 128
SSM_CONV = 4
SSM_CONV_DIM = BRANCH + 2 * SSM_GROUPS * SSM_STATE
SSM_HPG = SSM_HEADS // SSM_GROUPS
SSM_P = 64
S5_GROUPS = 64
S5_STATE = 64
S5_BLK_CH = 128
S5_BLK_ST = 512
ATT_D = 128
ATT_HEADS = 8
KV_HEADS = 2
IDX_HEADS = 16
IDX_D = 64
TOPK_MAX = 256
Q_BLK = 128
PAGE = 128
REL_BUCKETS = 32
REL_MAX_DIST = 1024
EPS = 1e-6
MASK_VALUE = -1e30
F_FLOOR = 1e-30
INT_MIN = -2 ** 31

LANES = 128
SUBLANES = 8
V7X_VMEM_LIMIT = 56 * 1024 * 1024

_SRC_DT, _SRC_S5U, _SRC_ATK, _SRC_ATG, _SRC_IXK, D_IN_SRC = 7168, 7184, 10256, 10768, 12816, 12896

OFF_HG_Q, OFF_HG_F, OFF_HG_I, OFF_HG_G = 0, 1024, 2048, 3072
OFF_SM_Z, OFF_SM_XBC = 4096, 5120
OFF_S5_U, OFF_S5_G = 7168, 8192
OFF_AT_Q, OFF_AT_G, OFF_IX_Q = 9216, 10240, 11264
OFF_AT_KV = 12288
OFF_SMALL = 12800
SMALL_IXW, SMALL_DT = 64, 80
D_IN_PAD = 13312

PACK_TN = 1024
PACK_TK = 1024
_PACK_LAST = D_IN_PAD // PACK_TN - 1
_PACK_ROW0 = (tuple(range(0, _SRC_DT, PACK_TN)) + tuple(range(_SRC_S5U, _SRC_ATK, PACK_TN))
              + tuple(range(_SRC_ATG, _SRC_IXK, PACK_TN)) + (_SRC_ATK,))
N_KV_ROWS = _SRC_ATG - _SRC_ATK
N_IX_ROWS = D_IN_SRC - _SRC_IXK
N_DT_ROWS = _SRC_S5U - _SRC_DT
NORM_ROWS = 128
S5_BULK_STEPS = 32

N_BIAS_TILES = 9
SSD_GPB = 2
HG_SUB = 16
HG_HPB = 8
HG_ROW_BLOCK = 512
N_CAUSAL_VARIANTS = 8
NPG = 16
T_PAD = 8
CHUNK_KEYS = NPG * PAGE
SAMPLE_ROWS = 16

_TN = (((0,), (0,)), ((), ()))
_NT = (((1,), (1,)), ((), ()))


def _params(n_axes):
    return pltpu.CompilerParams(dimension_semantics=("arbitrary",) * n_axes, vmem_limit_bytes=V7X_VMEM_LIMIT)


def _bf(x):
    return x.astype(jnp.bfloat16)


def _dot(a, b, dims=None):
    if dims is None:
        return jnp.dot(_bf(a), _bf(b), preferred_element_type=jnp.float32)
    return lax.dot_general(_bf(a), _bf(b), dims, preferred_element_type=jnp.float32)


def _sigmoid(x):
    return 1.0 / (1.0 + jnp.exp(-x))


def _silu(x):
    return x * _sigmoid(x)


def _softplus(x):
    return jnp.maximum(x, 0.0) + jnp.log(1.0 + jnp.exp(-jnp.abs(x)))


def _gelu_tanh(x):
    return 0.5 * x * (1.0 + jnp.tanh(math.sqrt(2.0 / math.pi) * (x + 0.044715 * (x * x * x))))


def _cumsum_rows(x):
    n = x.shape[0]
    row = lax.broadcasted_iota(jnp.int32, x.shape, 0)
    sh = 1
    while sh < n:
        x = x + jnp.where(row >= sh, pltpu.roll(x, sh, axis=0), 0.0)
        sh *= 2
    return x


def _split3(x):
    hi = x.astype(jnp.bfloat16)
    r = x - hi.astype(jnp.float32)
    mid = r.astype(jnp.bfloat16)
    lo = (r - mid.astype(jnp.float32)).astype(jnp.bfloat16)
    return hi, mid, lo


def _select_lanes_as_rows(sel, x):
    out = None
    for part in _split3(x):
        t = lax.dot_general(sel, part, _NT, preferred_element_type=jnp.float32)
        out = t if out is None else out + t
    return out


def _cmul(ar, ai, br, bi):
    return ar * br - ai * bi, ar * bi + ai * br


def _pack_kernel(off_ref, w_ref, ix_ref, dt_ref, o_ref):
    j = pl.program_id(0)

    @pl.when(j < _PACK_LAST)
    def _():
        o_ref[...] = w_ref[0].astype(o_ref.dtype)

    @pl.when(j == _PACK_LAST)
    def _():
        o_ref[pl.ds(0, N_KV_ROWS), :] = w_ref[0, pl.ds(0, N_KV_ROWS), :].astype(o_ref.dtype)
        o_ref[pl.ds(N_KV_ROWS, N_IX_ROWS), :] = ix_ref[0].astype(o_ref.dtype)
        o_ref[pl.ds(N_KV_ROWS + N_IX_ROWS, N_DT_ROWS), :] = dt_ref[0].astype(o_ref.dtype)
        used = N_KV_ROWS + N_IX_ROWS + N_DT_ROWS
        o_ref[pl.ds(used, PACK_TN - used), :] = jnp.zeros((PACK_TN - used, PACK_TK), o_ref.dtype)


def _pack_w_in(w_in_t, layer):
    assert all(r % SUBLANES == 0 for r in _PACK_ROW0)
    row0 = jnp.asarray([r // SUBLANES for r in _PACK_ROW0], jnp.int32)
    E = pl.Element
    fixed = lambda rows, row: pl.BlockSpec(
        (E(1), E(rows), E(PACK_TK)), lambda j, kc, off: (layer, row, kc * PACK_TK))
    grid_spec = pltpu.PrefetchScalarGridSpec(
        num_scalar_prefetch=1, grid=(D_IN_PAD // PACK_TN, D_MODEL // PACK_TK),
        in_specs=[pl.BlockSpec((E(1), E(PACK_TN), E(PACK_TK)),
                               lambda j, kc, off: (layer, off[j] * SUBLANES, kc * PACK_TK)),
                  fixed(N_IX_ROWS, _SRC_IXK), fixed(N_DT_ROWS, _SRC_DT)],
        out_specs=pl.BlockSpec((PACK_TN, PACK_TK), lambda j, kc, off: (j, kc)))
    return pl.pallas_call(
        _pack_kernel, grid_spec=grid_spec,
        out_shape=jax.ShapeDtypeStruct((D_IN_PAD, D_MODEL), jnp.bfloat16),
        compiler_params=_params(2),
        name="pack_w_in",
    )(row0, w_in_t, w_in_t, w_in_t)


def _inproj_kernel(x_ref, nw_ref, w_ref, z_ref, hn_ref):
    @pl.when(pl.program_id(1) == 0)
    def _():
        nw = nw_ref[...]
        for r0 in range(0, x_ref.shape[0], NORM_ROWS):
            xf = x_ref[pl.ds(r0, NORM_ROWS), :]
            ms = jnp.mean(xf * xf, axis=-1, keepdims=True)
            hn_ref[pl.ds(r0, NORM_ROWS), :] = (xf * lax.rsqrt(ms + EPS) * nw).astype(jnp.bfloat16)

    z_ref[...] = lax.dot_general(hn_ref[...], w_ref[...], _NT, preferred_element_type=jnp.float32)


def _inproj(x2d, norm_w, w_packed, tm, tn=1024):
    m = x2d.shape[0]
    return pl.pallas_call(
        _inproj_kernel,
        grid=(m // tm, D_IN_PAD // tn),
        in_specs=[pl.BlockSpec((tm, D_MODEL), lambda i, j: (i, 0)),
                  pl.BlockSpec((1, D_MODEL), lambda i, j: (0, 0)),
                  pl.BlockSpec((tn, D_MODEL), lambda i, j: (j, 0))],
        out_specs=pl.BlockSpec((tm, tn), lambda i, j: (i, j)),
        out_shape=jax.ShapeDtypeStruct((m, D_IN_PAD), jnp.float32),
        scratch_shapes=[pltpu.VMEM((tm, D_MODEL), jnp.bfloat16)],
        compiler_params=_params(2),
        name="inproj",
    )(x2d, norm_w.reshape(1, D_MODEL), w_packed)


def _outproj_kernel(m0_ref, m1_ref, m2_ref, m3_ref, w_ref, x_ref, y_ref):
    acc = x_ref[...]
    for i, m_ref in enumerate((m0_ref, m1_ref, m2_ref, m3_ref)):
        acc = acc + jnp.dot(m_ref[...], w_ref[pl.ds(BRANCH * i, BRANCH), :], preferred_element_type=jnp.float32)
    y_ref[...] = acc


def _outproj(mixed4, w_out_bf16, layer, x2d, tm, tn=1024):
    m = x2d.shape[0]
    mspec = pl.BlockSpec((tm, BRANCH), lambda i, j: (i, 0))
    return pl.pallas_call(
        _outproj_kernel,
        grid=(m // tm, D_MODEL // tn),
        in_specs=[mspec, mspec, mspec, mspec,
                  pl.BlockSpec((None, D_MODEL, tn), lambda i, j: (layer, 0, j)),
                  pl.BlockSpec((tm, tn), lambda i, j: (i, j))],
        out_specs=pl.BlockSpec((tm, tn), lambda i, j: (i, j)),
        out_shape=jax.ShapeDtypeStruct((m, D_MODEL), jnp.float32),
        compiler_params=_params(2),
        name="outproj",
    )(*mixed4, w_out_bf16, x2d)


def _hgrn_kernel(q_ref, f_ref, i_ref, g_ref, lb_ref, nw_ref, s0_ref, o_ref, s_out_ref, st_ref, *, chunk, l_valid):
    LB = q_ref.shape[1]
    C = chunk
    nblk = C // SUBLANES
    lb_i = pl.program_id(2)

    @pl.when(lb_i == 0)
    def _():
        for hh in range(HG_HPB):
            st_ref[hh] = s0_ref[0, hh].T

    nw = nw_ref[...]
    rowi = lax.broadcasted_iota(jnp.int32, (SUBLANES, LANES), 0)

    def head_chunk(hh, t0):
        lanes = pl.ds(LANES * hh, LANES)
        lb = lb_ref[0, :, lanes]
        fp = f_ref[0, pl.ds(t0, C), lanes]
        qp = q_ref[0, pl.ds(t0, C), lanes]
        v = i_ref[0, pl.ds(t0, C), lanes]
        gp = g_ref[0, pl.ds(t0, C), lanes]
        fg = lb + (1.0 - lb) * _sigmoid(fp)
        logf = jnp.log(jnp.maximum(fg, F_FLOOR))
        kk = (1.0 - lb) * _sigmoid(-fp)
        if l_valid < LB * pl.num_programs(2):
            rows = lax.broadcasted_iota(jnp.int32, (C, LANES), 0) + t0 + lb_i * LB
            logf = jnp.where(rows < l_valid, logf, 0.0)
            kk = jnp.where(rows < l_valid, kk, 0.0)
        qq = _silu(qp)
        G = _cumsum_rows(logf)
        st = st_ref[hh]
        o_inter = _dot(qq * jnp.exp(G), st, _NT)
        acc = [None] * nblk
        qb = [qq[SUBLANES * tb:SUBLANES * (tb + 1)] for tb in range(nblk)]
        Gb = [G[SUBLANES * tb:SUBLANES * (tb + 1)] for tb in range(nblk)]
        for s in range(C):
            sb = s // SUBLANES
            gs = G[s:s + 1]
            ks = kk[s:s + 1]
            vs = v[s:s + 1]
            for tb in range(sb, (s // HG_SUB + 1) * (HG_SUB // SUBLANES)):
                d = Gb[tb] - gs
                if tb == sb:
                    d = jnp.where(rowi >= s - SUBLANES * sb, d, MASK_VALUE)
                w = jnp.sum(qb[tb] * ks * jnp.exp(d), axis=-1, keepdims=True)
                contrib = w * vs
                acc[tb] = contrib if acc[tb] is None else acc[tb] + contrib
        for j in range(1, C // HG_SUB):
            r0 = HG_SUB * j
            gb = G[r0 - 1:r0]
            qj = qq[r0:r0 + HG_SUB] * jnp.exp(G[r0:r0 + HG_SUB] - gb)
            kj = kk[:r0] * jnp.exp(gb - G[:r0])
            oj = _dot(_dot(qj, kj, _NT), v[:r0])
            for half in range(HG_SUB // SUBLANES):
                tb = r0 // SUBLANES + half
                acc[tb] = acc[tb] + oj[SUBLANES * half:SUBLANES * (half + 1)]
        o = o_inter + jnp.concatenate(acc, axis=0)
        g_last = G[C - 1:C]
        kd = kk * jnp.exp(g_last - G)
        st_ref[hh] = jnp.exp(g_last) * st + _dot(v, kd, _TN)
        ms = jnp.mean(o * o, axis=-1, keepdims=True)
        on = o * lax.rsqrt(ms + EPS) * nw
        o_ref[0, pl.ds(t0, C), lanes] = (on * _silu(gp)).astype(o_ref.dtype)

    def body(c, carry):
        t0 = pl.multiple_of(c * C, C)
        for hh in range(HG_HPB):
            head_chunk(hh, t0)
        return carry

    lax.fori_loop(0, LB // C, body, 0)

    @pl.when(lb_i == pl.num_programs(2) - 1)
    def _():
        for hh in range(HG_HPB):
            s_out_ref[0, hh] = st_ref[hh].T


def _hgrn_call(z, lb, hg_norm_w, s0, *, chunk, l_valid):
    B, L, _ = z.shape
    W = HG_HPB * LANES
    LB = min(L, HG_ROW_BLOCK)
    zspec = lambda off: pl.BlockSpec((1, LB, W), lambda b, h, r: (b, r, off // W + h))
    return pl.pallas_call(
        functools.partial(_hgrn_kernel, chunk=chunk, l_valid=l_valid),
        grid=(B, HG_HEADS // HG_HPB, L // LB),
        in_specs=[zspec(OFF_HG_Q), zspec(OFF_HG_F), zspec(OFF_HG_I), zspec(OFF_HG_G),
                  pl.BlockSpec((1, 1, W), lambda b, h, r: (h, 0, 0)),
                  pl.BlockSpec((1, LANES), lambda b, h, r: (0, 0)),
                  pl.BlockSpec((1, HG_HPB, LANES, LANES), lambda b, h, r: (b, h, 0, 0))],
        out_specs=[pl.BlockSpec((1, LB, W), lambda b, h, r: (b, r, h)),
                   pl.BlockSpec((1, HG_HPB, LANES, LANES), lambda b, h, r: (b, h, 0, 0))],
        out_shape=[jax.ShapeDtypeStruct((B, L, BRANCH), jnp.bfloat16),
                   jax.ShapeDtypeStruct((B, HG_HEADS, LANES, LANES), jnp.float32)],
        scratch_shapes=[pltpu.VMEM((HG_HPB, LANES, LANES), jnp.float32)],
        compiler_params=_params(3),
        name="hgrn",
    )(z, z, z, z, lb.reshape(HG_HEADS // HG_HPB, 1, W), hg_norm_w.reshape(1, LANES), s0)


def _ssd_kernel(xs_ref, b_ref, c_ref, zg_ref, sm_ref, wx_ref, wb_ref, wc_ref, bx_ref, bb_ref, bc_ref,
                cx_ref, cb_ref, cc_ref, dtb_ref, alog_ref, dskip_ref, nw_ref, s0_ref,
                y_ref, s_out_ref, ox_ref, ob_ref, oc_ref,
                ax_ref, ab_ref, ac_ref, hx_ref, hb_ref, hc_ref, sp_ref, *, chunk, l_valid, conv_rows):
    L = xs_ref.shape[1]
    C = chunk
    R = conv_rows
    gb = pl.program_id(1)
    n_heads = SSD_GPB * SSM_HPG
    n_pairs = n_heads // 2

    def conv(src_ref, head_ref, cst_ref, w_ref, bias_ref, act_ref, out_state_ref):
        head_ref[pl.ds(0, SUBLANES), :] = cst_ref[0]
        head_ref[pl.ds(SUBLANES, R), :] = src_ref[0, pl.ds(0, R), :]
        w = w_ref[...]
        for r0 in range(0, L, R):
            acc = bias_ref[...]
            for j in range(SSM_CONV):
                if r0 == 0:
                    xj = head_ref[pl.ds(SUBLANES - j, R), :]
                else:
                    xj = src_ref[0, pl.ds(r0 - j, R), :]
                acc = acc + xj * w[SSM_CONV - 1 - j:SSM_CONV - j]
            act_ref[pl.ds(r0, R), :] = _silu(acc)
        if l_valid >= SUBLANES:
            out_state_ref[0] = src_ref[0, pl.ds(l_valid - SUBLANES, SUBLANES), :]
        else:
            out_state_ref[0] = head_ref[pl.ds(l_valid, SUBLANES), :]

    conv(xs_ref, hx_ref, cx_ref, wx_ref, bx_ref, ax_ref, ox_ref)
    conv(b_ref, hb_ref, cb_ref, wb_ref, bb_ref, ab_ref, ob_ref)
    conv(c_ref, hc_ref, cc_ref, wc_ref, bc_ref, ac_ref, oc_ref)

    for p in range(n_pairs):
        sp_ref[p] = s0_ref[0, 2 * p:2 * p + 2].reshape(2 * SSM_P, LANES)

    lane1 = lax.broadcasted_iota(jnp.int32, (1, LANES), 1)
    lane8 = lax.broadcasted_iota(jnp.int32, (SUBLANES, LANES), 1)
    row8 = lax.broadcasted_iota(jnp.int32, (SUBLANES, LANES), 0)
    lane0 = SMALL_DT + n_heads * gb
    sel = jnp.where((lane8 == lane0 + row8) & (row8 < n_heads), 1.0, 0.0).astype(jnp.bfloat16)
    a_all = -jnp.exp(alog_ref[...])
    dskip = [jnp.sum(jnp.where(lane1 == lane0 + j, dskip_ref[...], 0.0), axis=-1, keepdims=True)
             for j in range(n_heads)]
    pair_w = 2 * SSM_P
    lane_c = lax.broadcasted_iota(jnp.int32, (C, LANES), 1)
    first_half = lane_c < SSM_P
    row_p = lax.broadcasted_iota(jnp.int32, (2 * SSM_P, LANES), 0) < SSM_P
    tril = lax.broadcasted_iota(jnp.int32, (C, C), 0) >= lax.broadcasted_iota(jnp.int32, (C, C), 1)
    nw = nw_ref[...]

    def body(c, carry):
        t0 = pl.multiple_of(c * C, C)
        sm = sm_ref[0, pl.ds(t0, C), :]
        dt_all = _softplus(sm + dtb_ref[...])
        if l_valid < L:
            rows = lax.broadcasted_iota(jnp.int32, (C, LANES), 0) + t0
            dt_all = jnp.where(rows < l_valid, dt_all, 0.0)
        cum = _cumsum_rows(dt_all * a_all)
        cum_rows = _select_lanes_as_rows(sel, cum)
        col = [jnp.sum(jnp.where(lane_c == lane0 + j, cum, 0.0), axis=-1, keepdims=True) for j in range(n_heads)]
        dtc = [jnp.sum(jnp.where(lane_c == lane0 + j, dt_all, 0.0), axis=-1, keepdims=True) for j in range(n_heads)]
        for lg in range(SSD_GPB):
            bact = ab_ref[pl.ds(t0, C), pl.ds(LANES * lg, LANES)]
            cact = ac_ref[pl.ds(t0, C), pl.ds(LANES * lg, LANES)]
            cb = _dot(cact, bact, _NT)
            ys = []
            for p in range(SSM_HPG // 2):
                pp = lg * (SSM_HPG // 2) + p
                ja, jb = 2 * pp, 2 * pp + 1
                xs = ax_ref[pl.ds(t0, C), pl.ds(pair_w * pp, pair_w)]
                xdt = xs * jnp.where(first_half, dtc[ja], dtc[jb])
                dec_a = jnp.exp(jnp.where(tril, col[ja] - cum_rows[ja:ja + 1], MASK_VALUE))
                dec_b = jnp.exp(jnp.where(tril, col[jb] - cum_rows[jb:jb + 1], MASK_VALUE))
                y = jnp.where(first_half, _dot(cb * dec_a, xdt), _dot(cb * dec_b, xdt))
                sp = sp_ref[pp]
                y = y + _dot(cact, sp, _NT) * jnp.where(first_half, jnp.exp(col[ja]), jnp.exp(col[jb]))
                y = y + jnp.where(first_half, dskip[ja], dskip[jb]) * xs
                last_a, last_b = col[ja][C - 1:C], col[jb][C - 1:C]
                xw = xdt * jnp.where(first_half, jnp.exp(last_a - col[ja]), jnp.exp(last_b - col[jb]))
                sp_ref[pp] = jnp.where(row_p, jnp.exp(last_a), jnp.exp(last_b)) * sp + _dot(xw, bact, _TN)
                zg = zg_ref[0, pl.ds(t0, C), pl.ds(pair_w * pp, pair_w)]
                ys.append(y * _silu(zg))
            ms = sum(jnp.sum(y * y, axis=-1, keepdims=True) for y in ys) * (1.0 / (SSM_HPG * SSM_P))
            inv = lax.rsqrt(ms + EPS)
            for p in range(SSM_HPG // 2):
                pp = lg * (SSM_HPG // 2) + p
                y_ref[0, pl.ds(t0, C), pl.ds(pair_w * pp, pair_w)] = (
                    ys[p] * inv * nw[:, pair_w * pp:pair_w * (pp + 1)]).astype(y_ref.dtype)
        return carry

    lax.fori_loop(0, L // C, body, 0)
    for p in range(n_pairs):
        s_out_ref[0, 2 * p:2 * p + 2] = sp_ref[p].reshape(2, SSM_P, LANES)


def _ssd_call(z, conv_w, conv_b, conv_state8, dtb_pad, alog_pad, dskip_pad, norm_w, s0, *, chunk, l_valid):
    B, L, _ = z.shape
    assert l_valid >= SSM_CONV - 1
    xw = SSD_GPB * SSM_HPG * SSM_P
    bw = SSD_GPB * SSM_STATE
    nh = SSD_GPB * SSM_HPG
    conv_rows = min(L, 128)
    ox, ob = OFF_SM_XBC // xw, (OFF_SM_XBC + BRANCH) // bw
    oc = (OFF_SM_XBC + BRANCH + SSM_GROUPS * SSM_STATE) // bw
    wb0, wc0 = BRANCH // bw, (BRANCH + SSM_GROUPS * SSM_STATE) // bw
    in_specs = [
        pl.BlockSpec((1, L, xw), lambda b, g: (b, 0, ox + g)),
        pl.BlockSpec((1, L, bw), lambda b, g: (b, 0, ob + g)),
        pl.BlockSpec((1, L, bw), lambda b, g: (b, 0, oc + g)),
        pl.BlockSpec((1, L, xw), lambda b, g: (b, 0, OFF_SM_Z // xw + g)),
        pl.BlockSpec((1, L, LANES), lambda b, g: (b, 0, OFF_SMALL // LANES)),
        pl.BlockSpec((SSM_CONV, xw), lambda b, g: (0, g)),
        pl.BlockSpec((SSM_CONV, bw), lambda b, g: (0, wb0 + g)),
        pl.BlockSpec((SSM_CONV, bw), lambda b, g: (0, wc0 + g)),
        pl.BlockSpec((1, xw), lambda b, g: (0, g)),
        pl.BlockSpec((1, bw), lambda b, g: (0, wb0 + g)),
        pl.BlockSpec((1, bw), lambda b, g: (0, wc0 + g)),
        pl.BlockSpec((1, SUBLANES, xw), lambda b, g: (b, 0, g)),
        pl.BlockSpec((1, SUBLANES, bw), lambda b, g: (b, 0, wb0 + g)),
        pl.BlockSpec((1, SUBLANES, bw), lambda b, g: (b, 0, wc0 + g)),
        pl.BlockSpec((1, LANES), lambda b, g: (0, 0)),
        pl.BlockSpec((1, LANES), lambda b, g: (0, 0)),
        pl.BlockSpec((1, LANES), lambda b, g: (0, 0)),
        pl.BlockSpec((1, xw), lambda b, g: (0, g)),
        pl.BlockSpec((1, nh, SSM_P, LANES), lambda b, g: (b, g, 0, 0)),
    ]
    out_specs = [
        pl.BlockSpec((1, L, xw), lambda b, g: (b, 0, g)),
        pl.BlockSpec((1, nh, SSM_P, LANES), lambda b, g: (b, g, 0, 0)),
        pl.BlockSpec((1, SUBLANES, xw), lambda b, g: (b, 0, g)),
        pl.BlockSpec((1, SUBLANES, bw), lambda b, g: (b, 0, g)),
        pl.BlockSpec((1, SUBLANES, bw), lambda b, g: (b, 0, g)),
    ]
    f32 = jnp.float32
    out_shape = [
        jax.ShapeDtypeStruct((B, L, BRANCH), jnp.bfloat16),
        jax.ShapeDtypeStruct((B, SSM_HEADS, SSM_P, LANES), f32),
        jax.ShapeDtypeStruct((B, SUBLANES, BRANCH), f32),
        jax.ShapeDtypeStruct((B, SUBLANES, SSM_GROUPS * SSM_STATE), f32),
        jax.ShapeDtypeStruct((B, SUBLANES, SSM_GROUPS * SSM_STATE), f32),
    ]
    scratch = [pltpu.VMEM((L, xw), f32), pltpu.VMEM((L, bw), f32), pltpu.VMEM((L, bw), f32),
               pltpu.VMEM((SUBLANES + conv_rows, xw), f32), pltpu.VMEM((SUBLANES + conv_rows, bw), f32),
               pltpu.VMEM((SUBLANES + conv_rows, bw), f32),
               pltpu.VMEM((nh // 2, 2 * SSM_P, LANES), f32)]
    cbias = conv_b.reshape(1, -1)
    return pl.pallas_call(
        functools.partial(_ssd_kernel, chunk=chunk, l_valid=l_valid, conv_rows=conv_rows),
        grid=(B, SSM_GROUPS // SSD_GPB), in_specs=in_specs, out_specs=out_specs, out_shape=out_shape, scratch_shapes=scratch,
        compiler_params=_params(2),
        name="ssd",
    )(z, z, z, z, z, conv_w, conv_w, conv_w, cbias, cbias, cbias, conv_state8, conv_state8, conv_state8,
      dtb_pad, alog_pad, dskip_pad, norm_w.reshape(1, -1), s0)


def _s5_kernel(u_ref, bre_ref, bim_ref, cre_ref, cim_ref, are_ref, aim_ref, ldt_ref, d_ref, x0r_ref, x0i_ref,
               h_ref, xr_out_ref, xi_out_ref, bbr_ref, bbi_ref, xr_ref, xi_ref, pwr_ref, pwi_ref,
               *, nseg, seg_len):
    L = u_ref.shape[1]
    n_scan = nseg * seg_len
    row_blk = min(L, 256)
    a_re, a_im = are_ref[...], aim_ref[...]
    dt = jnp.exp(ldt_ref[...])
    mag = jnp.exp(a_re * dt)
    ab_re, ab_im = mag * jnp.cos(a_im * dt), mag * jnp.sin(a_im * dt)
    den = a_re * a_re + a_im * a_im
    nr = ab_re - 1.0
    coef_re = (nr * a_re + ab_im * a_im) / den
    coef_im = (ab_im * a_re - nr * a_im) / den

    NQ = S5_BLK_ST // LANES
    lq = lambda v, q: v[:, LANES * q:LANES * (q + 1)]
    abr = [lq(ab_re, q) for q in range(NQ)]
    abi = [lq(ab_im, q) for q in range(NQ)]

    for r0 in range(0, L, row_blk):
        u = u_ref[0, pl.ds(r0, row_blk), :]
        bu_re, bu_im = _dot(u, bre_ref[0]), _dot(u, bim_ref[0])
        bb_re = coef_re * bu_re - coef_im * bu_im
        bb_im = coef_re * bu_im + coef_im * bu_re
        for q in range(NQ):
            bbr_ref[q, pl.ds(r0, row_blk), :] = lq(bb_re, q)
            bbi_ref[q, pl.ds(r0, row_blk), :] = lq(bb_im, q)

    if n_scan < L:
        for q in range(NQ):
            xr_ref[q, pl.ds(n_scan, L - n_scan), :] = jnp.zeros((L - n_scan, LANES), jnp.float32)
            xi_ref[q, pl.ds(n_scan, L - n_scan), :] = jnp.zeros((L - n_scan, LANES), jnp.float32)

    def rows(i):
        return pl.ds(i, nseg, stride=seg_len) if nseg > 1 else pl.ds(i, 1)

    def scan(i, carry):
        out = []
        for q in range(NQ):
            xr, xi = carry[q]
            pr, pi = _cmul(abr[q], abi[q], xr, xi)
            xr, xi = pr + bbr_ref[q, rows(i), :], pi + bbi_ref[q, rows(i), :]
            xr_ref[q, rows(i), :] = xr
            xi_ref[q, rows(i), :] = xi
            out.append((xr, xi))
        return tuple(out)

    zero = jnp.zeros((nseg, LANES), jnp.float32)
    ends = lax.fori_loop(0, seg_len, scan, tuple((zero, zero) for _ in range(NQ)), unroll=4)

    for q in range(NQ):
        pwr_ref[q, pl.ds(0, 1), :] = abr[q]
        pwi_ref[q, pl.ds(0, 1), :] = abi[q]
    an_r, an_i = ab_re, ab_im
    n = 1
    while n < seg_len:
        m = min(n, seg_len - n)
        for q in range(NQ):
            pr, pi = _cmul(pwr_ref[q, pl.ds(0, m), :], pwi_ref[q, pl.ds(0, m), :], lq(an_r, q), lq(an_i, q))
            pwr_ref[q, pl.ds(n, m), :] = pr
            pwi_ref[q, pl.ds(n, m), :] = pi
        an_r, an_i = _cmul(an_r, an_i, an_r, an_i)
        n *= 2
    seg_r = [pwr_ref[q, pl.ds(seg_len - 1, 1), :] for q in range(NQ)]
    seg_i = [pwi_ref[q, pl.ds(seg_len - 1, 1), :] for q in range(NQ)]

    x0r, x0i = x0r_ref[0], x0i_ref[0]
    dskip = d_ref[...]
    c_r = [lq(x0r, q) for q in range(NQ)]
    c_i = [lq(x0i, q) for q in range(NQ)]
    for k in range(nseg):
        r0 = k * seg_len
        xr_q, xi_q = [], []
        for q in range(NQ):
            dr, di = _cmul(pwr_ref[q], pwi_ref[q], c_r[q], c_i[q])
            xr_q.append(xr_ref[q, pl.ds(r0, seg_len), :] + dr)
            xi_q.append(xi_ref[q, pl.ds(r0, seg_len), :] + di)
            pr, pi = _cmul(seg_r[q], seg_i[q], c_r[q], c_i[q])
            c_r[q], c_i[q] = pr + ends[q][0][k:k + 1], pi + ends[q][1][k:k + 1]
        if k == nseg - 1:
            for q in range(NQ):
                xr_out_ref[0, :, pl.ds(LANES * q, LANES)] = c_r[q]
                xi_out_ref[0, :, pl.ds(LANES * q, LANES)] = c_i[q]
        if n_scan == L:
            u = u_ref[0, pl.ds(r0, seg_len), :]
            y = (_dot(jnp.concatenate(xr_q, axis=-1), cre_ref[0]) - _dot(jnp.concatenate(xi_q, axis=-1), cim_ref[0])
                 + dskip * u)
            h_ref[0, pl.ds(r0, seg_len), :] = _gelu_tanh(y)
        else:
            for q in range(NQ):
                xr_ref[q, pl.ds(r0, seg_len), :] = xr_q[q]
                xi_ref[q, pl.ds(r0, seg_len), :] = xi_q[q]
    if n_scan < L:
        xr = jnp.concatenate([xr_ref[q] for q in range(NQ)], axis=-1)
        xi = jnp.concatenate([xi_ref[q] for q in range(NQ)], axis=-1)
        y = _dot(xr, cre_ref[0]) - _dot(xi, cim_ref[0]) + dskip * u_ref[0]
        h_ref[0] = _gelu_tanh(y)


def _s5seg_kernel(u_ref, bre_ref, bim_ref, cre_ref, cim_ref, are_ref, aim_ref, ldt_ref, d_ref, x0r_ref, x0i_ref,
                  h_ref, xr_out_ref, xi_out_ref, up_ref, bbr_ref, bbi_ref, xr_ref, xi_ref, pwr_ref, pwi_ref, yp_ref,
                  *, seg_len):
    NS = SUBLANES
    a_re, a_im = are_ref[...], aim_ref[...]
    dt = jnp.exp(ldt_ref[...])
    mag = jnp.exp(a_re * dt)
    ab_re, ab_im = mag * jnp.cos(a_im * dt), mag * jnp.sin(a_im * dt)
    den = a_re * a_re + a_im * a_im
    nr = ab_re - 1.0
    coef_re = (nr * a_re + ab_im * a_im) / den
    coef_im = (ab_im * a_re - nr * a_im) / den

    NQ = S5_BLK_ST // LANES
    lq = lambda v, q: v[:, LANES * q:LANES * (q + 1)]
    rep = lambda v: jnp.broadcast_to(v, (NS, LANES))
    abr = [rep(lq(ab_re, q)) for q in range(NQ)]
    abi = [rep(lq(ab_im, q)) for q in range(NQ)]
    blk = S5_BULK_STEPS
    n_blk = seg_len // blk

    def gather(i, carry):
        up_ref[pl.ds(pl.multiple_of(i * NS, NS), NS), :] = u_ref[0, pl.ds(i, NS, stride=seg_len), :]
        return carry

    lax.fori_loop(0, seg_len, gather, 0, unroll=8)

    for rb in range(n_blk):
        u = up_ref[pl.ds(rb * blk * NS, blk * NS), :]
        bu_re, bu_im = _dot(u, bre_ref[0]), _dot(u, bim_ref[0])
        bb_re = coef_re * bu_re - coef_im * bu_im
        bb_im = coef_re * bu_im + coef_im * bu_re
        for q in range(NQ):
            bbr_ref[q, pl.ds(rb * blk, blk)] = lq(bb_re, q).reshape(blk, NS, LANES)
            bbi_ref[q, pl.ds(rb * blk, blk)] = lq(bb_im, q).reshape(blk, NS, LANES)

    def scan(i, carry):
        out = []
        for q in range(NQ):
            xr, xi = carry[q]
            pr, pi = _cmul(abr[q], abi[q], xr, xi)
            xr, xi = pr + bbr_ref[q, i], pi + bbi_ref[q, i]
            xr_ref[q, i] = xr
            xi_ref[q, i] = xi
            out.append((xr, xi))
        return tuple(out)

    zero = jnp.zeros((NS, LANES), jnp.float32)
    ends = lax.fori_loop(0, seg_len, scan, tuple((zero, zero) for _ in range(NQ)), unroll=8)

    for q in range(NQ):
        pwr_ref[q, 0] = abr[q]
        pwi_ref[q, 0] = abi[q]
    an = [(abr[q], abi[q]) for q in range(NQ)]
    n = 1
    while n < seg_len:
        m = min(n, seg_len - n)
        for q in range(NQ):
            pr, pi = _cmul(pwr_ref[q, pl.ds(0, m)], pwi_ref[q, pl.ds(0, m)], an[q][0], an[q][1])
            pwr_ref[q, pl.ds(n, m)] = pr
            pwi_ref[q, pl.ds(n, m)] = pi
            an[q] = _cmul(an[q][0], an[q][1], an[q][0], an[q][1])
        n *= 2

    x0r, x0i = x0r_ref[0], x0i_ref[0]
    cs = []
    for q in range(NQ):
        seg_r, seg_i = pwr_ref[q, seg_len - 1][:1], pwi_ref[q, seg_len - 1][:1]
        cr, ci = [lq(x0r, q)], [lq(x0i, q)]
        for k in range(NS):
            pr, pi = _cmul(seg_r, seg_i, cr[k], ci[k])
            cr.append(pr + ends[q][0][k:k + 1])
            ci.append(pi + ends[q][1][k:k + 1])
        xr_out_ref[0, :, pl.ds(LANES * q, LANES)] = cr[NS]
        xi_out_ref[0, :, pl.ds(LANES * q, LANES)] = ci[NS]
        cs.append((jnp.concatenate(cr[:NS], axis=0), jnp.concatenate(ci[:NS], axis=0)))

    dskip = d_ref[...]
    for rb in range(n_blk):
        xr_q, xi_q = [], []
        for q in range(NQ):
            dr, di = _cmul(pwr_ref[q, pl.ds(rb * blk, blk)], pwi_ref[q, pl.ds(rb * blk, blk)], cs[q][0], cs[q][1])
            xr_q.append((xr_ref[q, pl.ds(rb * blk, blk)] + dr).reshape(blk * NS, LANES))
            xi_q.append((xi_ref[q, pl.ds(rb * blk, blk)] + di).reshape(blk * NS, LANES))
        u = up_ref[pl.ds(rb * blk * NS, blk * NS), :]
        y = (_dot(jnp.concatenate(xr_q, axis=-1), cre_ref[0]) - _dot(jnp.concatenate(xi_q, axis=-1), cim_ref[0])
             + dskip * u)
        yp_ref[pl.ds(rb * blk * NS, blk * NS), :] = _gelu_tanh(y)

    for k in range(NS):
        h_ref[0, pl.ds(k * seg_len, seg_len), :] = yp_ref[pl.ds(k, seg_len, stride=NS), :]


def _s5_call(z, bblk_re, bblk_im, cblk_re, cblk_im, a_re, a_im, log_dt_exp, d_flat, x0_re, x0_im, *, l_valid):
    B, L, _ = z.shape
    nb = bblk_re.shape[0]
    NQ = S5_BLK_ST // LANES
    f32 = jnp.float32
    if l_valid == L and L % (SUBLANES * S5_BULK_STEPS) == 0:
        seg_len = L // SUBLANES
        kern = functools.partial(_s5seg_kernel, seg_len=seg_len)
        tile = pltpu.VMEM((NQ, seg_len, SUBLANES, LANES), f32)
        scratch = [pltpu.VMEM((L, LANES), f32)] + [tile] * 6 + [pltpu.VMEM((L, LANES), f32)]
    else:
        kern = functools.partial(_s5_kernel, nseg=1, seg_len=l_valid)
        scratch = [pltpu.VMEM((NQ, L, LANES), f32)] * 4 + [pltpu.VMEM((NQ, l_valid, LANES), f32)] * 2
    vec = lambda w: pl.BlockSpec((1, w), lambda b, j: (0, j))
    st = pl.BlockSpec((1, 1, S5_BLK_ST), lambda b, j: (b, 0, j))
    return pl.pallas_call(
        kern,
        grid=(B, nb),
        in_specs=[pl.BlockSpec((1, L, S5_BLK_CH), lambda b, j: (b, 0, OFF_S5_U // S5_BLK_CH + j)),
                  pl.BlockSpec((1, S5_BLK_CH, S5_BLK_ST), lambda b, j: (j, 0, 0)),
                  pl.BlockSpec((1, S5_BLK_CH, S5_BLK_ST), lambda b, j: (j, 0, 0)),
                  pl.BlockSpec((1, S5_BLK_ST, S5_BLK_CH), lambda b, j: (j, 0, 0)),
                  pl.BlockSpec((1, S5_BLK_ST, S5_BLK_CH), lambda b, j: (j, 0, 0)),
                  vec(S5_BLK_ST), vec(S5_BLK_ST), vec(S5_BLK_ST), vec(S5_BLK_CH), st, st],
        out_specs=[pl.BlockSpec((1, L, S5_BLK_CH), lambda b, j: (b, 0, j)), st, st],
        out_shape=[jax.ShapeDtypeStruct((B, L, nb * S5_BLK_CH), jnp.float32),
                   jax.ShapeDtypeStruct((B, 1, nb * S5_BLK_ST), jnp.float32),
                   jax.ShapeDtypeStruct((B, 1, nb * S5_BLK_ST), jnp.float32)],
        scratch_shapes=scratch,
        compiler_params=_params(2),
        name="s5",
    )(z, bblk_re, bblk_im, cblk_re, cblk_im, a_re, a_im, log_dt_exp, d_flat, x0_re, x0_im)


def _s5_block_diag(b_re, b_im, c_re, c_im):
    G, P, Cc = b_re.shape
    nb = G // 8
    same = jnp.eye(8, dtype=bool)

    def bblk(b):
        t = jnp.transpose(b.reshape(nb, 8, P, Cc), (0, 1, 3, 2))[:, :, :, None, :]
        t = jnp.where(same[None, :, None, :, None], t, 0.0)
        return t.reshape(nb, 8 * Cc, 8 * P).astype(jnp.bfloat16)

    def cblk(c):
        t = jnp.transpose(c.reshape(nb, 8, Cc, P), (0, 1, 3, 2))[:, :, :, None, :]
        t = jnp.where(same[None, :, None, :, None], t, 0.0)
        return t.reshape(nb, 8 * P, 8 * Cc).astype(jnp.bfloat16)

    return bblk(b_re), bblk(b_im), cblk(c_re), cblk(c_im)


def _glu_kernel(h_ref, w_ref, b_ref, g_ref, o_ref):
    h = h_ref[...]
    t = _dot(h, w_ref[...]) + b_ref[...]
    o_ref[...] = (h * _sigmoid(t) * _silu(g_ref[...])).astype(o_ref.dtype)


def _glu_call(h2d, glu_w_bf16, glu_b, z2d, *, tm):
    M, W = h2d.shape
    return pl.pallas_call(
        _glu_kernel,
        grid=(M // tm,),
        in_specs=[pl.BlockSpec((tm, W), lambda i: (i, 0)),
                  pl.BlockSpec((W, W), lambda i: (0, 0)),
                  pl.BlockSpec((1, W), lambda i: (0, 0)),
                  pl.BlockSpec((tm, W), lambda i: (i, OFF_S5_G // W))],
        out_specs=pl.BlockSpec((tm, W), lambda i: (i, 0)),
        out_shape=jax.ShapeDtypeStruct((M, W), jnp.bfloat16),
        compiler_params=_params(1),
        name="glu",
    )(h2d, glu_w_bf16, glu_b.reshape(1, W), z2d)


def _kvprep_kernel(kv_ref, sm_ref, knw_ref, k_ref, v_ref, ik_ref, *rest, transposed):
    kvw = KV_HEADS * ATT_D
    kv = kv_ref[0]
    knw = knw_ref[...]
    ks = []
    for h in range(KV_HEADS):
        kh = kv[:, ATT_D * h:ATT_D * (h + 1)]
        ms = jnp.mean(kh * kh, axis=-1, keepdims=True)
        ks.append(kh * lax.rsqrt(ms + EPS) * knw)
    k = jnp.concatenate(ks, axis=-1)
    v = kv[:, kvw:]
    sm = sm_ref[0]
    k_ref[0] = k
    v_ref[0] = v
    ik_ref[0] = sm[:, :IDX_D]
    if transposed:
        kbf_ref, vt_ref, kia_ref, kib_ref = rest
        kbf_ref[0] = k.astype(jnp.bfloat16)
        vt_ref[0] = v.T.astype(jnp.bfloat16)
        lane = lax.broadcasted_iota(jnp.int32, sm.shape, 1)
        kia_ref[0] = jnp.where(lane < IDX_D, sm, 0.0).astype(jnp.bfloat16)
        kib_ref[0] = jnp.where(lane >= IDX_D, pltpu.roll(sm, IDX_D, axis=1), 0.0).astype(jnp.bfloat16)


def _kvprep_call(z, k_norm_w, *, tr, transposed):
    B, L, _ = z.shape
    kvw = KV_HEADS * ATT_D
    f32, bf = jnp.float32, jnp.bfloat16
    rowblk = lambda w: pl.BlockSpec((1, tr, w), lambda b, r: (b, r, 0))
    out_specs = [rowblk(kvw), rowblk(kvw), rowblk(IDX_D)]
    out_shape = [jax.ShapeDtypeStruct((B, L, kvw), f32), jax.ShapeDtypeStruct((B, L, kvw), f32),
                 jax.ShapeDtypeStruct((B, L, IDX_D), f32)]
    if transposed:
        out_specs += [rowblk(kvw), pl.BlockSpec((1, kvw, tr), lambda b, r: (b, 0, r)), rowblk(LANES), rowblk(LANES)]
        out_shape += [jax.ShapeDtypeStruct((B, L, kvw), bf), jax.ShapeDtypeStruct((B, kvw, L), bf),
                      jax.ShapeDtypeStruct((B, L, LANES), bf), jax.ShapeDtypeStruct((B, L, LANES), bf)]
    return pl.pallas_call(
        functools.partial(_kvprep_kernel, transposed=transposed),
        grid=(B, L // tr),
        in_specs=[pl.BlockSpec((1, tr, 2 * kvw), lambda b, r: (b, r, OFF_AT_KV // (2 * kvw))),
                  pl.BlockSpec((1, tr, LANES), lambda b, r: (b, r, OFF_SMALL // LANES)),
                  pl.BlockSpec((1, ATT_D), lambda b, r: (0, 0))],
        out_specs=out_specs, out_shape=out_shape,
        compiler_params=_params(2),
        name="kvprep",
    )(z, z, k_norm_w.reshape(1, ATT_D))


def _sortable_key(score):
    bits = lax.bitcast_convert_type(score, jnp.int32)
    return jnp.where(bits < 0, bits ^ 0x7FFFFFFF, bits)


def _rows_reduce(x, op):
    n = x.shape[0]
    slab = 8 * SUBLANES
    if n > slab and n % slab == 0:
        x = op(x.reshape(n // slab, slab, x.shape[1]), axis=0)
    return op(x, axis=0, keepdims=True)


def _lanes_sum(x):
    parts = [x[:, LANES * c:LANES * (c + 1)] for c in range(x.shape[1] // LANES)]
    while len(parts) > 1:
        parts = [a + b for a, b in zip(parts[::2], parts[1::2])] + ([parts[-1]] if len(parts) % 2 else [])
    return jnp.sum(parts[0], axis=1, keepdims=True)


def _kth_largest_key(key, k, axis):
    shape = list(key.shape)
    shape[axis] = 1

    def it(n, tu):
        cand_u = tu | jnp.left_shift(jnp.int32(1), 31 - n)
        cand_s = cand_u ^ jnp.int32(INT_MIN)
        ones = jnp.where(key >= cand_s, 1.0, 0.0)
        cnt = _rows_reduce(ones, jnp.sum) if axis == 0 else _lanes_sum(ones)
        return jnp.where(cnt >= k, cand_u, tu)

    tu = lax.fori_loop(0, 32, it, jnp.zeros(shape, jnp.int32))
    return tu ^ jnp.int32(INT_MIN)


def _t5_bucket(dist):
    exact = REL_BUCKETS // 2
    d = dist.astype(jnp.float32)
    large = exact + jnp.log(jnp.maximum(d, 1.0) / exact) / math.log(REL_MAX_DIST / exact) * (REL_BUCKETS - exact)
    large = jnp.minimum(jnp.maximum(large, 0.0).astype(jnp.int32), REL_BUCKETS - 1)
    return jnp.where(dist < exact, dist, large)


def _bias_lookup(rel_bias, bucket):
    onehot = (bucket[..., None] == jnp.arange(REL_BUCKETS, dtype=jnp.int32)).astype(jnp.float32)
    return jnp.einsum('...k,kh->...h', onehot, rel_bias.astype(jnp.float32), precision=lax.Precision.HIGHEST)


def _pattn_kernel(zq_ref, zg_ref, ziq_ref, zsm_ref, kbf_ref, vt_ref, kia_ref, kib_ref, qnw_ref, bias_ref, o_ref,
                  mask_ref, *, topk):
    i = pl.program_id(1)
    nqb = kbf_ref.shape[1] // Q_BLK
    qnw = qnw_ref[...]
    scale = ATT_D ** -0.5

    def body(S):
        nkb = S // Q_BLK
        wt = zsm_ref[0].T * (1.0 / 32.0)
        kia, kib = kia_ref[0, pl.ds(0, S), :], kib_ref[0, pl.ds(0, S), :]
        score = jnp.zeros((S, Q_BLK), jnp.float32)
        for p in range(IDX_HEADS // 2):
            qi = ziq_ref[0, :, pl.ds(LANES * p, LANES)].astype(jnp.bfloat16)
            da = lax.dot_general(kia, qi, _NT, preferred_element_type=jnp.float32)
            db = lax.dot_general(kib, qi, _NT, preferred_element_type=jnp.float32)
            r = SMALL_IXW + 2 * p
            score = score + jnp.maximum(da, 0.0) * wt[r:r + 1] + jnp.maximum(db, 0.0) * wt[r + 1:r + 2]
        s_pos = lax.broadcasted_iota(jnp.int32, (S, Q_BLK), 0)
        t_pos = lax.broadcasted_iota(jnp.int32, (S, Q_BLK), 1) + i * Q_BLK
        adm = s_pos <= t_pos
        key = jnp.where(adm, _sortable_key(score), jnp.int32(INT_MIN))
        thr = _kth_largest_key(key, topk, 0)
        ge = key >= thr
        cnt_ge = _rows_reduce(jnp.where(ge, 1.0, 0.0), jnp.sum)
        tied = (cnt_ge > topk) & (thr > jnp.int32(INT_MIN))
        any_tied = jnp.max(jnp.where(tied, 1.0, 0.0)) > 0.0

        @pl.when(jnp.logical_not(any_tied))
        def _():
            mask_ref[pl.ds(0, S), :] = jnp.where(ge & adm, 0.0, MASK_VALUE)

        @pl.when(any_tied)
        def _():
            gt = key > thr
            eq = key == thr
            need = topk - _rows_reduce(jnp.where(gt, 1.0, 0.0), jnp.sum)
            nbits = S.bit_length()

            def it(n, j_sel):
                cand = j_sel | jnp.left_shift(jnp.int32(1), nbits - 1 - n)
                c = _rows_reduce(jnp.where(eq & (s_pos < cand), 1.0, 0.0), jnp.sum)
                return jnp.where(c <= need, cand, j_sel)

            j_sel = lax.fori_loop(0, nbits, it, jnp.zeros((1, Q_BLK), jnp.int32))
            sel = (gt | (eq & (s_pos < j_sel))) & adm
            mask_ref[pl.ds(0, S), :] = jnp.where(sel, 0.0, MASK_VALUE)

        for h in range(ATT_HEADS):
            kvh = h // (ATT_HEADS // KV_HEADS)
            q = zq_ref[0, :, pl.ds(ATT_D * h, ATT_D)]
            ms = jnp.mean(q * q, axis=-1, keepdims=True)
            qn = (q * lax.rsqrt(ms + EPS) * qnw).astype(jnp.bfloat16)
            logit = lax.dot_general(kbf_ref[0, pl.ds(0, S), pl.ds(ATT_D * kvh, ATT_D)], qn, _NT,
                                    preferred_element_type=jnp.float32) * scale
            bias = jnp.concatenate([bias_ref[jnp.clip(i - j, 0, N_BIAS_TILES - 1), h] for j in range(nkb)], axis=0)
            x = logit + bias + mask_ref[pl.ds(0, S), :]
            m = _rows_reduce(x, jnp.max)
            pexp = jnp.exp(x - m)
            l = _rows_reduce(pexp, jnp.sum)
            ot = jnp.dot(vt_ref[0, pl.ds(ATT_D * kvh, ATT_D), pl.ds(0, S)], pexp.astype(jnp.bfloat16),
                         preferred_element_type=jnp.float32)
            out = (ot / l).T
            g = zg_ref[0, :, pl.ds(ATT_D * h, ATT_D)]
            o_ref[0, :, pl.ds(ATT_D * h, ATT_D)] = (out * _silu(g)).astype(o_ref.dtype)

    nv = min(N_CAUSAL_VARIANTS, nqb)
    per = nqb // nv
    for c in range(nv):
        pl.when(i // per == c)(functools.partial(body, (c + 1) * per * Q_BLK))


def _pattn_call(z, kbf, vt, kia, kib, q_norm_w, bias_tiles, *, topk):
    B, L, _ = z.shape
    W = ATT_HEADS * ATT_D
    kvw = KV_HEADS * ATT_D
    zblk = lambda off, w: pl.BlockSpec((1, Q_BLK, w), lambda b, i: (b, i, off // w))
    full = lambda shape: pl.BlockSpec((1,) + shape, lambda b, i: (b, 0, 0))
    return pl.pallas_call(
        functools.partial(_pattn_kernel, topk=topk),
        grid=(B, L // Q_BLK),
        in_specs=[zblk(OFF_AT_Q, W), zblk(OFF_AT_G, W), zblk(OFF_IX_Q, W), zblk(OFF_SMALL, LANES),
                  full((L, kvw)), full((kvw, L)), full((L, LANES)), full((L, LANES)),
                  pl.BlockSpec((1, ATT_D), lambda b, i: (0, 0)),
                  pl.BlockSpec((N_BIAS_TILES, ATT_HEADS, Q_BLK, Q_BLK), lambda b, i: (0, 0, 0, 0))],
        out_specs=pl.BlockSpec((1, Q_BLK, W), lambda b, i: (b, i, 0)),
        out_shape=jax.ShapeDtypeStruct((B, L, W), jnp.bfloat16),
        scratch_shapes=[pltpu.VMEM((L, Q_BLK), jnp.float32)],
        compiler_params=_params(2),
        name="pattn",
    )(z, z, z, z, kbf, vt, kia, kib, q_norm_w.reshape(1, ATT_D), bias_tiles)


def _prompt_bias_tiles(rel_bias):
    o = jnp.arange(N_BIAS_TILES, dtype=jnp.int32)[:, None, None]
    s = jnp.arange(Q_BLK, dtype=jnp.int32)[None, :, None]
    t = jnp.arange(Q_BLK, dtype=jnp.int32)[None, None, :]
    tiles = _bias_lookup(rel_bias, _t5_bucket(jnp.maximum(Q_BLK * o + t - s, 0)))
    return jnp.moveaxis(tiles, -1, 1)


def _sattn_score_kernel(pt_ref, qi_ref, w_ref, iknew_ref, *rest):
    pages, s_ref = rest[:NPG], rest[NPG]
    pc = pl.program_id(1)
    last = pl.num_programs(1) - 1
    qi = qi_ref[0]
    w = w_ref[0]

    def head_sum(d):
        r = jnp.maximum(d, 0.0) * w
        return jnp.sum(r.reshape(T_PAD, IDX_HEADS, r.shape[-1]), axis=1)

    @pl.when(pc < last)
    def _():
        for i in range(NPG):
            d = jnp.dot(qi, pages[i][0, 0].astype(jnp.bfloat16), preferred_element_type=jnp.float32)
            s_ref[0, :, pl.ds(PAGE * i, PAGE)] = head_sum(d)

    @pl.when(pc == last)
    def _():
        s_ref[0] = jnp.full((T_PAD, CHUNK_KEYS), MASK_VALUE, jnp.float32)
        d = lax.dot_general(qi, iknew_ref[0].astype(jnp.bfloat16), _NT, preferred_element_type=jnp.float32)
        s_ref[0, :, pl.ds(0, PAGE)] = head_sum(d)


def _sattn_score_call(page_table, qi_rows, w_col, ik_new, cache_ik_t, layer):
    B, n_pages = page_table.shape
    n_chunks = n_pages // NPG
    page_spec = lambda i: pl.BlockSpec(
        (1, 1, IDX_D, PAGE), lambda b, pc, pt: (layer, pt[b, jnp.minimum(pc * NPG + i, n_pages - 1)], 0, 0))
    per_b = lambda shape: pl.BlockSpec((1,) + shape, lambda b, pc, pt: (b, 0, 0))
    grid_spec = pltpu.PrefetchScalarGridSpec(
        num_scalar_prefetch=1, grid=(B, n_chunks + 1),
        in_specs=[per_b((T_PAD * IDX_HEADS, IDX_D)), per_b((T_PAD * IDX_HEADS, 1)), per_b((PAGE, IDX_D))]
        + [page_spec(i) for i in range(NPG)],
        out_specs=pl.BlockSpec((1, T_PAD, CHUNK_KEYS), lambda b, pc, pt: (b, 0, pc)))
    return pl.pallas_call(
        _sattn_score_kernel,
        grid_spec=grid_spec,
        out_shape=jax.ShapeDtypeStruct((B, T_PAD, (n_chunks + 1) * CHUNK_KEYS), jnp.float32),
        compiler_params=_params(2),
        name="sattn_score",
    )(page_table, qi_rows, w_col, ik_new, *([cache_ik_t] * NPG))


def _sattn_kernel(pt_ref, s_ref, q_ref, g_ref, qnw_ref, bias_ref, knew_ref, vnew_ref, *rest, topk, n_new, past):
    kp, vp = rest[:NPG], rest[NPG:2 * NPG]
    o_ref, thr_ref, tie_ref, qn_ref, m_ref, l_ref, acc_ref = rest[2 * NPG:]
    pc = pl.program_id(1)
    last = pl.num_programs(1) - 1
    G = ATT_HEADS // KV_HEADS

    def admissible(col0, width):
        pos = lax.broadcasted_iota(jnp.int32, (T_PAD, width), 1) + col0
        t = lax.broadcasted_iota(jnp.int32, (T_PAD, width), 0)
        return (pos < past) | ((pos - past <= t) & (pos - past < n_new))

    @pl.when(pc == 0)
    def _():
        sc = s_ref[0]
        n_all = sc.shape[-1]
        adm0 = admissible(0, n_all)
        key = jnp.where(adm0, _sortable_key(sc), jnp.int32(INT_MIN))
        thr = _kth_largest_key(key, topk, 1)
        thr_ref[...] = jnp.broadcast_to(thr, thr_ref.shape)
        tie_ref[...] = jnp.full(tie_ref.shape, n_all, jnp.int32)
        cnt_ge = _lanes_sum(jnp.where(key >= thr, 1.0, 0.0))
        tied = (cnt_ge > topk) & (thr > jnp.int32(INT_MIN))

        @pl.when(jnp.max(jnp.where(tied, 1.0, 0.0)) > 0.0)
        def _():
            eq = key == thr
            pos = lax.broadcasted_iota(jnp.int32, key.shape, 1)
            need = topk - _lanes_sum(jnp.where(key > thr, 1.0, 0.0))
            nbits = n_all.bit_length()

            def it(n, j_sel):
                cand = j_sel | jnp.left_shift(jnp.int32(1), nbits - 1 - n)
                c = _lanes_sum(jnp.where(eq & (pos < cand), 1.0, 0.0))
                return jnp.where(c <= need, cand, j_sel)

            j_sel = lax.fori_loop(0, nbits, it, jnp.zeros((T_PAD, 1), jnp.int32))
            tie_ref[...] = jnp.broadcast_to(j_sel, tie_ref.shape)

        q = q_ref[0]
        ms = jnp.mean(q * q, axis=-1, keepdims=True)
        qn_ref[...] = (q * lax.rsqrt(ms + EPS) * qnw_ref[...]).astype(jnp.bfloat16)
        m_ref[...] = jnp.full(m_ref.shape, MASK_VALUE, jnp.float32)
        l_ref[...] = jnp.zeros(l_ref.shape, jnp.float32)
        acc_ref[...] = jnp.zeros(acc_ref.shape, jnp.float32)

    def attend(get_k, get_v, width):
        col0 = pl.multiple_of(pc * CHUNK_KEYS, CHUNK_KEYS)
        sc = s_ref[0, :, pl.ds(col0, width)]
        adm = admissible(col0, width)
        key = jnp.where(adm, _sortable_key(sc), jnp.int32(INT_MIN))
        pos = lax.broadcasted_iota(jnp.int32, (T_PAD, width), 1) + col0
        thr = thr_ref[:, :1]
        sel8 = ((key > thr) | ((key == thr) & (pos < tie_ref[:, :1]))) & adm
        sel = jnp.concatenate([sel8] * G, axis=0)
        for kv in range(KV_HEADS):
            kk = get_k(kv).astype(jnp.bfloat16)
            x = lax.dot_general(qn_ref[kv], kk, _NT, preferred_element_type=jnp.float32) * (ATT_D ** -0.5)
            x = jnp.where(sel, x + bias_ref[kv, :, pl.ds(0, width)], MASK_VALUE)
            m_old = m_ref[kv]
            m_new = jnp.maximum(m_old, jnp.max(x, axis=-1, keepdims=True))
            p = jnp.where(sel, jnp.exp(x - m_new), 0.0)
            alpha = jnp.exp(m_old - m_new)
            l_ref[kv] = alpha * l_ref[kv] + jnp.sum(p, axis=-1, keepdims=True)
            acc_ref[kv] = alpha * acc_ref[kv] + jnp.dot(
                p.astype(jnp.bfloat16), get_v(kv).astype(jnp.bfloat16),
                preferred_element_type=jnp.float32)
            m_ref[kv] = m_new

    @pl.when(pc < last)
    def _():
        attend(lambda kv: jnp.concatenate([kp[i][0, 0, pl.ds(kv, PAGE, stride=KV_HEADS), :] for i in range(NPG)], axis=0),
               lambda kv: jnp.concatenate([vp[i][0, 0, pl.ds(kv, PAGE, stride=KV_HEADS), :] for i in range(NPG)], axis=0),
               CHUNK_KEYS)

    @pl.when(pc == last)
    def _():
        attend(lambda kv: knew_ref[0, :, pl.ds(ATT_D * kv, ATT_D)],
               lambda kv: vnew_ref[0, :, pl.ds(ATT_D * kv, ATT_D)], PAGE)
        o_ref[0] = acc_ref[...] / l_ref[...] * _silu(g_ref[0])


def _sattn_call(page_table, scores, q_rows, g_rows, q_norm_w, bias, k_new, v_new, cache_k, cache_v, layer,
                *, topk, n_new):
    B, n_pages = page_table.shape
    n_chunks = n_pages // NPG
    kvw = KV_HEADS * ATT_D
    rows = (ATT_HEADS // KV_HEADS) * T_PAD
    n_keys = (n_chunks + 1) * CHUNK_KEYS
    page_spec = lambda i: pl.BlockSpec(
        (1, 1, PAGE * KV_HEADS, ATT_D),
        lambda b, pc, pt: (layer, pt[b, jnp.minimum(pc * NPG + i, n_pages - 1)], 0, 0))
    per_b = lambda shape: pl.BlockSpec((1,) + shape, lambda b, pc, pt: (b,) + (0,) * len(shape))
    grid_spec = pltpu.PrefetchScalarGridSpec(
        num_scalar_prefetch=1, grid=(B, n_chunks + 1),
        in_specs=[per_b((T_PAD, n_keys)), per_b((KV_HEADS, rows, ATT_D)), per_b((KV_HEADS, rows, ATT_D)),
                  pl.BlockSpec((1, ATT_D), lambda b, pc, pt: (0, 0)),
                  pl.BlockSpec((KV_HEADS, rows, CHUNK_KEYS), lambda b, pc, pt: (0, 0, pc)),
                  per_b((PAGE, kvw)), per_b((PAGE, kvw))]
        + [page_spec(i) for i in range(NPG)] + [page_spec(i) for i in range(NPG)],
        out_specs=per_b((KV_HEADS, rows, ATT_D)),
        scratch_shapes=[pltpu.VMEM((T_PAD, LANES), jnp.int32), pltpu.VMEM((T_PAD, LANES), jnp.int32),
                        pltpu.VMEM((KV_HEADS, rows, ATT_D), jnp.bfloat16),
                        pltpu.VMEM((KV_HEADS, rows, 1), jnp.float32), pltpu.VMEM((KV_HEADS, rows, 1), jnp.float32),
                        pltpu.VMEM((KV_HEADS, rows, ATT_D), jnp.float32)])
    return pl.pallas_call(
        functools.partial(_sattn_kernel, topk=topk, n_new=n_new, past=n_pages * PAGE),
        grid_spec=grid_spec,
        out_shape=jax.ShapeDtypeStruct((B, KV_HEADS, rows, ATT_D), jnp.float32),
        compiler_params=_params(2),
        name="sattn",
    )(page_table, scores, q_rows, g_rows, q_norm_w.reshape(1, ATT_D), bias, k_new, v_new,
      *([cache_k] * NPG), *([cache_v] * NPG))


def _sample_bias(rel_bias, past, n_keys):
    G = ATT_HEADS // KV_HEADS
    t = jnp.arange(T_PAD, dtype=jnp.int32)[:, None]
    pos = jnp.arange(n_keys, dtype=jnp.int32)[None, :]
    tab = _bias_lookup(rel_bias, _t5_bucket(jnp.maximum(past + t - pos, 0)))
    return jnp.moveaxis(tab, -1, 0).reshape(KV_HEADS, G * T_PAD, n_keys)


def _to_head_rows(a):
    B = a.shape[0]
    G = ATT_HEADS // KV_HEADS
    a = a.reshape(B, T_PAD, KV_HEADS, G, ATT_D)
    return jnp.transpose(a, (0, 2, 3, 1, 4)).reshape(B, KV_HEADS, G * T_PAD, ATT_D)


def _from_head_rows(a):
    B = a.shape[0]
    G = ATT_HEADS // KV_HEADS
    a = a.reshape(B, KV_HEADS, G, T_PAD, ATT_D)
    return jnp.transpose(a, (0, 3, 1, 2, 4)).reshape(B, T_PAD, ATT_HEADS * ATT_D)


def _layer(x, lp, states, attend, *, l_valid, tm, chunk_hg, chunk_ssd):
    B, L, _ = x.shape
    hg_s0, ssm_s0, conv_s0, s5_re0, s5_im0 = states
    x2d = x.reshape(B * L, D_MODEL)
    z2d = _inproj(x2d, lp['norm_w'], lp['w_in'], tm)
    z = z2d.reshape(B, L, D_IN_PAD)

    o_hg, hg_s = _hgrn_call(z, lp['lb'], lp['hg_norm_w'], hg_s0, chunk=chunk_hg, l_valid=l_valid)

    conv8 = jnp.pad(conv_s0, ((0, 0), (SUBLANES - (SSM_CONV - 1), 0), (0, 0)))
    y_ssm, ssm_s, cx, cb, cc = _ssd_call(z, lp['conv_w'], lp['conv_b'], conv8, lp['dtb_pad'], lp['alog_pad'],
                                         lp['dskip_pad'], lp['ssm_norm_w'], ssm_s0, chunk=chunk_ssd, l_valid=l_valid)
    tail = SUBLANES - (SSM_CONV - 1)
    conv_s = jnp.concatenate([cx[:, tail:], cb[:, tail:], cc[:, tail:]], axis=-1)

    h5, s5_re, s5_im = _s5_call(z, *lp['s5_blocks'], lp['a_re'], lp['a_im'], lp['log_dt'], lp['s5_d'],
                                s5_re0.reshape(B, 1, -1), s5_im0.reshape(B, 1, -1), l_valid=l_valid)
    o5 = _glu_call(h5.reshape(B * L, BRANCH), lp['glu_w'], lp['glu_b'], z2d, tm=tm)

    o_att, k, v, ik = attend(z)

    y = _outproj((o_hg.reshape(B * L, BRANCH), y_ssm.reshape(B * L, BRANCH), o5, o_att.reshape(B * L, BRANCH)),
                 lp['w_out'], lp['layer'], x2d, tm)
    st = (k[:, :l_valid].reshape(B, l_valid, KV_HEADS, ATT_D), v[:, :l_valid].reshape(B, l_valid, KV_HEADS, ATT_D),
          ik[:, :l_valid], hg_s, ssm_s, conv_s,
          s5_re.reshape(B, S5_GROUPS, S5_STATE), s5_im.reshape(B, S5_GROUPS, S5_STATE))
    return y.reshape(B, L, D_MODEL), st


def kernel(x_prompt, x_sample, cache_k, cache_v, cache_idx_k, state_hgrn, state_ssm, state_conv, state_s5_re, state_s5_im, page_table, norm_w, w_in, w_out, hg_lb_logits, hg_norm_w, ssm_conv_w, ssm_conv_b, ssm_dt_bias, ssm_a_log, ssm_d, ssm_norm_w, s5_a_re, s5_a_im, s5_log_dt, s5_b_re, s5_b_im, s5_c_re, s5_c_im, s5_d, s5_glu_w, s5_glu_b, att_q_norm, att_k_norm, rel_bias):
    f32 = jnp.float32
    bp, lp_len, _ = x_prompt.shape
    bs, ls, _ = x_sample.shape
    n_pages = page_table.shape[1]
    past = n_pages * PAGE
    n_keys = (n_pages // NPG + 1) * CHUNK_KEYS

    sm = jax.nn.softmax(hg_lb_logits.astype(f32), axis=0)
    lower = jnp.cumsum(sm, axis=0) - sm[0]

    w_out_bf16 = w_out.astype(jnp.bfloat16)
    w_in_t = jnp.swapaxes(w_in, 1, 2)
    cache_ik_t = jnp.swapaxes(cache_idx_k, 2, 3)
    page_rows = lambda c: c.reshape(c.shape[0], c.shape[1], PAGE * KV_HEADS, ATT_D)
    cache_k2, cache_v2 = page_rows(cache_k), page_rows(cache_v)
    bias_tiles = _prompt_bias_tiles(rel_bias)
    bias_sample = _sample_bias(rel_bias, past, n_keys)
    topk_p = min(TOPK_MAX, lp_len // 4)
    topk_s = min(TOPK_MAX, (past + ls) // 4)

    def head_lanes(v):
        return jnp.zeros((1, LANES), f32).at[0, SMALL_DT:SMALL_DT + SSM_HEADS].set(v)

    xs_pad = jnp.pad(x_sample, ((0, 0), (0, SAMPLE_ROWS - ls), (0, 0)))
    pad_page = lambda a: jnp.pad(a, ((0, 0), (0, PAGE - a.shape[1]), (0, 0)))

    yp, ys = x_prompt, xs_pad
    new_p = [[] for _ in range(8)]
    new_s = [[] for _ in range(8)]
    for l in range(DEPTH):
        lp = {'norm_w': norm_w[l], 'w_in': _pack_w_in(w_in_t, l), 'w_out': w_out_bf16, 'layer': l,
              'lb': lower[l], 'hg_norm_w': hg_norm_w[l],
              'conv_w': ssm_conv_w[l], 'conv_b': ssm_conv_b[l], 'dtb_pad': head_lanes(ssm_dt_bias[l]),
              'alog_pad': head_lanes(ssm_a_log[l]), 'dskip_pad': head_lanes(ssm_d[l]), 'ssm_norm_w': ssm_norm_w[l],
              's5_blocks': _s5_block_diag(s5_b_re[l], s5_b_im[l], s5_c_re[l], s5_c_im[l]),
              'a_re': s5_a_re[l].reshape(1, -1), 'a_im': s5_a_im[l].reshape(1, -1),
              'log_dt': jnp.repeat(s5_log_dt[l], S5_STATE).reshape(1, -1), 's5_d': s5_d[l].reshape(1, -1),
              'glu_w': s5_glu_w[l].astype(jnp.bfloat16), 'glu_b': s5_glu_b[l]}

        def attend_prompt(z):
            k, v, ik, kbf, vt, kia, kib = _kvprep_call(z, att_k_norm[l], tr=256, transposed=True)
            o = _pattn_call(z, kbf, vt, kia, kib, att_q_norm[l], bias_tiles, topk=topk_p)
            return o, k, v, ik

        def attend_sample(z):
            k, v, ik = _kvprep_call(z, att_k_norm[l], tr=SAMPLE_ROWS, transposed=False)
            z8 = z[:, :T_PAD]
            qi_rows = z8[..., OFF_IX_Q:OFF_IX_Q + IDX_HEADS * IDX_D].reshape(bs, T_PAD * IDX_HEADS, IDX_D)
            w_col = z8[..., OFF_SMALL + SMALL_IXW:OFF_SMALL + SMALL_IXW + IDX_HEADS] * (1.0 / 32.0)
            scores = _sattn_score_call(page_table, qi_rows.astype(jnp.bfloat16),
                                       w_col.reshape(bs, T_PAD * IDX_HEADS, 1), pad_page(ik), cache_ik_t, l)
            o = _sattn_call(page_table, scores, _to_head_rows(z8[..., OFF_AT_Q:OFF_AT_Q + BRANCH]),
                            _to_head_rows(z8[..., OFF_AT_G:OFF_AT_G + BRANCH]), att_q_norm[l], bias_sample,
                            pad_page(k), pad_page(v), cache_k2, cache_v2, l, topk=topk_s, n_new=ls)
            o = jnp.pad(_from_head_rows(o), ((0, 0), (0, SAMPLE_ROWS - T_PAD), (0, 0)))
            return o.astype(jnp.bfloat16), k, v, ik

        zero_states = (jnp.zeros((bp, HG_HEADS, LANES, LANES), f32),
                       jnp.zeros((bp, SSM_HEADS, SSM_P, SSM_STATE), f32),
                       jnp.zeros((bp, SSM_CONV - 1, SSM_CONV_DIM), f32),
                       jnp.zeros((bp, S5_GROUPS, S5_STATE), f32),
                       jnp.zeros((bp, S5_GROUPS, S5_STATE), f32))
        yp, st_p = _layer(yp, lp, zero_states, attend_prompt, l_valid=lp_len, tm=512, chunk_hg=64, chunk_ssd=128)
        samp_states = (state_hgrn[l], state_ssm[l], state_conv[l], state_s5_re[l], state_s5_im[l])
        ys, st_s = _layer(ys, lp, samp_states, attend_sample, l_valid=ls, tm=bs * SAMPLE_ROWS,
                          chunk_hg=SAMPLE_ROWS, chunk_ssd=SAMPLE_ROWS)
        for i in range(8):
            new_p[i].append(st_p[i])
            new_s[i].append(st_s[i])
    pk, pv, pik, phg, pssm, pconv, ps5r, ps5i = [jnp.stack(a) for a in new_p]
    sk, sv, sik, shg, sssm, sconv, ss5r, ss5i = [jnp.stack(a) for a in new_s]
    return (yp, ys[:, :ls], pk, pv, pik, phg, pssm, pconv, ps5r, ps5i, sk, sv, sik, shg, sssm, sconv, ss5r, ss5i)
```

```python
import functools
import math

import jax
import jax.numpy as jnp
from jax import lax
from jax.experimental import pallas as pl
from jax.experimental.pallas import tpu as pltpu

D_MODEL = 4096
DEPTH = 2
BRANCH = D_MODEL // 4
HG_HEADS = 8
SSM_HEADS = 16
SSM_GROUPS = 4
SSM_STATE = 128
SSM_CONV = 4
SSM_CONV_DIM = BRANCH + 2 * SSM_GROUPS * SSM_STATE
SSM_HPG = SSM_HEADS // SSM_GROUPS
SSM_P = 64
S5_GROUPS = 64
S5_STATE = 64
S5_BLK_CH = 128
S5_BLK_ST = 512
ATT_D = 128
ATT_HEADS = 8
KV_HEADS = 2
IDX_HEADS = 16
IDX_D = 64
TOPK_MAX = 256
Q_BLK = 128
PAGE = 128
REL_BUCKETS = 32
REL_MAX_DIST = 1024
EPS = 1e-6
MASK_VALUE = -1e30
F_FLOOR = 1e-30
INT_MIN = -2 ** 31

LANES = 128
SUBLANES = 8
V7X_VMEM_LIMIT = 56 * 1024 * 1024

_SRC_DT, _SRC_S5U, _SRC_ATK, _SRC_ATG, _SRC_IXK, D_IN_SRC = 7168, 7184, 10256, 10768, 12816, 12896

OFF_HG_Q, OFF_HG_F, OFF_HG_I, OFF_HG_G = 0, 1024, 2048, 3072
OFF_SM_Z, OFF_SM_XBC = 4096, 5120
OFF_S5_U, OFF_S5_G = 7168, 8192
OFF_AT_Q, OFF_AT_G, OFF_IX_Q = 9216, 10240, 11264
OFF_AT_KV = 12288
OFF_SMALL = 12800
SMALL_IXW, SMALL_DT = 64, 80
D_IN_PAD = 13312

PACK_TN = 1024
PACK_TK = 1024
_PACK_LAST = D_IN_PAD // PACK_TN - 1
_PACK_ROW0 = (tuple(range(0, _SRC_DT, PACK_TN)) + tuple(range(_SRC_S5U, _SRC_ATK, PACK_TN))
              + tuple(range(_SRC_ATG, _SRC_IXK, PACK_TN)) + (_SRC_ATK,))
N_KV_ROWS = _SRC_ATG - _SRC_ATK
N_IX_ROWS = D_IN_SRC - _SRC_IXK
N_DT_ROWS = _SRC_S5U - _SRC_DT
NORM_ROWS = 128
S5_BULK_STEPS = 32

N_BIAS_TILES = 9
SSD_GPB = 2
HG_SUB = 16
HG_HPB = 8
HG_ROW_BLOCK = 512
N_CAUSAL_VARIANTS = 4
NPG = 16
T_PAD = 8
CHUNK_KEYS = NPG * PAGE
SAMPLE_ROWS = 16

_TN = (((0,), (0,)), ((), ()))
_NT = (((1,), (1,)), ((), ()))


def _params(n_axes):
    return pltpu.CompilerParams(dimension_semantics=("arbitrary",) * n_axes, vmem_limit_bytes=V7X_VMEM_LIMIT)


def _bf(x):
    return x.astype(jnp.bfloat16)


def _dot(a, b, dims=None):
    if dims is None:
        return jnp.dot(_bf(a), _bf(b), preferred_element_type=jnp.float32)
    return lax.dot_general(_bf(a), _bf(b), dims, preferred_element_type=jnp.float32)


def _sigmoid(x):
    return 1.0 / (1.0 + jnp.exp(-x))


def _silu(x):
    return x * _sigmoid(x)


def _softplus(x):
    return jnp.maximum(x, 0.0) + jnp.log(1.0 + jnp.exp(-jnp.abs(x)))


def _gelu_tanh(x):
    return 0.5 * x * (1.0 + jnp.tanh(math.sqrt(2.0 / math.pi) * (x + 0.044715 * (x * x * x))))


def _cumsum_rows(x):
    n = x.shape[0]
    row = lax.broadcasted_iota(jnp.int32, x.shape, 0)
    sh = 1
    while sh < n:
        x = x + jnp.where(row >= sh, pltpu.roll(x, sh, axis=0), 0.0)
        sh *= 2
    return x


def _split3(x):
    hi = x.astype(jnp.bfloat16)
    r = x - hi.astype(jnp.float32)
    mid = r.astype(jnp.bfloat16)
    lo = (r - mid.astype(jnp.float32)).astype(jnp.bfloat16)
    return hi, mid, lo


def _select_lanes_as_rows(sel, x):
    out = None
    for part in _split3(x):
        t = lax.dot_general(sel, part, _NT, preferred_element_type=jnp.float32)
        out = t if out is None else out + t
    return out


def _cmul(ar, ai, br, bi):
    return ar * br - ai * bi, ar * bi + ai * br


def _pack_kernel(off_ref, w_ref, ix_ref, dt_ref, o_ref):
    j = pl.program_id(0)

    @pl.when(j < _PACK_LAST)
    def _():
        o_ref[...] = w_ref[0].astype(o_ref.dtype)

    @pl.when(j == _PACK_LAST)
    def _():
        o_ref[pl.ds(0, N_KV_ROWS), :] = w_ref[0, pl.ds(0, N_KV_ROWS), :].astype(o_ref.dtype)
        o_ref[pl.ds(N_KV_ROWS, N_IX_ROWS), :] = ix_ref[0].astype(o_ref.dtype)
        o_ref[pl.ds(N_KV_ROWS + N_IX_ROWS, N_DT_ROWS), :] = dt_ref[0].astype(o_ref.dtype)
        used = N_KV_ROWS + N_IX_ROWS + N_DT_ROWS
        o_ref[pl.ds(used, PACK_TN - used), :] = jnp.zeros((PACK_TN - used, PACK_TK), o_ref.dtype)


def _pack_w_in(w_in_t, layer):
    assert all(r % SUBLANES == 0 for r in _PACK_ROW0)
    row0 = jnp.asarray([r // SUBLANES for r in _PACK_ROW0], jnp.int32)
    E = pl.Element
    fixed = lambda rows, row: pl.BlockSpec(
        (E(1), E(rows), E(PACK_TK)), lambda j, kc, off: (layer, row, kc * PACK_TK))
    grid_spec = pltpu.PrefetchScalarGridSpec(
        num_scalar_prefetch=1, grid=(D_IN_PAD // PACK_TN, D_MODEL // PACK_TK),
        in_specs=[pl.BlockSpec((E(1), E(PACK_TN), E(PACK_TK)),
                               lambda j, kc, off: (layer, off[j] * SUBLANES, kc * PACK_TK)),
                  fixed(N_IX_ROWS, _SRC_IXK), fixed(N_DT_ROWS, _SRC_DT)],
        out_specs=pl.BlockSpec((PACK_TN, PACK_TK), lambda j, kc, off: (j, kc)))
    return pl.pallas_call(
        _pack_kernel, grid_spec=grid_spec,
        out_shape=jax.ShapeDtypeStruct((D_IN_PAD, D_MODEL), jnp.bfloat16),
        compiler_params=_params(2),
        name="pack_w_in",
    )(row0, w_in_t, w_in_t, w_in_t)


def _inproj_kernel(x_ref, nw_ref, w_ref, z_ref, hn_ref):
    @pl.when(pl.program_id(1) == 0)
    def _():
        nw = nw_ref[...]
        for r0 in range(0, x_ref.shape[0], NORM_ROWS):
            xf = x_ref[pl.ds(r0, NORM_ROWS), :]
            ms = jnp.mean(xf * xf, axis=-1, keepdims=True)
            hn_ref[pl.ds(r0, NORM_ROWS), :] = (xf * lax.rsqrt(ms + EPS) * nw).astype(jnp.bfloat16)

    z_ref[...] = lax.dot_general(hn_ref[...], w_ref[...], _NT, preferred_element_type=jnp.float32)


def _inproj(x2d, norm_w, w_packed, tm, tn=1024):
    m = x2d.shape[0]
    return pl.pallas_call(
        _inproj_kernel,
        grid=(m // tm, D_IN_PAD // tn),
        in_specs=[pl.BlockSpec((tm, D_MODEL), lambda i, j: (i, 0)),
                  pl.BlockSpec((1, D_MODEL), lambda i, j: (0, 0)),
                  pl.BlockSpec((tn, D_MODEL), lambda i, j: (j, 0))],
        out_specs=pl.BlockSpec((tm, tn), lambda i, j: (i, j)),
        out_shape=jax.ShapeDtypeStruct((m, D_IN_PAD), jnp.float32),
        scratch_shapes=[pltpu.VMEM((tm, D_MODEL), jnp.bfloat16)],
        compiler_params=_params(2),
        name="inproj",
    )(x2d, norm_w.reshape(1, D_MODEL), w_packed)


def _outproj_kernel(m0_ref, m1_ref, m2_ref, m3_ref, w_ref, x_ref, y_ref):
    acc = x_ref[...]
    for i, m_ref in enumerate((m0_ref, m1_ref, m2_ref, m3_ref)):
        acc = acc + jnp.dot(m_ref[...], w_ref[pl.ds(BRANCH * i, BRANCH), :], preferred_element_type=jnp.float32)
    y_ref[...] = acc


def _outproj(mixed4, w_out_bf16, layer, x2d, tm, tn=1024):
    m = x2d.shape[0]
    mspec = pl.BlockSpec((tm, BRANCH), lambda i, j: (i, 0))
    return pl.pallas_call(
        _outproj_kernel,
        grid=(m // tm, D_MODEL // tn),
        in_specs=[mspec, mspec, mspec, mspec,
                  pl.BlockSpec((None, D_MODEL, tn), lambda i, j: (layer, 0, j)),
                  pl.BlockSpec((tm, tn), lambda i, j: (i, j))],
        out_specs=pl.BlockSpec((tm, tn), lambda i, j: (i, j)),
        out_shape=jax.ShapeDtypeStruct((m, D_MODEL), jnp.float32),
        compiler_params=_params(2),
        name="outproj",
    )(*mixed4, w_out_bf16, x2d)


def _hgrn_kernel(q_ref, f_ref, i_ref, g_ref, lb_ref, nw_ref, s0_ref, o_ref, s_out_ref, st_ref, *, chunk, l_valid):
    LB = q_ref.shape[1]
    C = chunk
    nblk = C // SUBLANES
    lb_i = pl.program_id(2)

    @pl.when(lb_i == 0)
    def _():
        for hh in range(HG_HPB):
            st_ref[hh] = s0_ref[0, hh].T

    nw = nw_ref[...]
    rowi = lax.broadcasted_iota(jnp.int32, (SUBLANES, LANES), 0)

    def head_chunk(hh, t0):
        lanes = pl.ds(LANES * hh, LANES)
        lb = lb_ref[0, :, lanes]
        fp = f_ref[0, pl.ds(t0, C), lanes]
        qp = q_ref[0, pl.ds(t0, C), lanes]
        v = i_ref[0, pl.ds(t0, C), lanes]
        gp = g_ref[0, pl.ds(t0, C), lanes]
        fg = lb + (1.0 - lb) * _sigmoid(fp)
        logf = jnp.log(jnp.maximum(fg, F_FLOOR))
        kk = (1.0 - lb) * _sigmoid(-fp)
        if l_valid < LB * pl.num_programs(2):
            rows = lax.broadcasted_iota(jnp.int32, (C, LANES), 0) + t0 + lb_i * LB
            logf = jnp.where(rows < l_valid, logf, 0.0)
            kk = jnp.where(rows < l_valid, kk, 0.0)
        qq = _silu(qp)
        G = _cumsum_rows(logf)
        st = st_ref[hh]
        o_inter = _dot(qq * jnp.exp(G), st, _NT)
        acc = [None] * nblk
        qb = [qq[SUBLANES * tb:SUBLANES * (tb + 1)] for tb in range(nblk)]
        Gb = [G[SUBLANES * tb:SUBLANES * (tb + 1)] for tb in range(nblk)]
        for s in range(C):
            sb = s // SUBLANES
            gs = G[s:s + 1]
            ks = kk[s:s + 1]
            vs = v[s:s + 1]
            for tb in range(sb, (s // HG_SUB + 1) * (HG_SUB // SUBLANES)):
                d = Gb[tb] - gs
                if tb == sb:
                    d = jnp.where(rowi >= s - SUBLANES * sb, d, MASK_VALUE)
                w = jnp.sum(qb[tb] * ks * jnp.exp(d), axis=-1, keepdims=True)
                contrib = w * vs
                acc[tb] = contrib if acc[tb] is None else acc[tb] + contrib
        for j in range(1, C // HG_SUB):
            r0 = HG_SUB * j
            gb = G[r0 - 1:r0]
            qj = qq[r0:r0 + HG_SUB] * jnp.exp(G[r0:r0 + HG_SUB] - gb)
            kj = kk[:r0] * jnp.exp(gb - G[:r0])
            oj = _dot(_dot(qj, kj, _NT), v[:r0])
            for half in range(HG_SUB // SUBLANES):
                tb = r0 // SUBLANES + half
                acc[tb] = acc[tb] + oj[SUBLANES * half:SUBLANES * (half + 1)]
        o = o_inter + jnp.concatenate(acc, axis=0)
        g_last = G[C - 1:C]
        kd = kk * jnp.exp(g_last - G)
        st_ref[hh] = jnp.exp(g_last) * st + _dot(v, kd, _TN)
        ms = jnp.mean(o * o, axis=-1, keepdims=True)
        on = o * lax.rsqrt(ms + EPS) * nw
        o_ref[0, pl.ds(t0, C), lanes] = (on * _silu(gp)).astype(o_ref.dtype)

    def body(c, carry):
        t0 = pl.multiple_of(c * C, C)
        for hh in range(HG_HPB):
            head_chunk(hh, t0)
        return carry

    lax.fori_loop(0, LB // C, body, 0)

    @pl.when(lb_i == pl.num_programs(2) - 1)
    def _():
        for hh in range(HG_HPB):
            s_out_ref[0, hh] = st_ref[hh].T


def _hgrn_call(z, lb, hg_norm_w, s0, *, chunk, l_valid):
    B, L, _ = z.shape
    W = HG_HPB * LANES
    LB = min(L, HG_ROW_BLOCK)
    zspec = lambda off: pl.BlockSpec((1, LB, W), lambda b, h, r: (b, r, off // W + h))
    return pl.pallas_call(
        functools.partial(_hgrn_kernel, chunk=chunk, l_valid=l_valid),
        grid=(B, HG_HEADS // HG_HPB, L // LB),
        in_specs=[zspec(OFF_HG_Q), zspec(OFF_HG_F), zspec(OFF_HG_I), zspec(OFF_HG_G),
                  pl.BlockSpec((1, 1, W), lambda b, h, r: (h, 0, 0)),
                  pl.BlockSpec((1, LANES), lambda b, h, r: (0, 0)),
                  pl.BlockSpec((1, HG_HPB, LANES, LANES), lambda b, h, r: (b, h, 0, 0))],
        out_specs=[pl.BlockSpec((1, LB, W), lambda b, h, r: (b, r, h)),
                   pl.BlockSpec((1, HG_HPB, LANES, LANES), lambda b, h, r: (b, h, 0, 0))],
        out_shape=[jax.ShapeDtypeStruct((B, L, BRANCH), jnp.bfloat16),
                   jax.ShapeDtypeStruct((B, HG_HEADS, LANES, LANES), jnp.float32)],
        scratch_shapes=[pltpu.VMEM((HG_HPB, LANES, LANES), jnp.float32)],
        compiler_params=_params(3),
        name="hgrn",
    )(z, z, z, z, lb.reshape(HG_HEADS // HG_HPB, 1, W), hg_norm_w.reshape(1, LANES), s0)


def _ssd_kernel(xs_ref, b_ref, c_ref, zg_ref, sm_ref, wx_ref, wb_ref, wc_ref, bx_ref, bb_ref, bc_ref,
                cx_ref, cb_ref, cc_ref, dtb_ref, alog_ref, dskip_ref, nw_ref, s0_ref,
                y_ref, s_out_ref, ox_ref, ob_ref, oc_ref,
                ax_ref, ab_ref, ac_ref, hx_ref, hb_ref, hc_ref, sp_ref, *, chunk, l_valid, conv_rows):
    L = xs_ref.shape[1]
    C = chunk
    R = conv_rows
    gb = pl.program_id(1)
    n_heads = SSD_GPB * SSM_HPG
    n_pairs = n_heads // 2

    def conv(src_ref, head_ref, cst_ref, w_ref, bias_ref, act_ref, out_state_ref):
        head_ref[pl.ds(0, SUBLANES), :] = cst_ref[0]
        head_ref[pl.ds(SUBLANES, R), :] = src_ref[0, pl.ds(0, R), :]
        w = w_ref[...]
        for r0 in range(0, L, R):
            acc = bias_ref[...]
            for j in range(SSM_CONV):
                if r0 == 0:
                    xj = head_ref[pl.ds(SUBLANES - j, R), :]
                else:
                    xj = src_ref[0, pl.ds(r0 - j, R), :]
                acc = acc + xj * w[SSM_CONV - 1 - j:SSM_CONV - j]
            act_ref[pl.ds(r0, R), :] = _silu(acc)
        if l_valid >= SUBLANES:
            out_state_ref[0] = src_ref[0, pl.ds(l_valid - SUBLANES, SUBLANES), :]
        else:
            out_state_ref[0] = head_ref[pl.ds(l_valid, SUBLANES), :]

    conv(xs_ref, hx_ref, cx_ref, wx_ref, bx_ref, ax_ref, ox_ref)
    conv(b_ref, hb_ref, cb_ref, wb_ref, bb_ref, ab_ref, ob_ref)
    conv(c_ref, hc_ref, cc_ref, wc_ref, bc_ref, ac_ref, oc_ref)

    for p in range(n_pairs):
        sp_ref[p] = s0_ref[0, 2 * p:2 * p + 2].reshape(2 * SSM_P, LANES)

    lane1 = lax.broadcasted_iota(jnp.int32, (1, LANES), 1)
    lane8 = lax.broadcasted_iota(jnp.int32, (SUBLANES, LANES), 1)
    row8 = lax.broadcasted_iota(jnp.int32, (SUBLANES, LANES), 0)
    lane0 = SMALL_DT + n_heads * gb
    sel = jnp.where((lane8 == lane0 + row8) & (row8 < n_heads), 1.0, 0.0).astype(jnp.bfloat16)
    a_all = -jnp.exp(alog_ref[...])
    dskip = [jnp.sum(jnp.where(lane1 == lane0 + j, dskip_ref[...], 0.0), axis=-1, keepdims=True)
             for j in range(n_heads)]
    pair_w = 2 * SSM_P
    lane_c = lax.broadcasted_iota(jnp.int32, (C, LANES), 1)
    first_half = lane_c < SSM_P
    row_p = lax.broadcasted_iota(jnp.int32, (2 * SSM_P, LANES), 0) < SSM_P
    tril = lax.broadcasted_iota(jnp.int32, (C, C), 0) >= lax.broadcasted_iota(jnp.int32, (C, C), 1)
    nw = nw_ref[...]

    def body(c, carry):
        t0 = pl.multiple_of(c * C, C)
        sm = sm_ref[0, pl.ds(t0, C), :]
        dt_all = _softplus(sm + dtb_ref[...])
        if l_valid < L:
            rows = lax.broadcasted_iota(jnp.int32, (C, LANES), 0) + t0
            dt_all = jnp.where(rows < l_valid, dt_all, 0.0)
        cum = _cumsum_rows(dt_all * a_all)
        cum_rows = _select_lanes_as_rows(sel, cum)
        col = [jnp.sum(jnp.where(lane_c == lane0 + j, cum, 0.0), axis=-1, keepdims=True) for j in range(n_heads)]
        dtc = [jnp.sum(jnp.where(lane_c == lane0 + j, dt_all, 0.0), axis=-1, keepdims=True) for j in range(n_heads)]
        for lg in range(SSD_GPB):
            bact = ab_ref[pl.ds(t0, C), pl.ds(LANES * lg, LANES)]
            cact = ac_ref[pl.ds(t0, C), pl.ds(LANES * lg, LANES)]
            cb = _dot(cact, bact, _NT)
            ys = []
            for p in range(SSM_HPG // 2):
                pp = lg * (SSM_HPG // 2) + p
                ja, jb = 2 * pp, 2 * pp + 1
                xs = ax_ref[pl.ds(t0, C), pl.ds(pair_w * pp, pair_w)]
                xdt = xs * jnp.where(first_half, dtc[ja], dtc[jb])
                dec_a = jnp.exp(jnp.where(tril, col[ja] - cum_rows[ja:ja + 1], MASK_VALUE))
                dec_b = jnp.exp(jnp.where(tril, col[jb] - cum_rows[jb:jb + 1], MASK_VALUE))
                y = jnp.where(first_half, _dot(cb * dec_a, xdt), _dot(cb * dec_b, xdt))
                sp = sp_ref[pp]
                y = y + _dot(cact, sp, _NT) * jnp.where(first_half, jnp.exp(col[ja]), jnp.exp(col[jb]))
                y = y + jnp.where(first_half, dskip[ja], dskip[jb]) * xs
                last_a, last_b = col[ja][C - 1:C], col[jb][C - 1:C]
                xw = xdt * jnp.where(first_half, jnp.exp(last_a - col[ja]), jnp.exp(last_b - col[jb]))
                sp_ref[pp] = jnp.where(row_p, jnp.exp(last_a), jnp.exp(last_b)) * sp + _dot(xw, bact, _TN)
                zg = zg_ref[0, pl.ds(t0, C), pl.ds(pair_w * pp, pair_w)]
                ys.append(y * _silu(zg))
            ms = sum(jnp.sum(y * y, axis=-1, keepdims=True) for y in ys) * (1.0 / (SSM_HPG * SSM_P))
            inv = lax.rsqrt(ms + EPS)
            for p in range(SSM_HPG // 2):
                pp = lg * (SSM_HPG // 2) + p
                y_ref[0, pl.ds(t0, C), pl.ds(pair_w * pp, pair_w)] = (
                    ys[p] * inv * nw[:, pair_w * pp:pair_w * (pp + 1)]).astype(y_ref.dtype)
        return carry

    lax.fori_loop(0, L // C, body, 0)
    for p in range(n_pairs):
        s_out_ref[0, 2 * p:2 * p + 2] = sp_ref[p].reshape(2, SSM_P, LANES)


def _ssd_call(z, conv_w, conv_b, conv_state8, dtb_pad, alog_pad, dskip_pad, norm_w, s0, *, chunk, l_valid):
    B, L, _ = z.shape
    assert l_valid >= SSM_CONV - 1
    xw = SSD_GPB * SSM_HPG * SSM_P
    bw = SSD_GPB * SSM_STATE
    nh = SSD_GPB * SSM_HPG
    conv_rows = min(L, 128)
    ox, ob = OFF_SM_XBC // xw, (OFF_SM_XBC + BRANCH) // bw
    oc = (OFF_SM_XBC + BRANCH + SSM_GROUPS * SSM_STATE) // bw
    wb0, wc0 = BRANCH // bw, (BRANCH + SSM_GROUPS * SSM_STATE) // bw
    in_specs = [
        pl.BlockSpec((1, L, xw), lambda b, g: (b, 0, ox + g)),
        pl.BlockSpec((1, L, bw), lambda b, g: (b, 0, ob + g)),
        pl.BlockSpec((1, L, bw), lambda b, g: (b, 0, oc + g)),
        pl.BlockSpec((1, L, xw), lambda b, g: (b, 0, OFF_SM_Z // xw + g)),
        pl.BlockSpec((1, L, LANES), lambda b, g: (b, 0, OFF_SMALL // LANES)),
        pl.BlockSpec((SSM_CONV, xw), lambda b, g: (0, g)),
        pl.BlockSpec((SSM_CONV, bw), lambda b, g: (0, wb0 + g)),
        pl.BlockSpec((SSM_CONV, bw), lambda b, g: (0, wc0 + g)),
        pl.BlockSpec((1, xw), lambda b, g: (0, g)),
        pl.BlockSpec((1, bw), lambda b, g: (0, wb0 + g)),
        pl.BlockSpec((1, bw), lambda b, g: (0, wc0 + g)),
        pl.BlockSpec((1, SUBLANES, xw), lambda b, g: (b, 0, g)),
        pl.BlockSpec((1, SUBLANES, bw), lambda b, g: (b, 0, wb0 + g)),
        pl.BlockSpec((1, SUBLANES, bw), lambda b, g: (b, 0, wc0 + g)),
        pl.BlockSpec((1, LANES), lambda b, g: (0, 0)),
        pl.BlockSpec((1, LANES), lambda b, g: (0, 0)),
        pl.BlockSpec((1, LANES), lambda b, g: (0, 0)),
        pl.BlockSpec((1, xw), lambda b, g: (0, g)),
        pl.BlockSpec((1, nh, SSM_P, LANES), lambda b, g: (b, g, 0, 0)),
    ]
    out_specs = [
        pl.BlockSpec((1, L, xw), lambda b, g: (b, 0, g)),
        pl.BlockSpec((1, nh, SSM_P, LANES), lambda b, g: (b, g, 0, 0)),
        pl.BlockSpec((1, SUBLANES, xw), lambda b, g: (b, 0, g)),
        pl.BlockSpec((1, SUBLANES, bw), lambda b, g: (b, 0, g)),
        pl.BlockSpec((1, SUBLANES, bw), lambda b, g: (b, 0, g)),
    ]
    f32 = jnp.float32
    out_shape = [
        jax.ShapeDtypeStruct((B, L, BRANCH), jnp.bfloat16),
        jax.ShapeDtypeStruct((B, SSM_HEADS, SSM_P, LANES), f32),
        jax.ShapeDtypeStruct((B, SUBLANES, BRANCH), f32),
        jax.ShapeDtypeStruct((B, SUBLANES, SSM_GROUPS * SSM_STATE), f32),
        jax.ShapeDtypeStruct((B, SUBLANES, SSM_GROUPS * SSM_STATE), f32),
    ]
    scratch = [pltpu.VMEM((L, xw), f32), pltpu.VMEM((L, bw), f32), pltpu.VMEM((L, bw), f32),
               pltpu.VMEM((SUBLANES + conv_rows, xw), f32), pltpu.VMEM((SUBLANES + conv_rows, bw), f32),
               pltpu.VMEM((SUBLANES + conv_rows, bw), f32),
               pltpu.VMEM((nh // 2, 2 * SSM_P, LANES), f32)]
    cbias = conv_b.reshape(1, -1)
    return pl.pallas_call(
        functools.partial(_ssd_kernel, chunk=chunk, l_valid=l_valid, conv_rows=conv_rows),
        grid=(B, SSM_GROUPS // SSD_GPB), in_specs=in_specs, out_specs=out_specs, out_shape=out_shape, scratch_shapes=scratch,
        compiler_params=_params(2),
        name="ssd",
    )(z, z, z, z, z, conv_w, conv_w, conv_w, cbias, cbias, cbias, conv_state8, conv_state8, conv_state8,
      dtb_pad, alog_pad, dskip_pad, norm_w.reshape(1, -1), s0)


def _s5_kernel(u_ref, bre_ref, bim_ref, cre_ref, cim_ref, are_ref, aim_ref, ldt_ref, d_ref, x0r_ref, x0i_ref,
               h_ref, xr_out_ref, xi_out_ref, bbr_ref, bbi_ref, xr_ref, xi_ref, pwr_ref, pwi_ref,
               *, nseg, seg_len):
    L = u_ref.shape[1]
    n_scan = nseg * seg_len
    row_blk = min(L, 256)
    a_re, a_im = are_ref[...], aim_ref[...]
    dt = jnp.exp(ldt_ref[...])
    mag = jnp.exp(a_re * dt)
    ab_re, ab_im = mag * jnp.cos(a_im * dt), mag * jnp.sin(a_im * dt)
    den = a_re * a_re + a_im * a_im
    nr = ab_re - 1.0
    coef_re = (nr * a_re + ab_im * a_im) / den
    coef_im = (ab_im * a_re - nr * a_im) / den

    NQ = S5_BLK_ST // LANES
    lq = lambda v, q: v[:, LANES * q:LANES * (q + 1)]
    abr = [lq(ab_re, q) for q in range(NQ)]
    abi = [lq(ab_im, q) for q in range(NQ)]

    for r0 in range(0, L, row_blk):
        u = u_ref[0, pl.ds(r0, row_blk), :]
        bu_re, bu_im = _dot(u, bre_ref[0]), _dot(u, bim_ref[0])
        bb_re = coef_re * bu_re - coef_im * bu_im
        bb_im = coef_re * bu_im + coef_im * bu_re
        for q in range(NQ):
            bbr_ref[q, pl.ds(r0, row_blk), :] = lq(bb_re, q)
            bbi_ref[q, pl.ds(r0, row_blk), :] = lq(bb_im, q)

    if n_scan < L:
        for q in range(NQ):
            xr_ref[q, pl.ds(n_scan, L - n_scan), :] = jnp.zeros((L - n_scan, LANES), jnp.float32)
            xi_ref[q, pl.ds(n_scan, L - n_scan), :] = jnp.zeros((L - n_scan, LANES), jnp.float32)

    def rows(i):
        return pl.ds(i, nseg, stride=seg_len) if nseg > 1 else pl.ds(i, 1)

    def scan(i, carry):
        out = []
        for q in range(NQ):
            xr, xi = carry[q]
            pr, pi = _cmul(abr[q], abi[q], xr, xi)
            xr, xi = pr + bbr_ref[q, rows(i), :], pi + bbi_ref[q, rows(i), :]
            xr_ref[q, rows(i), :] = xr
            xi_ref[q, rows(i), :] = xi
            out.append((xr, xi))
        return tuple(out)

    zero = jnp.zeros((nseg, LANES), jnp.float32)
    ends = lax.fori_loop(0, seg_len, scan, tuple((zero, zero) for _ in range(NQ)), unroll=4)

    for q in range(NQ):
        pwr_ref[q, pl.ds(0, 1), :] = abr[q]
        pwi_ref[q, pl.ds(0, 1), :] = abi[q]
    an_r, an_i = ab_re, ab_im
    n = 1
    while n < seg_len:
        m = min(n, seg_len - n)
        for q in range(NQ):
            pr, pi = _cmul(pwr_ref[q, pl.ds(0, m), :], pwi_ref[q, pl.ds(0, m), :], lq(an_r, q), lq(an_i, q))
            pwr_ref[q, pl.ds(n, m), :] = pr
            pwi_ref[q, pl.ds(n, m), :] = pi
        an_r, an_i = _cmul(an_r, an_i, an_r, an_i)
        n *= 2
    seg_r = [pwr_ref[q, pl.ds(seg_len - 1, 1), :] for q in range(NQ)]
    seg_i = [pwi_ref[q, pl.ds(seg_len - 1, 1), :] for q in range(NQ)]

    x0r, x0i = x0r_ref[0], x0i_ref[0]
    dskip = d_ref[...]
    c_r = [lq(x0r, q) for q in range(NQ)]
    c_i = [lq(x0i, q) for q in range(NQ)]
    for k in range(nseg):
        r0 = k * seg_len
        xr_q, xi_q = [], []
        for q in range(NQ):
            dr, di = _cmul(pwr_ref[q], pwi_ref[q], c_r[q], c_i[q])
            xr_q.append(xr_ref[q, pl.ds(r0, seg_len), :] + dr)
            xi_q.append(xi_ref[q, pl.ds(r0, seg_len), :] + di)
            pr, pi = _cmul(seg_r[q], seg_i[q], c_r[q], c_i[q])
            c_r[q], c_i[q] = pr + ends[q][0][k:k + 1], pi + ends[q][1][k:k + 1]
        if k == nseg - 1:
            for q in range(NQ):
                xr_out_ref[0, :, pl.ds(LANES * q, LANES)] = c_r[q]
                xi_out_ref[0, :, pl.ds(LANES * q, LANES)] = c_i[q]
        if n_scan == L:
            u = u_ref[0, pl.ds(r0, seg_len), :]
            y = (_dot(jnp.concatenate(xr_q, axis=-1), cre_ref[0]) - _dot(jnp.concatenate(xi_q, axis=-1), cim_ref[0])
                 + dskip * u)
            h_ref[0, pl.ds(r0, seg_len), :] = _gelu_tanh(y)
        else:
            for q in range(NQ):
                xr_ref[q, pl.ds(r0, seg_len), :] = xr_q[q]
                xi_ref[q, pl.ds(r0, seg_len), :] = xi_q[q]
    if n_scan < L:
        xr = jnp.concatenate([xr_ref[q] for q in range(NQ)], axis=-1)
        xi = jnp.concatenate([xi_ref[q] for q in range(NQ)], axis=-1)
        y = _dot(xr, cre_ref[0]) - _dot(xi, cim_ref[0]) + dskip * u_ref[0]
        h_ref[0] = _gelu_tanh(y)


def _s5seg_kernel(u_ref, bre_ref, bim_ref, cre_ref, cim_ref, are_ref, aim_ref, ldt_ref, d_ref, x0r_ref, x0i_ref,
                  h_ref, xr_out_ref, xi_out_ref, up_ref, bbr_ref, bbi_ref, xr_ref, xi_ref, pwr_ref, pwi_ref, yp_ref,
                  *, seg_len):
    NS = SUBLANES
    a_re, a_im = are_ref[...], aim_ref[...]
    dt = jnp.exp(ldt_ref[...])
    mag = jnp.exp(a_re * dt)
    ab_re, ab_im = mag * jnp.cos(a_im * dt), mag * jnp.sin(a_im * dt)
    den = a_re * a_re + a_im * a_im
    nr = ab_re - 1.0
    coef_re = (nr * a_re + ab_im * a_im) / den
    coef_im = (ab_im * a_re - nr * a_im) / den

    NQ = S5_BLK_ST // LANES
    lq = lambda v, q: v[:, LANES * q:LANES * (q + 1)]
    rep = lambda v: jnp.broadcast_to(v, (NS, LANES))
    abr = [rep(lq(ab_re, q)) for q in range(NQ)]
    abi = [rep(lq(ab_im, q)) for q in range(NQ)]
    blk = S5_BULK_STEPS
    n_blk = seg_len // blk

    def gather(i, carry):
        up_ref[pl.ds(pl.multiple_of(i * NS, NS), NS), :] = u_ref[0, pl.ds(i, NS, stride=seg_len), :]
        return carry

    lax.fori_loop(0, seg_len, gather, 0, unroll=8)

    for rb in range(n_blk):
        u = up_ref[pl.ds(rb * blk * NS, blk * NS), :]
        bu_re, bu_im = _dot(u, bre_ref[0]), _dot(u, bim_ref[0])
        bb_re = coef_re * bu_re - coef_im * bu_im
        bb_im = coef_re * bu_im + coef_im * bu_re
        for q in range(NQ):
            bbr_ref[q, pl.ds(rb * blk, blk)] = lq(bb_re, q).reshape(blk, NS, LANES)
            bbi_ref[q, pl.ds(rb * blk, blk)] = lq(bb_im, q).reshape(blk, NS, LANES)

    def scan(i, carry):
        out = []
        for q in range(NQ):
            xr, xi = carry[q]
            pr, pi = _cmul(abr[q], abi[q], xr, xi)
            xr, xi = pr + bbr_ref[q, i], pi + bbi_ref[q, i]
            xr_ref[q, i] = xr
            xi_ref[q, i] = xi
            out.append((xr, xi))
        return tuple(out)

    zero = jnp.zeros((NS, LANES), jnp.float32)
    ends = lax.fori_loop(0, seg_len, scan, tuple((zero, zero) for _ in range(NQ)), unroll=8)

    for q in range(NQ):
        pwr_ref[q, 0] = abr[q]
        pwi_ref[q, 0] = abi[q]
    an = [(abr[q], abi[q]) for q in range(NQ)]
    n = 1
    while n < seg_len:
        m = min(n, seg_len - n)
        for q in range(NQ):
            pr, pi = _cmul(pwr_ref[q, pl.ds(0, m)], pwi_ref[q, pl.ds(0, m)], an[q][0], an[q][1])
            pwr_ref[q, pl.ds(n, m)] = pr
            pwi_ref[q, pl.ds(n, m)] = pi
            an[q] = _cmul(an[q][0], an[q][1], an[q][0], an[q][1])
        n *= 2

    x0r, x0i = x0r_ref[0], x0i_ref[0]
    cs = []
    for q in range(NQ):
        seg_r, seg_i = pwr_ref[q, seg_len - 1][:1], pwi_ref[q, seg_len - 1][:1]
        cr, ci = [lq(x0r, q)], [lq(x0i, q)]
        for k in range(NS):
            pr, pi = _cmul(seg_r, seg_i, cr[k], ci[k])
            cr.append(pr + ends[q][0][k:k + 1])
            ci.append(pi + ends[q][1][k:k + 1])
        xr_out_ref[0, :, pl.ds(LANES * q, LANES)] = cr[NS]
        xi_out_ref[0, :, pl.ds(LANES * q, LANES)] = ci[NS]
        cs.append((jnp.concatenate(cr[:NS], axis=0), jnp.concatenate(ci[:NS], axis=0)))

    dskip = d_ref[...]
    for rb in range(n_blk):
        xr_q, xi_q = [], []
        for q in range(NQ):
            dr, di = _cmul(pwr_ref[q, pl.ds(rb * blk, blk)], pwi_ref[q, pl.ds(rb * blk, blk)], cs[q][0], cs[q][1])
            xr_q.append((xr_ref[q, pl.ds(rb * blk, blk)] + dr).reshape(blk * NS, LANES))
            xi_q.append((xi_ref[q, pl.ds(rb * blk, blk)] + di).reshape(blk * NS, LANES))
        u = up_ref[pl.ds(rb * blk * NS, blk * NS), :]
        y = (_dot(jnp.concatenate(xr_q, axis=-1), cre_ref[0]) - _dot(jnp.concatenate(xi_q, axis=-1), cim_ref[0])
             + dskip * u)
        yp_ref[pl.ds(rb * blk * NS, blk * NS), :] = _gelu_tanh(y)

    for k in range(NS):
        h_ref[0, pl.ds(k * seg_len, seg_len), :] = yp_ref[pl.ds(k, seg_len, stride=NS), :]


def _s5_call(z, bblk_re, bblk_im, cblk_re, cblk_im, a_re, a_im, log_dt_exp, d_flat, x0_re, x0_im, *, l_valid):
    B, L, _ = z.shape
    nb = bblk_re.shape[0]
    NQ = S5_BLK_ST // LANES
    f32 = jnp.float32
    if l_valid == L and L % (SUBLANES * S5_BULK_STEPS) == 0:
        seg_len = L // SUBLANES
        kern = functools.partial(_s5seg_kernel, seg_len=seg_len)
        tile = pltpu.VMEM((NQ, seg_len, SUBLANES, LANES), f32)
        scratch = [pltpu.VMEM((L, LANES), f32)] + [tile] * 6 + [pltpu.VMEM((L, LANES), f32)]
    else:
        kern = functools.partial(_s5_kernel, nseg=1, seg_len=l_valid)
        scratch = [pltpu.VMEM((NQ, L, LANES), f32)] * 4 + [pltpu.VMEM((NQ, l_valid, LANES), f32)] * 2
    vec = lambda w: pl.BlockSpec((1, w), lambda b, j: (0, j))
    st = pl.BlockSpec((1, 1, S5_BLK_ST), lambda b, j: (b, 0, j))
    return pl.pallas_call(
        kern,
        grid=(B, nb),
        in_specs=[pl.BlockSpec((1, L, S5_BLK_CH), lambda b, j: (b, 0, OFF_S5_U // S5_BLK_CH + j)),
                  pl.BlockSpec((1, S5_BLK_CH, S5_BLK_ST), lambda b, j: (j, 0, 0)),
                  pl.BlockSpec((1, S5_BLK_CH, S5_BLK_ST), lambda b, j: (j, 0, 0)),
                  pl.BlockSpec((1, S5_BLK_ST, S5_BLK_CH), lambda b, j: (j, 0, 0)),
                  pl.BlockSpec((1, S5_BLK_ST, S5_BLK_CH), lambda b, j: (j, 0, 0)),
                  vec(S5_BLK_ST), vec(S5_BLK_ST), vec(S5_BLK_ST), vec(S5_BLK_CH), st, st],
        out_specs=[pl.BlockSpec((1, L, S5_BLK_CH), lambda b, j: (b, 0, j)), st, st],
        out_shape=[jax.ShapeDtypeStruct((B, L, nb * S5_BLK_CH), jnp.float32),
                   jax.ShapeDtypeStruct((B, 1, nb * S5_BLK_ST), jnp.float32),
                   jax.ShapeDtypeStruct((B, 1, nb * S5_BLK_ST), jnp.float32)],
        scratch_shapes=scratch,
        compiler_params=_params(2),
        name="s5",
    )(z, bblk_re, bblk_im, cblk_re, cblk_im, a_re, a_im, log_dt_exp, d_flat, x0_re, x0_im)


def _s5_block_diag(b_re, b_im, c_re, c_im):
    G, P, Cc = b_re.shape
    nb = G // 8
    same = jnp.eye(8, dtype=bool)

    def bblk(b):
        t = jnp.transpose(b.reshape(nb, 8, P, Cc), (0, 1, 3, 2))[:, :, :, None, :]
        t = jnp.where(same[None, :, None, :, None], t, 0.0)
        return t.reshape(nb, 8 * Cc, 8 * P).astype(jnp.bfloat16)

    def cblk(c):
        t = jnp.transpose(c.reshape(nb, 8, Cc, P), (0, 1, 3, 2))[:, :, :, None, :]
        t = jnp.where(same[None, :, None, :, None], t, 0.0)
        return t.reshape(nb, 8 * P, 8 * Cc).astype(jnp.bfloat16)

    return bblk(b_re), bblk(b_im), cblk(c_re), cblk(c_im)


def _glu_kernel(h_ref, w_ref, b_ref, g_ref, o_ref):
    h = h_ref[...]
    t = _dot(h, w_ref[...]) + b_ref[...]
    o_ref[...] = (h * _sigmoid(t) * _silu(g_ref[...])).astype(o_ref.dtype)


def _glu_call(h2d, glu_w_bf16, glu_b, z2d, *, tm):
    M, W = h2d.shape
    return pl.pallas_call(
        _glu_kernel,
        grid=(M // tm,),
        in_specs=[pl.BlockSpec((tm, W), lambda i: (i, 0)),
                  pl.BlockSpec((W, W), lambda i: (0, 0)),
                  pl.BlockSpec((1, W), lambda i: (0, 0)),
                  pl.BlockSpec((tm, W), lambda i: (i, OFF_S5_G // W))],
        out_specs=pl.BlockSpec((tm, W), lambda i: (i, 0)),
        out_shape=jax.ShapeDtypeStruct((M, W), jnp.bfloat16),
        compiler_params=_params(1),
        name="glu",
    )(h2d, glu_w_bf16, glu_b.reshape(1, W), z2d)


def _kvprep_kernel(kv_ref, sm_ref, knw_ref, k_ref, v_ref, ik_ref, *rest, transposed):
    kvw = KV_HEADS * ATT_D
    kv = kv_ref[0]
    knw = knw_ref[...]
    ks = []
    for h in range(KV_HEADS):
        kh = kv[:, ATT_D * h:ATT_D * (h + 1)]
        ms = jnp.mean(kh * kh, axis=-1, keepdims=True)
        ks.append(kh * lax.rsqrt(ms + EPS) * knw)
    k = jnp.concatenate(ks, axis=-1)
    v = kv[:, kvw:]
    sm = sm_ref[0]
    k_ref[0] = k
    v_ref[0] = v
    ik_ref[0] = sm[:, :IDX_D]
    if transposed:
        kbf_ref, vt_ref, kia_ref, kib_ref = rest
        kbf_ref[0] = k.astype(jnp.bfloat16)
        vt_ref[0] = v.T.astype(jnp.bfloat16)
        lane = lax.broadcasted_iota(jnp.int32, sm.shape, 1)
        kia_ref[0] = jnp.where(lane < IDX_D, sm, 0.0).astype(jnp.bfloat16)
        kib_ref[0] = jnp.where(lane >= IDX_D, pltpu.roll(sm, IDX_D, axis=1), 0.0).astype(jnp.bfloat16)


def _kvprep_call(z, k_norm_w, *, tr, transposed):
    B, L, _ = z.shape
    kvw = KV_HEADS * ATT_D
    f32, bf = jnp.float32, jnp.bfloat16
    rowblk = lambda w: pl.BlockSpec((1, tr, w), lambda b, r: (b, r, 0))
    out_specs = [rowblk(kvw), rowblk(kvw), rowblk(IDX_D)]
    out_shape = [jax.ShapeDtypeStruct((B, L, kvw), f32), jax.ShapeDtypeStruct((B, L, kvw), f32),
                 jax.ShapeDtypeStruct((B, L, IDX_D), f32)]
    if transposed:
        out_specs += [rowblk(kvw), pl.BlockSpec((1, kvw, tr), lambda b, r: (b, 0, r)), rowblk(LANES), rowblk(LANES)]
        out_shape += [jax.ShapeDtypeStruct((B, L, kvw), bf), jax.ShapeDtypeStruct((B, kvw, L), bf),
                      jax.ShapeDtypeStruct((B, L, LANES), bf), jax.ShapeDtypeStruct((B, L, LANES), bf)]
    return pl.pallas_call(
        functools.partial(_kvprep_kernel, transposed=transposed),
        grid=(B, L // tr),
        in_specs=[pl.BlockSpec((1, tr, 2 * kvw), lambda b, r: (b, r, OFF_AT_KV // (2 * kvw))),
                  pl.BlockSpec((1, tr, LANES), lambda b, r: (b, r, OFF_SMALL // LANES)),
                  pl.BlockSpec((1, ATT_D), lambda b, r: (0, 0))],
        out_specs=out_specs, out_shape=out_shape,
        compiler_params=_params(2),
        name="kvprep",
    )(z, z, k_norm_w.reshape(1, ATT_D))


def _sortable_key(score):
    bits = lax.bitcast_convert_type(score, jnp.int32)
    return jnp.where(bits < 0, bits ^ 0x7FFFFFFF, bits)


def _rows_reduce(x, op):
    n = x.shape[0]
    slab = 8 * SUBLANES
    if n > slab and n % slab == 0:
        x = op(x.reshape(n // slab, slab, x.shape[1]), axis=0)
    return op(x, axis=0, keepdims=True)


def _lanes_sum(x):
    parts = [x[:, LANES * c:LANES * (c + 1)] for c in range(x.shape[1] // LANES)]
    while len(parts) > 1:
        parts = [a + b for a, b in zip(parts[::2], parts[1::2])] + ([parts[-1]] if len(parts) % 2 else [])
    return jnp.sum(parts[0], axis=1, keepdims=True)


def _kth_largest_key(key, k, axis):
    shape = list(key.shape)
    shape[axis] = 1

    def it(n, tu):
        cand_u = tu | jnp.left_shift(jnp.int32(1), 31 - n)
        cand_s = cand_u ^ jnp.int32(INT_MIN)
        ones = jnp.where(key >= cand_s, 1.0, 0.0)
        cnt = _rows_reduce(ones, jnp.sum) if axis == 0 else _lanes_sum(ones)
        return jnp.where(cnt >= k, cand_u, tu)

    tu = lax.fori_loop(0, 32, it, jnp.zeros(shape, jnp.int32))
    return tu ^ jnp.int32(INT_MIN)


def _t5_bucket(dist):
    exact = REL_BUCKETS // 2
    d = dist.astype(jnp.float32)
    large = exact + jnp.log(jnp.maximum(d, 1.0) / exact) / math.log(REL_MAX_DIST / exact) * (REL_BUCKETS - exact)
    large = jnp.minimum(jnp.maximum(large, 0.0).astype(jnp.int32), REL_BUCKETS - 1)
    return jnp.where(dist < exact, dist, large)


def _bias_lookup(rel_bias, bucket):
    onehot = (bucket[..., None] == jnp.arange(REL_BUCKETS, dtype=jnp.int32)).astype(jnp.float32)
    return jnp.einsum('...k,kh->...h', onehot, rel_bias.astype(jnp.float32), precision=lax.Precision.HIGHEST)


def _pattn_kernel(zq_ref, zg_ref, ziq_ref, zsm_ref, kbf_ref, vt_ref, kia_ref, kib_ref, qnw_ref, bias_ref, o_ref,
                  mask_ref, *, topk):
    i = pl.program_id(1)
    nqb = kbf_ref.shape[1] // Q_BLK
    qnw = qnw_ref[...]
    scale = ATT_D ** -0.5

    def body(S):
        nkb = S // Q_BLK
        wt = zsm_ref[0].T * (1.0 / 32.0)
        kia, kib = kia_ref[0, pl.ds(0, S), :], kib_ref[0, pl.ds(0, S), :]
        score = jnp.zeros((S, Q_BLK), jnp.float32)
        for p in range(IDX_HEADS // 2):
            qi = ziq_ref[0, :, pl.ds(LANES * p, LANES)].astype(jnp.bfloat16)
            da = lax.dot_general(kia, qi, _NT, preferred_element_type=jnp.float32)
            db = lax.dot_general(kib, qi, _NT, preferred_element_type=jnp.float32)
            r = SMALL_IXW + 2 * p
            score = score + jnp.maximum(da, 0.0) * wt[r:r + 1] + jnp.maximum(db, 0.0) * wt[r + 1:r + 2]
        s_pos = lax.broadcasted_iota(jnp.int32, (S, Q_BLK), 0)
        t_pos = lax.broadcasted_iota(jnp.int32, (S, Q_BLK), 1) + i * Q_BLK
        adm = s_pos <= t_pos
        key = jnp.where(adm, _sortable_key(score), jnp.int32(INT_MIN))
        thr = _kth_largest_key(key, topk, 0)
        ge = key >= thr
        cnt_ge = _rows_reduce(jnp.where(ge, 1.0, 0.0), jnp.sum)
        tied = (cnt_ge > topk) & (thr > jnp.int32(INT_MIN))
        any_tied = jnp.max(jnp.where(tied, 1.0, 0.0)) > 0.0

        @pl.when(jnp.logical_not(any_tied))
        def _():
            mask_ref[pl.ds(0, S), :] = jnp.where(ge & adm, 0.0, MASK_VALUE)

        @pl.when(any_tied)
        def _():
            gt = key > thr
            eq = key == thr
            need = topk - _rows_reduce(jnp.where(gt, 1.0, 0.0), jnp.sum)
            nbits = S.bit_length()

            def it(n, j_sel):
                cand = j_sel | jnp.left_shift(jnp.int32(1), nbits - 1 - n)
                c = _rows_reduce(jnp.where(eq & (s_pos < cand), 1.0, 0.0), jnp.sum)
                return jnp.where(c <= need, cand, j_sel)

            j_sel = lax.fori_loop(0, nbits, it, jnp.zeros((1, Q_BLK), jnp.int32))
            sel = (gt | (eq & (s_pos < j_sel))) & adm
            mask_ref[pl.ds(0, S), :] = jnp.where(sel, 0.0, MASK_VALUE)

        for h in range(ATT_HEADS):
            kvh = h // (ATT_HEADS // KV_HEADS)
            q = zq_ref[0, :, pl.ds(ATT_D * h, ATT_D)]
            ms = jnp.mean(q * q, axis=-1, keepdims=True)
            qn = (q * lax.rsqrt(ms + EPS) * qnw).astype(jnp.bfloat16)
            logit = lax.dot_general(kbf_ref[0, pl.ds(0, S), pl.ds(ATT_D * kvh, ATT_D)], qn, _NT,
                                    preferred_element_type=jnp.float32) * scale
            bias = jnp.concatenate([bias_ref[jnp.clip(i - j, 0, N_BIAS_TILES - 1), h] for j in range(nkb)], axis=0)
            x = logit + bias + mask_ref[pl.ds(0, S), :]
            m = _rows_reduce(x, jnp.max)
            pexp = jnp.exp(x - m)
            l = _rows_reduce(pexp, jnp.sum)
            ot = jnp.dot(vt_ref[0, pl.ds(ATT_D * kvh, ATT_D), pl.ds(0, S)], pexp.astype(jnp.bfloat16),
                         preferred_element_type=jnp.float32)
            out = (ot / l).T
            g = zg_ref[0, :, pl.ds(ATT_D * h, ATT_D)]
            o_ref[0, :, pl.ds(ATT_D * h, ATT_D)] = (out * _silu(g)).astype(o_ref.dtype)

    nv = min(N_CAUSAL_VARIANTS, nqb)
    per = nqb // nv
    for c in range(nv):
        pl.when(i // per == c)(functools.partial(body, (c + 1) * per * Q_BLK))


def _pattn_call(z, kbf, vt, kia, kib, q_norm_w, bias_tiles, *, topk):
    B, L, _ = z.shape
    W = ATT_HEADS * ATT_D
    kvw = KV_HEADS * ATT_D
    zblk = lambda off, w: pl.BlockSpec((1, Q_BLK, w), lambda b, i: (b, i, off // w))
    full = lambda shape: pl.BlockSpec((1,) + shape, lambda b, i: (b, 0, 0))
    return pl.pallas_call(
        functools.partial(_pattn_kernel, topk=topk),
        grid=(B, L // Q_BLK),
        in_specs=[zblk(OFF_AT_Q, W), zblk(OFF_AT_G, W), zblk(OFF_IX_Q, W), zblk(OFF_SMALL, LANES),
                  full((L, kvw)), full((kvw, L)), full((L, LANES)), full((L, LANES)),
                  pl.BlockSpec((1, ATT_D), lambda b, i: (0, 0)),
                  pl.BlockSpec((N_BIAS_TILES, ATT_HEADS, Q_BLK, Q_BLK), lambda b, i: (0, 0, 0, 0))],
        out_specs=pl.BlockSpec((1, Q_BLK, W), lambda b, i: (b, i, 0)),
        out_shape=jax.ShapeDtypeStruct((B, L, W), jnp.bfloat16),
        scratch_shapes=[pltpu.VMEM((L, Q_BLK), jnp.float32)],
        compiler_params=_params(2),
        name="pattn",
    )(z, z, z, z, kbf, vt, kia, kib, q_norm_w.reshape(1, ATT_D), bias_tiles)


def _prompt_bias_tiles(rel_bias):
    o = jnp.arange(N_BIAS_TILES, dtype=jnp.int32)[:, None, None]
    s = jnp.arange(Q_BLK, dtype=jnp.int32)[None, :, None]
    t = jnp.arange(Q_BLK, dtype=jnp.int32)[None, None, :]
    tiles = _bias_lookup(rel_bias, _t5_bucket(jnp.maximum(Q_BLK * o + t - s, 0)))
    return jnp.moveaxis(tiles, -1, 1)


def _sattn_score_kernel(pt_ref, qi_ref, w_ref, iknew_ref, *rest):
    pages, s_ref = rest[:NPG], rest[NPG]
    pc = pl.program_id(1)
    last = pl.num_programs(1) - 1
    qi = qi_ref[0]
    w = w_ref[0]

    def head_sum(d):
        r = jnp.maximum(d, 0.0) * w
        return jnp.sum(r.reshape(T_PAD, IDX_HEADS, r.shape[-1]), axis=1)

    @pl.when(pc < last)
    def _():
        for i in range(NPG):
            d = jnp.dot(qi, pages[i][0, 0].astype(jnp.bfloat16), preferred_element_type=jnp.float32)
            s_ref[0, :, pl.ds(PAGE * i, PAGE)] = head_sum(d)

    @pl.when(pc == last)
    def _():
        s_ref[0] = jnp.full((T_PAD, CHUNK_KEYS), MASK_VALUE, jnp.float32)
        d = lax.dot_general(qi, iknew_ref[0].astype(jnp.bfloat16), _NT, preferred_element_type=jnp.float32)
        s_ref[0, :, pl.ds(0, PAGE)] = head_sum(d)


def _sattn_score_call(page_table, qi_rows, w_col, ik_new, cache_ik_t, layer):
    B, n_pages = page_table.shape
    n_chunks = n_pages // NPG
    page_spec = lambda i: pl.BlockSpec(
        (1, 1, IDX_D, PAGE), lambda b, pc, pt: (layer, pt[b, jnp.minimum(pc * NPG + i, n_pages - 1)], 0, 0))
    per_b = lambda shape: pl.BlockSpec((1,) + shape, lambda b, pc, pt: (b, 0, 0))
    grid_spec = pltpu.PrefetchScalarGridSpec(
        num_scalar_prefetch=1, grid=(B, n_chunks + 1),
        in_specs=[per_b((T_PAD * IDX_HEADS, IDX_D)), per_b((T_PAD * IDX_HEADS, 1)), per_b((PAGE, IDX_D))]
        + [page_spec(i) for i in range(NPG)],
        out_specs=pl.BlockSpec((1, T_PAD, CHUNK_KEYS), lambda b, pc, pt: (b, 0, pc)))
    return pl.pallas_call(
        _sattn_score_kernel,
        grid_spec=grid_spec,
        out_shape=jax.ShapeDtypeStruct((B, T_PAD, (n_chunks + 1) * CHUNK_KEYS), jnp.float32),
        compiler_params=_params(2),
        name="sattn_score",
    )(page_table, qi_rows, w_col, ik_new, *([cache_ik_t] * NPG))


def _sattn_kernel(pt_ref, s_ref, q_ref, g_ref, qnw_ref, bias_ref, knew_ref, vnew_ref, *rest, topk, n_new, past):
    kp, vp = rest[:NPG], rest[NPG:2 * NPG]
    o_ref, thr_ref, tie_ref, qn_ref, m_ref, l_ref, acc_ref = rest[2 * NPG:]
    pc = pl.program_id(1)
    last = pl.num_programs(1) - 1
    G = ATT_HEADS // KV_HEADS

    def admissible(col0, width):
        pos = lax.broadcasted_iota(jnp.int32, (T_PAD, width), 1) + col0
        t = lax.broadcasted_iota(jnp.int32, (T_PAD, width), 0)
        return (pos < past) | ((pos - past <= t) & (pos - past < n_new))

    @pl.when(pc == 0)
    def _():
        sc = s_ref[0]
        n_all = sc.shape[-1]
        adm0 = admissible(0, n_all)
        key = jnp.where(adm0, _sortable_key(sc), jnp.int32(INT_MIN))
        thr = _kth_largest_key(key, topk, 1)
        thr_ref[...] = jnp.broadcast_to(thr, thr_ref.shape)
        tie_ref[...] = jnp.full(tie_ref.shape, n_all, jnp.int32)
        cnt_ge = _lanes_sum(jnp.where(key >= thr, 1.0, 0.0))
        tied = (cnt_ge > topk) & (thr > jnp.int32(INT_MIN))

        @pl.when(jnp.max(jnp.where(tied, 1.0, 0.0)) > 0.0)
        def _():
            eq = key == thr
            pos = lax.broadcasted_iota(jnp.int32, key.shape, 1)
            need = topk - _lanes_sum(jnp.where(key > thr, 1.0, 0.0))
            nbits = n_all.bit_length()

            def it(n, j_sel):
                cand = j_sel | jnp.left_shift(jnp.int32(1), nbits - 1 - n)
                c = _lanes_sum(jnp.where(eq & (pos < cand), 1.0, 0.0))
                return jnp.where(c <= need, cand, j_sel)

            j_sel = lax.fori_loop(0, nbits, it, jnp.zeros((T_PAD, 1), jnp.int32))
            tie_ref[...] = jnp.broadcast_to(j_sel, tie_ref.shape)

        q = q_ref[0]
        ms = jnp.mean(q * q, axis=-1, keepdims=True)
        qn_ref[...] = (q * lax.rsqrt(ms + EPS) * qnw_ref[...]).astype(jnp.bfloat16)
        m_ref[...] = jnp.full(m_ref.shape, MASK_VALUE, jnp.float32)
        l_ref[...] = jnp.zeros(l_ref.shape, jnp.float32)
        acc_ref[...] = jnp.zeros(acc_ref.shape, jnp.float32)

    def attend(get_k, get_v, width):
        col0 = pl.multiple_of(pc * CHUNK_KEYS, CHUNK_KEYS)
        sc = s_ref[0, :, pl.ds(col0, width)]
        adm = admissible(col0, width)
        key = jnp.where(adm, _sortable_key(sc), jnp.int32(INT_MIN))
        pos = lax.broadcasted_iota(jnp.int32, (T_PAD, width), 1) + col0
        thr = thr_ref[:, :1]
        sel8 = ((key > thr) | ((key == thr) & (pos < tie_ref[:, :1]))) & adm
        sel = jnp.concatenate([sel8] * G, axis=0)
        for kv in range(KV_HEADS):
            kk = get_k(kv).astype(jnp.bfloat16)
            x = lax.dot_general(qn_ref[kv], kk, _NT, preferred_element_type=jnp.float32) * (ATT_D ** -0.5)
            x = jnp.where(sel, x + bias_ref[kv, :, pl.ds(0, width)], MASK_VALUE)
            m_old = m_ref[kv]
            m_new = jnp.maximum(m_old, jnp.max(x, axis=-1, keepdims=True))
            p = jnp.where(sel, jnp.exp(x - m_new), 0.0)
            alpha = jnp.exp(m_old - m_new)
            l_ref[kv] = alpha * l_ref[kv] + jnp.sum(p, axis=-1, keepdims=True)
            acc_ref[kv] = alpha * acc_ref[kv] + jnp.dot(
                p.astype(jnp.bfloat16), get_v(kv).astype(jnp.bfloat16),
                preferred_element_type=jnp.float32)
            m_ref[kv] = m_new

    @pl.when(pc < last)
    def _():
        attend(lambda kv: jnp.concatenate([kp[i][0, 0, pl.ds(kv, PAGE, stride=KV_HEADS), :] for i in range(NPG)], axis=0),
               lambda kv: jnp.concatenate([vp[i][0, 0, pl.ds(kv, PAGE, stride=KV_HEADS), :] for i in range(NPG)], axis=0),
               CHUNK_KEYS)

    @pl.when(pc == last)
    def _():
        attend(lambda kv: knew_ref[0, :, pl.ds(ATT_D * kv, ATT_D)],
               lambda kv: vnew_ref[0, :, pl.ds(ATT_D * kv, ATT_D)], PAGE)
        o_ref[0] = acc_ref[...] / l_ref[...] * _silu(g_ref[0])


def _sattn_call(page_table, scores, q_rows, g_rows, q_norm_w, bias, k_new, v_new, cache_k, cache_v, layer,
                *, topk, n_new):
    B, n_pages = page_table.shape
    n_chunks = n_pages // NPG
    kvw = KV_HEADS * ATT_D
    rows = (ATT_HEADS // KV_HEADS) * T_PAD
    n_keys = (n_chunks + 1) * CHUNK_KEYS
    page_spec = lambda i: pl.BlockSpec(
        (1, 1, PAGE * KV_HEADS, ATT_D),
        lambda b, pc, pt: (layer, pt[b, jnp.minimum(pc * NPG + i, n_pages - 1)], 0, 0))
    per_b = lambda shape: pl.BlockSpec((1,) + shape, lambda b, pc, pt: (b,) + (0,) * len(shape))
    grid_spec = pltpu.PrefetchScalarGridSpec(
        num_scalar_prefetch=1, grid=(B, n_chunks + 1),
        in_specs=[per_b((T_PAD, n_keys)), per_b((KV_HEADS, rows, ATT_D)), per_b((KV_HEADS, rows, ATT_D)),
                  pl.BlockSpec((1, ATT_D), lambda b, pc, pt: (0, 0)),
                  pl.BlockSpec((KV_HEADS, rows, CHUNK_KEYS), lambda b, pc, pt: (0, 0, pc)),
                  per_b((PAGE, kvw)), per_b((PAGE, kvw))]
        + [page_spec(i) for i in range(NPG)] + [page_spec(i) for i in range(NPG)],
        out_specs=per_b((KV_HEADS, rows, ATT_D)),
        scratch_shapes=[pltpu.VMEM((T_PAD, LANES), jnp.int32), pltpu.VMEM((T_PAD, LANES), jnp.int32),
                        pltpu.VMEM((KV_HEADS, rows, ATT_D), jnp.bfloat16),
                        pltpu.VMEM((KV_HEADS, rows, 1), jnp.float32), pltpu.VMEM((KV_HEADS, rows, 1), jnp.float32),
                        pltpu.VMEM((KV_HEADS, rows, ATT_D), jnp.float32)])
    return pl.pallas_call(
        functools.partial(_sattn_kernel, topk=topk, n_new=n_new, past=n_pages * PAGE),
        grid_spec=grid_spec,
        out_shape=jax.ShapeDtypeStruct((B, KV_HEADS, rows, ATT_D), jnp.float32),
        compiler_params=_params(2),
        name="sattn",
    )(page_table, scores, q_rows, g_rows, q_norm_w.reshape(1, ATT_D), bias, k_new, v_new,
      *([cache_k] * NPG), *([cache_v] * NPG))


def _sample_bias(rel_bias, past, n_keys):
    G = ATT_HEADS // KV_HEADS
    t = jnp.arange(T_PAD, dtype=jnp.int32)[:, None]
    pos = jnp.arange(n_keys, dtype=jnp.int32)[None, :]
    tab = _bias_lookup(rel_bias, _t5_bucket(jnp.maximum(past + t - pos, 0)))
    return jnp.moveaxis(tab, -1, 0).reshape(KV_HEADS, G * T_PAD, n_keys)


def _to_head_rows(a):
    B = a.shape[0]
    G = ATT_HEADS // KV_HEADS
    a = a.reshape(B, T_PAD, KV_HEADS, G, ATT_D)
    return jnp.transpose(a, (0, 2, 3, 1, 4)).reshape(B, KV_HEADS, G * T_PAD, ATT_D)


def _from_head_rows(a):
    B = a.shape[0]
    G = ATT_HEADS // KV_HEADS
    a = a.reshape(B, KV_HEADS, G, T_PAD, ATT_D)
    return jnp.transpose(a, (0, 3, 1, 2, 4)).reshape(B, T_PAD, ATT_HEADS * ATT_D)


def _layer(x, lp, states, attend, *, l_valid, tm, chunk_hg, chunk_ssd):
    B, L, _ = x.shape
    hg_s0, ssm_s0, conv_s0, s5_re0, s5_im0 = states
    x2d = x.reshape(B * L, D_MODEL)
    z2d = _inproj(x2d, lp['norm_w'], lp['w_in'], tm)
    z = z2d.reshape(B, L, D_IN_PAD)

    o_hg, hg_s = _hgrn_call(z, lp['lb'], lp['hg_norm_w'], hg_s0, chunk=chunk_hg, l_valid=l_valid)

    conv8 = jnp.pad(conv_s0, ((0, 0), (SUBLANES - (SSM_CONV - 1), 0), (0, 0)))
    y_ssm, ssm_s, cx, cb, cc = _ssd_call(z, lp['conv_w'], lp['conv_b'], conv8, lp['dtb_pad'], lp['alog_pad'],
                                         lp['dskip_pad'], lp['ssm_norm_w'], ssm_s0, chunk=chunk_ssd, l_valid=l_valid)
    tail = SUBLANES - (SSM_CONV - 1)
    conv_s = jnp.concatenate([cx[:, tail:], cb[:, tail:], cc[:, tail:]], axis=-1)

    h5, s5_re, s5_im = _s5_call(z, *lp['s5_blocks'], lp['a_re'], lp['a_im'], lp['log_dt'], lp['s5_d'],
                                s5_re0.reshape(B, 1, -1), s5_im0.reshape(B, 1, -1), l_valid=l_valid)
    o5 = _glu_call(h5.reshape(B * L, BRANCH), lp['glu_w'], lp['glu_b'], z2d, tm=tm)

    o_att, k, v, ik = attend(z)

    y = _outproj((o_hg.reshape(B * L, BRANCH), y_ssm.reshape(B * L, BRANCH), o5, o_att.reshape(B * L, BRANCH)),
                 lp['w_out'], lp['layer'], x2d, tm)
    st = (k[:, :l_valid].reshape(B, l_valid, KV_HEADS, ATT_D), v[:, :l_valid].reshape(B, l_valid, KV_HEADS, ATT_D),
          ik[:, :l_valid], hg_s, ssm_s, conv_s,
          s5_re.reshape(B, S5_GROUPS, S5_STATE), s5_im.reshape(B, S5_GROUPS, S5_STATE))
    return y.reshape(B, L, D_MODEL), st


def kernel(x_prompt, x_sample, cache_k, cache_v, cache_idx_k, state_hgrn, state_ssm, state_conv, state_s5_re, state_s5_im, page_table, norm_w, w_in, w_out, hg_lb_logits, hg_norm_w, ssm_conv_w, ssm_conv_b, ssm_dt_bias, ssm_a_log, ssm_d, ssm_norm_w, s5_a_re, s5_a_im, s5_log_dt, s5_b_re, s5_b_im, s5_c_re, s5_c_im, s5_d, s5_glu_w, s5_glu_b, att_q_norm, att_k_norm, rel_bias):
    f32 = jnp.float32
    bp, lp_len, _ = x_prompt.shape
    bs, ls, _ = x_sample.shape
    n_pages = page_table.shape[1]
    past = n_pages * PAGE
    n_keys = (n_pages // NPG + 1) * CHUNK_KEYS

    sm = jax.nn.softmax(hg_lb_logits.astype(f32), axis=0)
    lower = jnp.cumsum(sm, axis=0) - sm[0]

    w_out_bf16 = w_out.astype(jnp.bfloat16)
    w_in_t = jnp.swapaxes(w_in, 1, 2)
    cache_ik_t = jnp.swapaxes(cache_idx_k, 2, 3)
    page_rows = lambda c: c.reshape(c.shape[0], c.shape[1], PAGE * KV_HEADS, ATT_D)
    cache_k2, cache_v2 = page_rows(cache_k), page_rows(cache_v)
    bias_tiles = _prompt_bias_tiles(rel_bias)
    bias_sample = _sample_bias(rel_bias, past, n_keys)
    topk_p = min(TOPK_MAX, lp_len // 4)
    topk_s = min(TOPK_MAX, (past + ls) // 4)

    def head_lanes(v):
        return jnp.zeros((1, LANES), f32).at[0, SMALL_DT:SMALL_DT + SSM_HEADS].set(v)

    xs_pad = jnp.pad(x_sample, ((0, 0), (0, SAMPLE_ROWS - ls), (0, 0)))
    pad_page = lambda a: jnp.pad(a, ((0, 0), (0, PAGE - a.shape[1]), (0, 0)))

    yp, ys = x_prompt, xs_pad
    new_p = [[] for _ in range(8)]
    new_s = [[] for _ in range(8)]
    for l in range(DEPTH):
        lp = {'norm_w': norm_w[l], 'w_in': _pack_w_in(w_in_t, l), 'w_out': w_out_bf16, 'layer': l,
              'lb': lower[l], 'hg_norm_w': hg_norm_w[l],
              'conv_w': ssm_conv_w[l], 'conv_b': ssm_conv_b[l], 'dtb_pad': head_lanes(ssm_dt_bias[l]),
              'alog_pad': head_lanes(ssm_a_log[l]), 'dskip_pad': head_lanes(ssm_d[l]), 'ssm_norm_w': ssm_norm_w[l],
              's5_blocks': _s5_block_diag(s5_b_re[l], s5_b_im[l], s5_c_re[l], s5_c_im[l]),
              'a_re': s5_a_re[l].reshape(1, -1), 'a_im': s5_a_im[l].reshape(1, -1),
              'log_dt': jnp.repeat(s5_log_dt[l], S5_STATE).reshape(1, -1), 's5_d': s5_d[l].reshape(1, -1),
              'glu_w': s5_glu_w[l].astype(jnp.bfloat16), 'glu_b': s5_glu_b[l]}

        def attend_prompt(z):
            k, v, ik, kbf, vt, kia, kib = _kvprep_call(z, att_k_norm[l], tr=256, transposed=True)
            o = _pattn_call(z, kbf, vt, kia, kib, att_q_norm[l], bias_tiles, topk=topk_p)
            return o, k, v, ik

        def attend_sample(z):
            k, v, ik = _kvprep_call(z, att_k_norm[l], tr=SAMPLE_ROWS, transposed=False)
            z8 = z[:, :T_PAD]
            qi_rows = z8[..., OFF_IX_Q:OFF_IX_Q + IDX_HEADS * IDX_D].reshape(bs, T_PAD * IDX_HEADS, IDX_D)
            w_col = z8[..., OFF_SMALL + SMALL_IXW:OFF_SMALL + SMALL_IXW + IDX_HEADS] * (1.0 / 32.0)
            scores = _sattn_score_call(page_table, qi_rows.astype(jnp.bfloat16),
                                       w_col.reshape(bs, T_PAD * IDX_HEADS, 1), pad_page(ik), cache_ik_t, l)
            o = _sattn_call(page_table, scores, _to_head_rows(z8[..., OFF_AT_Q:OFF_AT_Q + BRANCH]),
                            _to_head_rows(z8[..., OFF_AT_G:OFF_AT_G + BRANCH]), att_q_norm[l], bias_sample,
                            pad_page(k), pad_page(v), cache_k2, cache_v2, l, topk=topk_s, n_new=ls)
            o = jnp.pad(_from_head_rows(o), ((0, 0), (0, SAMPLE_ROWS - T_PAD), (0, 0)))
            return o.astype(jnp.bfloat16), k, v, ik

        zero_states = (jnp.zeros((bp, HG_HEADS, LANES, LANES), f32),
                       jnp.zeros((bp, SSM_HEADS, SSM_P, SSM_STATE), f32),
                       jnp.zeros((bp, SSM_CONV - 1, SSM_CONV_DIM), f32),
                       jnp.zeros((bp, S5_GROUPS, S5_STATE), f32),
                       jnp.zeros((bp, S5_GROUPS, S5_STATE), f32))
        yp, st_p = _layer(yp, lp, zero_states, attend_prompt, l_valid=lp_len, tm=512, chunk_hg=64, chunk_ssd=128)
        samp_states = (state_hgrn[l], state_ssm[l], state_conv[l], state_s5_re[l], state_s5_im[l])
        ys, st_s = _layer(ys, lp, samp_states, attend_sample, l_valid=ls, tm=bs * SAMPLE_ROWS,
                          chunk_hg=SAMPLE_ROWS, chunk_ssd=SAMPLE_ROWS)
        for i in range(8):
            new_p[i].append(st_p[i])
            new_s[i].append(st_s[i])
    pk, pv, pik, phg, pssm, pconv, ps5r, ps5i = [jnp.stack(a) for a in new_p]
    sk, sv, sik, shg, sssm, sconv, ss5r, ss5i = [jnp.stack(a) for a in new_s]
    return (yp, ys[:, :ls], pk, pv, pik, phg, pssm, pconv, ps5r, ps5i, sk, sv, sik, shg, sssm, sconv, ss5r, ss5i)
```

```python
import functools
import math

import jax
import jax.numpy as jnp
from jax import lax
from jax.experimental import pallas as pl
from jax.experimental.pallas import tpu as pltpu

D_MODEL = 4096
DEPTH = 2
BRANCH = D_MODEL // 4
HG_HEADS = 8
SSM_HEADS = 16
SSM_GROUPS = 4
SSM_STATE = 128
SSM_CONV = 4
SSM_CONV_DIM = BRANCH + 2 * SSM_GROUPS * SSM_STATE
SSM_HPG = SSM_HEADS // SSM_GROUPS
SSM_P = 64
S5_GROUPS = 64
S5_STATE = 64
S5_BLK_CH = 128
S5_BLK_ST = 512
ATT_D = 128
ATT_HEADS = 8
KV_HEADS = 2
IDX_HEADS = 16
IDX_D = 64
TOPK_MAX = 256
Q_BLK = 128
PAGE = 128
REL_BUCKETS = 32
REL_MAX_DIST = 1024
EPS = 1e-6
MASK_VALUE = -1e30
F_FLOOR = 1e-30
INT_MIN = -2 ** 31

LANES = 128
SUBLANES = 8
V7X_VMEM_LIMIT = 56 * 1024 * 1024

_SRC_DT, _SRC_S5U, _SRC_ATK, _SRC_ATG, _SRC_IXK, D_IN_SRC = 7168, 7184, 10256, 10768, 12816, 12896

OFF_HG_Q, OFF_HG_F, OFF_HG_I, OFF_HG_G = 0, 1024, 2048, 3072
OFF_SM_Z, OFF_SM_XBC = 4096, 5120
OFF_S5_U, OFF_S5_G = 7168, 8192
OFF_AT_Q, OFF_AT_G, OFF_IX_Q = 9216, 10240, 11264
OFF_AT_KV = 12288
OFF_SMALL = 12800
SMALL_IXW, SMALL_DT = 64, 80
D_IN_PAD = 13312

PACK_TN = 1024
PACK_TK = 1024
_PACK_LAST = D_IN_PAD // PACK_TN - 1
_PACK_ROW0 = (tuple(range(0, _SRC_DT, PACK_TN)) + tuple(range(_SRC_S5U, _SRC_ATK, PACK_TN))
              + tuple(range(_SRC_ATG, _SRC_IXK, PACK_TN)) + (_SRC_ATK,))
N_KV_ROWS = _SRC_ATG - _SRC_ATK
N_IX_ROWS = D_IN_SRC - _SRC_IXK
N_DT_ROWS = _SRC_S5U - _SRC_DT
NORM_ROWS = 128
S5_BULK_STEPS = 32

N_BIAS_TILES = 9
SSD_GPB = 2
HG_SUB = 16
HG_HPB = 8
HG_ROW_BLOCK = 512
N_CAUSAL_VARIANTS = 4
NPG = 32
T_PAD = 8
CHUNK_KEYS = NPG * PAGE
SAMPLE_ROWS = 16

_TN = (((0,), (0,)), ((), ()))
_NT = (((1,), (1,)), ((), ()))


def _params(n_axes):
    return pltpu.CompilerParams(dimension_semantics=("arbitrary",) * n_axes, vmem_limit_bytes=V7X_VMEM_LIMIT)


def _bf(x):
    return x.astype(jnp.bfloat16)


def _dot(a, b, dims=None):
    if dims is None:
        return jnp.dot(_bf(a), _bf(b), preferred_element_type=jnp.float32)
    return lax.dot_general(_bf(a), _bf(b), dims, preferred_element_type=jnp.float32)


def _sigmoid(x):
    return 1.0 / (1.0 + jnp.exp(-x))


def _silu(x):
    return x * _sigmoid(x)


def _softplus(x):
    return jnp.maximum(x, 0.0) + jnp.log(1.0 + jnp.exp(-jnp.abs(x)))


def _gelu_tanh(x):
    return 0.5 * x * (1.0 + jnp.tanh(math.sqrt(2.0 / math.pi) * (x + 0.044715 * (x * x * x))))


def _cumsum_rows(x):
    n = x.shape[0]
    row = lax.broadcasted_iota(jnp.int32, x.shape, 0)
    sh = 1
    while sh < n:
        x = x + jnp.where(row >= sh, pltpu.roll(x, sh, axis=0), 0.0)
        sh *= 2
    return x


def _split3(x):
    hi = x.astype(jnp.bfloat16)
    r = x - hi.astype(jnp.float32)
    mid = r.astype(jnp.bfloat16)
    lo = (r - mid.astype(jnp.float32)).astype(jnp.bfloat16)
    return hi, mid, lo


def _select_lanes_as_rows(sel, x):
    out = None
    for part in _split3(x):
        t = lax.dot_general(sel, part, _NT, preferred_element_type=jnp.float32)
        out = t if out is None else out + t
    return out


def _cmul(ar, ai, br, bi):
    return ar * br - ai * bi, ar * bi + ai * br


def _pack_kernel(off_ref, w_ref, ix_ref, dt_ref, o_ref):
    j = pl.program_id(0)

    @pl.when(j < _PACK_LAST)
    def _():
        o_ref[...] = w_ref[0].astype(o_ref.dtype)

    @pl.when(j == _PACK_LAST)
    def _():
        o_ref[pl.ds(0, N_KV_ROWS), :] = w_ref[0, pl.ds(0, N_KV_ROWS), :].astype(o_ref.dtype)
        o_ref[pl.ds(N_KV_ROWS, N_IX_ROWS), :] = ix_ref[0].astype(o_ref.dtype)
        o_ref[pl.ds(N_KV_ROWS + N_IX_ROWS, N_DT_ROWS), :] = dt_ref[0].astype(o_ref.dtype)
        used = N_KV_ROWS + N_IX_ROWS + N_DT_ROWS
        o_ref[pl.ds(used, PACK_TN - used), :] = jnp.zeros((PACK_TN - used, PACK_TK), o_ref.dtype)


def _pack_w_in(w_in_t, layer):
    assert all(r % SUBLANES == 0 for r in _PACK_ROW0)
    row0 = jnp.asarray([r // SUBLANES for r in _PACK_ROW0], jnp.int32)
    E = pl.Element
    fixed = lambda rows, row: pl.BlockSpec(
        (E(1), E(rows), E(PACK_TK)), lambda j, kc, off: (layer, row, kc * PACK_TK))
    grid_spec = pltpu.PrefetchScalarGridSpec(
        num_scalar_prefetch=1, grid=(D_IN_PAD // PACK_TN, D_MODEL // PACK_TK),
        in_specs=[pl.BlockSpec((E(1), E(PACK_TN), E(PACK_TK)),
                               lambda j, kc, off: (layer, off[j] * SUBLANES, kc * PACK_TK)),
                  fixed(N_IX_ROWS, _SRC_IXK), fixed(N_DT_ROWS, _SRC_DT)],
        out_specs=pl.BlockSpec((PACK_TN, PACK_TK), lambda j, kc, off: (j, kc)))
    return pl.pallas_call(
        _pack_kernel, grid_spec=grid_spec,
        out_shape=jax.ShapeDtypeStruct((D_IN_PAD, D_MODEL), jnp.bfloat16),
        compiler_params=_params(2),
        name="pack_w_in",
    )(row0, w_in_t, w_in_t, w_in_t)


def _inproj_kernel(x_ref, nw_ref, w_ref, z_ref, hn_ref):
    @pl.when(pl.program_id(1) == 0)
    def _():
        nw = nw_ref[...]
        for r0 in range(0, x_ref.shape[0], NORM_ROWS):
            xf = x_ref[pl.ds(r0, NORM_ROWS), :]
            ms = jnp.mean(xf * xf, axis=-1, keepdims=True)
            hn_ref[pl.ds(r0, NORM_ROWS), :] = (xf * lax.rsqrt(ms + EPS) * nw).astype(jnp.bfloat16)

    z_ref[...] = lax.dot_general(hn_ref[...], w_ref[...], _NT, preferred_element_type=jnp.float32)


def _inproj(x2d, norm_w, w_packed, tm, tn=1024):
    m = x2d.shape[0]
    return pl.pallas_call(
        _inproj_kernel,
        grid=(m // tm, D_IN_PAD // tn),
        in_specs=[pl.BlockSpec((tm, D_MODEL), lambda i, j: (i, 0)),
                  pl.BlockSpec((1, D_MODEL), lambda i, j: (0, 0)),
                  pl.BlockSpec((tn, D_MODEL), lambda i, j: (j, 0))],
        out_specs=pl.BlockSpec((tm, tn), lambda i, j: (i, j)),
        out_shape=jax.ShapeDtypeStruct((m, D_IN_PAD), jnp.float32),
        scratch_shapes=[pltpu.VMEM((tm, D_MODEL), jnp.bfloat16)],
        compiler_params=_params(2),
        name="inproj",
    )(x2d, norm_w.reshape(1, D_MODEL), w_packed)


def _outproj_kernel(m0_ref, m1_ref, m2_ref, m3_ref, w_ref, x_ref, y_ref):
    acc = x_ref[...]
    for i, m_ref in enumerate((m0_ref, m1_ref, m2_ref, m3_ref)):
        acc = acc + jnp.dot(m_ref[...], w_ref[pl.ds(BRANCH * i, BRANCH), :], preferred_element_type=jnp.float32)
    y_ref[...] = acc


def _outproj(mixed4, w_out_bf16, layer, x2d, tm, tn=1024):
    m = x2d.shape[0]
    mspec = pl.BlockSpec((tm, BRANCH), lambda i, j: (i, 0))
    return pl.pallas_call(
        _outproj_kernel,
        grid=(m // tm, D_MODEL // tn),
        in_specs=[mspec, mspec, mspec, mspec,
                  pl.BlockSpec((None, D_MODEL, tn), lambda i, j: (layer, 0, j)),
                  pl.BlockSpec((tm, tn), lambda i, j: (i, j))],
        out_specs=pl.BlockSpec((tm, tn), lambda i, j: (i, j)),
        out_shape=jax.ShapeDtypeStruct((m, D_MODEL), jnp.float32),
        compiler_params=_params(2),
        name="outproj",
    )(*mixed4, w_out_bf16, x2d)


def _hgrn_kernel(q_ref, f_ref, i_ref, g_ref, lb_ref, nw_ref, s0_ref, o_ref, s_out_ref, st_ref, *, chunk, l_valid):
    LB = q_ref.shape[1]
    C = chunk
    nblk = C // SUBLANES
    lb_i = pl.program_id(2)

    @pl.when(lb_i == 0)
    def _():
        for hh in range(HG_HPB):
            st_ref[hh] = s0_ref[0, hh].T

    nw = nw_ref[...]
    rowi = lax.broadcasted_iota(jnp.int32, (SUBLANES, LANES), 0)

    def head_chunk(hh, t0):
        lanes = pl.ds(LANES * hh, LANES)
        lb = lb_ref[0, :, lanes]
        fp = f_ref[0, pl.ds(t0, C), lanes]
        qp = q_ref[0, pl.ds(t0, C), lanes]
        v = i_ref[0, pl.ds(t0, C), lanes]
        gp = g_ref[0, pl.ds(t0, C), lanes]
        fg = lb + (1.0 - lb) * _sigmoid(fp)
        logf = jnp.log(jnp.maximum(fg, F_FLOOR))
        kk = (1.0 - lb) * _sigmoid(-fp)
        if l_valid < LB * pl.num_programs(2):
            rows = lax.broadcasted_iota(jnp.int32, (C, LANES), 0) + t0 + lb_i * LB
            logf = jnp.where(rows < l_valid, logf, 0.0)
            kk = jnp.where(rows < l_valid, kk, 0.0)
        qq = _silu(qp)
        G = _cumsum_rows(logf)
        st = st_ref[hh]
        o_inter = _dot(qq * jnp.exp(G), st, _NT)
        acc = [None] * nblk
        qb = [qq[SUBLANES * tb:SUBLANES * (tb + 1)] for tb in range(nblk)]
        Gb = [G[SUBLANES * tb:SUBLANES * (tb + 1)] for tb in range(nblk)]
        for s in range(C):
            sb = s // SUBLANES
            gs = G[s:s + 1]
            ks = kk[s:s + 1]
            vs = v[s:s + 1]
            for tb in range(sb, (s // HG_SUB + 1) * (HG_SUB // SUBLANES)):
                d = Gb[tb] - gs
                if tb == sb:
                    d = jnp.where(rowi >= s - SUBLANES * sb, d, MASK_VALUE)
                w = jnp.sum(qb[tb] * ks * jnp.exp(d), axis=-1, keepdims=True)
                contrib = w * vs
                acc[tb] = contrib if acc[tb] is None else acc[tb] + contrib
        for j in range(1, C // HG_SUB):
            r0 = HG_SUB * j
            gb = G[r0 - 1:r0]
            qj = qq[r0:r0 + HG_SUB] * jnp.exp(G[r0:r0 + HG_SUB] - gb)
            kj = kk[:r0] * jnp.exp(gb - G[:r0])
            oj = _dot(_dot(qj, kj, _NT), v[:r0])
            for half in range(HG_SUB // SUBLANES):
                tb = r0 // SUBLANES + half
                acc[tb] = acc[tb] + oj[SUBLANES * half:SUBLANES * (half + 1)]
        o = o_inter + jnp.concatenate(acc, axis=0)
        g_last = G[C - 1:C]
        kd = kk * jnp.exp(g_last - G)
        st_ref[hh] = jnp.exp(g_last) * st + _dot(v, kd, _TN)
        ms = jnp.mean(o * o, axis=-1, keepdims=True)
        on = o * lax.rsqrt(ms + EPS) * nw
        o_ref[0, pl.ds(t0, C), lanes] = (on * _silu(gp)).astype(o_ref.dtype)

    def body(c, carry):
        t0 = pl.multiple_of(c * C, C)
        for hh in range(HG_HPB):
            head_chunk(hh, t0)
        return carry

    lax.fori_loop(0, LB // C, body, 0)

    @pl.when(lb_i == pl.num_programs(2) - 1)
    def _():
        for hh in range(HG_HPB):
            s_out_ref[0, hh] = st_ref[hh].T


def _hgrn_call(z, lb, hg_norm_w, s0, *, chunk, l_valid):
    B, L, _ = z.shape
    W = HG_HPB * LANES
    LB = min(L, HG_ROW_BLOCK)
    zspec = lambda off: pl.BlockSpec((1, LB, W), lambda b, h, r: (b, r, off // W + h))
    return pl.pallas_call(
        functools.partial(_hgrn_kernel, chunk=chunk, l_valid=l_valid),
        grid=(B, HG_HEADS // HG_HPB, L // LB),
        in_specs=[zspec(OFF_HG_Q), zspec(OFF_HG_F), zspec(OFF_HG_I), zspec(OFF_HG_G),
                  pl.BlockSpec((1, 1, W), lambda b, h, r: (h, 0, 0)),
                  pl.BlockSpec((1, LANES), lambda b, h, r: (0, 0)),
                  pl.BlockSpec((1, HG_HPB, LANES, LANES), lambda b, h, r: (b, h, 0, 0))],
        out_specs=[pl.BlockSpec((1, LB, W), lambda b, h, r: (b, r, h)),
                   pl.BlockSpec((1, HG_HPB, LANES, LANES), lambda b, h, r: (b, h, 0, 0))],
        out_shape=[jax.ShapeDtypeStruct((B, L, BRANCH), jnp.bfloat16),
                   jax.ShapeDtypeStruct((B, HG_HEADS, LANES, LANES), jnp.float32)],
        scratch_shapes=[pltpu.VMEM((HG_HPB, LANES, LANES), jnp.float32)],
        compiler_params=_params(3),
        name="hgrn",
    )(z, z, z, z, lb.reshape(HG_HEADS // HG_HPB, 1, W), hg_norm_w.reshape(1, LANES), s0)


def _ssd_kernel(xs_ref, b_ref, c_ref, zg_ref, sm_ref, wx_ref, wb_ref, wc_ref, bx_ref, bb_ref, bc_ref,
                cx_ref, cb_ref, cc_ref, dtb_ref, alog_ref, dskip_ref, nw_ref, s0_ref,
                y_ref, s_out_ref, ox_ref, ob_ref, oc_ref,
                ax_ref, ab_ref, ac_ref, hx_ref, hb_ref, hc_ref, sp_ref, *, chunk, l_valid, conv_rows):
    L = xs_ref.shape[1]
    C = chunk
    R = conv_rows
    gb = pl.program_id(1)
    n_heads = SSD_GPB * SSM_HPG
    n_pairs = n_heads // 2

    def conv(src_ref, head_ref, cst_ref, w_ref, bias_ref, act_ref, out_state_ref):
        head_ref[pl.ds(0, SUBLANES), :] = cst_ref[0]
        head_ref[pl.ds(SUBLANES, R), :] = src_ref[0, pl.ds(0, R), :]
        w = w_ref[...]
        for r0 in range(0, L, R):
            acc = bias_ref[...]
            for j in range(SSM_CONV):
                if r0 == 0:
                    xj = head_ref[pl.ds(SUBLANES - j, R), :]
                else:
                    xj = src_ref[0, pl.ds(r0 - j, R), :]
                acc = acc + xj * w[SSM_CONV - 1 - j:SSM_CONV - j]
            act_ref[pl.ds(r0, R), :] = _silu(acc)
        if l_valid >= SUBLANES:
            out_state_ref[0] = src_ref[0, pl.ds(l_valid - SUBLANES, SUBLANES), :]
        else:
            out_state_ref[0] = head_ref[pl.ds(l_valid, SUBLANES), :]

    conv(xs_ref, hx_ref, cx_ref, wx_ref, bx_ref, ax_ref, ox_ref)
    conv(b_ref, hb_ref, cb_ref, wb_ref, bb_ref, ab_ref, ob_ref)
    conv(c_ref, hc_ref, cc_ref, wc_ref, bc_ref, ac_ref, oc_ref)

    for p in range(n_pairs):
        sp_ref[p] = s0_ref[0, 2 * p:2 * p + 2].reshape(2 * SSM_P, LANES)

    lane1 = lax.broadcasted_iota(jnp.int32, (1, LANES), 1)
    lane8 = lax.broadcasted_iota(jnp.int32, (SUBLANES, LANES), 1)
    row8 = lax.broadcasted_iota(jnp.int32, (SUBLANES, LANES), 0)
    lane0 = SMALL_DT + n_heads * gb
    sel = jnp.where((lane8 == lane0 + row8) & (row8 < n_heads), 1.0, 0.0).astype(jnp.bfloat16)
    a_all = -jnp.exp(alog_ref[...])
    dskip = [jnp.sum(jnp.where(lane1 == lane0 + j, dskip_ref[...], 0.0), axis=-1, keepdims=True)
             for j in range(n_heads)]
    pair_w = 2 * SSM_P
    lane_c = lax.broadcasted_iota(jnp.int32, (C, LANES), 1)
    first_half = lane_c < SSM_P
    row_p = lax.broadcasted_iota(jnp.int32, (2 * SSM_P, LANES), 0) < SSM_P
    tril = lax.broadcasted_iota(jnp.int32, (C, C), 0) >= lax.broadcasted_iota(jnp.int32, (C, C), 1)
    nw = nw_ref[...]

    def body(c, carry):
        t0 = pl.multiple_of(c * C, C)
        sm = sm_ref[0, pl.ds(t0, C), :]
        dt_all = _softplus(sm + dtb_ref[...])
        if l_valid < L:
            rows = lax.broadcasted_iota(jnp.int32, (C, LANES), 0) + t0
            dt_all = jnp.where(rows < l_valid, dt_all, 0.0)
        cum = _cumsum_rows(dt_all * a_all)
        cum_rows = _select_lanes_as_rows(sel, cum)
        col = [jnp.sum(jnp.where(lane_c == lane0 + j, cum, 0.0), axis=-1, keepdims=True) for j in range(n_heads)]
        dtc = [jnp.sum(jnp.where(lane_c == lane0 + j, dt_all, 0.0), axis=-1, keepdims=True) for j in range(n_heads)]
        for lg in range(SSD_GPB):
            bact = ab_ref[pl.ds(t0, C), pl.ds(LANES * lg, LANES)]
            cact = ac_ref[pl.ds(t0, C), pl.ds(LANES * lg, LANES)]
            cb = _dot(cact, bact, _NT)
            ys = []
            for p in range(SSM_HPG // 2):
                pp = lg * (SSM_HPG // 2) + p
                ja, jb = 2 * pp, 2 * pp + 1
                xs = ax_ref[pl.ds(t0, C), pl.ds(pair_w * pp, pair_w)]
                xdt = xs * jnp.where(first_half, dtc[ja], dtc[jb])
                dec_a = jnp.exp(jnp.where(tril, col[ja] - cum_rows[ja:ja + 1], MASK_VALUE))
                dec_b = jnp.exp(jnp.where(tril, col[jb] - cum_rows[jb:jb + 1], MASK_VALUE))
                y = jnp.where(first_half, _dot(cb * dec_a, xdt), _dot(cb * dec_b, xdt))
                sp = sp_ref[pp]
                y = y + _dot(cact, sp, _NT) * jnp.where(first_half, jnp.exp(col[ja]), jnp.exp(col[jb]))
                y = y + jnp.where(first_half, dskip[ja], dskip[jb]) * xs
                last_a, last_b = col[ja][C - 1:C], col[jb][C - 1:C]
                xw = xdt * jnp.where(first_half, jnp.exp(last_a - col[ja]), jnp.exp(last_b - col[jb]))
                sp_ref[pp] = jnp.where(row_p, jnp.exp(last_a), jnp.exp(last_b)) * sp + _dot(xw, bact, _TN)
                zg = zg_ref[0, pl.ds(t0, C), pl.ds(pair_w * pp, pair_w)]
                ys.append(y * _silu(zg))
            ms = sum(jnp.sum(y * y, axis=-1, keepdims=True) for y in ys) * (1.0 / (SSM_HPG * SSM_P))
            inv = lax.rsqrt(ms + EPS)
            for p in range(SSM_HPG // 2):
                pp = lg * (SSM_HPG // 2) + p
                y_ref[0, pl.ds(t0, C), pl.ds(pair_w * pp, pair_w)] = (
                    ys[p] * inv * nw[:, pair_w * pp:pair_w * (pp + 1)]).astype(y_ref.dtype)
        return carry

    lax.fori_loop(0, L // C, body, 0)
    for p in range(n_pairs):
        s_out_ref[0, 2 * p:2 * p + 2] = sp_ref[p].reshape(2, SSM_P, LANES)


def _ssd_call(z, conv_w, conv_b, conv_state8, dtb_pad, alog_pad, dskip_pad, norm_w, s0, *, chunk, l_valid):
    B, L, _ = z.shape
    assert l_valid >= SSM_CONV - 1
    xw = SSD_GPB * SSM_HPG * SSM_P
    bw = SSD_GPB * SSM_STATE
    nh = SSD_GPB * SSM_HPG
    conv_rows = min(L, 128)
    ox, ob = OFF_SM_XBC // xw, (OFF_SM_XBC + BRANCH) // bw
    oc = (OFF_SM_XBC + BRANCH + SSM_GROUPS * SSM_STATE) // bw
    wb0, wc0 = BRANCH // bw, (BRANCH + SSM_GROUPS * SSM_STATE) // bw
    in_specs = [
        pl.BlockSpec((1, L, xw), lambda b, g: (b, 0, ox + g)),
        pl.BlockSpec((1, L, bw), lambda b, g: (b, 0, ob + g)),
        pl.BlockSpec((1, L, bw), lambda b, g: (b, 0, oc + g)),
        pl.BlockSpec((1, L, xw), lambda b, g: (b, 0, OFF_SM_Z // xw + g)),
        pl.BlockSpec((1, L, LANES), lambda b, g: (b, 0, OFF_SMALL // LANES)),
        pl.BlockSpec((SSM_CONV, xw), lambda b, g: (0, g)),
        pl.BlockSpec((SSM_CONV, bw), lambda b, g: (0, wb0 + g)),
        pl.BlockSpec((SSM_CONV, bw), lambda b, g: (0, wc0 + g)),
        pl.BlockSpec((1, xw), lambda b, g: (0, g)),
        pl.BlockSpec((1, bw), lambda b, g: (0, wb0 + g)),
        pl.BlockSpec((1, bw), lambda b, g: (0, wc0 + g)),
        pl.BlockSpec((1, SUBLANES, xw), lambda b, g: (b, 0, g)),
        pl.BlockSpec((1, SUBLANES, bw), lambda b, g: (b, 0, wb0 + g)),
        pl.BlockSpec((1, SUBLANES, bw), lambda b, g: (b, 0, wc0 + g)),
        pl.BlockSpec((1, LANES), lambda b, g: (0, 0)),
        pl.BlockSpec((1, LANES), lambda b, g: (0, 0)),
        pl.BlockSpec((1, LANES), lambda b, g: (0, 0)),
        pl.BlockSpec((1, xw), lambda b, g: (0, g)),
        pl.BlockSpec((1, nh, SSM_P, LANES), lambda b, g: (b, g, 0, 0)),
    ]
    out_specs = [
        pl.BlockSpec((1, L, xw), lambda b, g: (b, 0, g)),
        pl.BlockSpec((1, nh, SSM_P, LANES), lambda b, g: (b, g, 0, 0)),
        pl.BlockSpec((1, SUBLANES, xw), lambda b, g: (b, 0, g)),
        pl.BlockSpec((1, SUBLANES, bw), lambda b, g: (b, 0, g)),
        pl.BlockSpec((1, SUBLANES, bw), lambda b, g: (b, 0, g)),
    ]
    f32 = jnp.float32
    out_shape = [
        jax.ShapeDtypeStruct((B, L, BRANCH), jnp.bfloat16),
        jax.ShapeDtypeStruct((B, SSM_HEADS, SSM_P, LANES), f32),
        jax.ShapeDtypeStruct((B, SUBLANES, BRANCH), f32),
        jax.ShapeDtypeStruct((B, SUBLANES, SSM_GROUPS * SSM_STATE), f32),
        jax.ShapeDtypeStruct((B, SUBLANES, SSM_GROUPS * SSM_STATE), f32),
    ]
    scratch = [pltpu.VMEM((L, xw), f32), pltpu.VMEM((L, bw), f32), pltpu.VMEM((L, bw), f32),
               pltpu.VMEM((SUBLANES + conv_rows, xw), f32), pltpu.VMEM((SUBLANES + conv_rows, bw), f32),
               pltpu.VMEM((SUBLANES + conv_rows, bw), f32),
               pltpu.VMEM((nh // 2, 2 * SSM_P, LANES), f32)]
    cbias = conv_b.reshape(1, -1)
    return pl.pallas_call(
        functools.partial(_ssd_kernel, chunk=chunk, l_valid=l_valid, conv_rows=conv_rows),
        grid=(B, SSM_GROUPS // SSD_GPB), in_specs=in_specs, out_specs=out_specs, out_shape=out_shape, scratch_shapes=scratch,
        compiler_params=_params(2),
        name="ssd",
    )(z, z, z, z, z, conv_w, conv_w, conv_w, cbias, cbias, cbias, conv_state8, conv_state8, conv_state8,
      dtb_pad, alog_pad, dskip_pad, norm_w.reshape(1, -1), s0)


def _s5_kernel(u_ref, bre_ref, bim_ref, cre_ref, cim_ref, are_ref, aim_ref, ldt_ref, d_ref, x0r_ref, x0i_ref,
               h_ref, xr_out_ref, xi_out_ref, bbr_ref, bbi_ref, xr_ref, xi_ref, pwr_ref, pwi_ref,
               *, nseg, seg_len):
    L = u_ref.shape[1]
    n_scan = nseg * seg_len
    row_blk = min(L, 256)
    a_re, a_im = are_ref[...], aim_ref[...]
    dt = jnp.exp(ldt_ref[...])
    mag = jnp.exp(a_re * dt)
    ab_re, ab_im = mag * jnp.cos(a_im * dt), mag * jnp.sin(a_im * dt)
    den = a_re * a_re + a_im * a_im
    nr = ab_re - 1.0
    coef_re = (nr * a_re + ab_im * a_im) / den
    coef_im = (ab_im * a_re - nr * a_im) / den

    NQ = S5_BLK_ST // LANES
    lq = lambda v, q: v[:, LANES * q:LANES * (q + 1)]
    abr = [lq(ab_re, q) for q in range(NQ)]
    abi = [lq(ab_im, q) for q in range(NQ)]

    for r0 in range(0, L, row_blk):
        u = u_ref[0, pl.ds(r0, row_blk), :]
        bu_re, bu_im = _dot(u, bre_ref[0]), _dot(u, bim_ref[0])
        bb_re = coef_re * bu_re - coef_im * bu_im
        bb_im = coef_re * bu_im + coef_im * bu_re
        for q in range(NQ):
            bbr_ref[q, pl.ds(r0, row_blk), :] = lq(bb_re, q)
            bbi_ref[q, pl.ds(r0, row_blk), :] = lq(bb_im, q)

    if n_scan < L:
        for q in range(NQ):
            xr_ref[q, pl.ds(n_scan, L - n_scan), :] = jnp.zeros((L - n_scan, LANES), jnp.float32)
            xi_ref[q, pl.ds(n_scan, L - n_scan), :] = jnp.zeros((L - n_scan, LANES), jnp.float32)

    def rows(i):
        return pl.ds(i, nseg, stride=seg_len) if nseg > 1 else pl.ds(i, 1)

    def scan(i, carry):
        out = []
        for q in range(NQ):
            xr, xi = carry[q]
            pr, pi = _cmul(abr[q], abi[q], xr, xi)
            xr, xi = pr + bbr_ref[q, rows(i), :], pi + bbi_ref[q, rows(i), :]
            xr_ref[q, rows(i), :] = xr
            xi_ref[q, rows(i), :] = xi
            out.append((xr, xi))
        return tuple(out)

    zero = jnp.zeros((nseg, LANES), jnp.float32)
    ends = lax.fori_loop(0, seg_len, scan, tuple((zero, zero) for _ in range(NQ)), unroll=4)

    for q in range(NQ):
        pwr_ref[q, pl.ds(0, 1), :] = abr[q]
        pwi_ref[q, pl.ds(0, 1), :] = abi[q]
    an_r, an_i = ab_re, ab_im
    n = 1
    while n < seg_len:
        m = min(n, seg_len - n)
        for q in range(NQ):
            pr, pi = _cmul(pwr_ref[q, pl.ds(0, m), :], pwi_ref[q, pl.ds(0, m), :], lq(an_r, q), lq(an_i, q))
            pwr_ref[q, pl.ds(n, m), :] = pr
            pwi_ref[q, pl.ds(n, m), :] = pi
        an_r, an_i = _cmul(an_r, an_i, an_r, an_i)
        n *= 2
    seg_r = [pwr_ref[q, pl.ds(seg_len - 1, 1), :] for q in range(NQ)]
    seg_i = [pwi_ref[q, pl.ds(seg_len - 1, 1), :] for q in range(NQ)]

    x0r, x0i = x0r_ref[0], x0i_ref[0]
    dskip = d_ref[...]
    c_r = [lq(x0r, q) for q in range(NQ)]
    c_i = [lq(x0i, q) for q in range(NQ)]
    for k in range(nseg):
        r0 = k * seg_len
        xr_q, xi_q = [], []
        for q in range(NQ):
            dr, di = _cmul(pwr_ref[q], pwi_ref[q], c_r[q], c_i[q])
            xr_q.append(xr_ref[q, pl.ds(r0, seg_len), :] + dr)
            xi_q.append(xi_ref[q, pl.ds(r0, seg_len), :] + di)
            pr, pi = _cmul(seg_r[q], seg_i[q], c_r[q], c_i[q])
            c_r[q], c_i[q] = pr + ends[q][0][k:k + 1], pi + ends[q][1][k:k + 1]
        if k == nseg - 1:
            for q in range(NQ):
                xr_out_ref[0, :, pl.ds(LANES * q, LANES)] = c_r[q]
                xi_out_ref[0, :, pl.ds(LANES * q, LANES)] = c_i[q]
        if n_scan == L:
            u = u_ref[0, pl.ds(r0, seg_len), :]
            y = (_dot(jnp.concatenate(xr_q, axis=-1), cre_ref[0]) - _dot(jnp.concatenate(xi_q, axis=-1), cim_ref[0])
                 + dskip * u)
            h_ref[0, pl.ds(r0, seg_len), :] = _gelu_tanh(y)
        else:
            for q in range(NQ):
                xr_ref[q, pl.ds(r0, seg_len), :] = xr_q[q]
                xi_ref[q, pl.ds(r0, seg_len), :] = xi_q[q]
    if n_scan < L:
        xr = jnp.concatenate([xr_ref[q] for q in range(NQ)], axis=-1)
        xi = jnp.concatenate([xi_ref[q] for q in range(NQ)], axis=-1)
        y = _dot(xr, cre_ref[0]) - _dot(xi, cim_ref[0]) + dskip * u_ref[0]
        h_ref[0] = _gelu_tanh(y)


def _s5seg_kernel(u_ref, bre_ref, bim_ref, cre_ref, cim_ref, are_ref, aim_ref, ldt_ref, d_ref, x0r_ref, x0i_ref,
                  h_ref, xr_out_ref, xi_out_ref, up_ref, bbr_ref, bbi_ref, xr_ref, xi_ref, pwr_ref, pwi_ref, yp_ref,
                  *, seg_len):
    NS = SUBLANES
    a_re, a_im = are_ref[...], aim_ref[...]
    dt = jnp.exp(ldt_ref[...])
    mag = jnp.exp(a_re * dt)
    ab_re, ab_im = mag * jnp.cos(a_im * dt), mag * jnp.sin(a_im * dt)
    den = a_re * a_re + a_im * a_im
    nr = ab_re - 1.0
    coef_re = (nr * a_re + ab_im * a_im) / den
    coef_im = (ab_im * a_re - nr * a_im) / den

    NQ = S5_BLK_ST // LANES
    lq = lambda v, q: v[:, LANES * q:LANES * (q + 1)]
    rep = lambda v: jnp.broadcast_to(v, (NS, LANES))
    abr = [rep(lq(ab_re, q)) for q in range(NQ)]
    abi = [rep(lq(ab_im, q)) for q in range(NQ)]
    blk = S5_BULK_STEPS
    n_blk = seg_len // blk

    def gather(i, carry):
        up_ref[pl.ds(pl.multiple_of(i * NS, NS), NS), :] = u_ref[0, pl.ds(i, NS, stride=seg_len), :]
        return carry

    lax.fori_loop(0, seg_len, gather, 0, unroll=8)

    for rb in range(n_blk):
        u = up_ref[pl.ds(rb * blk * NS, blk * NS), :]
        bu_re, bu_im = _dot(u, bre_ref[0]), _dot(u, bim_ref[0])
        bb_re = coef_re * bu_re - coef_im * bu_im
        bb_im = coef_re * bu_im + coef_im * bu_re
        for q in range(NQ):
            bbr_ref[q, pl.ds(rb * blk, blk)] = lq(bb_re, q).reshape(blk, NS, LANES)
            bbi_ref[q, pl.ds(rb * blk, blk)] = lq(bb_im, q).reshape(blk, NS, LANES)

    def scan(i, carry):
        out = []
        for q in range(NQ):
            xr, xi = carry[q]
            pr, pi = _cmul(abr[q], abi[q], xr, xi)
            xr, xi = pr + bbr_ref[q, i], pi + bbi_ref[q, i]
            xr_ref[q, i] = xr
            xi_ref[q, i] = xi
            out.append((xr, xi))
        return tuple(out)

    zero = jnp.zeros((NS, LANES), jnp.float32)
    ends = lax.fori_loop(0, seg_len, scan, tuple((zero, zero) for _ in range(NQ)), unroll=8)

    for q in range(NQ):
        pwr_ref[q, 0] = abr[q]
        pwi_ref[q, 0] = abi[q]
    an = [(abr[q], abi[q]) for q in range(NQ)]
    n = 1
    while n < seg_len:
        m = min(n, seg_len - n)
        for q in range(NQ):
            pr, pi = _cmul(pwr_ref[q, pl.ds(0, m)], pwi_ref[q, pl.ds(0, m)], an[q][0], an[q][1])
            pwr_ref[q, pl.ds(n, m)] = pr
            pwi_ref[q, pl.ds(n, m)] = pi
            an[q] = _cmul(an[q][0], an[q][1], an[q][0], an[q][1])
        n *= 2

    x0r, x0i = x0r_ref[0], x0i_ref[0]
    cs = []
    for q in range(NQ):
        seg_r, seg_i = pwr_ref[q, seg_len - 1][:1], pwi_ref[q, seg_len - 1][:1]
        cr, ci = [lq(x0r, q)], [lq(x0i, q)]
        for k in range(NS):
            pr, pi = _cmul(seg_r, seg_i, cr[k], ci[k])
            cr.append(pr + ends[q][0][k:k + 1])
            ci.append(pi + ends[q][1][k:k + 1])
        xr_out_ref[0, :, pl.ds(LANES * q, LANES)] = cr[NS]
        xi_out_ref[0, :, pl.ds(LANES * q, LANES)] = ci[NS]
        cs.append((jnp.concatenate(cr[:NS], axis=0), jnp.concatenate(ci[:NS], axis=0)))

    dskip = d_ref[...]
    for rb in range(n_blk):
        xr_q, xi_q = [], []
        for q in range(NQ):
            dr, di = _cmul(pwr_ref[q, pl.ds(rb * blk, blk)], pwi_ref[q, pl.ds(rb * blk, blk)], cs[q][0], cs[q][1])
            xr_q.append((xr_ref[q, pl.ds(rb * blk, blk)] + dr).reshape(blk * NS, LANES))
            xi_q.append((xi_ref[q, pl.ds(rb * blk, blk)] + di).reshape(blk * NS, LANES))
        u = up_ref[pl.ds(rb * blk * NS, blk * NS), :]
        y = (_dot(jnp.concatenate(xr_q, axis=-1), cre_ref[0]) - _dot(jnp.concatenate(xi_q, axis=-1), cim_ref[0])
             + dskip * u)
        yp_ref[pl.ds(rb * blk * NS, blk * NS), :] = _gelu_tanh(y)

    for k in range(NS):
        h_ref[0, pl.ds(k * seg_len, seg_len), :] = yp_ref[pl.ds(k, seg_len, stride=NS), :]


def _s5_call(z, bblk_re, bblk_im, cblk_re, cblk_im, a_re, a_im, log_dt_exp, d_flat, x0_re, x0_im, *, l_valid):
    B, L, _ = z.shape
    nb = bblk_re.shape[0]
    NQ = S5_BLK_ST // LANES
    f32 = jnp.float32
    if l_valid == L and L % (SUBLANES * S5_BULK_STEPS) == 0:
        seg_len = L // SUBLANES
        kern = functools.partial(_s5seg_kernel, seg_len=seg_len)
        tile = pltpu.VMEM((NQ, seg_len, SUBLANES, LANES), f32)
        scratch = [pltpu.VMEM((L, LANES), f32)] + [tile] * 6 + [pltpu.VMEM((L, LANES), f32)]
    else:
        kern = functools.partial(_s5_kernel, nseg=1, seg_len=l_valid)
        scratch = [pltpu.VMEM((NQ, L, LANES), f32)] * 4 + [pltpu.VMEM((NQ, l_valid, LANES), f32)] * 2
    vec = lambda w: pl.BlockSpec((1, w), lambda b, j: (0, j))
    st = pl.BlockSpec((1, 1, S5_BLK_ST), lambda b, j: (b, 0, j))
    return pl.pallas_call(
        kern,
        grid=(B, nb),
        in_specs=[pl.BlockSpec((1, L, S5_BLK_CH), lambda b, j: (b, 0, OFF_S5_U // S5_BLK_CH + j)),
                  pl.BlockSpec((1, S5_BLK_CH, S5_BLK_ST), lambda b, j: (j, 0, 0)),
                  pl.BlockSpec((1, S5_BLK_CH, S5_BLK_ST), lambda b, j: (j, 0, 0)),
                  pl.BlockSpec((1, S5_BLK_ST, S5_BLK_CH), lambda b, j: (j, 0, 0)),
                  pl.BlockSpec((1, S5_BLK_ST, S5_BLK_CH), lambda b, j: (j, 0, 0)),
                  vec(S5_BLK_ST), vec(S5_BLK_ST), vec(S5_BLK_ST), vec(S5_BLK_CH), st, st],
        out_specs=[pl.BlockSpec((1, L, S5_BLK_CH), lambda b, j: (b, 0, j)), st, st],
        out_shape=[jax.ShapeDtypeStruct((B, L, nb * S5_BLK_CH), jnp.float32),
                   jax.ShapeDtypeStruct((B, 1, nb * S5_BLK_ST), jnp.float32),
                   jax.ShapeDtypeStruct((B, 1, nb * S5_BLK_ST), jnp.float32)],
        scratch_shapes=scratch,
        compiler_params=_params(2),
        name="s5",
    )(z, bblk_re, bblk_im, cblk_re, cblk_im, a_re, a_im, log_dt_exp, d_flat, x0_re, x0_im)


def _s5_block_diag(b_re, b_im, c_re, c_im):
    G, P, Cc = b_re.shape
    nb = G // 8
    same = jnp.eye(8, dtype=bool)

    def bblk(b):
        t = jnp.transpose(b.reshape(nb, 8, P, Cc), (0, 1, 3, 2))[:, :, :, None, :]
        t = jnp.where(same[None, :, None, :, None], t, 0.0)
        return t.reshape(nb, 8 * Cc, 8 * P).astype(jnp.bfloat16)

    def cblk(c):
        t = jnp.transpose(c.reshape(nb, 8, Cc, P), (0, 1, 3, 2))[:, :, :, None, :]
        t = jnp.where(same[None, :, None, :, None], t, 0.0)
        return t.reshape(nb, 8 * P, 8 * Cc).astype(jnp.bfloat16)

    return bblk(b_re), bblk(b_im), cblk(c_re), cblk(c_im)


def _glu_kernel(h_ref, w_ref, b_ref, g_ref, o_ref):
    h = h_ref[...]
    t = _dot(h, w_ref[...]) + b_ref[...]
    o_ref[...] = (h * _sigmoid(t) * _silu(g_ref[...])).astype(o_ref.dtype)


def _glu_call(h2d, glu_w_bf16, glu_b, z2d, *, tm):
    M, W = h2d.shape
    return pl.pallas_call(
        _glu_kernel,
        grid=(M // tm,),
        in_specs=[pl.BlockSpec((tm, W), lambda i: (i, 0)),
                  pl.BlockSpec((W, W), lambda i: (0, 0)),
                  pl.BlockSpec((1, W), lambda i: (0, 0)),
                  pl.BlockSpec((tm, W), lambda i: (i, OFF_S5_G // W))],
        out_specs=pl.BlockSpec((tm, W), lambda i: (i, 0)),
        out_shape=jax.ShapeDtypeStruct((M, W), jnp.bfloat16),
        compiler_params=_params(1),
        name="glu",
    )(h2d, glu_w_bf16, glu_b.reshape(1, W), z2d)


def _kvprep_kernel(kv_ref, sm_ref, knw_ref, k_ref, v_ref, ik_ref, *rest, transposed):
    kvw = KV_HEADS * ATT_D
    kv = kv_ref[0]
    knw = knw_ref[...]
    ks = []
    for h in range(KV_HEADS):
        kh = kv[:, ATT_D * h:ATT_D * (h + 1)]
        ms = jnp.mean(kh * kh, axis=-1, keepdims=True)
        ks.append(kh * lax.rsqrt(ms + EPS) * knw)
    k = jnp.concatenate(ks, axis=-1)
    v = kv[:, kvw:]
    sm = sm_ref[0]
    k_ref[0] = k
    v_ref[0] = v
    ik_ref[0] = sm[:, :IDX_D]
    if transposed:
        kbf_ref, vt_ref, kia_ref, kib_ref = rest
        kbf_ref[0] = k.astype(jnp.bfloat16)
        vt_ref[0] = v.T.astype(jnp.bfloat16)
        lane = lax.broadcasted_iota(jnp.int32, sm.shape, 1)
        kia_ref[0] = jnp.where(lane < IDX_D, sm, 0.0).astype(jnp.bfloat16)
        kib_ref[0] = jnp.where(lane >= IDX_D, pltpu.roll(sm, IDX_D, axis=1), 0.0).astype(jnp.bfloat16)


def _kvprep_call(z, k_norm_w, *, tr, transposed):
    B, L, _ = z.shape
    kvw = KV_HEADS * ATT_D
    f32, bf = jnp.float32, jnp.bfloat16
    rowblk = lambda w: pl.BlockSpec((1, tr, w), lambda b, r: (b, r, 0))
    out_specs = [rowblk(kvw), rowblk(kvw), rowblk(IDX_D)]
    out_shape = [jax.ShapeDtypeStruct((B, L, kvw), f32), jax.ShapeDtypeStruct((B, L, kvw), f32),
                 jax.ShapeDtypeStruct((B, L, IDX_D), f32)]
    if transposed:
        out_specs += [rowblk(kvw), pl.BlockSpec((1, kvw, tr), lambda b, r: (b, 0, r)), rowblk(LANES), rowblk(LANES)]
        out_shape += [jax.ShapeDtypeStruct((B, L, kvw), bf), jax.ShapeDtypeStruct((B, kvw, L), bf),
                      jax.ShapeDtypeStruct((B, L, LANES), bf), jax.ShapeDtypeStruct((B, L, LANES), bf)]
    return pl.pallas_call(
        functools.partial(_kvprep_kernel, transposed=transposed),
        grid=(B, L // tr),
        in_specs=[pl.BlockSpec((1, tr, 2 * kvw), lambda b, r: (b, r, OFF_AT_KV // (2 * kvw))),
                  pl.BlockSpec((1, tr, LANES), lambda b, r: (b, r, OFF_SMALL // LANES)),
                  pl.BlockSpec((1, ATT_D), lambda b, r: (0, 0))],
        out_specs=out_specs, out_shape=out_shape,
        compiler_params=_params(2),
        name="kvprep",
    )(z, z, k_norm_w.reshape(1, ATT_D))


def _sortable_key(score):
    bits = lax.bitcast_convert_type(score, jnp.int32)
    return jnp.where(bits < 0, bits ^ 0x7FFFFFFF, bits)


def _rows_reduce(x, op):
    n = x.shape[0]
    slab = 8 * SUBLANES
    if n > slab and n % slab == 0:
        x = op(x.reshape(n // slab, slab, x.shape[1]), axis=0)
    return op(x, axis=0, keepdims=True)


def _lanes_sum(x):
    parts = [x[:, LANES * c:LANES * (c + 1)] for c in range(x.shape[1] // LANES)]
    while len(parts) > 1:
        parts = [a + b for a, b in zip(parts[::2], parts[1::2])] + ([parts[-1]] if len(parts) % 2 else [])
    return jnp.sum(parts[0], axis=1, keepdims=True)


def _kth_largest_key(key, k, axis):
    shape = list(key.shape)
    shape[axis] = 1

    def it(n, tu):
        cand_u = tu | jnp.left_shift(jnp.int32(1), 31 - n)
        cand_s = cand_u ^ jnp.int32(INT_MIN)
        ones = jnp.where(key >= cand_s, 1.0, 0.0)
        cnt = _rows_reduce(ones, jnp.sum) if axis == 0 else _lanes_sum(ones)
        return jnp.where(cnt >= k, cand_u, tu)

    tu = lax.fori_loop(0, 32, it, jnp.zeros(shape, jnp.int32))
    return tu ^ jnp.int32(INT_MIN)


def _t5_bucket(dist):
    exact = REL_BUCKETS // 2
    d = dist.astype(jnp.float32)
    large = exact + jnp.log(jnp.maximum(d, 1.0) / exact) / math.log(REL_MAX_DIST / exact) * (REL_BUCKETS - exact)
    large = jnp.minimum(jnp.maximum(large, 0.0).astype(jnp.int32), REL_BUCKETS - 1)
    return jnp.where(dist < exact, dist, large)


def _bias_lookup(rel_bias, bucket):
    onehot = (bucket[..., None] == jnp.arange(REL_BUCKETS, dtype=jnp.int32)).astype(jnp.float32)
    return jnp.einsum('...k,kh->...h', onehot, rel_bias.astype(jnp.float32), precision=lax.Precision.HIGHEST)


def _pattn_kernel(zq_ref, zg_ref, ziq_ref, zsm_ref, kbf_ref, vt_ref, kia_ref, kib_ref, qnw_ref, bias_ref, o_ref,
                  mask_ref, *, topk):
    i = pl.program_id(1)
    nqb = kbf_ref.shape[1] // Q_BLK
    qnw = qnw_ref[...]
    scale = ATT_D ** -0.5

    def body(S):
        nkb = S // Q_BLK
        wt = zsm_ref[0].T * (1.0 / 32.0)
        kia, kib = kia_ref[0, pl.ds(0, S), :], kib_ref[0, pl.ds(0, S), :]
        score = jnp.zeros((S, Q_BLK), jnp.float32)
        for p in range(IDX_HEADS // 2):
            qi = ziq_ref[0, :, pl.ds(LANES * p, LANES)].astype(jnp.bfloat16)
            da = lax.dot_general(kia, qi, _NT, preferred_element_type=jnp.float32)
            db = lax.dot_general(kib, qi, _NT, preferred_element_type=jnp.float32)
            r = SMALL_IXW + 2 * p
            score = score + jnp.maximum(da, 0.0) * wt[r:r + 1] + jnp.maximum(db, 0.0) * wt[r + 1:r + 2]
        s_pos = lax.broadcasted_iota(jnp.int32, (S, Q_BLK), 0)
        t_pos = lax.broadcasted_iota(jnp.int32, (S, Q_BLK), 1) + i * Q_BLK
        adm = s_pos <= t_pos
        key = jnp.where(adm, _sortable_key(score), jnp.int32(INT_MIN))
        thr = _kth_largest_key(key, topk, 0)
        ge = key >= thr
        cnt_ge = _rows_reduce(jnp.where(ge, 1.0, 0.0), jnp.sum)
        tied = (cnt_ge > topk) & (thr > jnp.int32(INT_MIN))
        any_tied = jnp.max(jnp.where(tied, 1.0, 0.0)) > 0.0

        @pl.when(jnp.logical_not(any_tied))
        def _():
            mask_ref[pl.ds(0, S), :] = jnp.where(ge & adm, 0.0, MASK_VALUE)

        @pl.when(any_tied)
        def _():
            gt = key > thr
            eq = key == thr
            need = topk - _rows_reduce(jnp.where(gt, 1.0, 0.0), jnp.sum)
            nbits = S.bit_length()

            def it(n, j_sel):
                cand = j_sel | jnp.left_shift(jnp.int32(1), nbits - 1 - n)
                c = _rows_reduce(jnp.where(eq & (s_pos < cand), 1.0, 0.0), jnp.sum)
                return jnp.where(c <= need, cand, j_sel)

            j_sel = lax.fori_loop(0, nbits, it, jnp.zeros((1, Q_BLK), jnp.int32))
            sel = (gt | (eq & (s_pos < j_sel))) & adm
            mask_ref[pl.ds(0, S), :] = jnp.where(sel, 0.0, MASK_VALUE)

        for h in range(ATT_HEADS):
            kvh = h // (ATT_HEADS // KV_HEADS)
            q = zq_ref[0, :, pl.ds(ATT_D * h, ATT_D)]
            ms = jnp.mean(q * q, axis=-1, keepdims=True)
            qn = (q * lax.rsqrt(ms + EPS) * qnw).astype(jnp.bfloat16)
            logit = lax.dot_general(kbf_ref[0, pl.ds(0, S), pl.ds(ATT_D * kvh, ATT_D)], qn, _NT,
                                    preferred_element_type=jnp.float32) * scale
            bias = jnp.concatenate([bias_ref[jnp.clip(i - j, 0, N_BIAS_TILES - 1), h] for j in range(nkb)], axis=0)
            x = logit + bias + mask_ref[pl.ds(0, S), :]
            m = _rows_reduce(x, jnp.max)
            pexp = jnp.exp(x - m)
            l = _rows_reduce(pexp, jnp.sum)
            ot = jnp.dot(vt_ref[0, pl.ds(ATT_D * kvh, ATT_D), pl.ds(0, S)], pexp.astype(jnp.bfloat16),
                         preferred_element_type=jnp.float32)
            out = (ot / l).T
            g = zg_ref[0, :, pl.ds(ATT_D * h, ATT_D)]
            o_ref[0, :, pl.ds(ATT_D * h, ATT_D)] = (out * _silu(g)).astype(o_ref.dtype)

    nv = min(N_CAUSAL_VARIANTS, nqb)
    per = nqb // nv
    for c in range(nv):
        pl.when(i // per == c)(functools.partial(body, (c + 1) * per * Q_BLK))


def _pattn_call(z, kbf, vt, kia, kib, q_norm_w, bias_tiles, *, topk):
    B, L, _ = z.shape
    W = ATT_HEADS * ATT_D
    kvw = KV_HEADS * ATT_D
    zblk = lambda off, w: pl.BlockSpec((1, Q_BLK, w), lambda b, i: (b, i, off // w))
    full = lambda shape: pl.BlockSpec((1,) + shape, lambda b, i: (b, 0, 0))
    return pl.pallas_call(
        functools.partial(_pattn_kernel, topk=topk),
        grid=(B, L // Q_BLK),
        in_specs=[zblk(OFF_AT_Q, W), zblk(OFF_AT_G, W), zblk(OFF_IX_Q, W), zblk(OFF_SMALL, LANES),
                  full((L, kvw)), full((kvw, L)), full((L, LANES)), full((L, LANES)),
                  pl.BlockSpec((1, ATT_D), lambda b, i: (0, 0)),
                  pl.BlockSpec((N_BIAS_TILES, ATT_HEADS, Q_BLK, Q_BLK), lambda b, i: (0, 0, 0, 0))],
        out_specs=pl.BlockSpec((1, Q_BLK, W), lambda b, i: (b, i, 0)),
        out_shape=jax.ShapeDtypeStruct((B, L, W), jnp.bfloat16),
        scratch_shapes=[pltpu.VMEM((L, Q_BLK), jnp.float32)],
        compiler_params=_params(2),
        name="pattn",
    )(z, z, z, z, kbf, vt, kia, kib, q_norm_w.reshape(1, ATT_D), bias_tiles)


def _prompt_bias_tiles(rel_bias):
    o = jnp.arange(N_BIAS_TILES, dtype=jnp.int32)[:, None, None]
    s = jnp.arange(Q_BLK, dtype=jnp.int32)[None, :, None]
    t = jnp.arange(Q_BLK, dtype=jnp.int32)[None, None, :]
    tiles = _bias_lookup(rel_bias, _t5_bucket(jnp.maximum(Q_BLK * o + t - s, 0)))
    return jnp.moveaxis(tiles, -1, 1)


def _sattn_score_kernel(pt_ref, qi_ref, w_ref, iknew_ref, *rest):
    pages, s_ref = rest[:NPG], rest[NPG]
    pc = pl.program_id(1)
    last = pl.num_programs(1) - 1
    qi = qi_ref[0]
    w = w_ref[0]

    def head_sum(d):
        r = jnp.maximum(d, 0.0) * w
        return jnp.sum(r.reshape(T_PAD, IDX_HEADS, r.shape[-1]), axis=1)

    @pl.when(pc < last)
    def _():
        for i in range(NPG):
            d = jnp.dot(qi, pages[i][0, 0].astype(jnp.bfloat16), preferred_element_type=jnp.float32)
            s_ref[0, :, pl.ds(PAGE * i, PAGE)] = head_sum(d)

    @pl.when(pc == last)
    def _():
        s_ref[0] = jnp.full((T_PAD, CHUNK_KEYS), MASK_VALUE, jnp.float32)
        d = lax.dot_general(qi, iknew_ref[0].astype(jnp.bfloat16), _NT, preferred_element_type=jnp.float32)
        s_ref[0, :, pl.ds(0, PAGE)] = head_sum(d)


def _sattn_score_call(page_table, qi_rows, w_col, ik_new, cache_ik_t, layer):
    B, n_pages = page_table.shape
    n_chunks = n_pages // NPG
    page_spec = lambda i: pl.BlockSpec(
        (1, 1, IDX_D, PAGE), lambda b, pc, pt: (layer, pt[b, jnp.minimum(pc * NPG + i, n_pages - 1)], 0, 0))
    per_b = lambda shape: pl.BlockSpec((1,) + shape, lambda b, pc, pt: (b, 0, 0))
    grid_spec = pltpu.PrefetchScalarGridSpec(
        num_scalar_prefetch=1, grid=(B, n_chunks + 1),
        in_specs=[per_b((T_PAD * IDX_HEADS, IDX_D)), per_b((T_PAD * IDX_HEADS, 1)), per_b((PAGE, IDX_D))]
        + [page_spec(i) for i in range(NPG)],
        out_specs=pl.BlockSpec((1, T_PAD, CHUNK_KEYS), lambda b, pc, pt: (b, 0, pc)))
    return pl.pallas_call(
        _sattn_score_kernel,
        grid_spec=grid_spec,
        out_shape=jax.ShapeDtypeStruct((B, T_PAD, (n_chunks + 1) * CHUNK_KEYS), jnp.float32),
        compiler_params=_params(2),
        name="sattn_score",
    )(page_table, qi_rows, w_col, ik_new, *([cache_ik_t] * NPG))


def _sattn_kernel(pt_ref, s_ref, q_ref, g_ref, qnw_ref, bias_ref, knew_ref, vnew_ref, *rest, topk, n_new, past):
    kp, vp = rest[:NPG], rest[NPG:2 * NPG]
    o_ref, thr_ref, tie_ref, qn_ref, m_ref, l_ref, acc_ref = rest[2 * NPG:]
    pc = pl.program_id(1)
    last = pl.num_programs(1) - 1
    G = ATT_HEADS // KV_HEADS

    def admissible(col0, width):
        pos = lax.broadcasted_iota(jnp.int32, (T_PAD, width), 1) + col0
        t = lax.broadcasted_iota(jnp.int32, (T_PAD, width), 0)
        return (pos < past) | ((pos - past <= t) & (pos - past < n_new))

    @pl.when(pc == 0)
    def _():
        sc = s_ref[0]
        n_all = sc.shape[-1]
        adm0 = admissible(0, n_all)
        key = jnp.where(adm0, _sortable_key(sc), jnp.int32(INT_MIN))
        thr = _kth_largest_key(key, topk, 1)
        thr_ref[...] = jnp.broadcast_to(thr, thr_ref.shape)
        tie_ref[...] = jnp.full(tie_ref.shape, n_all, jnp.int32)
        cnt_ge = _lanes_sum(jnp.where(key >= thr, 1.0, 0.0))
        tied = (cnt_ge > topk) & (thr > jnp.int32(INT_MIN))

        @pl.when(jnp.max(jnp.where(tied, 1.0, 0.0)) > 0.0)
        def _():
            eq = key == thr
            pos = lax.broadcasted_iota(jnp.int32, key.shape, 1)
            need = topk - _lanes_sum(jnp.where(key > thr, 1.0, 0.0))
            nbits = n_all.bit_length()

            def it(n, j_sel):
                cand = j_sel | jnp.left_shift(jnp.int32(1), nbits - 1 - n)
                c = _lanes_sum(jnp.where(eq & (pos < cand), 1.0, 0.0))
                return jnp.where(c <= need, cand, j_sel)

            j_sel = lax.fori_loop(0, nbits, it, jnp.zeros((T_PAD, 1), jnp.int32))
            tie_ref[...] = jnp.broadcast_to(j_sel, tie_ref.shape)

        q = q_ref[0]
        ms = jnp.mean(q * q, axis=-1, keepdims=True)
        qn_ref[...] = (q * lax.rsqrt(ms + EPS) * qnw_ref[...]).astype(jnp.bfloat16)
        m_ref[...] = jnp.full(m_ref.shape, MASK_VALUE, jnp.float32)
        l_ref[...] = jnp.zeros(l_ref.shape, jnp.float32)
        acc_ref[...] = jnp.zeros(acc_ref.shape, jnp.float32)

    def attend(get_k, get_v, width):
        col0 = pl.multiple_of(pc * CHUNK_KEYS, CHUNK_KEYS)
        sc = s_ref[0, :, pl.ds(col0, width)]
        adm = admissible(col0, width)
        key = jnp.where(adm, _sortable_key(sc), jnp.int32(INT_MIN))
        pos = lax.broadcasted_iota(jnp.int32, (T_PAD, width), 1) + col0
        thr = thr_ref[:, :1]
        sel8 = ((key > thr) | ((key == thr) & (pos < tie_ref[:, :1]))) & adm
        sel = jnp.concatenate([sel8] * G, axis=0)
        for kv in range(KV_HEADS):
            kk = get_k(kv).astype(jnp.bfloat16)
            x = lax.dot_general(qn_ref[kv], kk, _NT, preferred_element_type=jnp.float32) * (ATT_D ** -0.5)
            x = jnp.where(sel, x + bias_ref[kv, :, pl.ds(0, width)], MASK_VALUE)
            m_old = m_ref[kv]
            m_new = jnp.maximum(m_old, jnp.max(x, axis=-1, keepdims=True))
            p = jnp.where(sel, jnp.exp(x - m_new), 0.0)
            alpha = jnp.exp(m_old - m_new)
            l_ref[kv] = alpha * l_ref[kv] + jnp.sum(p, axis=-1, keepdims=True)
            acc_ref[kv] = alpha * acc_ref[kv] + jnp.dot(
                p.astype(jnp.bfloat16), get_v(kv).astype(jnp.bfloat16),
                preferred_element_type=jnp.float32)
            m_ref[kv] = m_new

    @pl.when(pc < last)
    def _():
        attend(lambda kv: jnp.concatenate([kp[i][0, 0, pl.ds(kv, PAGE, stride=KV_HEADS), :] for i in range(NPG)], axis=0),
               lambda kv: jnp.concatenate([vp[i][0, 0, pl.ds(kv, PAGE, stride=KV_HEADS), :] for i in range(NPG)], axis=0),
               CHUNK_KEYS)

    @pl.when(pc == last)
    def _():
        attend(lambda kv: knew_ref[0, :, pl.ds(ATT_D * kv, ATT_D)],
               lambda kv: vnew_ref[0, :, pl.ds(ATT_D * kv, ATT_D)], PAGE)
        o_ref[0] = acc_ref[...] / l_ref[...] * _silu(g_ref[0])


def _sattn_call(page_table, scores, q_rows, g_rows, q_norm_w, bias, k_new, v_new, cache_k, cache_v, layer,
                *, topk, n_new):
    B, n_pages = page_table.shape
    n_chunks = n_pages // NPG
    kvw = KV_HEADS * ATT_D
    rows = (ATT_HEADS // KV_HEADS) * T_PAD
    n_keys = (n_chunks + 1) * CHUNK_KEYS
    page_spec = lambda i: pl.BlockSpec(
        (1, 1, PAGE * KV_HEADS, ATT_D),
        lambda b, pc, pt: (layer, pt[b, jnp.minimum(pc * NPG + i, n_pages - 1)], 0, 0))
    per_b = lambda shape: pl.BlockSpec((1,) + shape, lambda b, pc, pt: (b,) + (0,) * len(shape))
    grid_spec = pltpu.PrefetchScalarGridSpec(
        num_scalar_prefetch=1, grid=(B, n_chunks + 1),
        in_specs=[per_b((T_PAD, n_keys)), per_b((KV_HEADS, rows, ATT_D)), per_b((KV_HEADS, rows, ATT_D)),
                  pl.BlockSpec((1, ATT_D), lambda b, pc, pt: (0, 0)),
                  pl.BlockSpec((KV_HEADS, rows, CHUNK_KEYS), lambda b, pc, pt: (0, 0, pc)),
                  per_b((PAGE, kvw)), per_b((PAGE, kvw))]
        + [page_spec(i) for i in range(NPG)] + [page_spec(i) for i in range(NPG)],
        out_specs=per_b((KV_HEADS, rows, ATT_D)),
        scratch_shapes=[pltpu.VMEM((T_PAD, LANES), jnp.int32), pltpu.VMEM((T_PAD, LANES), jnp.int32),
                        pltpu.VMEM((KV_HEADS, rows, ATT_D), jnp.bfloat16),
                        pltpu.VMEM((KV_HEADS, rows, 1), jnp.float32), pltpu.VMEM((KV_HEADS, rows, 1), jnp.float32),
                        pltpu.VMEM((KV_HEADS, rows, ATT_D), jnp.float32)])
    return pl.pallas_call(
        functools.partial(_sattn_kernel, topk=topk, n_new=n_new, past=n_pages * PAGE),
        grid_spec=grid_spec,
        out_shape=jax.ShapeDtypeStruct((B, KV_HEADS, rows, ATT_D), jnp.float32),
        compiler_params=_params(2),
        name="sattn",
    )(page_table, scores, q_rows, g_rows, q_norm_w.reshape(1, ATT_D), bias, k_new, v_new,
      *([cache_k] * NPG), *([cache_v] * NPG))


def _sample_bias(rel_bias, past, n_keys):
    G = ATT_HEADS // KV_HEADS
    t = jnp.arange(T_PAD, dtype=jnp.int32)[:, None]
    pos = jnp.arange(n_keys, dtype=jnp.int32)[None, :]
    tab = _bias_lookup(rel_bias, _t5_bucket(jnp.maximum(past + t - pos, 0)))
    return jnp.moveaxis(tab, -1, 0).reshape(KV_HEADS, G * T_PAD, n_keys)


def _to_head_rows(a):
    B = a.shape[0]
    G = ATT_HEADS // KV_HEADS
    a = a.reshape(B, T_PAD, KV_HEADS, G, ATT_D)
    return jnp.transpose(a, (0, 2, 3, 1, 4)).reshape(B, KV_HEADS, G * T_PAD, ATT_D)


def _from_head_rows(a):
    B = a.shape[0]
    G = ATT_HEADS // KV_HEADS
    a = a.reshape(B, KV_HEADS, G, T_PAD, ATT_D)
    return jnp.transpose(a, (0, 3, 1, 2, 4)).reshape(B, T_PAD, ATT_HEADS * ATT_D)


def _layer(x, lp, states, attend, *, l_valid, tm, chunk_hg, chunk_ssd):
    B, L, _ = x.shape
    hg_s0, ssm_s0, conv_s0, s5_re0, s5_im0 = states
    x2d = x.reshape(B * L, D_MODEL)
    z2d = _inproj(x2d, lp['norm_w'], lp['w_in'], tm)
    z = z2d.reshape(B, L, D_IN_PAD)

    o_hg, hg_s = _hgrn_call(z, lp['lb'], lp['hg_norm_w'], hg_s0, chunk=chunk_hg, l_valid=l_valid)

    conv8 = jnp.pad(conv_s0, ((0, 0), (SUBLANES - (SSM_CONV - 1), 0), (0, 0)))
    y_ssm, ssm_s, cx, cb, cc = _ssd_call(z, lp['conv_w'], lp['conv_b'], conv8, lp['dtb_pad'], lp['alog_pad'],
                                         lp['dskip_pad'], lp['ssm_norm_w'], ssm_s0, chunk=chunk_ssd, l_valid=l_valid)
    tail = SUBLANES - (SSM_CONV - 1)
    conv_s = jnp.concatenate([cx[:, tail:], cb[:, tail:], cc[:, tail:]], axis=-1)

    h5, s5_re, s5_im = _s5_call(z, *lp['s5_blocks'], lp['a_re'], lp['a_im'], lp['log_dt'], lp['s5_d'],
                                s5_re0.reshape(B, 1, -1), s5_im0.reshape(B, 1, -1), l_valid=l_valid)
    o5 = _glu_call(h5.reshape(B * L, BRANCH), lp['glu_w'], lp['glu_b'], z2d, tm=tm)

    o_att, k, v, ik = attend(z)

    y = _outproj((o_hg.reshape(B * L, BRANCH), y_ssm.reshape(B * L, BRANCH), o5, o_att.reshape(B * L, BRANCH)),
                 lp['w_out'], lp['layer'], x2d, tm)
    st = (k[:, :l_valid].reshape(B, l_valid, KV_HEADS, ATT_D), v[:, :l_valid].reshape(B, l_valid, KV_HEADS, ATT_D),
          ik[:, :l_valid], hg_s, ssm_s, conv_s,
          s5_re.reshape(B, S5_GROUPS, S5_STATE), s5_im.reshape(B, S5_GROUPS, S5_STATE))
    return y.reshape(B, L, D_MODEL), st


def kernel(x_prompt, x_sample, cache_k, cache_v, cache_idx_k, state_hgrn, state_ssm, state_conv, state_s5_re, state_s5_im, page_table, norm_w, w_in, w_out, hg_lb_logits, hg_norm_w, ssm_conv_w, ssm_conv_b, ssm_dt_bias, ssm_a_log, ssm_d, ssm_norm_w, s5_a_re, s5_a_im, s5_log_dt, s5_b_re, s5_b_im, s5_c_re, s5_c_im, s5_d, s5_glu_w, s5_glu_b, att_q_norm, att_k_norm, rel_bias):
    f32 = jnp.float32
    bp, lp_len, _ = x_prompt.shape
    bs, ls, _ = x_sample.shape
    n_pages = page_table.shape[1]
    past = n_pages * PAGE
    n_keys = (n_pages // NPG + 1) * CHUNK_KEYS

    sm = jax.nn.softmax(hg_lb_logits.astype(f32), axis=0)
    lower = jnp.cumsum(sm, axis=0) - sm[0]

    w_out_bf16 = w_out.astype(jnp.bfloat16)
    w_in_t = jnp.swapaxes(w_in, 1, 2)
    cache_ik_t = jnp.swapaxes(cache_idx_k, 2, 3)
    page_rows = lambda c: c.reshape(c.shape[0], c.shape[1], PAGE * KV_HEADS, ATT_D)
    cache_k2, cache_v2 = page_rows(cache_k), page_rows(cache_v)
    bias_tiles = _prompt_bias_tiles(rel_bias)
    bias_sample = _sample_bias(rel_bias, past, n_keys)
    topk_p = min(TOPK_MAX, lp_len // 4)
    topk_s = min(TOPK_MAX, (past + ls) // 4)

    def head_lanes(v):
        return jnp.zeros((1, LANES), f32).at[0, SMALL_DT:SMALL_DT + SSM_HEADS].set(v)

    xs_pad = jnp.pad(x_sample, ((0, 0), (0, SAMPLE_ROWS - ls), (0, 0)))
    pad_page = lambda a: jnp.pad(a, ((0, 0), (0, PAGE - a.shape[1]), (0, 0)))

    yp, ys = x_prompt, xs_pad
    new_p = [[] for _ in range(8)]
    new_s = [[] for _ in range(8)]
    for l in range(DEPTH):
        lp = {'norm_w': norm_w[l], 'w_in': _pack_w_in(w_in_t, l), 'w_out': w_out_bf16, 'layer': l,
              'lb': lower[l], 'hg_norm_w': hg_norm_w[l],
              'conv_w': ssm_conv_w[l], 'conv_b': ssm_conv_b[l], 'dtb_pad': head_lanes(ssm_dt_bias[l]),
              'alog_pad': head_lanes(ssm_a_log[l]), 'dskip_pad': head_lanes(ssm_d[l]), 'ssm_norm_w': ssm_norm_w[l],
              's5_blocks': _s5_block_diag(s5_b_re[l], s5_b_im[l], s5_c_re[l], s5_c_im[l]),
              'a_re': s5_a_re[l].reshape(1, -1), 'a_im': s5_a_im[l].reshape(1, -1),
              'log_dt': jnp.repeat(s5_log_dt[l], S5_STATE).reshape(1, -1), 's5_d': s5_d[l].reshape(1, -1),
              'glu_w': s5_glu_w[l].astype(jnp.bfloat16), 'glu_b': s5_glu_b[l]}

        def attend_prompt(z):
            k, v, ik, kbf, vt, kia, kib = _kvprep_call(z, att_k_norm[l], tr=256, transposed=True)
            o = _pattn_call(z, kbf, vt, kia, kib, att_q_norm[l], bias_tiles, topk=topk_p)
            return o, k, v, ik

        def attend_sample(z):
            k, v, ik = _kvprep_call(z, att_k_norm[l], tr=SAMPLE_ROWS, transposed=False)
            z8 = z[:, :T_PAD]
            qi_rows = z8[..., OFF_IX_Q:OFF_IX_Q + IDX_HEADS * IDX_D].reshape(bs, T_PAD * IDX_HEADS, IDX_D)
            w_col = z8[..., OFF_SMALL + SMALL_IXW:OFF_SMALL + SMALL_IXW + IDX_HEADS] * (1.0 / 32.0)
            scores = _sattn_score_call(page_table, qi_rows.astype(jnp.bfloat16),
                                       w_col.reshape(bs, T_PAD * IDX_HEADS, 1), pad_page(ik), cache_ik_t, l)
            o = _sattn_call(page_table, scores, _to_head_rows(z8[..., OFF_AT_Q:OFF_AT_Q + BRANCH]),
                            _to_head_rows(z8[..., OFF_AT_G:OFF_AT_G + BRANCH]), att_q_norm[l], bias_sample,
                            pad_page(k), pad_page(v), cache_k2, cache_v2, l, topk=topk_s, n_new=ls)
            o = jnp.pad(_from_head_rows(o), ((0, 0), (0, SAMPLE_ROWS - T_PAD), (0, 0)))
            return o.astype(jnp.bfloat16), k, v, ik

        zero_states = (jnp.zeros((bp, HG_HEADS, LANES, LANES), f32),
                       jnp.zeros((bp, SSM_HEADS, SSM_P, SSM_STATE), f32),
                       jnp.zeros((bp, SSM_CONV - 1, SSM_CONV_DIM), f32),
                       jnp.zeros((bp, S5_GROUPS, S5_STATE), f32),
                       jnp.zeros((bp, S5_GROUPS, S5_STATE), f32))
        yp, st_p = _layer(yp, lp, zero_states, attend_prompt, l_valid=lp_len, tm=512, chunk_hg=64, chunk_ssd=128)
        samp_states = (state_hgrn[l], state_ssm[l], state_conv[l], state_s5_re[l], state_s5_im[l])
        ys, st_s = _layer(ys, lp, samp_states, attend_sample, l_valid=ls, tm=bs * SAMPLE_ROWS,
                          chunk_hg=SAMPLE_ROWS, chunk_ssd=SAMPLE_ROWS)
        for i in range(8):
            new_p[i].append(st_p[i])
            new_s[i].append(st_s[i])
    pk, pv, pik, phg, pssm, pconv, ps5r, ps5i = [jnp.stack(a) for a in new_p]
    sk, sv, sik, shg, sssm, sconv, ss5r, ss5i = [jnp.stack(a) for a in new_s]
    return (yp, ys[:, :ls], pk, pv, pik, phg, pssm, pconv, ps5r, ps5i, sk, sv, sik, shg, sssm, sconv, ss5r, ss5i)
```

```python
import functools
import math

import jax
import jax.numpy as jnp
from jax import lax
from jax.experimental import pallas as pl
from jax.experimental.pallas import tpu as pltpu

D_MODEL = 4096
DEPTH = 2
BRANCH = D_MODEL // 4
HG_HEADS = 8
SSM_HEADS = 16
SSM_GROUPS = 4
SSM_STATE = 128
SSM_CONV = 4
SSM_CONV_DIM = BRANCH + 2 * SSM_GROUPS * SSM_STATE
SSM_HPG = SSM_HEADS // SSM_GROUPS
SSM_P = 64
S5_GROUPS = 64
S5_STATE = 64
S5_BLK_CH = 128
S5_BLK_ST = 512
ATT_D = 128
ATT_HEADS = 8
KV_HEADS = 2
IDX_HEADS = 16
IDX_D = 64
TOPK_MAX = 256
Q_BLK = 128
PAGE = 128
REL_BUCKETS = 32
REL_MAX_DIST = 1024
EPS = 1e-6
MASK_VALUE = -1e30
F_FLOOR = 1e-30
INT_MIN = -2 ** 31

LANES = 128
SUBLANES = 8
V7X_VMEM_LIMIT = 56 * 1024 * 1024

_SRC_DT, _SRC_S5U, _SRC_ATK, _SRC_ATG, _SRC_IXK, D_IN_SRC = 7168, 7184, 10256, 10768, 12816, 12896

OFF_HG_Q, OFF_HG_F, OFF_HG_I, OFF_HG_G = 0, 1024, 2048, 3072
OFF_SM_Z, OFF_SM_XBC = 4096, 5120
OFF_S5_U, OFF_S5_G = 7168, 8192
OFF_AT_Q, OFF_AT_G, OFF_IX_Q = 9216, 10240, 11264
OFF_AT_KV = 12288
OFF_SMALL = 12800
SMALL_IXW, SMALL_DT = 64, 80
D_IN_PAD = 13312

PACK_TN = 1024
PACK_TK = 1024
_PACK_LAST = D_IN_PAD // PACK_TN - 1
_PACK_ROW0 = (tuple(range(0, _SRC_DT, PACK_TN)) + tuple(range(_SRC_S5U, _SRC_ATK, PACK_TN))
              + tuple(range(_SRC_ATG, _SRC_IXK, PACK_TN)) + (_SRC_ATK,))
N_KV_ROWS = _SRC_ATG - _SRC_ATK
N_IX_ROWS = D_IN_SRC - _SRC_IXK
N_DT_ROWS = _SRC_S5U - _SRC_DT
NORM_ROWS = 128
S5_BULK_STEPS = 32

N_BIAS_TILES = 9
SSD_GPB = 2
HG_SUB = 16
HG_HPB = 8
HG_ROW_BLOCK = 512
N_CAUSAL_VARIANTS = 4
NPG = 16
T_PAD = 8
CHUNK_KEYS = NPG * PAGE
SAMPLE_ROWS = 16

_TN = (((0,), (0,)), ((), ()))
_NT = (((1,), (1,)), ((), ()))


def _params(n_axes):
    return pltpu.CompilerParams(dimension_semantics=("arbitrary",) * n_axes, vmem_limit_bytes=V7X_VMEM_LIMIT)


def _bf(x):
    return x.astype(jnp.bfloat16)


def _dot(a, b, dims=None):
    if dims is None:
        return jnp.dot(_bf(a), _bf(b), preferred_element_type=jnp.float32)
    return lax.dot_general(_bf(a), _bf(b), dims, preferred_element_type=jnp.float32)


def _sigmoid(x):
    return 1.0 / (1.0 + jnp.exp(-x))


def _silu(x):
    return x * _sigmoid(x)


def _softplus(x):
    return jnp.maximum(x, 0.0) + jnp.log(1.0 + jnp.exp(-jnp.abs(x)))


def _gelu_tanh(x):
    return 0.5 * x * (1.0 + jnp.tanh(math.sqrt(2.0 / math.pi) * (x + 0.044715 * (x * x * x))))


def _cumsum_rows(x):
    n = x.shape[0]
    row = lax.broadcasted_iota(jnp.int32, x.shape, 0)
    sh = 1
    while sh < n:
        x = x + jnp.where(row >= sh, pltpu.roll(x, sh, axis=0), 0.0)
        sh *= 2
    return x


def _split3(x):
    hi = x.astype(jnp.bfloat16)
    r = x - hi.astype(jnp.float32)
    mid = r.astype(jnp.bfloat16)
    lo = (r - mid.astype(jnp.float32)).astype(jnp.bfloat16)
    return hi, mid, lo


def _select_lanes_as_rows(sel, x):
    out = None
    for part in _split3(x):
        t = lax.dot_general(sel, part, _NT, preferred_element_type=jnp.float32)
        out = t if out is None else out + t
    return out


def _cmul(ar, ai, br, bi):
    return ar * br - ai * bi, ar * bi + ai * br


def _pack_kernel(off_ref, w_ref, ix_ref, dt_ref, o_ref):
    j = pl.program_id(0)

    @pl.when(j < _PACK_LAST)
    def _():
        o_ref[...] = w_ref[0].astype(o_ref.dtype)

    @pl.when(j == _PACK_LAST)
    def _():
        o_ref[pl.ds(0, N_KV_ROWS), :] = w_ref[0, pl.ds(0, N_KV_ROWS), :].astype(o_ref.dtype)
        o_ref[pl.ds(N_KV_ROWS, N_IX_ROWS), :] = ix_ref[0].astype(o_ref.dtype)
        o_ref[pl.ds(N_KV_ROWS + N_IX_ROWS, N_DT_ROWS), :] = dt_ref[0].astype(o_ref.dtype)
        used = N_KV_ROWS + N_IX_ROWS + N_DT_ROWS
        o_ref[pl.ds(used, PACK_TN - used), :] = jnp.zeros((PACK_TN - used, PACK_TK), o_ref.dtype)


def _pack_w_in(w_in_t, layer):
    assert all(r % SUBLANES == 0 for r in _PACK_ROW0)
    row0 = jnp.asarray([r // SUBLANES for r in _PACK_ROW0], jnp.int32)
    E = pl.Element
    fixed = lambda rows, row: pl.BlockSpec(
        (E(1), E(rows), E(PACK_TK)), lambda j, kc, off: (layer, row, kc * PACK_TK))
    grid_spec = pltpu.PrefetchScalarGridSpec(
        num_scalar_prefetch=1, grid=(D_IN_PAD // PACK_TN, D_MODEL // PACK_TK),
        in_specs=[pl.BlockSpec((E(1), E(PACK_TN), E(PACK_TK)),
                               lambda j, kc, off: (layer, off[j] * SUBLANES, kc * PACK_TK)),
                  fixed(N_IX_ROWS, _SRC_IXK), fixed(N_DT_ROWS, _SRC_DT)],
        out_specs=pl.BlockSpec((PACK_TN, PACK_TK), lambda j, kc, off: (j, kc)))
    return pl.pallas_call(
        _pack_kernel, grid_spec=grid_spec,
        out_shape=jax.ShapeDtypeStruct((D_IN_PAD, D_MODEL), jnp.bfloat16),
        compiler_params=_params(2),
        name="pack_w_in",
    )(row0, w_in_t, w_in_t, w_in_t)


def _inproj_kernel(x_ref, nw_ref, w_ref, z_ref, hn_ref):
    @pl.when(pl.program_id(1) == 0)
    def _():
        nw = nw_ref[...]
        for r0 in range(0, x_ref.shape[0], NORM_ROWS):
            xf = x_ref[pl.ds(r0, NORM_ROWS), :]
            ms = jnp.mean(xf * xf, axis=-1, keepdims=True)
            hn_ref[pl.ds(r0, NORM_ROWS), :] = (xf * lax.rsqrt(ms + EPS) * nw).astype(jnp.bfloat16)

    z_ref[...] = lax.dot_general(hn_ref[...], w_ref[...], _NT, preferred_element_type=jnp.float32)


def _inproj(x2d, norm_w, w_packed, tm, tn=1024):
    m = x2d.shape[0]
    return pl.pallas_call(
        _inproj_kernel,
        grid=(m // tm, D_IN_PAD // tn),
        in_specs=[pl.BlockSpec((tm, D_MODEL), lambda i, j: (i, 0)),
                  pl.BlockSpec((1, D_MODEL), lambda i, j: (0, 0)),
                  pl.BlockSpec((tn, D_MODEL), lambda i, j: (j, 0))],
        out_specs=pl.BlockSpec((tm, tn), lambda i, j: (i, j)),
        out_shape=jax.ShapeDtypeStruct((m, D_IN_PAD), jnp.float32),
        scratch_shapes=[pltpu.VMEM((tm, D_MODEL), jnp.bfloat16)],
        compiler_params=_params(2),
        name="inproj",
    )(x2d, norm_w.reshape(1, D_MODEL), w_packed)


def _outproj_kernel(m0_ref, m1_ref, m2_ref, m3_ref, w_ref, x_ref, y_ref):
    acc = x_ref[...]
    for i, m_ref in enumerate((m0_ref, m1_ref, m2_ref, m3_ref)):
        acc = acc + jnp.dot(m_ref[...], w_ref[pl.ds(BRANCH * i, BRANCH), :], preferred_element_type=jnp.float32)
    y_ref[...] = acc


def _outproj(mixed4, w_out_bf16, layer, x2d, tm, tn=1024):
    m = x2d.shape[0]
    mspec = pl.BlockSpec((tm, BRANCH), lambda i, j: (i, 0))
    return pl.pallas_call(
        _outproj_kernel,
        grid=(m // tm, D_MODEL // tn),
        in_specs=[mspec, mspec, mspec, mspec,
                  pl.BlockSpec((None, D_MODEL, tn), lambda i, j: (layer, 0, j)),
                  pl.BlockSpec((tm, tn), lambda i, j: (i, j))],
        out_specs=pl.BlockSpec((tm, tn), lambda i, j: (i, j)),
        out_shape=jax.ShapeDtypeStruct((m, D_MODEL), jnp.float32),
        compiler_params=_params(2),
        name="outproj",
    )(*mixed4, w_out_bf16, x2d)


def _hgrn_kernel(q_ref, f_ref, i_ref, g_ref, lb_ref, nw_ref, s0_ref, o_ref, s_out_ref, st_ref, *, chunk, l_valid):
    LB = q_ref.shape[1]
    C = chunk
    nblk = C // SUBLANES
    lb_i = pl.program_id(2)

    @pl.when(lb_i == 0)
    def _():
        for hh in range(HG_HPB):
            st_ref[hh] = s0_ref[0, hh].T

    nw = nw_ref[...]
    rowi = lax.broadcasted_iota(jnp.int32, (SUBLANES, LANES), 0)

    def head_chunk(hh, t0):
        lanes = pl.ds(LANES * hh, LANES)
        lb = lb_ref[0, :, lanes]
        fp = f_ref[0, pl.ds(t0, C), lanes]
        qp = q_ref[0, pl.ds(t0, C), lanes]
        v = i_ref[0, pl.ds(t0, C), lanes]
        gp = g_ref[0, pl.ds(t0, C), lanes]
        fg = lb + (1.0 - lb) * _sigmoid(fp)
        logf = jnp.log(jnp.maximum(fg, F_FLOOR))
        kk = (1.0 - lb) * _sigmoid(-fp)
        if l_valid < LB * pl.num_programs(2):
            rows = lax.broadcasted_iota(jnp.int32, (C, LANES), 0) + t0 + lb_i * LB
            logf = jnp.where(rows < l_valid, logf, 0.0)
            kk = jnp.where(rows < l_valid, kk, 0.0)
        qq = _silu(qp)
        G = _cumsum_rows(logf)
        st = st_ref[hh]
        o_inter = _dot(qq * jnp.exp(G), st, _NT)
        acc = [None] * nblk
        qb = [qq[SUBLANES * tb:SUBLANES * (tb + 1)] for tb in range(nblk)]
        Gb = [G[SUBLANES * tb:SUBLANES * (tb + 1)] for tb in range(nblk)]
        for s in range(C):
            sb = s // SUBLANES
            gs = G[s:s + 1]
            ks = kk[s:s + 1]
            vs = v[s:s + 1]
            for tb in range(sb, (s // HG_SUB + 1) * (HG_SUB // SUBLANES)):
                d = Gb[tb] - gs
                if tb == sb:
                    d = jnp.where(rowi >= s - SUBLANES * sb, d, MASK_VALUE)
                w = jnp.sum(qb[tb] * ks * jnp.exp(d), axis=-1, keepdims=True)
                contrib = w * vs
                acc[tb] = contrib if acc[tb] is None else acc[tb] + contrib
        for j in range(1, C // HG_SUB):
            r0 = HG_SUB * j
            gb = G[r0 - 1:r0]
            qj = qq[r0:r0 + HG_SUB] * jnp.exp(G[r0:r0 + HG_SUB] - gb)
            kj = kk[:r0] * jnp.exp(gb - G[:r0])
            oj = _dot(_dot(qj, kj, _NT), v[:r0])
            for half in range(HG_SUB // SUBLANES):
                tb = r0 // SUBLANES + half
                acc[tb] = acc[tb] + oj[SUBLANES * half:SUBLANES * (half + 1)]
        o = o_inter + jnp.concatenate(acc, axis=0)
        g_last = G[C - 1:C]
        kd = kk * jnp.exp(g_last - G)
        st_ref[hh] = jnp.exp(g_last) * st + _dot(v, kd, _TN)
        ms = jnp.mean(o * o, axis=-1, keepdims=True)
        on = o * lax.rsqrt(ms + EPS) * nw
        o_ref[0, pl.ds(t0, C), lanes] = (on * _silu(gp)).astype(o_ref.dtype)

    def body(c, carry):
        t0 = pl.multiple_of(c * C, C)
        for hh in range(HG_HPB):
            head_chunk(hh, t0)
        return carry

    lax.fori_loop(0, LB // C, body, 0)

    @pl.when(lb_i == pl.num_programs(2) - 1)
    def _():
        for hh in range(HG_HPB):
            s_out_ref[0, hh] = st_ref[hh].T


def _hgrn_call(z, lb, hg_norm_w, s0, *, chunk, l_valid):
    B, L, _ = z.shape
    W = HG_HPB * LANES
    LB = min(L, HG_ROW_BLOCK)
    zspec = lambda off: pl.BlockSpec((1, LB, W), lambda b, h, r: (b, r, off // W + h))
    return pl.pallas_call(
        functools.partial(_hgrn_kernel, chunk=chunk, l_valid=l_valid),
        grid=(B, HG_HEADS // HG_HPB, L // LB),
        in_specs=[zspec(OFF_HG_Q), zspec(OFF_HG_F), zspec(OFF_HG_I), zspec(OFF_HG_G),
                  pl.BlockSpec((1, 1, W), lambda b, h, r: (h, 0, 0)),
                  pl.BlockSpec((1, LANES), lambda b, h, r: (0, 0)),
                  pl.BlockSpec((1, HG_HPB, LANES, LANES), lambda b, h, r: (b, h, 0, 0))],
        out_specs=[pl.BlockSpec((1, LB, W), lambda b, h, r: (b, r, h)),
                   pl.BlockSpec((1, HG_HPB, LANES, LANES), lambda b, h, r: (b, h, 0, 0))],
        out_shape=[jax.ShapeDtypeStruct((B, L, BRANCH), jnp.bfloat16),
                   jax.ShapeDtypeStruct((B, HG_HEADS, LANES, LANES), jnp.float32)],
        scratch_shapes=[pltpu.VMEM((HG_HPB, LANES, LANES), jnp.float32)],
        compiler_params=_params(3),
        name="hgrn",
    )(z, z, z, z, lb.reshape(HG_HEADS // HG_HPB, 1, W), hg_norm_w.reshape(1, LANES), s0)


def _ssd_kernel(xs_ref, b_ref, c_ref, zg_ref, sm_ref, wx_ref, wb_ref, wc_ref, bx_ref, bb_ref, bc_ref,
                cx_ref, cb_ref, cc_ref, dtb_ref, alog_ref, dskip_ref, nw_ref, s0_ref,
                y_ref, s_out_ref, ox_ref, ob_ref, oc_ref,
                ax_ref, ab_ref, ac_ref, hx_ref, hb_ref, hc_ref, sp_ref, *, chunk, l_valid, conv_rows):
    L = xs_ref.shape[1]
    C = chunk
    R = conv_rows
    gb = pl.program_id(1)
    n_heads = SSD_GPB * SSM_HPG
    n_pairs = n_heads // 2

    def conv(src_ref, head_ref, cst_ref, w_ref, bias_ref, act_ref, out_state_ref):
        head_ref[pl.ds(0, SUBLANES), :] = cst_ref[0]
        head_ref[pl.ds(SUBLANES, R), :] = src_ref[0, pl.ds(0, R), :]
        w = w_ref[...]
        for r0 in range(0, L, R):
            acc = bias_ref[...]
            for j in range(SSM_CONV):
                if r0 == 0:
                    xj = head_ref[pl.ds(SUBLANES - j, R), :]
                else:
                    xj = src_ref[0, pl.ds(r0 - j, R), :]
                acc = acc + xj * w[SSM_CONV - 1 - j:SSM_CONV - j]
            act_ref[pl.ds(r0, R), :] = _silu(acc)
        if l_valid >= SUBLANES:
            out_state_ref[0] = src_ref[0, pl.ds(l_valid - SUBLANES, SUBLANES), :]
        else:
            out_state_ref[0] = head_ref[pl.ds(l_valid, SUBLANES), :]

    conv(xs_ref, hx_ref, cx_ref, wx_ref, bx_ref, ax_ref, ox_ref)
    conv(b_ref, hb_ref, cb_ref, wb_ref, bb_ref, ab_ref, ob_ref)
    conv(c_ref, hc_ref, cc_ref, wc_ref, bc_ref, ac_ref, oc_ref)

    for p in range(n_pairs):
        sp_ref[p] = s0_ref[0, 2 * p:2 * p + 2].reshape(2 * SSM_P, LANES)

    lane1 = lax.broadcasted_iota(jnp.int32, (1, LANES), 1)
    lane8 = lax.broadcasted_iota(jnp.int32, (SUBLANES, LANES), 1)
    row8 = lax.broadcasted_iota(jnp.int32, (SUBLANES, LANES), 0)
    lane0 = SMALL_DT + n_heads * gb
    sel = jnp.where((lane8 == lane0 + row8) & (row8 < n_heads), 1.0, 0.0).astype(jnp.bfloat16)
    a_all = -jnp.exp(alog_ref[...])
    dskip = [jnp.sum(jnp.where(lane1 == lane0 + j, dskip_ref[...], 0.0), axis=-1, keepdims=True)
             for j in range(n_heads)]
    pair_w = 2 * SSM_P
    lane_c = lax.broadcasted_iota(jnp.int32, (C, LANES), 1)
    first_half = lane_c < SSM_P
    row_p = lax.broadcasted_iota(jnp.int32, (2 * SSM_P, LANES), 0) < SSM_P
    tril = lax.broadcasted_iota(jnp.int32, (C, C), 0) >= lax.broadcasted_iota(jnp.int32, (C, C), 1)
    nw = nw_ref[...]

    def body(c, carry):
        t0 = pl.multiple_of(c * C, C)
        sm = sm_ref[0, pl.ds(t0, C), :]
        dt_all = _softplus(sm + dtb_ref[...])
        if l_valid < L:
            rows = lax.broadcasted_iota(jnp.int32, (C, LANES), 0) + t0
            dt_all = jnp.where(rows < l_valid, dt_all, 0.0)
        cum = _cumsum_rows(dt_all * a_all)
        cum_rows = _select_lanes_as_rows(sel, cum)
        col = [jnp.sum(jnp.where(lane_c == lane0 + j, cum, 0.0), axis=-1, keepdims=True) for j in range(n_heads)]
        dtc = [jnp.sum(jnp.where(lane_c == lane0 + j, dt_all, 0.0), axis=-1, keepdims=True) for j in range(n_heads)]
        for lg in range(SSD_GPB):
            bact = ab_ref[pl.ds(t0, C), pl.ds(LANES * lg, LANES)]
            cact = ac_ref[pl.ds(t0, C), pl.ds(LANES * lg, LANES)]
            cb = _dot(cact, bact, _NT)
            ys = []
            for p in range(SSM_HPG // 2):
                pp = lg * (SSM_HPG // 2) + p
                ja, jb = 2 * pp, 2 * pp + 1
                xs = ax_ref[pl.ds(t0, C), pl.ds(pair_w * pp, pair_w)]
                xdt = xs * jnp.where(first_half, dtc[ja], dtc[jb])
                dec_a = jnp.exp(jnp.where(tril, col[ja] - cum_rows[ja:ja + 1], MASK_VALUE))
                dec_b = jnp.exp(jnp.where(tril, col[jb] - cum_rows[jb:jb + 1], MASK_VALUE))
                y = jnp.where(first_half, _dot(cb * dec_a, xdt), _dot(cb * dec_b, xdt))
                sp = sp_ref[pp]
                y = y + _dot(cact, sp, _NT) * jnp.where(first_half, jnp.exp(col[ja]), jnp.exp(col[jb]))
                y = y + jnp.where(first_half, dskip[ja], dskip[jb]) * xs
                last_a, last_b = col[ja][C - 1:C], col[jb][C - 1:C]
                xw = xdt * jnp.where(first_half, jnp.exp(last_a - col[ja]), jnp.exp(last_b - col[jb]))
                sp_ref[pp] = jnp.where(row_p, jnp.exp(last_a), jnp.exp(last_b)) * sp + _dot(xw, bact, _TN)
                zg = zg_ref[0, pl.ds(t0, C), pl.ds(pair_w * pp, pair_w)]
                ys.append(y * _silu(zg))
            ms = sum(jnp.sum(y * y, axis=-1, keepdims=True) for y in ys) * (1.0 / (SSM_HPG * SSM_P))
            inv = lax.rsqrt(ms + EPS)
            for p in range(SSM_HPG // 2):
                pp = lg * (SSM_HPG // 2) + p
                y_ref[0, pl.ds(t0, C), pl.ds(pair_w * pp, pair_w)] = (
                    ys[p] * inv * nw[:, pair_w * pp:pair_w * (pp + 1)]).astype(y_ref.dtype)
        return carry

    lax.fori_loop(0, L // C, body, 0)
    for p in range(n_pairs):
        s_out_ref[0, 2 * p:2 * p + 2] = sp_ref[p].reshape(2, SSM_P, LANES)


def _ssd_call(z, conv_w, conv_b, conv_state8, dtb_pad, alog_pad, dskip_pad, norm_w, s0, *, chunk, l_valid):
    B, L, _ = z.shape
    assert l_valid >= SSM_CONV - 1
    xw = SSD_GPB * SSM_HPG * SSM_P
    bw = SSD_GPB * SSM_STATE
    nh = SSD_GPB * SSM_HPG
    conv_rows = min(L, 128)
    ox, ob = OFF_SM_XBC // xw, (OFF_SM_XBC + BRANCH) // bw
    oc = (OFF_SM_XBC + BRANCH + SSM_GROUPS * SSM_STATE) // bw
    wb0, wc0 = BRANCH // bw, (BRANCH + SSM_GROUPS * SSM_STATE) // bw
    in_specs = [
        pl.BlockSpec((1, L, xw), lambda b, g: (b, 0, ox + g)),
        pl.BlockSpec((1, L, bw), lambda b, g: (b, 0, ob + g)),
        pl.BlockSpec((1, L, bw), lambda b, g: (b, 0, oc + g)),
        pl.BlockSpec((1, L, xw), lambda b, g: (b, 0, OFF_SM_Z // xw + g)),
        pl.BlockSpec((1, L, LANES), lambda b, g: (b, 0, OFF_SMALL // LANES)),
        pl.BlockSpec((SSM_CONV, xw), lambda b, g: (0, g)),
        pl.BlockSpec((SSM_CONV, bw), lambda b, g: (0, wb0 + g)),
        pl.BlockSpec((SSM_CONV, bw), lambda b, g: (0, wc0 + g)),
        pl.BlockSpec((1, xw), lambda b, g: (0, g)),
        pl.BlockSpec((1, bw), lambda b, g: (0, wb0 + g)),
        pl.BlockSpec((1, bw), lambda b, g: (0, wc0 + g)),
        pl.BlockSpec((1, SUBLANES, xw), lambda b, g: (b, 0, g)),
        pl.BlockSpec((1, SUBLANES, bw), lambda b, g: (b, 0, wb0 + g)),
        pl.BlockSpec((1, SUBLANES, bw), lambda b, g: (b, 0, wc0 + g)),
        pl.BlockSpec((1, LANES), lambda b, g: (0, 0)),
        pl.BlockSpec((1, LANES), lambda b, g: (0, 0)),
        pl.BlockSpec((1, LANES), lambda b, g: (0, 0)),
        pl.BlockSpec((1, xw), lambda b, g: (0, g)),
        pl.BlockSpec((1, nh, SSM_P, LANES), lambda b, g: (b, g, 0, 0)),
    ]
    out_specs = [
        pl.BlockSpec((1, L, xw), lambda b, g: (b, 0, g)),
        pl.BlockSpec((1, nh, SSM_P, LANES), lambda b, g: (b, g, 0, 0)),
        pl.BlockSpec((1, SUBLANES, xw), lambda b, g: (b, 0, g)),
        pl.BlockSpec((1, SUBLANES, bw), lambda b, g: (b, 0, g)),
        pl.BlockSpec((1, SUBLANES, bw), lambda b, g: (b, 0, g)),
    ]
    f32 = jnp.float32
    out_shape = [
        jax.ShapeDtypeStruct((B, L, BRANCH), jnp.bfloat16),
        jax.ShapeDtypeStruct((B, SSM_HEADS, SSM_P, LANES), f32),
        jax.ShapeDtypeStruct((B, SUBLANES, BRANCH), f32),
        jax.ShapeDtypeStruct((B, SUBLANES, SSM_GROUPS * SSM_STATE), f32),
        jax.ShapeDtypeStruct((B, SUBLANES, SSM_GROUPS * SSM_STATE), f32),
    ]
    scratch = [pltpu.VMEM((L, xw), f32), pltpu.VMEM((L, bw), f32), pltpu.VMEM((L, bw), f32),
               pltpu.VMEM((SUBLANES + conv_rows, xw), f32), pltpu.VMEM((SUBLANES + conv_rows, bw), f32),
               pltpu.VMEM((SUBLANES + conv_rows, bw), f32),
               pltpu.VMEM((nh // 2, 2 * SSM_P, LANES), f32)]
    cbias = conv_b.reshape(1, -1)
    return pl.pallas_call(
        functools.partial(_ssd_kernel, chunk=chunk, l_valid=l_valid, conv_rows=conv_rows),
        grid=(B, SSM_GROUPS // SSD_GPB), in_specs=in_specs, out_specs=out_specs, out_shape=out_shape, scratch_shapes=scratch,
        compiler_params=_params(2),
        name="ssd",
    )(z, z, z, z, z, conv_w, conv_w, conv_w, cbias, cbias, cbias, conv_state8, conv_state8, conv_state8,
      dtb_pad, alog_pad, dskip_pad, norm_w.reshape(1, -1), s0)


def _s5_kernel(u_ref, bre_ref, bim_ref, cre_ref, cim_ref, are_ref, aim_ref, ldt_ref, d_ref, x0r_ref, x0i_ref,
               h_ref, xr_out_ref, xi_out_ref, bbr_ref, bbi_ref, xr_ref, xi_ref, pwr_ref, pwi_ref,
               *, nseg, seg_len):
    L = u_ref.shape[1]
    n_scan = nseg * seg_len
    row_blk = min(L, 256)
    a_re, a_im = are_ref[...], aim_ref[...]
    dt = jnp.exp(ldt_ref[...])
    mag = jnp.exp(a_re * dt)
    ab_re, ab_im = mag * jnp.cos(a_im * dt), mag * jnp.sin(a_im * dt)
    den = a_re * a_re + a_im * a_im
    nr = ab_re - 1.0
    coef_re = (nr * a_re + ab_im * a_im) / den
    coef_im = (ab_im * a_re - nr * a_im) / den

    NQ = S5_BLK_ST // LANES
    lq = lambda v, q: v[:, LANES * q:LANES * (q + 1)]
    abr = [lq(ab_re, q) for q in range(NQ)]
    abi = [lq(ab_im, q) for q in range(NQ)]

    for r0 in range(0, L, row_blk):
        u = u_ref[0, pl.ds(r0, row_blk), :]
        bu_re, bu_im = _dot(u, bre_ref[0]), _dot(u, bim_ref[0])
        bb_re = coef_re * bu_re - coef_im * bu_im
        bb_im = coef_re * bu_im + coef_im * bu_re
        for q in range(NQ):
            bbr_ref[q, pl.ds(r0, row_blk), :] = lq(bb_re, q)
            bbi_ref[q, pl.ds(r0, row_blk), :] = lq(bb_im, q)

    if n_scan < L:
        for q in range(NQ):
            xr_ref[q, pl.ds(n_scan, L - n_scan), :] = jnp.zeros((L - n_scan, LANES), jnp.float32)
            xi_ref[q, pl.ds(n_scan, L - n_scan), :] = jnp.zeros((L - n_scan, LANES), jnp.float32)

    def rows(i):
        return pl.ds(i, nseg, stride=seg_len) if nseg > 1 else pl.ds(i, 1)

    def scan(i, carry):
        out = []
        for q in range(NQ):
            xr, xi = carry[q]
            pr, pi = _cmul(abr[q], abi[q], xr, xi)
            xr, xi = pr + bbr_ref[q, rows(i), :], pi + bbi_ref[q, rows(i), :]
            xr_ref[q, rows(i), :] = xr
            xi_ref[q, rows(i), :] = xi
            out.append((xr, xi))
        return tuple(out)

    zero = jnp.zeros((nseg, LANES), jnp.float32)
    ends = lax.fori_loop(0, seg_len, scan, tuple((zero, zero) for _ in range(NQ)), unroll=4)

    for q in range(NQ):
        pwr_ref[q, pl.ds(0, 1), :] = abr[q]
        pwi_ref[q, pl.ds(0, 1), :] = abi[q]
    an_r, an_i = ab_re, ab_im
    n = 1
    while n < seg_len:
        m = min(n, seg_len - n)
        for q in range(NQ):
            pr, pi = _cmul(pwr_ref[q, pl.ds(0, m), :], pwi_ref[q, pl.ds(0, m), :], lq(an_r, q), lq(an_i, q))
            pwr_ref[q, pl.ds(n, m), :] = pr
            pwi_ref[q, pl.ds(n, m), :] = pi
        an_r, an_i = _cmul(an_r, an_i, an_r, an_i)
        n *= 2
    seg_r = [pwr_ref[q, pl.ds(seg_len - 1, 1), :] for q in range(NQ)]
    seg_i = [pwi_ref[q, pl.ds(seg_len - 1, 1), :] for q in range(NQ)]

    x0r, x0i = x0r_ref[0], x0i_ref[0]
    dskip = d_ref[...]
    c_r = [lq(x0r, q) for q in range(NQ)]
    c_i = [lq(x0i, q) for q in range(NQ)]
    for k in range(nseg):
        r0 = k * seg_len
        xr_q, xi_q = [], []
        for q in range(NQ):
            dr, di = _cmul(pwr_ref[q], pwi_ref[q], c_r[q], c_i[q])
            xr_q.append(xr_ref[q, pl.ds(r0, seg_len), :] + dr)
            xi_q.append(xi_ref[q, pl.ds(r0, seg_len), :] + di)
            pr, pi = _cmul(seg_r[q], seg_i[q], c_r[q], c_i[q])
            c_r[q], c_i[q] = pr + ends[q][0][k:k + 1], pi + ends[q][1][k:k + 1]
        if k == nseg - 1:
            for q in range(NQ):
                xr_out_ref[0, :, pl.ds(LANES * q, LANES)] = c_r[q]
                xi_out_ref[0, :, pl.ds(LANES * q, LANES)] = c_i[q]
        if n_scan == L:
            u = u_ref[0, pl.ds(r0, seg_len), :]
            y = (_dot(jnp.concatenate(xr_q, axis=-1), cre_ref[0]) - _dot(jnp.concatenate(xi_q, axis=-1), cim_ref[0])
                 + dskip * u)
            h_ref[0, pl.ds(r0, seg_len), :] = _gelu_tanh(y)
        else:
            for q in range(NQ):
                xr_ref[q, pl.ds(r0, seg_len), :] = xr_q[q]
                xi_ref[q, pl.ds(r0, seg_len), :] = xi_q[q]
    if n_scan < L:
        xr = jnp.concatenate([xr_ref[q] for q in range(NQ)], axis=-1)
        xi = jnp.concatenate([xi_ref[q] for q in range(NQ)], axis=-1)
        y = _dot(xr, cre_ref[0]) - _dot(xi, cim_ref[0]) + dskip * u_ref[0]
        h_ref[0] = _gelu_tanh(y)


def _s5seg_kernel(u_ref, bre_ref, bim_ref, cre_ref, cim_ref, are_ref, aim_ref, ldt_ref, d_ref, x0r_ref, x0i_ref,
                  h_ref, xr_out_ref, xi_out_ref, up_ref, bbr_ref, bbi_ref, xr_ref, xi_ref, pwr_ref, pwi_ref, yp_ref,
                  *, seg_len):
    NS = SUBLANES
    a_re, a_im = are_ref[...], aim_ref[...]
    dt = jnp.exp(ldt_ref[...])
    mag = jnp.exp(a_re * dt)
    ab_re, ab_im = mag * jnp.cos(a_im * dt), mag * jnp.sin(a_im * dt)
    den = a_re * a_re + a_im * a_im
    nr = ab_re - 1.0
    coef_re = (nr * a_re + ab_im * a_im) / den
    coef_im = (ab_im * a_re - nr * a_im) / den

    NQ = S5_BLK_ST // LANES
    lq = lambda v, q: v[:, LANES * q:LANES * (q + 1)]
    rep = lambda v: jnp.broadcast_to(v, (NS, LANES))
    abr = [rep(lq(ab_re, q)) for q in range(NQ)]
    abi = [rep(lq(ab_im, q)) for q in range(NQ)]
    blk = S5_BULK_STEPS
    n_blk = seg_len // blk

    def gather(i, carry):
        up_ref[pl.ds(pl.multiple_of(i * NS, NS), NS), :] = u_ref[0, pl.ds(i, NS, stride=seg_len), :]
        return carry

    lax.fori_loop(0, seg_len, gather, 0, unroll=8)

    for rb in range(n_blk):
        u = up_ref[pl.ds(rb * blk * NS, blk * NS), :]
        bu_re, bu_im = _dot(u, bre_ref[0]), _dot(u, bim_ref[0])
        bb_re = coef_re * bu_re - coef_im * bu_im
        bb_im = coef_re * bu_im + coef_im * bu_re
        for q in range(NQ):
            bbr_ref[q, pl.ds(rb * blk, blk)] = lq(bb_re, q).reshape(blk, NS, LANES)
            bbi_ref[q, pl.ds(rb * blk, blk)] = lq(bb_im, q).reshape(blk, NS, LANES)

    def scan(i, carry):
        out = []
        for q in range(NQ):
            xr, xi = carry[q]
            pr, pi = _cmul(abr[q], abi[q], xr, xi)
            xr, xi = pr + bbr_ref[q, i], pi + bbi_ref[q, i]
            xr_ref[q, i] = xr
            xi_ref[q, i] = xi
            out.append((xr, xi))
        return tuple(out)

    zero = jnp.zeros((NS, LANES), jnp.float32)
    ends = lax.fori_loop(0, seg_len, scan, tuple((zero, zero) for _ in range(NQ)), unroll=8)

    for q in range(NQ):
        pwr_ref[q, 0] = abr[q]
        pwi_ref[q, 0] = abi[q]
    an = [(abr[q], abi[q]) for q in range(NQ)]
    n = 1
    while n < seg_len:
        m = min(n, seg_len - n)
        for q in range(NQ):
            pr, pi = _cmul(pwr_ref[q, pl.ds(0, m)], pwi_ref[q, pl.ds(0, m)], an[q][0], an[q][1])
            pwr_ref[q, pl.ds(n, m)] = pr
            pwi_ref[q, pl.ds(n, m)] = pi
            an[q] = _cmul(an[q][0], an[q][1], an[q][0], an[q][1])
        n *= 2

    x0r, x0i = x0r_ref[0], x0i_ref[0]
    cs = []
    for q in range(NQ):
        seg_r, seg_i = pwr_ref[q, seg_len - 1][:1], pwi_ref[q, seg_len - 1][:1]
        cr, ci = [lq(x0r, q)], [lq(x0i, q)]
        for k in range(NS):
            pr, pi = _cmul(seg_r, seg_i, cr[k], ci[k])
            cr.append(pr + ends[q][0][k:k + 1])
            ci.append(pi + ends[q][1][k:k + 1])
        xr_out_ref[0, :, pl.ds(LANES * q, LANES)] = cr[NS]
        xi_out_ref[0, :, pl.ds(LANES * q, LANES)] = ci[NS]
        cs.append((jnp.concatenate(cr[:NS], axis=0), jnp.concatenate(ci[:NS], axis=0)))

    dskip = d_ref[...]
    for rb in range(n_blk):
        xr_q, xi_q = [], []
        for q in range(NQ):
            dr, di = _cmul(pwr_ref[q, pl.ds(rb * blk, blk)], pwi_ref[q, pl.ds(rb * blk, blk)], cs[q][0], cs[q][1])
            xr_q.append((xr_ref[q, pl.ds(rb * blk, blk)] + dr).reshape(blk * NS, LANES))
            xi_q.append((xi_ref[q, pl.ds(rb * blk, blk)] + di).reshape(blk * NS, LANES))
        u = up_ref[pl.ds(rb * blk * NS, blk * NS), :]
        y = (_dot(jnp.concatenate(xr_q, axis=-1), cre_ref[0]) - _dot(jnp.concatenate(xi_q, axis=-1), cim_ref[0])
             + dskip * u)
        yp_ref[pl.ds(rb * blk * NS, blk * NS), :] = _gelu_tanh(y)

    for k in range(NS):
        h_ref[0, pl.ds(k * seg_len, seg_len), :] = yp_ref[pl.ds(k, seg_len, stride=NS), :]


def _s5_call(z, bblk_re, bblk_im, cblk_re, cblk_im, a_re, a_im, log_dt_exp, d_flat, x0_re, x0_im, *, l_valid):
    B, L, _ = z.shape
    nb = bblk_re.shape[0]
    NQ = S5_BLK_ST // LANES
    f32 = jnp.float32
    if l_valid == L and L % (SUBLANES * S5_BULK_STEPS) == 0:
        seg_len = L // SUBLANES
        kern = functools.partial(_s5seg_kernel, seg_len=seg_len)
        tile = pltpu.VMEM((NQ, seg_len, SUBLANES, LANES), f32)
        scratch = [pltpu.VMEM((L, LANES), f32)] + [tile] * 6 + [pltpu.VMEM((L, LANES), f32)]
    else:
        kern = functools.partial(_s5_kernel, nseg=1, seg_len=l_valid)
        scratch = [pltpu.VMEM((NQ, L, LANES), f32)] * 4 + [pltpu.VMEM((NQ, l_valid, LANES), f32)] * 2
    vec = lambda w: pl.BlockSpec((1, w), lambda b, j: (0, j))
    st = pl.BlockSpec((1, 1, S5_BLK_ST), lambda b, j: (b, 0, j))
    return pl.pallas_call(
        kern,
        grid=(B, nb),
        in_specs=[pl.BlockSpec((1, L, S5_BLK_CH), lambda b, j: (b, 0, OFF_S5_U // S5_BLK_CH + j)),
                  pl.BlockSpec((1, S5_BLK_CH, S5_BLK_ST), lambda b, j: (j, 0, 0)),
                  pl.BlockSpec((1, S5_BLK_CH, S5_BLK_ST), lambda b, j: (j, 0, 0)),
                  pl.BlockSpec((1, S5_BLK_ST, S5_BLK_CH), lambda b, j: (j, 0, 0)),
                  pl.BlockSpec((1, S5_BLK_ST, S5_BLK_CH), lambda b, j: (j, 0, 0)),
                  vec(S5_BLK_ST), vec(S5_BLK_ST), vec(S5_BLK_ST), vec(S5_BLK_CH), st, st],
        out_specs=[pl.BlockSpec((1, L, S5_BLK_CH), lambda b, j: (b, 0, j)), st, st],
        out_shape=[jax.ShapeDtypeStruct((B, L, nb * S5_BLK_CH), jnp.float32),
                   jax.ShapeDtypeStruct((B, 1, nb * S5_BLK_ST), jnp.float32),
                   jax.ShapeDtypeStruct((B, 1, nb * S5_BLK_ST), jnp.float32)],
        scratch_shapes=scratch,
        compiler_params=_params(2),
        name="s5",
    )(z, bblk_re, bblk_im, cblk_re, cblk_im, a_re, a_im, log_dt_exp, d_flat, x0_re, x0_im)


def _s5_block_diag(b_re, b_im, c_re, c_im):
    G, P, Cc = b_re.shape
    nb = G // 8
    same = jnp.eye(8, dtype=bool)

    def bblk(b):
        t = jnp.transpose(b.reshape(nb, 8, P, Cc), (0, 1, 3, 2))[:, :, :, None, :]
        t = jnp.where(same[None, :, None, :, None], t, 0.0)
        return t.reshape(nb, 8 * Cc, 8 * P).astype(jnp.bfloat16)

    def cblk(c):
        t = jnp.transpose(c.reshape(nb, 8, Cc, P), (0, 1, 3, 2))[:, :, :, None, :]
        t = jnp.where(same[None, :, None, :, None], t, 0.0)
        return t.reshape(nb, 8 * P, 8 * Cc).astype(jnp.bfloat16)

    return bblk(b_re), bblk(b_im), cblk(c_re), cblk(c_im)


def _glu_kernel(h_ref, w_ref, b_ref, g_ref, o_ref):
    h = h_ref[...]
    t = _dot(h, w_ref[...]) + b_ref[...]
    o_ref[...] = (h * _sigmoid(t) * _silu(g_ref[...])).astype(o_ref.dtype)


def _glu_call(h2d, glu_w_bf16, glu_b, z2d, *, tm):
    M, W = h2d.shape
    return pl.pallas_call(
        _glu_kernel,
        grid=(M // tm,),
        in_specs=[pl.BlockSpec((tm, W), lambda i: (i, 0)),
                  pl.BlockSpec((W, W), lambda i: (0, 0)),
                  pl.BlockSpec((1, W), lambda i: (0, 0)),
                  pl.BlockSpec((tm, W), lambda i: (i, OFF_S5_G // W))],
        out_specs=pl.BlockSpec((tm, W), lambda i: (i, 0)),
        out_shape=jax.ShapeDtypeStruct((M, W), jnp.bfloat16),
        compiler_params=_params(1),
        name="glu",
    )(h2d, glu_w_bf16, glu_b.reshape(1, W), z2d)


def _kvprep_kernel(kv_ref, sm_ref, knw_ref, k_ref, v_ref, ik_ref, *rest, transposed):
    kvw = KV_HEADS * ATT_D
    kv = kv_ref[0]
    knw = knw_ref[...]
    ks = []
    for h in range(KV_HEADS):
        kh = kv[:, ATT_D * h:ATT_D * (h + 1)]
        ms = jnp.mean(kh * kh, axis=-1, keepdims=True)
        ks.append(kh * lax.rsqrt(ms + EPS) * knw)
    k = jnp.concatenate(ks, axis=-1)
    v = kv[:, kvw:]
    sm = sm_ref[0]
    k_ref[0] = k
    v_ref[0] = v
    ik_ref[0] = sm[:, :IDX_D]
    if transposed:
        kbf_ref, vt_ref, kia_ref, kib_ref = rest
        kbf_ref[0] = k.astype(jnp.bfloat16)
        vt_ref[0] = v.T.astype(jnp.bfloat16)
        lane = lax.broadcasted_iota(jnp.int32, sm.shape, 1)
        kia_ref[0] = jnp.where(lane < IDX_D, sm, 0.0).astype(jnp.bfloat16)
        kib_ref[0] = jnp.where(lane >= IDX_D, pltpu.roll(sm, IDX_D, axis=1), 0.0).astype(jnp.bfloat16)


def _kvprep_call(z, k_norm_w, *, tr, transposed):
    B, L, _ = z.shape
    kvw = KV_HEADS * ATT_D
    f32, bf = jnp.float32, jnp.bfloat16
    rowblk = lambda w: pl.BlockSpec((1, tr, w), lambda b, r: (b, r, 0))
    out_specs = [rowblk(kvw), rowblk(kvw), rowblk(IDX_D)]
    out_shape = [jax.ShapeDtypeStruct((B, L, kvw), f32), jax.ShapeDtypeStruct((B, L, kvw), f32),
                 jax.ShapeDtypeStruct((B, L, IDX_D), f32)]
    if transposed:
        out_specs += [rowblk(kvw), pl.BlockSpec((1, kvw, tr), lambda b, r: (b, 0, r)), rowblk(LANES), rowblk(LANES)]
        out_shape += [jax.ShapeDtypeStruct((B, L, kvw), bf), jax.ShapeDtypeStruct((B, kvw, L), bf),
                      jax.ShapeDtypeStruct((B, L, LANES), bf), jax.ShapeDtypeStruct((B, L, LANES), bf)]
    return pl.pallas_call(
        functools.partial(_kvprep_kernel, transposed=transposed),
        grid=(B, L // tr),
        in_specs=[pl.BlockSpec((1, tr, 2 * kvw), lambda b, r: (b, r, OFF_AT_KV // (2 * kvw))),
                  pl.BlockSpec((1, tr, LANES), lambda b, r: (b, r, OFF_SMALL // LANES)),
                  pl.BlockSpec((1, ATT_D), lambda b, r: (0, 0))],
        out_specs=out_specs, out_shape=out_shape,
        compiler_params=_params(2),
        name="kvprep",
    )(z, z, k_norm_w.reshape(1, ATT_D))


def _sortable_key(score):
    bits = lax.bitcast_convert_type(score, jnp.int32)
    return jnp.where(bits < 0, bits ^ 0x7FFFFFFF, bits)


def _rows_reduce(x, op):
    n = x.shape[0]
    slab = 8 * SUBLANES
    if n > slab and n % slab == 0:
        x = op(x.reshape(n // slab, slab, x.shape[1]), axis=0)
    return op(x, axis=0, keepdims=True)


def _lanes_sum(x):
    parts = [x[:, LANES * c:LANES * (c + 1)] for c in range(x.shape[1] // LANES)]
    while len(parts) > 1:
        parts = [a + b for a, b in zip(parts[::2], parts[1::2])] + ([parts[-1]] if len(parts) % 2 else [])
    return jnp.sum(parts[0], axis=1, keepdims=True)


def _kth_largest_key(key, k, axis):
    shape = list(key.shape)
    shape[axis] = 1

    def it(n, tu):
        cand_u = tu | jnp.left_shift(jnp.int32(1), 31 - n)
        cand_s = cand_u ^ jnp.int32(INT_MIN)
        ones = jnp.where(key >= cand_s, 1.0, 0.0)
        cnt = _rows_reduce(ones, jnp.sum) if axis == 0 else _lanes_sum(ones)
        return jnp.where(cnt >= k, cand_u, tu)

    tu = lax.fori_loop(0, 32, it, jnp.zeros(shape, jnp.int32))
    return tu ^ jnp.int32(INT_MIN)


def _t5_bucket(dist):
    exact = REL_BUCKETS // 2
    d = dist.astype(jnp.float32)
    large = exact + jnp.log(jnp.maximum(d, 1.0) / exact) / math.log(REL_MAX_DIST / exact) * (REL_BUCKETS - exact)
    large = jnp.minimum(jnp.maximum(large, 0.0).astype(jnp.int32), REL_BUCKETS - 1)
    return jnp.where(dist < exact, dist, large)


def _bias_lookup(rel_bias, bucket):
    onehot = (bucket[..., None] == jnp.arange(REL_BUCKETS, dtype=jnp.int32)).astype(jnp.float32)
    return jnp.einsum('...k,kh->...h', onehot, rel_bias.astype(jnp.float32), precision=lax.Precision.HIGHEST)


def _pattn_kernel(zq_ref, zg_ref, ziq_ref, zsm_ref, kbf_ref, vt_ref, kia_ref, kib_ref, qnw_ref, bias_ref, o_ref,
                  mask_ref, *, topk):
    i = pl.program_id(1)
    nqb = kbf_ref.shape[1] // Q_BLK
    qnw = qnw_ref[...]
    scale = ATT_D ** -0.5

    def body(S):
        nkb = S // Q_BLK
        wt = zsm_ref[0].T * (1.0 / 32.0)
        kia, kib = kia_ref[0, pl.ds(0, S), :], kib_ref[0, pl.ds(0, S), :]
        score = jnp.zeros((S, Q_BLK), jnp.float32)
        for p in range(IDX_HEADS // 2):
            qi = ziq_ref[0, :, pl.ds(LANES * p, LANES)].astype(jnp.bfloat16)
            da = lax.dot_general(kia, qi, _NT, preferred_element_type=jnp.float32)
            db = lax.dot_general(kib, qi, _NT, preferred_element_type=jnp.float32)
            r = SMALL_IXW + 2 * p
            score = score + jnp.maximum(da, 0.0) * wt[r:r + 1] + jnp.maximum(db, 0.0) * wt[r + 1:r + 2]
        s_pos = lax.broadcasted_iota(jnp.int32, (S, Q_BLK), 0)
        t_pos = lax.broadcasted_iota(jnp.int32, (S, Q_BLK), 1) + i * Q_BLK
        adm = s_pos <= t_pos
        key = jnp.where(adm, _sortable_key(score), jnp.int32(INT_MIN))
        thr = _kth_largest_key(key, topk, 0)
        ge = key >= thr
        cnt_ge = _rows_reduce(jnp.where(ge, 1.0, 0.0), jnp.sum)
        tied = (cnt_ge > topk) & (thr > jnp.int32(INT_MIN))
        any_tied = jnp.max(jnp.where(tied, 1.0, 0.0)) > 0.0

        @pl.when(jnp.logical_not(any_tied))
        def _():
            mask_ref[pl.ds(0, S), :] = jnp.where(ge & adm, 0.0, MASK_VALUE)

        @pl.when(any_tied)
        def _():
            gt = key > thr
            eq = key == thr
            need = topk - _rows_reduce(jnp.where(gt, 1.0, 0.0), jnp.sum)
            nbits = S.bit_length()

            def it(n, j_sel):
                cand = j_sel | jnp.left_shift(jnp.int32(1), nbits - 1 - n)
                c = _rows_reduce(jnp.where(eq & (s_pos < cand), 1.0, 0.0), jnp.sum)
                return jnp.where(c <= need, cand, j_sel)

            j_sel = lax.fori_loop(0, nbits, it, jnp.zeros((1, Q_BLK), jnp.int32))
            sel = (gt | (eq & (s_pos < j_sel))) & adm
            mask_ref[pl.ds(0, S), :] = jnp.where(sel, 0.0, MASK_VALUE)

        for h in range(ATT_HEADS):
            kvh = h // (ATT_HEADS // KV_HEADS)
            q = zq_ref[0, :, pl.ds(ATT_D * h, ATT_D)]
            ms = jnp.mean(q * q, axis=-1, keepdims=True)
            qn = (q * lax.rsqrt(ms + EPS) * qnw).astype(jnp.bfloat16)
            logit = lax.dot_general(kbf_ref[0, pl.ds(0, S), pl.ds(ATT_D * kvh, ATT_D)], qn, _NT,
                                    preferred_element_type=jnp.float32) * scale
            bias = jnp.concatenate([bias_ref[jnp.clip(i - j, 0, N_BIAS_TILES - 1), h] for j in range(nkb)], axis=0)
            x = logit + bias + mask_ref[pl.ds(0, S), :]
            m = _rows_reduce(x, jnp.max)
            pexp = jnp.exp(x - m)
            l = _rows_reduce(pexp, jnp.sum)
            ot = jnp.dot(vt_ref[0, pl.ds(ATT_D * kvh, ATT_D), pl.ds(0, S)], pexp.astype(jnp.bfloat16),
                         preferred_element_type=jnp.float32)
            out = (ot / l).T
            g = zg_ref[0, :, pl.ds(ATT_D * h, ATT_D)]
            o_ref[0, :, pl.ds(ATT_D * h, ATT_D)] = (out * _silu(g)).astype(o_ref.dtype)

    nv = min(N_CAUSAL_VARIANTS, nqb)
    per = nqb // nv
    for c in range(nv):
        pl.when(i // per == c)(functools.partial(body, (c + 1) * per * Q_BLK))


def _pattn_call(z, kbf, vt, kia, kib, q_norm_w, bias_tiles, *, topk):
    B, L, _ = z.shape
    W = ATT_HEADS * ATT_D
    kvw = KV_HEADS * ATT_D
    zblk = lambda off, w: pl.BlockSpec((1, Q_BLK, w), lambda b, i: (b, i, off // w))
    full = lambda shape: pl.BlockSpec((1,) + shape, lambda b, i: (b, 0, 0))
    return pl.pallas_call(
        functools.partial(_pattn_kernel, topk=topk),
        grid=(B, L // Q_BLK),
        in_specs=[zblk(OFF_AT_Q, W), zblk(OFF_AT_G, W), zblk(OFF_IX_Q, W), zblk(OFF_SMALL, LANES),
                  full((L, kvw)), full((kvw, L)), full((L, LANES)), full((L, LANES)),
                  pl.BlockSpec((1, ATT_D), lambda b, i: (0, 0)),
                  pl.BlockSpec((N_BIAS_TILES, ATT_HEADS, Q_BLK, Q_BLK), lambda b, i: (0, 0, 0, 0))],
        out_specs=pl.BlockSpec((1, Q_BLK, W), lambda b, i: (b, i, 0)),
        out_shape=jax.ShapeDtypeStruct((B, L, W), jnp.bfloat16),
        scratch_shapes=[pltpu.VMEM((L, Q_BLK), jnp.float32)],
        compiler_params=_params(2),
        name="pattn",
    )(z, z, z, z, kbf, vt, kia, kib, q_norm_w.reshape(1, ATT_D), bias_tiles)


def _prompt_bias_tiles(rel_bias):
    o = jnp.arange(N_BIAS_TILES, dtype=jnp.int32)[:, None, None]
    s = jnp.arange(Q_BLK, dtype=jnp.int32)[None, :, None]
    t = jnp.arange(Q_BLK, dtype=jnp.int32)[None, None, :]
    tiles = _bias_lookup(rel_bias, _t5_bucket(jnp.maximum(Q_BLK * o + t - s, 0)))
    return jnp.moveaxis(tiles, -1, 1)


def _sattn_score_kernel(pt_ref, qi_ref, w_ref, iknew_ref, *rest):
    pages, s_ref = rest[:NPG], rest[NPG]
    pc = pl.program_id(1)
    last = pl.num_programs(1) - 1
    qi = qi_ref[0]
    w = w_ref[0]

    def head_sum(d):
        r = jnp.maximum(d, 0.0) * w
        return jnp.sum(r.reshape(T_PAD, IDX_HEADS, r.shape[-1]), axis=1)

    @pl.when(pc < last)
    def _():
        for i in range(NPG):
            d = jnp.dot(qi, pages[i][0, 0].astype(jnp.bfloat16), preferred_element_type=jnp.float32)
            s_ref[0, :, pl.ds(PAGE * i, PAGE)] = head_sum(d)

    @pl.when(pc == last)
    def _():
        s_ref[0] = jnp.full((T_PAD, CHUNK_KEYS), MASK_VALUE, jnp.float32)
        d = lax.dot_general(qi, iknew_ref[0].astype(jnp.bfloat16), _NT, preferred_element_type=jnp.float32)
        s_ref[0, :, pl.ds(0, PAGE)] = head_sum(d)


def _sattn_score_call(page_table, qi_rows, w_col, ik_new, cache_ik_t, layer):
    B, n_pages = page_table.shape
    n_chunks = n_pages // NPG
    page_spec = lambda i: pl.BlockSpec(
        (1, 1, IDX_D, PAGE), lambda b, pc, pt: (layer, pt[b, jnp.minimum(pc * NPG + i, n_pages - 1)], 0, 0))
    per_b = lambda shape: pl.BlockSpec((1,) + shape, lambda b, pc, pt: (b, 0, 0))
    grid_spec = pltpu.PrefetchScalarGridSpec(
        num_scalar_prefetch=1, grid=(B, n_chunks + 1),
        in_specs=[per_b((T_PAD * IDX_HEADS, IDX_D)), per_b((T_PAD * IDX_HEADS, 1)), per_b((PAGE, IDX_D))]
        + [page_spec(i) for i in range(NPG)],
        out_specs=pl.BlockSpec((1, T_PAD, CHUNK_KEYS), lambda b, pc, pt: (b, 0, pc)))
    return pl.pallas_call(
        _sattn_score_kernel,
        grid_spec=grid_spec,
        out_shape=jax.ShapeDtypeStruct((B, T_PAD, (n_chunks + 1) * CHUNK_KEYS), jnp.float32),
        compiler_params=_params(2),
        name="sattn_score",
    )(page_table, qi_rows, w_col, ik_new, *([cache_ik_t] * NPG))


def _sattn_kernel(pt_ref, s_ref, q_ref, g_ref, qnw_ref, bias_ref, knew_ref, vnew_ref, *rest, topk, n_new, past):
    kp, vp = rest[:NPG], rest[NPG:2 * NPG]
    o_ref, thr_ref, tie_ref, qn_ref, m_ref, l_ref, acc_ref = rest[2 * NPG:]
    pc = pl.program_id(1)
    last = pl.num_programs(1) - 1
    G = ATT_HEADS // KV_HEADS

    def admissible(col0, width):
        pos = lax.broadcasted_iota(jnp.int32, (T_PAD, width), 1) + col0
        t = lax.broadcasted_iota(jnp.int32, (T_PAD, width), 0)
        return (pos < past) | ((pos - past <= t) & (pos - past < n_new))

    @pl.when(pc == 0)
    def _():
        sc = s_ref[0]
        n_all = sc.shape[-1]
        adm0 = admissible(0, n_all)
        key = jnp.where(adm0, _sortable_key(sc), jnp.int32(INT_MIN))
        thr = _kth_largest_key(key, topk, 1)
        thr_ref[...] = jnp.broadcast_to(thr, thr_ref.shape)
        tie_ref[...] = jnp.full(tie_ref.shape, n_all, jnp.int32)
        cnt_ge = _lanes_sum(jnp.where(key >= thr, 1.0, 0.0))
        tied = (cnt_ge > topk) & (thr > jnp.int32(INT_MIN))

        @pl.when(jnp.max(jnp.where(tied, 1.0, 0.0)) > 0.0)
        def _():
            eq = key == thr
            pos = lax.broadcasted_iota(jnp.int32, key.shape, 1)
            need = topk - _lanes_sum(jnp.where(key > thr, 1.0, 0.0))
            nbits = n_all.bit_length()

            def it(n, j_sel):
                cand = j_sel | jnp.left_shift(jnp.int32(1), nbits - 1 - n)
                c = _lanes_sum(jnp.where(eq & (pos < cand), 1.0, 0.0))
                return jnp.where(c <= need, cand, j_sel)

            j_sel = lax.fori_loop(0, nbits, it, jnp.zeros((T_PAD, 1), jnp.int32))
            tie_ref[...] = jnp.broadcast_to(j_sel, tie_ref.shape)

        q = q_ref[0]
        ms = jnp.mean(q * q, axis=-1, keepdims=True)
        qn_ref[...] = (q * lax.rsqrt(ms + EPS) * qnw_ref[...]).astype(jnp.bfloat16)
        m_ref[...] = jnp.full(m_ref.shape, MASK_VALUE, jnp.float32)
        l_ref[...] = jnp.zeros(l_ref.shape, jnp.float32)
        acc_ref[...] = jnp.zeros(acc_ref.shape, jnp.float32)

    def attend(get_k, get_v, width):
        col0 = pl.multiple_of(pc * CHUNK_KEYS, CHUNK_KEYS)
        sc = s_ref[0, :, pl.ds(col0, width)]
        adm = admissible(col0, width)
        key = jnp.where(adm, _sortable_key(sc), jnp.int32(INT_MIN))
        pos = lax.broadcasted_iota(jnp.int32, (T_PAD, width), 1) + col0
        thr = thr_ref[:, :1]
        sel8 = ((key > thr) | ((key == thr) & (pos < tie_ref[:, :1]))) & adm
        sel = jnp.concatenate([sel8] * G, axis=0)
        for kv in range(KV_HEADS):
            kk = get_k(kv).astype(jnp.bfloat16)
            x = lax.dot_general(qn_ref[kv], kk, _NT, preferred_element_type=jnp.float32) * (ATT_D ** -0.5)
            x = jnp.where(sel, x + bias_ref[kv, :, pl.ds(0, width)], MASK_VALUE)
            m_old = m_ref[kv]
            m_new = jnp.maximum(m_old, jnp.max(x, axis=-1, keepdims=True))
            p = jnp.where(sel, jnp.exp(x - m_new), 0.0)
            alpha = jnp.exp(m_old - m_new)
            l_ref[kv] = alpha * l_ref[kv] + jnp.sum(p, axis=-1, keepdims=True)
            acc_ref[kv] = alpha * acc_ref[kv] + jnp.dot(
                p.astype(jnp.bfloat16), get_v(kv).astype(jnp.bfloat16),
                preferred_element_type=jnp.float32)
            m_ref[kv] = m_new

    @pl.when(pc < last)
    def _():
        attend(lambda kv: jnp.concatenate([kp[i][0, 0, pl.ds(kv, PAGE, stride=KV_HEADS), :] for i in range(NPG)], axis=0),
               lambda kv: jnp.concatenate([vp[i][0, 0, pl.ds(kv, PAGE, stride=KV_HEADS), :] for i in range(NPG)], axis=0),
               CHUNK_KEYS)

    @pl.when(pc == last)
    def _():
        attend(lambda kv: knew_ref[0, :, pl.ds(ATT_D * kv, ATT_D)],
               lambda kv: vnew_ref[0, :, pl.ds(ATT_D * kv, ATT_D)], PAGE)
        o_ref[0] = acc_ref[...] / l_ref[...] * _silu(g_ref[0])


def _sattn_call(page_table, scores, q_rows, g_rows, q_norm_w, bias, k_new, v_new, cache_k, cache_v, layer,
                *, topk, n_new):
    B, n_pages = page_table.shape
    n_chunks = n_pages // NPG
    kvw = KV_HEADS * ATT_D
    rows = (ATT_HEADS // KV_HEADS) * T_PAD
    n_keys = (n_chunks + 1) * CHUNK_KEYS
    page_spec = lambda i: pl.BlockSpec(
        (1, 1, PAGE * KV_HEADS, ATT_D),
        lambda b, pc, pt: (layer, pt[b, jnp.minimum(pc * NPG + i, n_pages - 1)], 0, 0))
    per_b = lambda shape: pl.BlockSpec((1,) + shape, lambda b, pc, pt: (b,) + (0,) * len(shape))
    grid_spec = pltpu.PrefetchScalarGridSpec(
        num_scalar_prefetch=1, grid=(B, n_chunks + 1),
        in_specs=[per_b((T_PAD, n_keys)), per_b((KV_HEADS, rows, ATT_D)), per_b((KV_HEADS, rows, ATT_D)),
                  pl.BlockSpec((1, ATT_D), lambda b, pc, pt: (0, 0)),
                  pl.BlockSpec((KV_HEADS, rows, CHUNK_KEYS), lambda b, pc, pt: (0, 0, pc)),
                  per_b((PAGE, kvw)), per_b((PAGE, kvw))]
        + [page_spec(i) for i in range(NPG)] + [page_spec(i) for i in range(NPG)],
        out_specs=per_b((KV_HEADS, rows, ATT_D)),
        scratch_shapes=[pltpu.VMEM((T_PAD, LANES), jnp.int32), pltpu.VMEM((T_PAD, LANES), jnp.int32),
                        pltpu.VMEM((KV_HEADS, rows, ATT_D), jnp.bfloat16),
                        pltpu.VMEM((KV_HEADS, rows, 1), jnp.float32), pltpu.VMEM((KV_HEADS, rows, 1), jnp.float32),
                        pltpu.VMEM((KV_HEADS, rows, ATT_D), jnp.float32)])
    return pl.pallas_call(
        functools.partial(_sattn_kernel, topk=topk, n_new=n_new, past=n_pages * PAGE),
        grid_spec=grid_spec,
        out_shape=jax.ShapeDtypeStruct((B, KV_HEADS, rows, ATT_D), jnp.float32),
        compiler_params=_params(2),
        name="sattn",
    )(page_table, scores, q_rows, g_rows, q_norm_w.reshape(1, ATT_D), bias, k_new, v_new,
      *([cache_k] * NPG), *([cache_v] * NPG))


def _sample_bias(rel_bias, past, n_keys):
    G = ATT_HEADS // KV_HEADS
    t = jnp.arange(T_PAD, dtype=jnp.int32)[:, None]
    pos = jnp.arange(n_keys, dtype=jnp.int32)[None, :]
    tab = _bias_lookup(rel_bias, _t5_bucket(jnp.maximum(past + t - pos, 0)))
    return jnp.moveaxis(tab, -1, 0).reshape(KV_HEADS, G * T_PAD, n_keys)


def _to_head_rows(a):
    B = a.shape[0]
    G = ATT_HEADS // KV_HEADS
    a = a.reshape(B, T_PAD, KV_HEADS, G, ATT_D)
    return jnp.transpose(a, (0, 2, 3, 1, 4)).reshape(B, KV_HEADS, G * T_PAD, ATT_D)


def _from_head_rows(a):
    B = a.shape[0]
    G = ATT_HEADS // KV_HEADS
    a = a.reshape(B, KV_HEADS, G, T_PAD, ATT_D)
    return jnp.transpose(a, (0, 3, 1, 2, 4)).reshape(B, T_PAD, ATT_HEADS * ATT_D)


def _layer(x, lp, states, attend, *, l_valid, tm, chunk_hg, chunk_ssd):
    B, L, _ = x.shape
    hg_s0, ssm_s0, conv_s0, s5_re0, s5_im0 = states
    x2d = x.reshape(B * L, D_MODEL)
    z2d = _inproj(x2d, lp['norm_w'], lp['w_in'], tm)
    z = z2d.reshape(B, L, D_IN_PAD)

    o_hg, hg_s = _hgrn_call(z, lp['lb'], lp['hg_norm_w'], hg_s0, chunk=chunk_hg, l_valid=l_valid)

    conv8 = jnp.pad(conv_s0, ((0, 0), (SUBLANES - (SSM_CONV - 1), 0), (0, 0)))
    y_ssm, ssm_s, cx, cb, cc = _ssd_call(z, lp['conv_w'], lp['conv_b'], conv8, lp['dtb_pad'], lp['alog_pad'],
                                         lp['dskip_pad'], lp['ssm_norm_w'], ssm_s0, chunk=chunk_ssd, l_valid=l_valid)
    tail = SUBLANES - (SSM_CONV - 1)
    conv_s = jnp.concatenate([cx[:, tail:], cb[:, tail:], cc[:, tail:]], axis=-1)

    h5, s5_re, s5_im = _s5_call(z, *lp['s5_blocks'], lp['a_re'], lp['a_im'], lp['log_dt'], lp['s5_d'],
                                s5_re0.reshape(B, 1, -1), s5_im0.reshape(B, 1, -1), l_valid=l_valid)
    o5 = _glu_call(h5.reshape(B * L, BRANCH), lp['glu_w'], lp['glu_b'], z2d, tm=tm)

    o_att, k, v, ik = attend(z)

    y = _outproj((o_hg.reshape(B * L, BRANCH), y_ssm.reshape(B * L, BRANCH), o5, o_att.reshape(B * L, BRANCH)),
                 lp['w_out'], lp['layer'], x2d, tm)
    st = (k[:, :l_valid].reshape(B, l_valid, KV_HEADS, ATT_D), v[:, :l_valid].reshape(B, l_valid, KV_HEADS, ATT_D),
          ik[:, :l_valid], hg_s, ssm_s, conv_s,
          s5_re.reshape(B, S5_GROUPS, S5_STATE), s5_im.reshape(B, S5_GROUPS, S5_STATE))
    return y.reshape(B, L, D_MODEL), st


def kernel(x_prompt, x_sample, cache_k, cache_v, cache_idx_k, state_hgrn, state_ssm, state_conv, state_s5_re, state_s5_im, page_table, norm_w, w_in, w_out, hg_lb_logits, hg_norm_w, ssm_conv_w, ssm_conv_b, ssm_dt_bias, ssm_a_log, ssm_d, ssm_norm_w, s5_a_re, s5_a_im, s5_log_dt, s5_b_re, s5_b_im, s5_c_re, s5_c_im, s5_d, s5_glu_w, s5_glu_b, att_q_norm, att_k_norm, rel_bias):
    f32 = jnp.float32
    bp, lp_len, _ = x_prompt.shape
    bs, ls, _ = x_sample.shape
    n_pages = page_table.shape[1]
    past = n_pages * PAGE
    n_keys = (n_pages // NPG + 1) * CHUNK_KEYS

    sm = jax.nn.softmax(hg_lb_logits.astype(f32), axis=0)
    lower = jnp.cumsum(sm, axis=0) - sm[0]

    w_out_bf16 = w_out.astype(jnp.bfloat16)
    w_in_t = jnp.swapaxes(w_in, 1, 2)
    cache_ik_t = jnp.swapaxes(cache_idx_k, 2, 3)
    page_rows = lambda c: c.reshape(c.shape[0], c.shape[1], PAGE * KV_HEADS, ATT_D)
    cache_k2, cache_v2 = page_rows(cache_k), page_rows(cache_v)
    bias_tiles = _prompt_bias_tiles(rel_bias)
    bias_sample = _sample_bias(rel_bias, past, n_keys)
    topk_p = min(TOPK_MAX, lp_len // 4)
    topk_s = min(TOPK_MAX, (past + ls) // 4)

    def head_lanes(v):
        return jnp.zeros((1, LANES), f32).at[0, SMALL_DT:SMALL_DT + SSM_HEADS].set(v)

    xs_pad = jnp.pad(x_sample, ((0, 0), (0, SAMPLE_ROWS - ls), (0, 0)))
    pad_page = lambda a: jnp.pad(a, ((0, 0), (0, PAGE - a.shape[1]), (0, 0)))

    yp, ys = x_prompt, xs_pad
    new_p = [[] for _ in range(8)]
    new_s = [[] for _ in range(8)]
    for l in range(DEPTH):
        lp = {'norm_w': norm_w[l], 'w_in': _pack_w_in(w_in_t, l), 'w_out': w_out_bf16, 'layer': l,
              'lb': lower[l], 'hg_norm_w': hg_norm_w[l],
              'conv_w': ssm_conv_w[l], 'conv_b': ssm_conv_b[l], 'dtb_pad': head_lanes(ssm_dt_bias[l]),
              'alog_pad': head_lanes(ssm_a_log[l]), 'dskip_pad': head_lanes(ssm_d[l]), 'ssm_norm_w': ssm_norm_w[l],
              's5_blocks': _s5_block_diag(s5_b_re[l], s5_b_im[l], s5_c_re[l], s5_c_im[l]),
              'a_re': s5_a_re[l].reshape(1, -1), 'a_im': s5_a_im[l].reshape(1, -1),
              'log_dt': jnp.repeat(s5_log_dt[l], S5_STATE).reshape(1, -1), 's5_d': s5_d[l].reshape(1, -1),
              'glu_w': s5_glu_w[l].astype(jnp.bfloat16), 'glu_b': s5_glu_b[l]}

        def attend_prompt(z):
            k, v, ik, kbf, vt, kia, kib = _kvprep_call(z, att_k_norm[l], tr=256, transposed=True)
            o = _pattn_call(z, kbf, vt, kia, kib, att_q_norm[l], bias_tiles, topk=topk_p)
            return o, k, v, ik

        def attend_sample(z):
            k, v, ik = _kvprep_call(z, att_k_norm[l], tr=SAMPLE_ROWS, transposed=False)
            z8 = z[:, :T_PAD]
            qi_rows = z8[..., OFF_IX_Q:OFF_IX_Q + IDX_HEADS * IDX_D].reshape(bs, T_PAD * IDX_HEADS, IDX_D)
            w_col = z8[..., OFF_SMALL + SMALL_IXW:OFF_SMALL + SMALL_IXW + IDX_HEADS] * (1.0 / 32.0)
            scores = _sattn_score_call(page_table, qi_rows.astype(jnp.bfloat16),
                                       w_col.reshape(bs, T_PAD * IDX_HEADS, 1), pad_page(ik), cache_ik_t, l)
            o = _sattn_call(page_table, scores, _to_head_rows(z8[..., OFF_AT_Q:OFF_AT_Q + BRANCH]),
                            _to_head_rows(z8[..., OFF_AT_G:OFF_AT_G + BRANCH]), att_q_norm[l], bias_sample,
                            pad_page(k), pad_page(v), cache_k2, cache_v2, l, topk=topk_s, n_new=ls)
            o = jnp.pad(_from_head_rows(o), ((0, 0), (0, SAMPLE_ROWS - T_PAD), (0, 0)))
            return o.astype(jnp.bfloat16), k, v, ik

        zero_states = (jnp.zeros((bp, HG_HEADS, LANES, LANES), f32),
                       jnp.zeros((bp, SSM_HEADS, SSM_P, SSM_STATE), f32),
                       jnp.zeros((bp, SSM_CONV - 1, SSM_CONV_DIM), f32),
                       jnp.zeros((bp, S5_GROUPS, S5_STATE), f32),
                       jnp.zeros((bp, S5_GROUPS, S5_STATE), f32))
        yp, st_p = _layer(yp, lp, zero_states, attend_prompt, l_valid=lp_len, tm=512, chunk_hg=32, chunk_ssd=128)
        samp_states = (state_hgrn[l], state_ssm[l], state_conv[l], state_s5_re[l], state_s5_im[l])
        ys, st_s = _layer(ys, lp, samp_states, attend_sample, l_valid=ls, tm=bs * SAMPLE_ROWS,
                          chunk_hg=SAMPLE_ROWS, chunk_ssd=SAMPLE_ROWS)
        for i in range(8):
            new_p[i].append(st_p[i])
            new_s[i].append(st_s[i])
    pk, pv, pik, phg, pssm, pconv, ps5r, ps5i = [jnp.stack(a) for a in new_p]
    sk, sv, sik, shg, sssm, sconv, ss5r, ss5i = [jnp.stack(a) for a in new_s]
    return (yp, ys[:, :ls], pk, pv, pik, phg, pssm, pconv, ps5r, ps5i, sk, sv, sik, shg, sssm, sconv, ss5r, ss5i)
```
